```python
import jax, jax.numpy as jnp
from jax import lax
import numpy as np

D_MODEL = 1024
BATCH = 2
SEQ = 16384
DEPTH = 4

N_A_LAYERS = DEPTH // 2
N_B_LAYERS = DEPTH - N_A_LAYERS
D_RNN = D_MODEL
LRU_BLOCKS = 4
LRU_BLOCK_W = D_RNN // LRU_BLOCKS
CONV_W = 4
LRU_C = 8.0
N_HEADS = 8
HEAD_DIM = D_MODEL // N_HEADS
ROT_DIM = HEAD_DIM // 4
ROPE_THETA = 500000.0
MOBA_BLOCK = 256
MOBA_TOPK = 3
Q_CHUNK = 64
N_EXPERTS = 64
N_GROUPS = 8
TOPK_GROUPS = 4
TOPK_EXPERTS = 8
D_EXPERT = 256
D_SHARED = 256
ROUTED_SCALE = 2.5
MOE_ROW_BLOCK = 256
PLE_DIM = 256
DN_ALPHA = (2 * DEPTH) ** 0.25
DN_BETA = (8 * DEPTH) ** -0.25
LN_EPS = 1e-5

kernel_name = 'yoco_rglru_moba_moe_deepnorm'


def layer_norm(x, g, b):
    xf = x.astype(jnp.float32)
    mu = xf.mean(-1, keepdims=True)
    var = jnp.square(xf - mu).mean(-1, keepdims=True)
    y = (xf - mu) * lax.rsqrt(var + LN_EPS) * g.astype(jnp.float32) + b.astype(jnp.float32)
    return y.astype(x.dtype)


def rope_tables(seq, dtype):
    inv = ROPE_THETA ** (-jnp.arange(0, ROT_DIM, 2, dtype=jnp.float32) / ROT_DIM)
    ang = jnp.arange(seq, dtype=jnp.float32)[:, None] * inv[None, :]
    return jnp.cos(ang).astype(dtype), jnp.sin(ang).astype(dtype)


def apply_partial_rope(t, cos, sin):
    c = cos[None, :, None, :]
    s = sin[None, :, None, :]
    half = ROT_DIM // 2
    x1 = t[..., :half]
    x2 = t[..., half:ROT_DIM]
    return jnp.concatenate([x1 * c - x2 * s, x2 * c + x1 * s, t[..., ROT_DIM:]], axis=-1)


def _lin_combine(c1, c2):
    a1, b1 = c1
    a2, b2 = c2
    return a1 * a2, a2 * b1 + b2


def rglru_block(x, w_in, b_in, conv_w, conv_b, ga_w, ga_b, gi_w, gi_b, lam, w_out):
    B, S, _ = x.shape
    xy = x @ w_in + b_in
    xb, yb = jnp.split(xy, 2, axis=-1)
    y = jax.nn.gelu(yb, approximate=True)
    xp = jnp.pad(xb, ((0, 0), (CONV_W - 1, 0), (0, 0)))
    xc = conv_b + sum(xp[:, k:k + S] * conv_w[k] for k in range(CONV_W))
    xg = xc.reshape(B, S, LRU_BLOCKS, LRU_BLOCK_W)
    r = jax.nn.sigmoid(jnp.einsum('bsnc,ncd->bsnd', xg, ga_w) + ga_b).reshape(B, S, D_RNN)
    i = jax.nn.sigmoid(jnp.einsum('bsnc,ncd->bsnd', xg, gi_w) + gi_b).reshape(B, S, D_RNN)
    log_a = (-LRU_C * r.astype(jnp.float32)) * jax.nn.softplus(-lam.astype(jnp.float32))
    a = jnp.exp(log_a)
    u = jnp.sqrt(-jnp.expm1(2.0 * log_a)) * (i * xc).astype(jnp.float32)
    _, h = lax.associative_scan(_lin_combine, (a, u), axis=1)
    return (h.astype(x.dtype) * y) @ w_out


def shared_kv(xs, g, b, w_kv, cos, sin):
    B, S, _ = xs.shape
    h = layer_norm(xs, g, b)
    kv = (h @ w_kv).reshape(B, S, 2, N_HEADS, HEAD_DIM)
    k = apply_partial_rope(kv[:, :, 0], cos, sin).transpose(0, 2, 1, 3)
    v = kv[:, :, 1].transpose(0, 2, 1, 3)
    nb = -(-S // MOBA_BLOCK)
    pad = nb * MOBA_BLOCK - S
    k_blk = jnp.pad(k, ((0, 0), (0, 0), (0, pad), (0, 0))).reshape(B, N_HEADS, nb, MOBA_BLOCK, HEAD_DIM)
    v_blk = jnp.pad(v, ((0, 0), (0, 0), (0, pad), (0, 0))).reshape(B, N_HEADS, nb, MOBA_BLOCK, HEAD_DIM)
    k_mean = k_blk.astype(jnp.float32).mean(axis=3)
    return k_blk, v_blk, k_mean


def moba_attention(x, w_q, w_o, k_blk, v_blk, k_mean, cos, sin):
    B, S, _ = x.shape
    nb = k_blk.shape[2]
    tk = min(MOBA_TOPK, nb)
    q = apply_partial_rope((x @ w_q).reshape(B, S, N_HEADS, HEAD_DIM), cos, sin) * (HEAD_DIM ** -0.5)
    q = q.transpose(0, 2, 1, 3)
    bi = jnp.arange(B)[:, None, None, None]
    hi = jnp.arange(N_HEADS)[None, :, None, None]
    blk_ids = jnp.arange(nb)

    def chunk(c):
        q0 = c * Q_CHUNK
        qc = lax.dynamic_slice_in_dim(q, q0, Q_CHUNK, axis=2)
        own = q0 // MOBA_BLOCK
        gate = jnp.einsum('bhqd,bhnd->bhqn', qc.astype(jnp.float32), k_mean)
        gate = jnp.where(blk_ids < own, gate, -jnp.inf)
        _, sel = lax.top_k(gate, tk)
        valid = sel < own
        k_sel = k_blk[bi, hi, sel]
        v_sel = v_blk[bi, hi, sel]
        s_sel = jnp.einsum('bhqd,bhqjkd->bhqjk', qc, k_sel, preferred_element_type=jnp.float32)
        s_sel = jnp.where(valid[..., None], s_sel, -jnp.inf).reshape(B, N_HEADS, Q_CHUNK, tk * MOBA_BLOCK)
        k_own = lax.dynamic_index_in_dim(k_blk, own, axis=2, keepdims=False)
        v_own = lax.dynamic_index_in_dim(v_blk, own, axis=2, keepdims=False)
        s_own = jnp.einsum('bhqd,bhkd->bhqk', qc, k_own, preferred_element_type=jnp.float32)
        qpos = q0 + jnp.arange(Q_CHUNK)
        kpos = own * MOBA_BLOCK + jnp.arange(MOBA_BLOCK)
        s_own = jnp.where(kpos[None, :] <= qpos[:, None], s_own, -jnp.inf)
        probs = jax.nn.softmax(jnp.concatenate([s_sel, s_own], axis=-1), axis=-1).astype(v_blk.dtype)
        p_sel = probs[..., :tk * MOBA_BLOCK].reshape(B, N_HEADS, Q_CHUNK, tk, MOBA_BLOCK)
        p_own = probs[..., tk * MOBA_BLOCK:]
        return (jnp.einsum('bhqjk,bhqjkd->bhqd', p_sel, v_sel)
                + jnp.einsum('bhqk,bhkd->bhqd', p_own, v_own))

    out = lax.map(chunk, jnp.arange(S // Q_CHUNK))
    out = out.transpose(1, 0, 3, 2, 4).reshape(B, S, N_HEADS * HEAD_DIM)
    return out @ w_o


def moe_ffn(x, router_w, router_b, w_gate, w_up, w_down, s_gate, s_up, s_down):
    B, S, D = x.shape
    N = B * S
    M = MOE_ROW_BLOCK
    xt = x.reshape(N, D)
    scores = jax.nn.sigmoid(jnp.einsum('nd,de->ne', xt, router_w, preferred_element_type=jnp.float32))
    choice = scores + router_b.astype(jnp.float32)
    grp = choice.reshape(N, N_GROUPS, N_EXPERTS // N_GROUPS)
    grp_score = lax.top_k(grp, 2)[0].sum(-1)
    _, grp_idx = lax.top_k(grp_score, TOPK_GROUPS)
    grp_mask = jax.nn.one_hot(grp_idx, N_GROUPS, dtype=jnp.float32).sum(1) > 0
    expert_mask = jnp.repeat(grp_mask, N_EXPERTS // N_GROUPS, axis=1)
    choice = jnp.where(expert_mask, choice, -jnp.inf)
    _, e_idx = lax.top_k(choice, TOPK_EXPERTS)
    w = jnp.take_along_axis(scores, e_idx, axis=1)
    w = w / w.sum(-1, keepdims=True) * ROUTED_SCALE
    NK = N * TOPK_EXPERTS
    flat_e = e_idx.reshape(NK)
    flat_tok = jnp.repeat(jnp.arange(N, dtype=jnp.int32), TOPK_EXPERTS)
    flat_w = w.reshape(NK)
    order = jnp.argsort(flat_e)
    se, stok, sw = flat_e[order], flat_tok[order], flat_w[order]
    counts = jnp.bincount(flat_e, length=N_EXPERTS)
    starts = jnp.cumsum(counts) - counts
    padded = (counts + M - 1) // M * M
    pends = jnp.cumsum(padded)
    pstarts = pends - padded
    pos = pstarts[se] + jnp.arange(NK) - starts[se]
    n_blocks = -(-NK // M) + N_EXPERTS
    R = n_blocks * M
    buf_tok = jnp.zeros((R,), jnp.int32).at[pos].set(stok)
    buf_w = jnp.zeros((R,), jnp.float32).at[pos].set(sw)
    blk_e = jnp.minimum(jnp.searchsorted(pends, jnp.arange(n_blocks) * M, side='right'), N_EXPERTS - 1)

    def step(acc, blk):
        tok, wt, e = blk
        xs = xt[tok]
        hdn = jax.nn.silu(xs @ w_gate[e]) * (xs @ w_up[e])
        y = (hdn @ w_down[e]).astype(jnp.float32) * wt[:, None]
        return acc.at[tok].add(y), None

    acc, _ = lax.scan(step, jnp.zeros((N, D), jnp.float32),
                      (buf_tok.reshape(n_blocks, M), buf_w.reshape(n_blocks, M), blk_e))
    shared = (jax.nn.silu(xt @ s_gate) * (xt @ s_up)) @ s_down
    return (acc.astype(x.dtype) + shared).reshape(B, S, D)


def setup_inputs(seed: int = 0) -> dict:
    key = jax.random.key(seed)
    ks = iter(jax.random.split(key, 32))
    f32 = jnp.float32

    def nrm(shape, scale):
        return jax.random.normal(next(ks), shape, f32) * scale

    x = nrm((BATCH, SEQ, D_MODEL), 1.0)
    p = nrm((DEPTH, BATCH, SEQ, PLE_DIM), 1.0)
    ln_g = 1.0 + nrm((DEPTH, 2, D_MODEL), 0.02)
    ln_b = nrm((DEPTH, 2, D_MODEL), 0.02)
    a_w_in = nrm((N_A_LAYERS, D_MODEL, 2 * D_RNN), D_MODEL ** -0.5)
    a_b_in = nrm((N_A_LAYERS, 2 * D_RNN), 0.01)
    a_conv_w = nrm((N_A_LAYERS, CONV_W, D_RNN), CONV_W ** -0.5)
    a_conv_b = nrm((N_A_LAYERS, D_RNN), 0.01)
    a_gate_a_w = nrm((N_A_LAYERS, LRU_BLOCKS, LRU_BLOCK_W, LRU_BLOCK_W), LRU_BLOCK_W ** -0.5)
    a_gate_a_b = nrm((N_A_LAYERS, LRU_BLOCKS, LRU_BLOCK_W), 0.01)
    a_gate_i_w = nrm((N_A_LAYERS, LRU_BLOCKS, LRU_BLOCK_W, LRU_BLOCK_W), LRU_BLOCK_W ** -0.5)
    a_gate_i_b = nrm((N_A_LAYERS, LRU_BLOCKS, LRU_BLOCK_W), 0.01)
    u = jax.random.uniform(next(ks), (N_A_LAYERS, D_RNN), f32, minval=0.9, maxval=0.999)
    s = u ** (1.0 / LRU_C)
    a_lambda = jnp.log(s) - jnp.log1p(-s)
    a_w_out = nrm((N_A_LAYERS, D_RNN, D_MODEL), D_RNN ** -0.5 * DN_BETA)
    kv_ln_g = 1.0 + nrm((D_MODEL,), 0.02)
    kv_ln_b = nrm((D_MODEL,), 0.02)
    w_kv = nrm((D_MODEL, 2 * N_HEADS * HEAD_DIM), D_MODEL ** -0.5)
    b_w_q = nrm((N_B_LAYERS, D_MODEL, N_HEADS * HEAD_DIM), D_MODEL ** -0.5)
    b_w_o = nrm((N_B_LAYERS, N_HEADS * HEAD_DIM, D_MODEL), (N_HEADS * HEAD_DIM) ** -0.5 * DN_BETA)
    router_w = nrm((DEPTH, D_MODEL, N_EXPERTS), D_MODEL ** -0.5)
    router_b = nrm((DEPTH, N_EXPERTS), 0.01)
    exp_w_gate = nrm((DEPTH, N_EXPERTS, D_MODEL, D_EXPERT), D_MODEL ** -0.5)
    exp_w_up = nrm((DEPTH, N_EXPERTS, D_MODEL, D_EXPERT), D_MODEL ** -0.5)
    exp_w_down = nrm((DEPTH, N_EXPERTS, D_EXPERT, D_MODEL), D_EXPERT ** -0.5 * DN_BETA)
    sh_w_gate = nrm((DEPTH, D_MODEL, D_SHARED), D_MODEL ** -0.5)
    sh_w_up = nrm((DEPTH, D_MODEL, D_SHARED), D_MODEL ** -0.5)
    sh_w_down = nrm((DEPTH, D_SHARED, D_MODEL), D_SHARED ** -0.5 * DN_BETA)
    ple_w = nrm((DEPTH, PLE_DIM, D_MODEL), PLE_DIM ** -0.5)
    ple_gate_w = nrm((DEPTH, D_MODEL, D_MODEL), D_MODEL ** -0.5)
    return {'x': x, 'p': p, 'ln_g': ln_g, 'ln_b': ln_b,
            'a_w_in': a_w_in, 'a_b_in': a_b_in, 'a_conv_w': a_conv_w, 'a_conv_b': a_conv_b,
            'a_gate_a_w': a_gate_a_w, 'a_gate_a_b': a_gate_a_b, 'a_gate_i_w': a_gate_i_w,
            'a_gate_i_b': a_gate_i_b, 'a_lambda': a_lambda, 'a_w_out': a_w_out,
            'kv_ln_g': kv_ln_g, 'kv_ln_b': kv_ln_b, 'w_kv': w_kv, 'b_w_q': b_w_q, 'b_w_o': b_w_o,
            'router_w': router_w, 'router_b': router_b, 'exp_w_gate': exp_w_gate,
            'exp_w_up': exp_w_up, 'exp_w_down': exp_w_down, 'sh_w_gate': sh_w_gate,
            'sh_w_up': sh_w_up, 'sh_w_down': sh_w_down, 'ple_w': ple_w, 'ple_gate_w': ple_gate_w}


def reference(x, p, ln_g, ln_b, a_w_in, a_b_in, a_conv_w, a_conv_b, a_gate_a_w, a_gate_a_b,
              a_gate_i_w, a_gate_i_b, a_lambda, a_w_out, kv_ln_g, kv_ln_b, w_kv, b_w_q, b_w_o,
              router_w, router_b, exp_w_gate, exp_w_up, exp_w_down, sh_w_gate, sh_w_up,
              sh_w_down, ple_w, ple_gate_w):
    S = x.shape[1]
    cos, sin = rope_tables(S, x.dtype)
    for i in range(DEPTH):
        if i < N_A_LAYERS:
            mix = rglru_block(x, a_w_in[i], a_b_in[i], a_conv_w[i], a_conv_b[i], a_gate_a_w[i],
                              a_gate_a_b[i], a_gate_i_w[i], a_gate_i_b[i], a_lambda[i], a_w_out[i])
        else:
            if i == N_A_LAYERS:
                k_blk, v_blk, k_mean = shared_kv(x, kv_ln_g, kv_ln_b, w_kv, cos, sin)
            j = i - N_A_LAYERS
            mix = moba_attention(x, b_w_q[j], b_w_o[j], k_blk, v_blk, k_mean, cos, sin)
        x = layer_norm(DN_ALPHA * x + mix, ln_g[i, 0], ln_b[i, 0])
        ffn = moe_ffn(x, router_w[i], router_b[i], exp_w_gate[i], exp_w_up[i], exp_w_down[i],
                      sh_w_gate[i], sh_w_up[i], sh_w_down[i])
        x = layer_norm(DN_ALPHA * x + ffn, ln_g[i, 1], ln_b[i, 1])
        x = x + (p[i] @ ple_w[i]) * jax.nn.sigmoid(x @ ple_gate_w[i])
    return x
```

```python
import functools

import jax
import jax.numpy as jnp
from jax import lax
from jax.experimental import pallas as pl
from jax.experimental.pallas import tpu as pltpu

F32 = jnp.float32
BF16 = jnp.bfloat16
I32 = jnp.int32

D_MODEL = 1024
DEPTH = 4
N_A_LAYERS = DEPTH // 2
D_RNN = D_MODEL
LRU_BLOCKS = 4
LRU_BLOCK_W = D_RNN // LRU_BLOCKS
CONV_W = 4
LRU_C = 8.0
N_HEADS = 8
HEAD_DIM = D_MODEL // N_HEADS
ROT_DIM = HEAD_DIM // 4
ROPE_THETA = 500000.0
MOBA_BLOCK = 256
MOBA_TOPK = 3
N_EXPERTS = 64
N_GROUPS = 8
GROUP_SIZE = N_EXPERTS // N_GROUPS
TOPK_GROUPS = 4
TOPK_EXPERTS = 8
D_EXPERT = 256
D_SHARED = 256
ROUTED_SCALE = 2.5
PLE_DIM = 256
DN_ALPHA = (2 * DEPTH) ** 0.25
LN_EPS = 1e-5

V7X_VMEM_LIMIT_BYTES = 56 * 1024 * 1024

MIXER_ROWS = 256
TOKEN_ROWS = 512
EXPERT_ROWS = 256
NEG_INF = float("-inf")


def _params(*sem):
    return pltpu.CompilerParams(dimension_semantics=sem, vmem_limit_bytes=V7X_VMEM_LIMIT_BYTES)


def _layer_norm(z, g, b):
    mu = jnp.mean(z, axis=-1, keepdims=True)
    zc = z - mu
    var = jnp.mean(zc * zc, axis=-1, keepdims=True)
    return zc * lax.rsqrt(var + LN_EPS) * g + b


def _silu(x):
    return x * jax.nn.sigmoid(x)


def _gelu_tanh(x):
    return 0.5 * x * (1.0 + jnp.tanh(0.7978845608028654 * (x + 0.044715 * (x * x * x))))


def _full(shape):
    return pl.BlockSpec(shape, lambda *_: (0,) * len(shape))


def _rglru_kernel(x_ref, win_ref, bin_ref, cw_ref, cb_ref, gaw_ref, gab_ref, giw_ref, gib_ref,
                  lam_ref, wout_ref, lng_ref, lnb_ref, o_ref, tail_ref, h_ref, *, tiles_per_seq):
    i = pl.program_id(0)
    tm = x_ref.shape[0]

    @pl.when(i % tiles_per_seq == 0)
    def _():
        tail_ref[...] = jnp.zeros_like(tail_ref)
        h_ref[...] = jnp.zeros_like(h_ref)

    x = x_ref[...]
    xy = jnp.dot(x.astype(BF16), win_ref[...], preferred_element_type=F32) + bin_ref[...]
    xb = xy[:, :D_RNN]
    y = _gelu_tanh(xy[:, D_RNN:])

    tail = tail_ref[...]
    row8 = lax.broadcasted_iota(I32, (8, 1), 0)
    xc = cb_ref[...] + xb * cw_ref[CONV_W - 1:CONV_W, :]
    for d in range(1, CONV_W):
        rolled = pltpu.roll(xb, d, 0)
        head = jnp.where(row8 < d, pltpu.roll(tail, d, 0), rolled[:8])
        shifted = jnp.concatenate([head, rolled[8:]], axis=0)
        xc = xc + shifted * cw_ref[CONV_W - 1 - d:CONV_W - d, :]
    tail_ref[...] = xb[tm - 8:, :]

    r_parts, i_parts = [], []
    for n in range(LRU_BLOCKS):
        xg = xc[:, n * LRU_BLOCK_W:(n + 1) * LRU_BLOCK_W].astype(BF16)
        r_parts.append(jnp.dot(xg, gaw_ref[n], preferred_element_type=F32))
        i_parts.append(jnp.dot(xg, giw_ref[n], preferred_element_type=F32))
    r = jax.nn.sigmoid(jnp.concatenate(r_parts, axis=1) + gab_ref[...])
    ig = jax.nn.sigmoid(jnp.concatenate(i_parts, axis=1) + gib_ref[...])

    lam = lam_ref[...]
    softplus_neg_lam = jnp.maximum(-lam, 0.0) + jnp.log1p(jnp.exp(-jnp.abs(lam)))
    log_a = (-LRU_C * r) * softplus_neg_lam
    a = jnp.exp(log_a)
    u = jnp.sqrt(1.0 - a * a) * (ig * xc)

    row = lax.broadcasted_iota(I32, (tm, 1), 0)
    acc_a, acc_h = a, u
    d = 1
    while d < tm:
        keep = row >= d
        sh_a = pltpu.roll(acc_a, d, 0)
        sh_h = pltpu.roll(acc_h, d, 0)
        acc_h = jnp.where(keep, acc_a * sh_h + acc_h, acc_h)
        acc_a = jnp.where(keep, acc_a * sh_a, acc_a)
        d *= 2
    h = acc_h + acc_a * h_ref[...]
    h_ref[...] = h[tm - 1:, :]

    mix = jnp.dot((h * y).astype(BF16), wout_ref[...], preferred_element_type=F32)
    o_ref[...] = _layer_norm(DN_ALPHA * x + mix, lng_ref[...], lnb_ref[...])


def _rglru_layer(x, seq, w_in, b_in, conv_w, conv_b, ga_w, ga_b, gi_w, gi_b, lam, w_out, ln_g, ln_b):
    n, d = x.shape
    tm = min(MIXER_ROWS, seq)
    row = lambda v: v.reshape(1, -1).astype(F32)
    return pl.pallas_call(
        functools.partial(_rglru_kernel, tiles_per_seq=seq // tm),
        grid=(n // tm,),
        in_specs=[pl.BlockSpec((tm, d), lambda i: (i, 0)),
                  _full((d, 2 * D_RNN)), _full((1, 2 * D_RNN)),
                  _full((CONV_W, D_RNN)), _full((1, D_RNN)),
                  _full((LRU_BLOCKS, LRU_BLOCK_W, LRU_BLOCK_W)), _full((1, D_RNN)),
                  _full((LRU_BLOCKS, LRU_BLOCK_W, LRU_BLOCK_W)), _full((1, D_RNN)),
                  _full((1, D_RNN)), _full((D_RNN, d)), _full((1, d)), _full((1, d))],
        out_specs=pl.BlockSpec((tm, d), lambda i: (i, 0)),
        out_shape=jax.ShapeDtypeStruct((n, d), F32),
        scratch_shapes=[pltpu.VMEM((8, D_RNN), F32), pltpu.VMEM((1, D_RNN), F32)],
        compiler_params=_params("arbitrary"),
        name="rglru_mixer",
    )(x, w_in.astype(BF16), row(b_in), conv_w, row(conv_b), ga_w.astype(BF16), row(ga_b),
      gi_w.astype(BF16), row(gi_b), row(lam), w_out.astype(BF16), row(ln_g), row(ln_b))


def _first_argmax(cur, idx, size, axis):
    m = jnp.max(cur, axis=axis, keepdims=True)
    first = jnp.min(jnp.where(cur == m, idx, size), axis=axis, keepdims=True)
    return m, first


def _router_kernel(x_ref, rw_ref, rb_ref, e_ref, w_ref, r_ref, cnt_ref, carry_ref):
    i = pl.program_id(0)
    tm = x_ref.shape[0]

    @pl.when(i == 0)
    def _():
        carry_ref[...] = jnp.zeros_like(carry_ref)

    logits = jnp.dot(x_ref[...], rw_ref[...], preferred_element_type=F32,
                     precision=lax.Precision.HIGHEST)
    scores = jax.nn.sigmoid(logits.T[:N_EXPERTS, :])
    choice = scores + rb_ref[...]

    c3 = choice.reshape(N_GROUPS, GROUP_SIZE, tm)
    in_grp = lax.broadcasted_iota(I32, c3.shape, 1)
    m1, i1 = _first_argmax(c3, in_grp, GROUP_SIZE, 1)
    m2 = jnp.max(jnp.where(in_grp == i1, NEG_INF, c3), axis=1, keepdims=True)
    grp_score = (m1 + m2)[:, 0, :]

    grp_id = lax.broadcasted_iota(I32, grp_score.shape, 0)
    grp_sel = jnp.zeros(grp_score.shape, jnp.bool_)
    cur = grp_score
    for _ in range(TOPK_GROUPS):
        _, gi = _first_argmax(cur, grp_id, N_GROUPS, 0)
        hit = grp_id == gi
        grp_sel = jnp.logical_or(grp_sel, hit)
        cur = jnp.where(hit, NEG_INF, cur)

    cur = jnp.where(grp_sel[:, None, :], c3, NEG_INF).reshape(N_EXPERTS, tm)
    exp_id = lax.broadcasted_iota(I32, cur.shape, 0)
    sel = jnp.zeros(cur.shape, F32)
    e_rows, s_rows = [], []
    for _ in range(TOPK_EXPERTS):
        _, ei = _first_argmax(cur, exp_id, N_EXPERTS, 0)
        hit = exp_id == ei
        e_rows.append(ei)
        s_rows.append(jnp.sum(jnp.where(hit, scores, 0.0), axis=0, keepdims=True))
        sel = jnp.where(hit, 1.0, sel)
        cur = jnp.where(hit, NEG_INF, cur)
    e_top = jnp.concatenate(e_rows, axis=0)
    s_top = jnp.concatenate(s_rows, axis=0)
    w_ref[...] = s_top / jnp.sum(s_top, axis=0, keepdims=True) * ROUTED_SCALE
    e_ref[...] = e_top

    t_row = lax.broadcasted_iota(I32, (tm, tm), 0)
    t_col = lax.broadcasted_iota(I32, (tm, tm), 1)
    before = (t_row < t_col).astype(BF16)
    cum = jnp.dot(sel.astype(BF16), before, preferred_element_type=F32) + carry_ref[...]
    r_rows = [jnp.sum(jnp.where(exp_id == e_rows[k], cum, 0.0), axis=0, keepdims=True)
              for k in range(TOPK_EXPERTS)]
    r_ref[...] = jnp.concatenate(r_rows, axis=0).astype(I32)
    carry_ref[...] = carry_ref[...] + jnp.sum(sel, axis=1, keepdims=True)
    cnt_ref[...] = jnp.broadcast_to(carry_ref[...], cnt_ref.shape).astype(I32)


def _router(x, router_w, router_b):
    n, d = x.shape
    tm = min(TOKEN_ROWS, n)
    rw = jnp.pad(router_w, ((0, 0), (0, 128 - N_EXPERTS)))
    k = TOPK_EXPERTS
    tok_spec = pl.BlockSpec((k, tm), lambda i: (0, i))
    return pl.pallas_call(
        _router_kernel,
        grid=(n // tm,),
        in_specs=[pl.BlockSpec((tm, d), lambda i: (i, 0)), _full((d, 128)), _full((N_EXPERTS, 1))],
        out_specs=[tok_spec, tok_spec, tok_spec, _full((N_EXPERTS, 128))],
        out_shape=[jax.ShapeDtypeStruct((k, n), I32), jax.ShapeDtypeStruct((k, n), F32),
                   jax.ShapeDtypeStruct((k, n), I32), jax.ShapeDtypeStruct((N_EXPERTS, 128), I32)],
        scratch_shapes=[pltpu.VMEM((N_EXPERTS, 1), F32)],
        compiler_params=_params("arbitrary"),
        name="moe_router",
    )(x, rw, router_b.reshape(N_EXPERTS, 1).astype(F32))


def _expert_kernel(blk_e_ref, nblk_ref, xs_ref, wg_ref, wu_ref, wd_ref, ys_ref):
    del blk_e_ref

    @pl.when(pl.program_id(0) < nblk_ref[0])
    def _():
        xs = xs_ref[...].astype(BF16)
        g = jnp.dot(xs, wg_ref[0], preferred_element_type=F32)
        u = jnp.dot(xs, wu_ref[0], preferred_element_type=F32)
        hdn = (_silu(g) * u).astype(BF16)
        ys_ref[...] = jnp.dot(hdn, wd_ref[0], preferred_element_type=F32)


def _experts(xs, blk_e, nblk, w_gate, w_up, w_down):
    rows, d = xs.shape
    m = EXPERT_ROWS
    row_map = lambda i, be, nb: (jnp.minimum(i, nb[0] - 1), 0)
    return pl.pallas_call(
        _expert_kernel,
        grid_spec=pltpu.PrefetchScalarGridSpec(
            num_scalar_prefetch=2,
            grid=(rows // m,),
            in_specs=[pl.BlockSpec((m, d), row_map),
                      pl.BlockSpec((1, d, D_EXPERT), lambda i, be, nb: (be[i], 0, 0)),
                      pl.BlockSpec((1, d, D_EXPERT), lambda i, be, nb: (be[i], 0, 0)),
                      pl.BlockSpec((1, D_EXPERT, d), lambda i, be, nb: (be[i], 0, 0))],
            out_specs=pl.BlockSpec((m, d), row_map)),
        out_shape=jax.ShapeDtypeStruct((rows, d), F32),
        compiler_params=_params("arbitrary"),
        name="moe_experts",
    )(blk_e, nblk, xs, w_gate, w_up, w_down)


def _combine_kernel(x_ref, yg_ref, w_ref, sg_ref, su_ref, sd_ref, lng_ref, lnb_ref,
                    p_ref, pw_ref, pg_ref, o_ref):
    x = x_ref[...]
    wt = w_ref[...].T
    moe = wt[:, 0:1] * yg_ref[0]
    for k in range(1, TOPK_EXPERTS):
        moe = moe + wt[:, k:k + 1] * yg_ref[k]
    xb = x.astype(BF16)
    hdn = _silu(jnp.dot(xb, sg_ref[...], preferred_element_type=F32)) * \
        jnp.dot(xb, su_ref[...], preferred_element_type=F32)
    shared = jnp.dot(hdn.astype(BF16), sd_ref[...], preferred_element_type=F32)
    x2 = _layer_norm(DN_ALPHA * x + (moe + shared), lng_ref[...], lnb_ref[...])
    emb = jnp.dot(p_ref[...].astype(BF16), pw_ref[...], preferred_element_type=F32)
    gate = jax.nn.sigmoid(jnp.dot(x2.astype(BF16), pg_ref[...], preferred_element_type=F32))
    o_ref[...] = x2 + emb * gate


def _combine(x, yg, wgt, s_gate, s_up, s_down, ln_g, ln_b, p, ple_w, ple_gate_w):
    n, d = x.shape
    tm = min(TOKEN_ROWS // 2, n)
    k = TOPK_EXPERTS
    row = lambda v: v.reshape(1, -1).astype(F32)
    return pl.pallas_call(
        _combine_kernel,
        grid=(n // tm,),
        in_specs=[pl.BlockSpec((tm, d), lambda i: (i, 0)),
                  pl.BlockSpec((k, tm, d), lambda i: (0, i, 0)),
                  pl.BlockSpec((k, tm), lambda i: (0, i)),
                  _full((d, D_SHARED)), _full((d, D_SHARED)), _full((D_SHARED, d)),
                  _full((1, d)), _full((1, d)),
                  pl.BlockSpec((tm, PLE_DIM), lambda i: (i, 0)),
                  _full((PLE_DIM, d)), _full((d, d))],
        out_specs=pl.BlockSpec((tm, d), lambda i: (i, 0)),
        out_shape=jax.ShapeDtypeStruct((n, d), F32),
        compiler_params=_params("parallel"),
        name="moe_combine",
    )(x, yg, wgt, s_gate.astype(BF16), s_up.astype(BF16), s_down.astype(BF16), row(ln_g), row(ln_b),
      p, ple_w.astype(BF16), ple_gate_w.astype(BF16))


def _moe_layer(x, router_w, router_b, w_gate, w_up, w_down, s_gate, s_up, s_down,
               ln_g, ln_b, p, ple_w, ple_gate_w):
    n, d = x.shape
    m = EXPERT_ROWS
    k = TOPK_EXPERTS
    e_idx, wgt, rank, cnt = _router(x, router_w, router_b)
    counts = cnt[:, 0]
    padded = (counts + m - 1) // m * m
    pends = jnp.cumsum(padded)
    pstarts = pends - padded
    pos = pstarts[e_idx] + rank
    n_blocks = (n * k) // m + N_EXPERTS
    blk_e = jnp.minimum(jnp.searchsorted(pends, jnp.arange(n_blocks, dtype=I32) * m, side="right"),
                        N_EXPERTS - 1).astype(I32)
    nblk = (pends[-1] // m).astype(I32).reshape(1)
    tok = jnp.broadcast_to(jnp.arange(n, dtype=I32)[None, :], (k, n))
    row_tok = jnp.zeros((n_blocks * m,), I32).at[pos.reshape(-1)].set(tok.reshape(-1))
    xs = jnp.take(x, row_tok, axis=0)
    ys = _experts(xs, blk_e, nblk, w_gate.astype(BF16), w_up.astype(BF16), w_down.astype(BF16))
    yg = jnp.take(ys, pos.reshape(-1), axis=0).reshape(k, n, d)
    return _combine(x, yg, wgt, s_gate, s_up, s_down, ln_g, ln_b, p, ple_w, ple_gate_w)


def _rope(t, cos_t, sin_t):
    half = ROT_DIM // 2
    width = t.shape[1]
    lane = lax.broadcasted_iota(I32, (1, width), 1) % HEAD_DIM
    partner = jnp.where(lane < half, pltpu.roll(t, width - half, 1), pltpu.roll(t, half, 1))
    cos_f = jnp.concatenate([cos_t] * N_HEADS, axis=1)
    sin_f = jnp.concatenate([sin_t] * N_HEADS, axis=1)
    return t * cos_f + partner * sin_f


def _kv_kernel(x_ref, g_ref, b_ref, wkv_ref, cos_ref, sin_ref, k_ref, vt_ref, km_ref):
    h = _layer_norm(x_ref[...], g_ref[...], b_ref[...])
    kv = jnp.dot(h.astype(BF16), wkv_ref[...], preferred_element_type=F32)
    k = _rope(kv[:, :D_MODEL], cos_ref[...], sin_ref[...])
    vt = kv[:, D_MODEL:].T
    km_ref[0] = jnp.mean(k, axis=0, keepdims=True)
    for hd in range(N_HEADS):
        k_ref[hd, 0] = k[:, hd * HEAD_DIM:(hd + 1) * HEAD_DIM].astype(BF16)
        vt_ref[hd, 0] = vt[hd * HEAD_DIM:(hd + 1) * HEAD_DIM, :].astype(BF16)


def _shared_kv(x, seq, ln_g, ln_b, w_kv, cos_t, sin_t):
    n, d = x.shape
    blk = MOBA_BLOCK
    nbt = n // blk
    spb = seq // blk
    row = lambda v: v.reshape(1, -1).astype(F32)
    return pl.pallas_call(
        _kv_kernel,
        grid=(nbt,),
        in_specs=[pl.BlockSpec((blk, d), lambda i: (i, 0)), _full((1, d)), _full((1, d)),
                  _full((d, 2 * d)),
                  pl.BlockSpec((blk, HEAD_DIM), lambda i: (i % spb, 0)),
                  pl.BlockSpec((blk, HEAD_DIM), lambda i: (i % spb, 0))],
        out_specs=[pl.BlockSpec((N_HEADS, 1, blk, HEAD_DIM), lambda i: (0, i, 0, 0)),
                   pl.BlockSpec((N_HEADS, 1, HEAD_DIM, blk), lambda i: (0, i, 0, 0)),
                   pl.BlockSpec((1, 1, d), lambda i: (i, 0, 0))],
        out_shape=[jax.ShapeDtypeStruct((N_HEADS, nbt, blk, HEAD_DIM), BF16),
                   jax.ShapeDtypeStruct((N_HEADS, nbt, HEAD_DIM, blk), BF16),
                   jax.ShapeDtypeStruct((nbt, 1, d), F32)],
        compiler_params=_params("parallel"),
        name="shared_kv",
    )(x, row(ln_g), row(ln_b), w_kv.astype(BF16), cos_t, sin_t)


def _query_kernel(x_ref, wq_ref, cos_ref, sin_ref, km_ref, qt_ref, sel_ref, *, tiles_per_seq):
    i = pl.program_id(0)
    tm = x_ref.shape[0]
    nb = km_ref.shape[0]
    q = jnp.dot(x_ref[...].astype(BF16), wq_ref[...], preferred_element_type=F32)
    q = _rope(q, cos_ref[...], sin_ref[...]) * (HEAD_DIM ** -0.5)
    qt = q.T
    qt_ref[...] = qt.astype(BF16)

    seq_pos = (i % tiles_per_seq) * tm + lax.broadcasted_iota(I32, (1, tm), 1)
    own = seq_pos // MOBA_BLOCK
    blk_id = lax.broadcasted_iota(I32, (nb, tm), 0)
    km = km_ref[...]
    for hd in range(N_HEADS):
        lo, hi = hd * HEAD_DIM, (hd + 1) * HEAD_DIM
        gate = jnp.dot(km[:, lo:hi], qt[lo:hi, :], preferred_element_type=F32,
                       precision=lax.Precision.HIGHEST)
        cur = jnp.where(blk_id < own, gate, NEG_INF)
        rows = []
        for _ in range(MOBA_TOPK):
            m, bi = _first_argmax(cur, blk_id, nb, 0)
            rows.append(jnp.where(m > NEG_INF, bi, -1))
            cur = jnp.where(blk_id == bi, NEG_INF, cur)
        rows.append(jnp.full((8 - MOBA_TOPK, tm), -1, I32))
        sel_ref[hd * 8:(hd + 1) * 8, :] = jnp.concatenate(rows, axis=0)


def _queries(x, seq, w_q, cos_t, sin_t, k_mean):
    n, d = x.shape
    tm = min(TOKEN_ROWS // 2, seq)
    tps = seq // tm
    nb = seq // MOBA_BLOCK
    return pl.pallas_call(
        functools.partial(_query_kernel, tiles_per_seq=tps),
        grid=(n // tm,),
        in_specs=[pl.BlockSpec((tm, d), lambda i: (i, 0)), _full((d, d)),
                  pl.BlockSpec((tm, HEAD_DIM), lambda i: (i % tps, 0)),
                  pl.BlockSpec((tm, HEAD_DIM), lambda i: (i % tps, 0)),
                  pl.BlockSpec((nb, d), lambda i: (i // tps, 0))],
        out_specs=[pl.BlockSpec((d, tm), lambda i: (0, i)),
                   pl.BlockSpec((N_HEADS * 8, tm), lambda i: (0, i))],
        out_shape=[jax.ShapeDtypeStruct((d, n), BF16), jax.ShapeDtypeStruct((N_HEADS * 8, n), I32)],
        compiler_params=_params("parallel"),
        name="moba_queries",
    )(x, w_q.astype(BF16), cos_t, sin_t, k_mean)


def _attn_kernel(qt_ref, sel_ref, k_ref, vt_ref, o_ref):
    own = pl.program_id(2)
    tq = qt_ref.shape[1]
    qt = qt_ref[...]
    sel = sel_ref[...]

    s = jnp.dot(k_ref[0, own], qt, preferred_element_type=F32)
    kpos = lax.broadcasted_iota(I32, (MOBA_BLOCK, tq), 0)
    qpos = lax.broadcasted_iota(I32, (MOBA_BLOCK, tq), 1)
    s = jnp.where(kpos <= qpos, s, NEG_INF)
    m = jnp.max(s, axis=0, keepdims=True)
    p = jnp.exp(s - m)
    l = jnp.sum(p, axis=0, keepdims=True)
    acc = jnp.dot(vt_ref[0, own], p.astype(BF16), preferred_element_type=F32)

    def body(j, carry):
        m, l, acc = carry
        chosen = (sel[0:1] == j) | (sel[1:2] == j) | (sel[2:3] == j)
        s = jnp.where(chosen, jnp.dot(k_ref[0, j], qt, preferred_element_type=F32), NEG_INF)
        m_new = jnp.maximum(m, jnp.max(s, axis=0, keepdims=True))
        alpha = jnp.exp(m - m_new)
        p = jnp.exp(s - m_new)
        l = alpha * l + jnp.sum(p, axis=0, keepdims=True)
        acc = alpha * acc + jnp.dot(vt_ref[0, j], p.astype(BF16), preferred_element_type=F32)
        return m_new, l, acc

    m, l, acc = lax.fori_loop(0, own, body, (m, l, acc))
    o_ref[...] = (acc / l).astype(BF16)


def _attention(qt, sel, k_blk, vt_blk, batch, seq):
    d, n = qt.shape
    nb = seq // MOBA_BLOCK
    tq = MOBA_BLOCK
    return pl.pallas_call(
        _attn_kernel,
        grid=(batch, N_HEADS, nb),
        in_specs=[pl.BlockSpec((HEAD_DIM, tq), lambda b, h, i: (h, b * nb + i)),
                  pl.BlockSpec((8, tq), lambda b, h, i: (h, b * nb + i)),
                  pl.BlockSpec((1, nb, MOBA_BLOCK, HEAD_DIM), lambda b, h, i: (h, b, 0, 0)),
                  pl.BlockSpec((1, nb, HEAD_DIM, MOBA_BLOCK), lambda b, h, i: (h, b, 0, 0))],
        out_specs=pl.BlockSpec((HEAD_DIM, tq), lambda b, h, i: (h, b * nb + i)),
        out_shape=jax.ShapeDtypeStruct((d, n), BF16),
        compiler_params=_params("parallel", "parallel", "arbitrary"),
        name="moba_attention",
    )(qt, sel, k_blk, vt_blk)


def _attn_out_kernel(x_ref, ot_ref, wo_ref, lng_ref, lnb_ref, o_ref):
    att = ot_ref[...].astype(F32).T.astype(BF16)
    mix = jnp.dot(att, wo_ref[...], preferred_element_type=F32)
    o_ref[...] = _layer_norm(DN_ALPHA * x_ref[...] + mix, lng_ref[...], lnb_ref[...])


def _attn_out(x, ot, w_o, ln_g, ln_b):
    n, d = x.shape
    tm = min(TOKEN_ROWS, n)
    row = lambda v: v.reshape(1, -1).astype(F32)
    return pl.pallas_call(
        _attn_out_kernel,
        grid=(n // tm,),
        in_specs=[pl.BlockSpec((tm, d), lambda i: (i, 0)), pl.BlockSpec((d, tm), lambda i: (0, i)),
                  _full((d, d)), _full((1, d)), _full((1, d))],
        out_specs=pl.BlockSpec((tm, d), lambda i: (i, 0)),
        out_shape=jax.ShapeDtypeStruct((n, d), F32),
        compiler_params=_params("parallel"),
        name="moba_out",
    )(x, ot, w_o.astype(BF16), row(ln_g), row(ln_b))


def _rope_tables(seq):
    half = ROT_DIM // 2
    inv = ROPE_THETA ** (-jnp.arange(0, ROT_DIM, 2, dtype=F32) / ROT_DIM)
    ang = jnp.arange(seq, dtype=F32)[:, None] * inv[None, :]
    cos, sin = jnp.cos(ang), jnp.sin(ang)
    rest = HEAD_DIM - ROT_DIM
    cos_t = jnp.concatenate([cos, cos, jnp.ones((seq, rest), F32)], axis=1)
    sin_t = jnp.concatenate([-sin, sin, jnp.zeros((seq, rest), F32)], axis=1)
    del half
    return cos_t, sin_t


def kernel(x, p, ln_g, ln_b, a_w_in, a_b_in, a_conv_w, a_conv_b, a_gate_a_w, a_gate_a_b, a_gate_i_w,
           a_gate_i_b, a_lambda, a_w_out, kv_ln_g, kv_ln_b, w_kv, b_w_q, b_w_o, router_w, router_b,
           exp_w_gate, exp_w_up, exp_w_down, sh_w_gate, sh_w_up, sh_w_down, ple_w, ple_gate_w):
    batch, seq, d = x.shape
    n = batch * seq
    xf = x.reshape(n, d)
    pf = p.reshape(DEPTH, n, PLE_DIM)
    cos_t, sin_t = _rope_tables(seq)
    k_blk = vt_blk = k_mean = None
    for i in range(DEPTH):
        if i < N_A_LAYERS:
            x1 = _rglru_layer(xf, seq, a_w_in[i], a_b_in[i], a_conv_w[i], a_conv_b[i], a_gate_a_w[i],
                              a_gate_a_b[i], a_gate_i_w[i], a_gate_i_b[i], a_lambda[i], a_w_out[i],
                              ln_g[i, 0], ln_b[i, 0])
        else:
            if i == N_A_LAYERS:
                k_blk, vt_blk, k_mean = _shared_kv(xf, seq, kv_ln_g, kv_ln_b, w_kv, cos_t, sin_t)
                k_mean = k_mean.reshape(n // MOBA_BLOCK, d)
            j = i - N_A_LAYERS
            qt, sel = _queries(xf, seq, b_w_q[j], cos_t, sin_t, k_mean)
            ot = _attention(qt, sel, k_blk, vt_blk, batch, seq)
            x1 = _attn_out(xf, ot, b_w_o[j], ln_g[i, 0], ln_b[i, 0])
        xf = _moe_layer(x1, router_w[i], router_b[i], exp_w_gate[i], exp_w_up[i], exp_w_down[i],
                        sh_w_gate[i], sh_w_up[i], sh_w_down[i], ln_g[i, 1], ln_b[i, 1],
                        pf[i], ple_w[i], ple_gate_w[i])
    return xf.reshape(batch, seq, d)
```

```python
import functools

import jax
import jax.numpy as jnp
from jax import lax
from jax.experimental import pallas as pl
from jax.experimental.pallas import tpu as pltpu
from jax.experimental.pallas import tpu_sc as plsc

F32 = jnp.float32
BF16 = jnp.bfloat16
I32 = jnp.int32
U32 = jnp.uint32
HIGH_HALF = 0xFFFF0000

SC_CORES = 2
SC_SUBCORES = 16
SC_WORKERS = SC_CORES * SC_SUBCORES
SC_INDEX_CHUNK = 128

D_MODEL = 1024
DEPTH = 4
N_A_LAYERS = DEPTH // 2
D_RNN = D_MODEL
LRU_BLOCKS = 4
LRU_BLOCK_W = D_RNN // LRU_BLOCKS
CONV_W = 4
LRU_C = 8.0
N_HEADS = 8
HEAD_DIM = D_MODEL // N_HEADS
ROT_DIM = HEAD_DIM // 4
ROPE_THETA = 500000.0
MOBA_BLOCK = 256
MOBA_TOPK = 3
N_EXPERTS = 64
N_GROUPS = 8
GROUP_SIZE = N_EXPERTS // N_GROUPS
TOPK_GROUPS = 4
TOPK_EXPERTS = 8
D_EXPERT = 256
D_SHARED = 256
ROUTED_SCALE = 2.5
PLE_DIM = 256
DN_ALPHA = (2 * DEPTH) ** 0.25
LN_EPS = 1e-5

V7X_VMEM_LIMIT_BYTES = 56 * 1024 * 1024

MIXER_ROWS = 256
TOKEN_ROWS = 512
EXPERT_ROWS = 256
NEG_INF = float("-inf")


def _params(*sem):
    return pltpu.CompilerParams(dimension_semantics=sem, vmem_limit_bytes=V7X_VMEM_LIMIT_BYTES)


def _layer_norm(z, g, b):
    mu = jnp.mean(z, axis=-1, keepdims=True)
    zc = z - mu
    var = jnp.mean(zc * zc, axis=-1, keepdims=True)
    return zc * lax.rsqrt(var + LN_EPS) * g + b


def _silu(x):
    return x * jax.nn.sigmoid(x)


def _gelu_tanh(x):
    return 0.5 * x * (1.0 + jnp.tanh(0.7978845608028654 * (x + 0.044715 * (x * x * x))))


def _full(shape):
    return pl.BlockSpec(shape, lambda *_: (0,) * len(shape))


def _pack_bf16_pairs(x):
    w = x.shape[1] // 2
    bits = lax.bitcast_convert_type(x.astype(BF16).astype(F32), U32)
    return (bits[:, :w] >> 16) | (bits[:, w:] & U32(HIGH_HALF))


def _unpack_bf16_pairs(u):
    lo = lax.bitcast_convert_type(u << 16, F32)
    hi = lax.bitcast_convert_type(u & U32(HIGH_HALF), F32)
    return lo, hi


def _sc_mesh():
    return plsc.VectorSubcoreMesh(core_axis_name="c", subcore_axis_name="s")


def _sc_worker_id():
    return lax.axis_index("s") * SC_CORES + lax.axis_index("c")


def _sc_gather_rows(table, idx):
    b = idx.shape[0]
    w = table.shape[1]
    chunk = SC_INDEX_CHUNK
    chunks_per_worker = b // (SC_WORKERS * chunk)
    assert chunks_per_worker * SC_WORKERS * chunk == b

    @functools.partial(
        pl.kernel, mesh=_sc_mesh(), out_type=jax.ShapeDtypeStruct((b, w), table.dtype),
        scratch_types=[pltpu.VMEM((chunk,), I32), pltpu.VMEM((chunk, w), table.dtype),
                       pltpu.SemaphoreType.DMA])
    def gather(table_hbm, idx_hbm, out_hbm, idx_v, rows_v, sem):
        first = _sc_worker_id() * chunks_per_worker

        @pl.loop(0, chunks_per_worker)
        def _(j):
            base = (first + j) * chunk
            pltpu.sync_copy(idx_hbm.at[pl.ds(base, chunk)], idx_v)
            pltpu.async_copy(table_hbm.at[idx_v], rows_v, sem).wait()
            pltpu.sync_copy(rows_v, out_hbm.at[pl.ds(base, chunk)])

    return gather(table, idx)


def _sc_scatter_rows(src, idx, out_rows):
    n, w = src.shape
    chunk = SC_INDEX_CHUNK
    fan = idx.shape[1]
    chunks_per_worker = n // (SC_WORKERS * chunk)
    assert idx.shape == (n // chunk, fan, chunk) and chunks_per_worker * SC_WORKERS * chunk == n

    @functools.partial(
        pl.kernel, mesh=_sc_mesh(), out_type=jax.ShapeDtypeStruct((out_rows, w), src.dtype),
        scratch_types=[pltpu.VMEM((fan, chunk), I32), pltpu.VMEM((chunk, w), src.dtype),
                       pltpu.SemaphoreType.DMA])
    def scatter(src_hbm, idx_hbm, out_hbm, idx_v, rows_v, sem):
        first = _sc_worker_id() * chunks_per_worker

        @pl.loop(0, chunks_per_worker)
        def _(j):
            c = first + j
            pltpu.sync_copy(src_hbm.at[pl.ds(c * chunk, chunk)], rows_v)
            pltpu.sync_copy(idx_hbm.at[c], idx_v)
            copies = [pltpu.async_copy(rows_v, out_hbm.at[idx_v.at[q]], sem) for q in range(fan)]
            for cp in copies:
                cp.wait()

    return scatter(src, idx)


def _rglru_kernel(x_ref, win_ref, bin_ref, cw_ref, cb_ref, gaw_ref, gab_ref, giw_ref, gib_ref,
                  lam_ref, wout_ref, lng_ref, lnb_ref, o_ref, op_ref, tail_ref, h_ref, *, tiles_per_seq):
    i = pl.program_id(0)
    tm = x_ref.shape[0]

    @pl.when(i % tiles_per_seq == 0)
    def _():
        tail_ref[...] = jnp.zeros_like(tail_ref)
        h_ref[...] = jnp.zeros_like(h_ref)

    x = x_ref[...]
    xy = jnp.dot(x.astype(BF16), win_ref[...], preferred_element_type=F32) + bin_ref[...]
    xb = xy[:, :D_RNN]
    y = _gelu_tanh(xy[:, D_RNN:])

    tail = tail_ref[...]
    row8 = lax.broadcasted_iota(I32, (8, 1), 0)
    xc = cb_ref[...] + xb * cw_ref[CONV_W - 1:CONV_W, :]
    for d in range(1, CONV_W):
        rolled = pltpu.roll(xb, d, 0)
        head = jnp.where(row8 < d, pltpu.roll(tail, d, 0), rolled[:8])
        shifted = jnp.concatenate([head, rolled[8:]], axis=0)
        xc = xc + shifted * cw_ref[CONV_W - 1 - d:CONV_W - d, :]
    tail_ref[...] = xb[tm - 8:, :]

    r_parts, i_parts = [], []
    for n in range(LRU_BLOCKS):
        xg = xc[:, n * LRU_BLOCK_W:(n + 1) * LRU_BLOCK_W].astype(BF16)
        r_parts.append(jnp.dot(xg, gaw_ref[n], preferred_element_type=F32))
        i_parts.append(jnp.dot(xg, giw_ref[n], preferred_element_type=F32))
    r = jax.nn.sigmoid(jnp.concatenate(r_parts, axis=1) + gab_ref[...])
    ig = jax.nn.sigmoid(jnp.concatenate(i_parts, axis=1) + gib_ref[...])

    lam = lam_ref[...]
    softplus_neg_lam = jnp.maximum(-lam, 0.0) + jnp.log1p(jnp.exp(-jnp.abs(lam)))
    log_a = (-LRU_C * r) * softplus_neg_lam
    a = jnp.exp(log_a)
    u = jnp.sqrt(1.0 - a * a) * (ig * xc)

    row = lax.broadcasted_iota(I32, (tm, 1), 0)
    acc_a, acc_h = a, u
    d = 1
    while d < tm:
        keep = row >= d
        sh_a = pltpu.roll(acc_a, d, 0)
        sh_h = pltpu.roll(acc_h, d, 0)
        acc_h = jnp.where(keep, acc_a * sh_h + acc_h, acc_h)
        acc_a = jnp.where(keep, acc_a * sh_a, acc_a)
        d *= 2
    h = acc_h + acc_a * h_ref[...]
    h_ref[...] = h[tm - 1:, :]

    mix = jnp.dot((h * y).astype(BF16), wout_ref[...], preferred_element_type=F32)
    x1 = _layer_norm(DN_ALPHA * x + mix, lng_ref[...], lnb_ref[...])
    o_ref[...] = x1
    op_ref[...] = _pack_bf16_pairs(x1)


def _rglru_layer(x, seq, w_in, b_in, conv_w, conv_b, ga_w, ga_b, gi_w, gi_b, lam, w_out, ln_g, ln_b):
    n, d = x.shape
    tm = min(MIXER_ROWS, seq)
    row = lambda v: v.reshape(1, -1).astype(F32)
    return pl.pallas_call(
        functools.partial(_rglru_kernel, tiles_per_seq=seq // tm),
        grid=(n // tm,),
        in_specs=[pl.BlockSpec((tm, d), lambda i: (i, 0)),
                  _full((d, 2 * D_RNN)), _full((1, 2 * D_RNN)),
                  _full((CONV_W, D_RNN)), _full((1, D_RNN)),
                  _full((LRU_BLOCKS, LRU_BLOCK_W, LRU_BLOCK_W)), _full((1, D_RNN)),
                  _full((LRU_BLOCKS, LRU_BLOCK_W, LRU_BLOCK_W)), _full((1, D_RNN)),
                  _full((1, D_RNN)), _full((D_RNN, d)), _full((1, d)), _full((1, d))],
        out_specs=[pl.BlockSpec((tm, d), lambda i: (i, 0)), pl.BlockSpec((tm, d // 2), lambda i: (i, 0))],
        out_shape=[jax.ShapeDtypeStruct((n, d), F32), jax.ShapeDtypeStruct((n, d // 2), U32)],
        scratch_shapes=[pltpu.VMEM((8, D_RNN), F32), pltpu.VMEM((1, D_RNN), F32)],
        compiler_params=_params("arbitrary"),
        name="rglru_mixer",
    )(x, w_in.astype(BF16), row(b_in), conv_w, row(conv_b), ga_w.astype(BF16), row(ga_b),
      gi_w.astype(BF16), row(gi_b), row(lam), w_out.astype(BF16), row(ln_g), row(ln_b))


def _first_argmax(cur, idx, size, axis):
    m = jnp.max(cur, axis=axis, keepdims=True)
    first = jnp.min(jnp.where(cur == m, idx, size), axis=axis, keepdims=True)
    return m, first


def _router_kernel(x_ref, rw_ref, rb_ref, e_ref, w_ref, r_ref, cnt_ref, carry_ref):
    i = pl.program_id(0)
    tm = x_ref.shape[0]

    @pl.when(i == 0)
    def _():
        carry_ref[...] = jnp.zeros_like(carry_ref)

    logits = jnp.dot(x_ref[...], rw_ref[...], preferred_element_type=F32,
                     precision=lax.Precision.HIGHEST)
    scores = jax.nn.sigmoid(logits.T[:N_EXPERTS, :])
    choice = scores + rb_ref[...]

    c3 = choice.reshape(N_GROUPS, GROUP_SIZE, tm)
    in_grp = lax.broadcasted_iota(I32, c3.shape, 1)
    m1, i1 = _first_argmax(c3, in_grp, GROUP_SIZE, 1)
    m2 = jnp.max(jnp.where(in_grp == i1, NEG_INF, c3), axis=1, keepdims=True)
    grp_score = (m1 + m2)[:, 0, :]

    grp_id = lax.broadcasted_iota(I32, grp_score.shape, 0)
    grp_sel = jnp.zeros(grp_score.shape, jnp.bool_)
    cur = grp_score
    for _ in range(TOPK_GROUPS):
        _, gi = _first_argmax(cur, grp_id, N_GROUPS, 0)
        hit = grp_id == gi
        grp_sel = jnp.logical_or(grp_sel, hit)
        cur = jnp.where(hit, NEG_INF, cur)

    cur = jnp.where(grp_sel[:, None, :], c3, NEG_INF).reshape(N_EXPERTS, tm)
    exp_id = lax.broadcasted_iota(I32, cur.shape, 0)
    sel = jnp.zeros(cur.shape, F32)
    e_rows, s_rows = [], []
    for _ in range(TOPK_EXPERTS):
        _, ei = _first_argmax(cur, exp_id, N_EXPERTS, 0)
        hit = exp_id == ei
        e_rows.append(ei)
        s_rows.append(jnp.sum(jnp.where(hit, scores, 0.0), axis=0, keepdims=True))
        sel = jnp.where(hit, 1.0, sel)
        cur = jnp.where(hit, NEG_INF, cur)
    e_top = jnp.concatenate(e_rows, axis=0)
    s_top = jnp.concatenate(s_rows, axis=0)
    w_ref[...] = s_top / jnp.sum(s_top, axis=0, keepdims=True) * ROUTED_SCALE
    e_ref[...] = e_top

    t_row = lax.broadcasted_iota(I32, (tm, tm), 0)
    t_col = lax.broadcasted_iota(I32, (tm, tm), 1)
    before = (t_row < t_col).astype(BF16)
    cum = jnp.dot(sel.astype(BF16), before, preferred_element_type=F32) + carry_ref[...]
    r_rows = [jnp.sum(jnp.where(exp_id == e_rows[k], cum, 0.0), axis=0, keepdims=True)
              for k in range(TOPK_EXPERTS)]
    r_ref[...] = jnp.concatenate(r_rows, axis=0).astype(I32)
    carry_ref[...] = carry_ref[...] + jnp.sum(sel, axis=1, keepdims=True)
    cnt_ref[...] = jnp.broadcast_to(carry_ref[...], cnt_ref.shape).astype(I32)


def _router(x, router_w, router_b):
    n, d = x.shape
    tm = min(TOKEN_ROWS, n)
    rw = jnp.pad(router_w, ((0, 0), (0, 128 - N_EXPERTS)))
    k = TOPK_EXPERTS
    tok_spec = pl.BlockSpec((k, tm), lambda i: (0, i))
    return pl.pallas_call(
        _router_kernel,
        grid=(n // tm,),
        in_specs=[pl.BlockSpec((tm, d), lambda i: (i, 0)), _full((d, 128)), _full((N_EXPERTS, 1))],
        out_specs=[tok_spec, tok_spec, tok_spec, _full((N_EXPERTS, 128))],
        out_shape=[jax.ShapeDtypeStruct((k, n), I32), jax.ShapeDtypeStruct((k, n), F32),
                   jax.ShapeDtypeStruct((k, n), I32), jax.ShapeDtypeStruct((N_EXPERTS, 128), I32)],
        scratch_shapes=[pltpu.VMEM((N_EXPERTS, 1), F32)],
        compiler_params=_params("arbitrary"),
        name="moe_router",
    )(x, rw, router_b.reshape(N_EXPERTS, 1).astype(F32))


def _expert_kernel(blk_e_ref, nblk_ref, xs_ref, wg_ref, wu_ref, wd_ref, ys_ref):
    del blk_e_ref

    @pl.when(pl.program_id(0) < nblk_ref[0])
    def _():
        lo, hi = _unpack_bf16_pairs(xs_ref[...])
        xs = jnp.concatenate([lo.astype(BF16), hi.astype(BF16)], axis=1)
        g = jnp.dot(xs, wg_ref[0], preferred_element_type=F32)
        u = jnp.dot(xs, wu_ref[0], preferred_element_type=F32)
        hdn = (_silu(g) * u).astype(BF16)
        ys_ref[...] = _pack_bf16_pairs(jnp.dot(hdn, wd_ref[0], preferred_element_type=F32))


def _experts(xs, blk_e, nblk, w_gate, w_up, w_down):
    rows, half = xs.shape
    d = 2 * half
    m = EXPERT_ROWS
    row_map = lambda i, be, nb: (jnp.minimum(i, nb[0] - 1), 0)
    return pl.pallas_call(
        _expert_kernel,
        grid_spec=pltpu.PrefetchScalarGridSpec(
            num_scalar_prefetch=2,
            grid=(rows // m,),
            in_specs=[pl.BlockSpec((m, half), row_map),
                      pl.BlockSpec((1, d, D_EXPERT), lambda i, be, nb: (be[i], 0, 0)),
                      pl.BlockSpec((1, d, D_EXPERT), lambda i, be, nb: (be[i], 0, 0)),
                      pl.BlockSpec((1, D_EXPERT, d), lambda i, be, nb: (be[i], 0, 0))],
            out_specs=pl.BlockSpec((m, half), row_map)),
        out_shape=jax.ShapeDtypeStruct((rows, half), U32),
        compiler_params=_params("arbitrary"),
        name="moe_experts",
    )(blk_e, nblk, xs, w_gate, w_up, w_down)


def _combine_kernel(x_ref, yg_ref, w_ref, sg_ref, su_ref, sd_ref, lng_ref, lnb_ref,
                    p_ref, pw_ref, pg_ref, o_ref):
    x = x_ref[...]
    wt = w_ref[...].T
    moe_lo, moe_hi = None, None
    for k in range(TOPK_EXPERTS):
        lo, hi = _unpack_bf16_pairs(yg_ref[k])
        wk = wt[:, k:k + 1]
        moe_lo = wk * lo if k == 0 else moe_lo + wk * lo
        moe_hi = wk * hi if k == 0 else moe_hi + wk * hi
    moe = jnp.concatenate([moe_lo, moe_hi], axis=1)
    xb = x.astype(BF16)
    hdn = _silu(jnp.dot(xb, sg_ref[...], preferred_element_type=F32)) * \
        jnp.dot(xb, su_ref[...], preferred_element_type=F32)
    shared = jnp.dot(hdn.astype(BF16), sd_ref[...], preferred_element_type=F32)
    x2 = _layer_norm(DN_ALPHA * x + (moe + shared), lng_ref[...], lnb_ref[...])
    emb = jnp.dot(p_ref[...].astype(BF16), pw_ref[...], preferred_element_type=F32)
    gate = jax.nn.sigmoid(jnp.dot(x2.astype(BF16), pg_ref[...], preferred_element_type=F32))
    o_ref[...] = x2 + emb * gate


def _combine(x, yg, wgt, s_gate, s_up, s_down, ln_g, ln_b, p, ple_w, ple_gate_w):
    n, d = x.shape
    tm = min(TOKEN_ROWS // 2, n)
    k = TOPK_EXPERTS
    row = lambda v: v.reshape(1, -1).astype(F32)
    return pl.pallas_call(
        _combine_kernel,
        grid=(n // tm,),
        in_specs=[pl.BlockSpec((tm, d), lambda i: (i, 0)),
                  pl.BlockSpec((k, tm, d // 2), lambda i: (0, i, 0)),
                  pl.BlockSpec((k, tm), lambda i: (0, i)),
                  _full((d, D_SHARED)), _full((d, D_SHARED)), _full((D_SHARED, d)),
                  _full((1, d)), _full((1, d)),
                  pl.BlockSpec((tm, PLE_DIM), lambda i: (i, 0)),
                  _full((PLE_DIM, d)), _full((d, d))],
        out_specs=pl.BlockSpec((tm, d), lambda i: (i, 0)),
        out_shape=jax.ShapeDtypeStruct((n, d), F32),
        compiler_params=_params("parallel"),
        name="moe_combine",
    )(x, yg, wgt, s_gate.astype(BF16), s_up.astype(BF16), s_down.astype(BF16), row(ln_g), row(ln_b),
      p, ple_w.astype(BF16), ple_gate_w.astype(BF16))


def _positions_kernel(e_ref, r_ref, start_ref, pos_ref):
    e = e_ref[...]
    start = start_ref[...]
    grp_id = lax.broadcasted_iota(I32, (start.shape[0], e.shape[1]), 0)
    rows = [jnp.sum(jnp.where(grp_id == e[k:k + 1], start, 0), axis=0, keepdims=True)
            for k in range(e.shape[0])]
    pos_ref[...] = jnp.concatenate(rows, axis=0) + r_ref[...]


def _positions(e_idx, rank, starts):
    k, n = e_idx.shape
    tm = min(4 * TOKEN_ROWS, n)
    groups = starts.shape[0]
    tok_spec = pl.BlockSpec((k, tm), lambda i: (0, i))
    return pl.pallas_call(
        _positions_kernel,
        grid=(n // tm,),
        in_specs=[tok_spec, tok_spec, _full((groups, 1))],
        out_specs=tok_spec,
        out_shape=jax.ShapeDtypeStruct((k, n), I32),
        compiler_params=_params("parallel"),
        name="group_positions",
    )(e_idx, rank, starts.reshape(groups, 1))


def _moe_layer(x, xp, router_w, router_b, w_gate, w_up, w_down, s_gate, s_up, s_down,
               ln_g, ln_b, p, ple_w, ple_gate_w):
    n, d = x.shape
    m = EXPERT_ROWS
    k = TOPK_EXPERTS
    e_idx, wgt, rank, cnt = _router(x, router_w, router_b)
    counts = cnt[:, 0]
    padded = (counts + m - 1) // m * m
    pends = jnp.cumsum(padded)
    n_blocks = (n * k) // m + N_EXPERTS
    blk_first = jnp.arange(n_blocks, dtype=I32) * m
    blk_e = jnp.minimum(jnp.sum((pends[None, :] <= blk_first[:, None]).astype(I32), axis=1), N_EXPERTS - 1)
    nblk = (pends[-1] // m).astype(I32).reshape(1)
    pos = _positions(e_idx, rank, (pends - padded).astype(I32))
    chunk = SC_INDEX_CHUNK
    pos_chunks = pos.reshape(k, n // chunk, chunk).transpose(1, 0, 2)
    xs = _sc_scatter_rows(xp, pos_chunks, n_blocks * m)
    ys = _experts(xs, blk_e, nblk, w_gate.astype(BF16), w_up.astype(BF16), w_down.astype(BF16))
    yg = _sc_gather_rows(ys, pos.reshape(-1)).reshape(k, n, d // 2)
    return _combine(x, yg, wgt, s_gate, s_up, s_down, ln_g, ln_b, p, ple_w, ple_gate_w)


def _rope(t, cos_t, sin_t):
    half = ROT_DIM // 2
    width = t.shape[1]
    lane = lax.broadcasted_iota(I32, (1, width), 1) % HEAD_DIM
    partner = jnp.where(lane < half, pltpu.roll(t, width - half, 1), pltpu.roll(t, half, 1))
    cos_f = jnp.concatenate([cos_t] * N_HEADS, axis=1)
    sin_f = jnp.concatenate([sin_t] * N_HEADS, axis=1)
    return t * cos_f + partner * sin_f


def _kv_kernel(x_ref, g_ref, b_ref, wkv_ref, cos_ref, sin_ref, k_ref, vt_ref, km_ref):
    h = _layer_norm(x_ref[...], g_ref[...], b_ref[...])
    kv = jnp.dot(h.astype(BF16), wkv_ref[...], preferred_element_type=F32)
    k = _rope(kv[:, :D_MODEL], cos_ref[...], sin_ref[...])
    vt = kv[:, D_MODEL:].T
    km_ref[0] = jnp.mean(k, axis=0, keepdims=True)
    for hd in range(N_HEADS):
        k_ref[hd, 0] = k[:, hd * HEAD_DIM:(hd + 1) * HEAD_DIM].astype(BF16)
        vt_ref[hd, 0] = vt[hd * HEAD_DIM:(hd + 1) * HEAD_DIM, :].astype(BF16)


def _shared_kv(x, seq, ln_g, ln_b, w_kv, cos_t, sin_t):
    n, d = x.shape
    blk = MOBA_BLOCK
    nbt = n // blk
    spb = seq // blk
    row = lambda v: v.reshape(1, -1).astype(F32)
    return pl.pallas_call(
        _kv_kernel,
        grid=(nbt,),
        in_specs=[pl.BlockSpec((blk, d), lambda i: (i, 0)), _full((1, d)), _full((1, d)),
                  _full((d, 2 * d)),
                  pl.BlockSpec((blk, HEAD_DIM), lambda i: (i % spb, 0)),
                  pl.BlockSpec((blk, HEAD_DIM), lambda i: (i % spb, 0))],
        out_specs=[pl.BlockSpec((N_HEADS, 1, blk, HEAD_DIM), lambda i: (0, i, 0, 0)),
                   pl.BlockSpec((N_HEADS, 1, HEAD_DIM, blk), lambda i: (0, i, 0, 0)),
                   pl.BlockSpec((1, 1, d), lambda i: (i, 0, 0))],
        out_shape=[jax.ShapeDtypeStruct((N_HEADS, nbt, blk, HEAD_DIM), BF16),
                   jax.ShapeDtypeStruct((N_HEADS, nbt, HEAD_DIM, blk), BF16),
                   jax.ShapeDtypeStruct((nbt, 1, d), F32)],
        compiler_params=_params("parallel"),
        name="shared_kv",
    )(x, row(ln_g), row(ln_b), w_kv.astype(BF16), cos_t, sin_t)


def _query_kernel(x_ref, wq_ref, cos_ref, sin_ref, km_ref, qt_ref, sel_ref, *, tiles_per_seq):
    i = pl.program_id(0)
    tm = x_ref.shape[0]
    nb = km_ref.shape[0]
    q = jnp.dot(x_ref[...].astype(BF16), wq_ref[...], preferred_element_type=F32)
    q = _rope(q, cos_ref[...], sin_ref[...]) * (HEAD_DIM ** -0.5)
    qt = q.T
    qt_ref[...] = qt.astype(BF16)

    seq_pos = (i % tiles_per_seq) * tm + lax.broadcasted_iota(I32, (1, tm), 1)
    own = seq_pos // MOBA_BLOCK
    blk_id = lax.broadcasted_iota(I32, (nb, tm), 0)
    km = km_ref[...]
    for hd in range(N_HEADS):
        lo, hi = hd * HEAD_DIM, (hd + 1) * HEAD_DIM
        gate = jnp.dot(km[:, lo:hi], qt[lo:hi, :], preferred_element_type=F32,
                       precision=lax.Precision.HIGHEST)
        cur = jnp.where(blk_id < own, gate, NEG_INF)
        rows = []
        for _ in range(MOBA_TOPK):
            m, bi = _first_argmax(cur, blk_id, nb, 0)
            rows.append(jnp.where(m > NEG_INF, bi, -1))
            cur = jnp.where(blk_id == bi, NEG_INF, cur)
        rows.append(jnp.full((8 - MOBA_TOPK, tm), -1, I32))
        sel_ref[hd * 8:(hd + 1) * 8, :] = jnp.concatenate(rows, axis=0)


def _queries(x, seq, w_q, cos_t, sin_t, k_mean):
    n, d = x.shape
    tm = min(TOKEN_ROWS // 2, seq)
    tps = seq // tm
    nb = seq // MOBA_BLOCK
    return pl.pallas_call(
        functools.partial(_query_kernel, tiles_per_seq=tps),
        grid=(n // tm,),
        in_specs=[pl.BlockSpec((tm, d), lambda i: (i, 0)), _full((d, d)),
                  pl.BlockSpec((tm, HEAD_DIM), lambda i: (i % tps, 0)),
                  pl.BlockSpec((tm, HEAD_DIM), lambda i: (i % tps, 0)),
                  pl.BlockSpec((nb, d), lambda i: (i // tps, 0))],
        out_specs=[pl.BlockSpec((d, tm), lambda i: (0, i)),
                   pl.BlockSpec((N_HEADS * 8, tm), lambda i: (0, i))],
        out_shape=[jax.ShapeDtypeStruct((d, n), BF16), jax.ShapeDtypeStruct((N_HEADS * 8, n), I32)],
        compiler_params=_params("parallel"),
        name="moba_queries",
    )(x, w_q.astype(BF16), cos_t, sin_t, k_mean)


def _attn_kernel(qt_ref, sel_ref, k_ref, vt_ref, o_ref):
    own = pl.program_id(2)
    tq = qt_ref.shape[1]
    qt = qt_ref[...]
    sel = sel_ref[...]

    s = jnp.dot(k_ref[0, own], qt, preferred_element_type=F32)
    kpos = lax.broadcasted_iota(I32, (MOBA_BLOCK, tq), 0)
    qpos = lax.broadcasted_iota(I32, (MOBA_BLOCK, tq), 1)
    s = jnp.where(kpos <= qpos, s, NEG_INF)
    m = jnp.max(s, axis=0, keepdims=True)
    p = jnp.exp(s - m)
    l = jnp.sum(p, axis=0, keepdims=True)
    acc = jnp.dot(vt_ref[0, own], p.astype(BF16), preferred_element_type=F32)

    def body(j, carry):
        m, l, acc = carry
        chosen = (sel[0:1] == j) | (sel[1:2] == j) | (sel[2:3] == j)
        s = jnp.where(chosen, jnp.dot(k_ref[0, j], qt, preferred_element_type=F32), NEG_INF)
        m_new = jnp.maximum(m, jnp.max(s, axis=0, keepdims=True))
        alpha = jnp.exp(m - m_new)
        p = jnp.exp(s - m_new)
        l = alpha * l + jnp.sum(p, axis=0, keepdims=True)
        acc = alpha * acc + jnp.dot(vt_ref[0, j], p.astype(BF16), preferred_element_type=F32)
        return m_new, l, acc

    m, l, acc = lax.fori_loop(0, own, body, (m, l, acc))
    o_ref[...] = (acc / l).astype(BF16)


def _attention(qt, sel, k_blk, vt_blk, batch, seq):
    d, n = qt.shape
    nb = seq // MOBA_BLOCK
    tq = MOBA_BLOCK
    return pl.pallas_call(
        _attn_kernel,
        grid=(batch, N_HEADS, nb),
        in_specs=[pl.BlockSpec((HEAD_DIM, tq), lambda b, h, i: (h, b * nb + i)),
                  pl.BlockSpec((8, tq), lambda b, h, i: (h, b * nb + i)),
                  pl.BlockSpec((1, nb, MOBA_BLOCK, HEAD_DIM), lambda b, h, i: (h, b, 0, 0)),
                  pl.BlockSpec((1, nb, HEAD_DIM, MOBA_BLOCK), lambda b, h, i: (h, b, 0, 0))],
        out_specs=pl.BlockSpec((HEAD_DIM, tq), lambda b, h, i: (h, b * nb + i)),
        out_shape=jax.ShapeDtypeStruct((d, n), BF16),
        compiler_params=_params("parallel", "parallel", "arbitrary"),
        name="moba_attention",
    )(qt, sel, k_blk, vt_blk)


def _attn_out_kernel(x_ref, ot_ref, wo_ref, lng_ref, lnb_ref, o_ref, op_ref):
    att = ot_ref[...].astype(F32).T.astype(BF16)
    mix = jnp.dot(att, wo_ref[...], preferred_element_type=F32)
    x1 = _layer_norm(DN_ALPHA * x_ref[...] + mix, lng_ref[...], lnb_ref[...])
    o_ref[...] = x1
    op_ref[...] = _pack_bf16_pairs(x1)


def _attn_out(x, ot, w_o, ln_g, ln_b):
    n, d = x.shape
    tm = min(TOKEN_ROWS, n)
    row = lambda v: v.reshape(1, -1).astype(F32)
    return pl.pallas_call(
        _attn_out_kernel,
        grid=(n // tm,),
        in_specs=[pl.BlockSpec((tm, d), lambda i: (i, 0)), pl.BlockSpec((d, tm), lambda i: (0, i)),
                  _full((d, d)), _full((1, d)), _full((1, d))],
        out_specs=[pl.BlockSpec((tm, d), lambda i: (i, 0)), pl.BlockSpec((tm, d // 2), lambda i: (i, 0))],
        out_shape=[jax.ShapeDtypeStruct((n, d), F32), jax.ShapeDtypeStruct((n, d // 2), U32)],
        compiler_params=_params("parallel"),
        name="moba_out",
    )(x, ot, w_o.astype(BF16), row(ln_g), row(ln_b))


def _rope_tables(seq):
    half = ROT_DIM // 2
    inv = ROPE_THETA ** (-jnp.arange(0, ROT_DIM, 2, dtype=F32) / ROT_DIM)
    ang = jnp.arange(seq, dtype=F32)[:, None] * inv[None, :]
    cos, sin = jnp.cos(ang), jnp.sin(ang)
    rest = HEAD_DIM - ROT_DIM
    cos_t = jnp.concatenate([cos, cos, jnp.ones((seq, rest), F32)], axis=1)
    sin_t = jnp.concatenate([-sin, sin, jnp.zeros((seq, rest), F32)], axis=1)
    del half
    return cos_t, sin_t


def kernel(x, p, ln_g, ln_b, a_w_in, a_b_in, a_conv_w, a_conv_b, a_gate_a_w, a_gate_a_b, a_gate_i_w,
           a_gate_i_b, a_lambda, a_w_out, kv_ln_g, kv_ln_b, w_kv, b_w_q, b_w_o, router_w, router_b,
           exp_w_gate, exp_w_up, exp_w_down, sh_w_gate, sh_w_up, sh_w_down, ple_w, ple_gate_w):
    batch, seq, d = x.shape
    n = batch * seq
    xf = x.reshape(n, d)
    pf = p.reshape(DEPTH, n, PLE_DIM)
    cos_t, sin_t = _rope_tables(seq)
    k_blk = vt_blk = k_mean = None
    for i in range(DEPTH):
        if i < N_A_LAYERS:
            x1, x1p = _rglru_layer(xf, seq, a_w_in[i], a_b_in[i], a_conv_w[i], a_conv_b[i],
                                   a_gate_a_w[i], a_gate_a_b[i], a_gate_i_w[i], a_gate_i_b[i],
                                   a_lambda[i], a_w_out[i], ln_g[i, 0], ln_b[i, 0])
        else:
            if i == N_A_LAYERS:
                k_blk, vt_blk, k_mean = _shared_kv(xf, seq, kv_ln_g, kv_ln_b, w_kv, cos_t, sin_t)
                k_mean = k_mean.reshape(n // MOBA_BLOCK, d)
            j = i - N_A_LAYERS
            qt, sel = _queries(xf, seq, b_w_q[j], cos_t, sin_t, k_mean)
            ot = _attention(qt, sel, k_blk, vt_blk, batch, seq)
            x1, x1p = _attn_out(xf, ot, b_w_o[j], ln_g[i, 0], ln_b[i, 0])
        xf = _moe_layer(x1, x1p, router_w[i], router_b[i], exp_w_gate[i], exp_w_up[i], exp_w_down[i],
                        sh_w_gate[i], sh_w_up[i], sh_w_down[i], ln_g[i, 1], ln_b[i, 1],
                        pf[i], ple_w[i], ple_gate_w[i])
    return xf.reshape(batch, seq, d)
```

```python
import functools

import jax
import jax.numpy as jnp
from jax import lax
from jax.experimental import pallas as pl
from jax.experimental.pallas import tpu as pltpu
from jax.experimental.pallas import tpu_sc as plsc

F32 = jnp.float32
BF16 = jnp.bfloat16
I32 = jnp.int32
U32 = jnp.uint32
HIGH_HALF = 0xFFFF0000

SC_CORES = 2
SC_SUBCORES = 16
SC_WORKERS = SC_CORES * SC_SUBCORES
SC_INDEX_CHUNK = 128
SC_STAGE_BYTES = 256 * 1024

D_MODEL = 1024
DEPTH = 4
N_A_LAYERS = DEPTH // 2
D_RNN = D_MODEL
LRU_BLOCKS = 4
LRU_BLOCK_W = D_RNN // LRU_BLOCKS
CONV_W = 4
LRU_C = 8.0
N_HEADS = 8
HEAD_DIM = D_MODEL // N_HEADS
ROT_DIM = HEAD_DIM // 4
ROPE_THETA = 500000.0
MOBA_BLOCK = 256
MOBA_TOPK = 3
N_EXPERTS = 64
N_GROUPS = 8
GROUP_SIZE = N_EXPERTS // N_GROUPS
TOPK_GROUPS = 4
TOPK_EXPERTS = 8
D_EXPERT = 256
D_SHARED = 256
ROUTED_SCALE = 2.5
PLE_DIM = 256
DN_ALPHA = (2 * DEPTH) ** 0.25
LN_EPS = 1e-5

V7X_VMEM_LIMIT_BYTES = 56 * 1024 * 1024

MIXER_ROWS = 256
TOKEN_ROWS = 512
EXPERT_ROWS = 256
NEG_INF = float("-inf")


def _params(*sem):
    return pltpu.CompilerParams(dimension_semantics=sem, vmem_limit_bytes=V7X_VMEM_LIMIT_BYTES)


def _layer_norm(z, g, b):
    mu = jnp.mean(z, axis=-1, keepdims=True)
    zc = z - mu
    var = jnp.mean(zc * zc, axis=-1, keepdims=True)
    return zc * lax.rsqrt(var + LN_EPS) * g + b


def _silu(x):
    return x * jax.nn.sigmoid(x)


def _gelu_tanh(x):
    return 0.5 * x * (1.0 + jnp.tanh(0.7978845608028654 * (x + 0.044715 * (x * x * x))))


def _full(shape):
    return pl.BlockSpec(shape, lambda *_: (0,) * len(shape))


def _pack_bf16_pairs(x):
    w = x.shape[1] // 2
    bits = lax.bitcast_convert_type(x.astype(BF16).astype(F32), U32)
    return (bits[:, :w] >> 16) | (bits[:, w:] & U32(HIGH_HALF))


def _unpack_bf16_pairs(u):
    lo = lax.bitcast_convert_type(u << 16, F32)
    hi = lax.bitcast_convert_type(u & U32(HIGH_HALF), F32)
    return lo, hi


def _sc_mesh():
    return plsc.VectorSubcoreMesh(core_axis_name="c", subcore_axis_name="s")


def _sc_worker_id():
    return lax.axis_index("s") * SC_CORES + lax.axis_index("c")


def _sc_chunks_per_step(chunks_per_worker, row_words):
    g = max(1, SC_STAGE_BYTES // (SC_INDEX_CHUNK * row_words * 4))
    while chunks_per_worker % g:
        g -= 1
    return g


def _sc_gather_rows(table, idx):
    b = idx.shape[0]
    w = table.shape[1]
    chunk = SC_INDEX_CHUNK
    chunks_per_worker = b // (SC_WORKERS * chunk)
    assert chunks_per_worker * SC_WORKERS * chunk == b
    g = _sc_chunks_per_step(chunks_per_worker, w)

    @functools.partial(
        pl.kernel, mesh=_sc_mesh(), out_type=jax.ShapeDtypeStruct((b, w), table.dtype),
        scratch_types=[pltpu.VMEM((g, chunk), I32), pltpu.VMEM((g * chunk, w), table.dtype),
                       pltpu.SemaphoreType.DMA])
    def gather(table_hbm, idx_hbm, out_hbm, idx_v, rows_v, sem):
        first = _sc_worker_id() * chunks_per_worker

        @pl.loop(0, chunks_per_worker // g)
        def _(j):
            c0 = first + j * g
            pltpu.sync_copy(idx_hbm.at[pl.ds(c0, g)], idx_v)
            copies = [pltpu.async_copy(table_hbm.at[idx_v.at[q]], rows_v.at[pl.ds(q * chunk, chunk)], sem)
                      for q in range(g)]
            for cp in copies:
                cp.wait()
            pltpu.sync_copy(rows_v, out_hbm.at[pl.ds(c0 * chunk, g * chunk)])

    return gather(table, idx.reshape(b // chunk, chunk))


def _sc_scatter_rows(src, idx, out_rows):
    n, w = src.shape
    chunk = SC_INDEX_CHUNK
    fan = idx.shape[1]
    chunks_per_worker = n // (SC_WORKERS * chunk)
    assert idx.shape == (n // chunk, fan, chunk) and chunks_per_worker * SC_WORKERS * chunk == n
    g = _sc_chunks_per_step(chunks_per_worker, w)

    @functools.partial(
        pl.kernel, mesh=_sc_mesh(), out_type=jax.ShapeDtypeStruct((out_rows, w), src.dtype),
        scratch_types=[pltpu.VMEM((g, fan, chunk), I32), pltpu.VMEM((g * chunk, w), src.dtype),
                       pltpu.SemaphoreType.DMA])
    def scatter(src_hbm, idx_hbm, out_hbm, idx_v, rows_v, sem):
        first = _sc_worker_id() * chunks_per_worker

        @pl.loop(0, chunks_per_worker // g)
        def _(j):
            c0 = first + j * g
            pltpu.sync_copy(src_hbm.at[pl.ds(c0 * chunk, g * chunk)], rows_v)
            pltpu.sync_copy(idx_hbm.at[pl.ds(c0, g)], idx_v)
            copies = [pltpu.async_copy(rows_v.at[pl.ds(q * chunk, chunk)], out_hbm.at[idx_v.at[q, f]], sem)
                      for q in range(g) for f in range(fan)]
            for cp in copies:
                cp.wait()

    return scatter(src, idx)


def _rglru_kernel(x_ref, win_ref, bin_ref, cw_ref, cb_ref, gaw_ref, gab_ref, giw_ref, gib_ref,
                  lam_ref, wout_ref, lng_ref, lnb_ref, o_ref, op_ref, tail_ref, h_ref, *, tiles_per_seq):
    i = pl.program_id(0)
    tm = x_ref.shape[0]

    @pl.when(i % tiles_per_seq == 0)
    def _():
        tail_ref[...] = jnp.zeros_like(tail_ref)
        h_ref[...] = jnp.zeros_like(h_ref)

    x = x_ref[...]
    xy = jnp.dot(x.astype(BF16), win_ref[...], preferred_element_type=F32) + bin_ref[...]
    xb = xy[:, :D_RNN]
    y = _gelu_tanh(xy[:, D_RNN:])

    tail = tail_ref[...]
    row8 = lax.broadcasted_iota(I32, (8, 1), 0)
    xc = cb_ref[...] + xb * cw_ref[CONV_W - 1:CONV_W, :]
    for d in range(1, CONV_W):
        rolled = pltpu.roll(xb, d, 0)
        head = jnp.where(row8 < d, pltpu.roll(tail, d, 0), rolled[:8])
        shifted = jnp.concatenate([head, rolled[8:]], axis=0)
        xc = xc + shifted * cw_ref[CONV_W - 1 - d:CONV_W - d, :]
    tail_ref[...] = xb[tm - 8:, :]

    r_parts, i_parts = [], []
    for n in range(LRU_BLOCKS):
        xg = xc[:, n * LRU_BLOCK_W:(n + 1) * LRU_BLOCK_W].astype(BF16)
        r_parts.append(jnp.dot(xg, gaw_ref[n], preferred_element_type=F32))
        i_parts.append(jnp.dot(xg, giw_ref[n], preferred_element_type=F32))
    r = jax.nn.sigmoid(jnp.concatenate(r_parts, axis=1) + gab_ref[...])
    ig = jax.nn.sigmoid(jnp.concatenate(i_parts, axis=1) + gib_ref[...])

    lam = lam_ref[...]
    softplus_neg_lam = jnp.maximum(-lam, 0.0) + jnp.log1p(jnp.exp(-jnp.abs(lam)))
    log_a = (-LRU_C * r) * softplus_neg_lam
    a = jnp.exp(log_a)
    u = jnp.sqrt(1.0 - a * a) * (ig * xc)

    row = lax.broadcasted_iota(I32, (tm, 1), 0)
    acc_a, acc_h = a, u
    d = 1
    while d < tm:
        keep = row >= d
        sh_a = pltpu.roll(acc_a, d, 0)
        sh_h = pltpu.roll(acc_h, d, 0)
        acc_h = jnp.where(keep, acc_a * sh_h + acc_h, acc_h)
        acc_a = jnp.where(keep, acc_a * sh_a, acc_a)
        d *= 2
    h = acc_h + acc_a * h_ref[...]
    h_ref[...] = h[tm - 1:, :]

    mix = jnp.dot((h * y).astype(BF16), wout_ref[...], preferred_element_type=F32)
    x1 = _layer_norm(DN_ALPHA * x + mix, lng_ref[...], lnb_ref[...])
    o_ref[...] = x1
    op_ref[...] = _pack_bf16_pairs(x1)


def _rglru_layer(x, seq, w_in, b_in, conv_w, conv_b, ga_w, ga_b, gi_w, gi_b, lam, w_out, ln_g, ln_b):
    n, d = x.shape
    tm = min(MIXER_ROWS, seq)
    row = lambda v: v.reshape(1, -1).astype(F32)
    return pl.pallas_call(
        functools.partial(_rglru_kernel, tiles_per_seq=seq // tm),
        grid=(n // tm,),
        in_specs=[pl.BlockSpec((tm, d), lambda i: (i, 0)),
                  _full((d, 2 * D_RNN)), _full((1, 2 * D_RNN)),
                  _full((CONV_W, D_RNN)), _full((1, D_RNN)),
                  _full((LRU_BLOCKS, LRU_BLOCK_W, LRU_BLOCK_W)), _full((1, D_RNN)),
                  _full((LRU_BLOCKS, LRU_BLOCK_W, LRU_BLOCK_W)), _full((1, D_RNN)),
                  _full((1, D_RNN)), _full((D_RNN, d)), _full((1, d)), _full((1, d))],
        out_specs=[pl.BlockSpec((tm, d), lambda i: (i, 0)), pl.BlockSpec((tm, d // 2), lambda i: (i, 0))],
        out_shape=[jax.ShapeDtypeStruct((n, d), F32), jax.ShapeDtypeStruct((n, d // 2), U32)],
        scratch_shapes=[pltpu.VMEM((8, D_RNN), F32), pltpu.VMEM((1, D_RNN), F32)],
        compiler_params=_params("arbitrary"),
        name="rglru_mixer",
    )(x, w_in.astype(BF16), row(b_in), conv_w, row(conv_b), ga_w.astype(BF16), row(ga_b),
      gi_w.astype(BF16), row(gi_b), row(lam), w_out.astype(BF16), row(ln_g), row(ln_b))


def _first_argmax(cur, idx, size, axis):
    m = jnp.max(cur, axis=axis, keepdims=True)
    first = jnp.min(jnp.where(cur == m, idx, size), axis=axis, keepdims=True)
    return m, first


def _router_kernel(x_ref, rw_ref, rb_ref, e_ref, w_ref, r_ref, cnt_ref, carry_ref):
    i = pl.program_id(0)
    tm = x_ref.shape[0]

    @pl.when(i == 0)
    def _():
        carry_ref[...] = jnp.zeros_like(carry_ref)

    logits = jnp.dot(x_ref[...], rw_ref[...], preferred_element_type=F32,
                     precision=lax.Precision.HIGHEST)
    scores = jax.nn.sigmoid(logits.T[:N_EXPERTS, :])
    choice = scores + rb_ref[...]

    c3 = choice.reshape(N_GROUPS, GROUP_SIZE, tm)
    in_grp = lax.broadcasted_iota(I32, c3.shape, 1)
    m1, i1 = _first_argmax(c3, in_grp, GROUP_SIZE, 1)
    m2 = jnp.max(jnp.where(in_grp == i1, NEG_INF, c3), axis=1, keepdims=True)
    grp_score = (m1 + m2)[:, 0, :]

    grp_id = lax.broadcasted_iota(I32, grp_score.shape, 0)
    grp_sel = jnp.zeros(grp_score.shape, jnp.bool_)
    cur = grp_score
    for _ in range(TOPK_GROUPS):
        _, gi = _first_argmax(cur, grp_id, N_GROUPS, 0)
        hit = grp_id == gi
        grp_sel = jnp.logical_or(grp_sel, hit)
        cur = jnp.where(hit, NEG_INF, cur)

    cur = jnp.where(grp_sel[:, None, :], c3, NEG_INF).reshape(N_EXPERTS, tm)
    exp_id = lax.broadcasted_iota(I32, cur.shape, 0)
    sel = jnp.zeros(cur.shape, F32)
    e_rows, s_rows = [], []
    for _ in range(TOPK_EXPERTS):
        _, ei = _first_argmax(cur, exp_id, N_EXPERTS, 0)
        hit = exp_id == ei
        e_rows.append(ei)
        s_rows.append(jnp.sum(jnp.where(hit, scores, 0.0), axis=0, keepdims=True))
        sel = jnp.where(hit, 1.0, sel)
        cur = jnp.where(hit, NEG_INF, cur)
    e_top = jnp.concatenate(e_rows, axis=0)
    s_top = jnp.concatenate(s_rows, axis=0)
    w_ref[...] = s_top / jnp.sum(s_top, axis=0, keepdims=True) * ROUTED_SCALE
    e_ref[...] = e_top

    t_row = lax.broadcasted_iota(I32, (tm, tm), 0)
    t_col = lax.broadcasted_iota(I32, (tm, tm), 1)
    before = (t_row < t_col).astype(BF16)
    cum = jnp.dot(sel.astype(BF16), before, preferred_element_type=F32) + carry_ref[...]
    r_rows = [jnp.sum(jnp.where(exp_id == e_rows[k], cum, 0.0), axis=0, keepdims=True)
              for k in range(TOPK_EXPERTS)]
    r_ref[...] = jnp.concatenate(r_rows, axis=0).astype(I32)
    carry_ref[...] = carry_ref[...] + jnp.sum(sel, axis=1, keepdims=True)
    cnt_ref[...] = jnp.broadcast_to(carry_ref[...], cnt_ref.shape).astype(I32)


def _router(x, router_w, router_b):
    n, d = x.shape
    tm = min(TOKEN_ROWS, n)
    rw = jnp.pad(router_w, ((0, 0), (0, 128 - N_EXPERTS)))
    k = TOPK_EXPERTS
    tok_spec = pl.BlockSpec((k, tm), lambda i: (0, i))
    return pl.pallas_call(
        _router_kernel,
        grid=(n // tm,),
        in_specs=[pl.BlockSpec((tm, d), lambda i: (i, 0)), _full((d, 128)), _full((N_EXPERTS, 1))],
        out_specs=[tok_spec, tok_spec, tok_spec, _full((N_EXPERTS, 128))],
        out_shape=[jax.ShapeDtypeStruct((k, n), I32), jax.ShapeDtypeStruct((k, n), F32),
                   jax.ShapeDtypeStruct((k, n), I32), jax.ShapeDtypeStruct((N_EXPERTS, 128), I32)],
        scratch_shapes=[pltpu.VMEM((N_EXPERTS, 1), F32)],
        compiler_params=_params("arbitrary"),
        name="moe_router",
    )(x, rw, router_b.reshape(N_EXPERTS, 1).astype(F32))


def _expert_kernel(blk_e_ref, nblk_ref, xs_ref, wg_ref, wu_ref, wd_ref, ys_ref):
    del blk_e_ref

    @pl.when(pl.program_id(0) < nblk_ref[0])
    def _():
        lo, hi = _unpack_bf16_pairs(xs_ref[...])
        xs = jnp.concatenate([lo.astype(BF16), hi.astype(BF16)], axis=1)
        g = jnp.dot(xs, wg_ref[0], preferred_element_type=F32)
        u = jnp.dot(xs, wu_ref[0], preferred_element_type=F32)
        hdn = (_silu(g) * u).astype(BF16)
        ys_ref[...] = _pack_bf16_pairs(jnp.dot(hdn, wd_ref[0], preferred_element_type=F32))


def _experts(xs, blk_e, nblk, w_gate, w_up, w_down):
    rows, half = xs.shape
    d = 2 * half
    m = EXPERT_ROWS
    row_map = lambda i, be, nb: (jnp.minimum(i, nb[0] - 1), 0)
    return pl.pallas_call(
        _expert_kernel,
        grid_spec=pltpu.PrefetchScalarGridSpec(
            num_scalar_prefetch=2,
            grid=(rows // m,),
            in_specs=[pl.BlockSpec((m, half), row_map),
                      pl.BlockSpec((1, d, D_EXPERT), lambda i, be, nb: (be[i], 0, 0)),
                      pl.BlockSpec((1, d, D_EXPERT), lambda i, be, nb: (be[i], 0, 0)),
                      pl.BlockSpec((1, D_EXPERT, d), lambda i, be, nb: (be[i], 0, 0))],
            out_specs=pl.BlockSpec((m, half), row_map)),
        out_shape=jax.ShapeDtypeStruct((rows, half), U32),
        compiler_params=_params("arbitrary"),
        name="moe_experts",
    )(blk_e, nblk, xs, w_gate, w_up, w_down)


def _combine_kernel(x_ref, yg_ref, w_ref, sg_ref, su_ref, sd_ref, lng_ref, lnb_ref,
                    p_ref, pw_ref, pg_ref, o_ref):
    x = x_ref[...]
    wt = w_ref[...].T
    moe_lo, moe_hi = None, None
    for k in range(TOPK_EXPERTS):
        lo, hi = _unpack_bf16_pairs(yg_ref[k])
        wk = wt[:, k:k + 1]
        moe_lo = wk * lo if k == 0 else moe_lo + wk * lo
        moe_hi = wk * hi if k == 0 else moe_hi + wk * hi
    moe = jnp.concatenate([moe_lo, moe_hi], axis=1)
    xb = x.astype(BF16)
    hdn = _silu(jnp.dot(xb, sg_ref[...], preferred_element_type=F32)) * \
        jnp.dot(xb, su_ref[...], preferred_element_type=F32)
    shared = jnp.dot(hdn.astype(BF16), sd_ref[...], preferred_element_type=F32)
    x2 = _layer_norm(DN_ALPHA * x + (moe + shared), lng_ref[...], lnb_ref[...])
    emb = jnp.dot(p_ref[...].astype(BF16), pw_ref[...], preferred_element_type=F32)
    gate = jax.nn.sigmoid(jnp.dot(x2.astype(BF16), pg_ref[...], preferred_element_type=F32))
    o_ref[...] = x2 + emb * gate


def _combine(x, yg, wgt, s_gate, s_up, s_down, ln_g, ln_b, p, ple_w, ple_gate_w):
    n, d = x.shape
    tm = min(TOKEN_ROWS // 2, n)
    k = TOPK_EXPERTS
    row = lambda v: v.reshape(1, -1).astype(F32)
    return pl.pallas_call(
        _combine_kernel,
        grid=(n // tm,),
        in_specs=[pl.BlockSpec((tm, d), lambda i: (i, 0)),
                  pl.BlockSpec((k, tm, d // 2), lambda i: (0, i, 0)),
                  pl.BlockSpec((k, tm), lambda i: (0, i)),
                  _full((d, D_SHARED)), _full((d, D_SHARED)), _full((D_SHARED, d)),
                  _full((1, d)), _full((1, d)),
                  pl.BlockSpec((tm, PLE_DIM), lambda i: (i, 0)),
                  _full((PLE_DIM, d)), _full((d, d))],
        out_specs=pl.BlockSpec((tm, d), lambda i: (i, 0)),
        out_shape=jax.ShapeDtypeStruct((n, d), F32),
        compiler_params=_params("parallel"),
        name="moe_combine",
    )(x, yg, wgt, s_gate.astype(BF16), s_up.astype(BF16), s_down.astype(BF16), row(ln_g), row(ln_b),
      p, ple_w.astype(BF16), ple_gate_w.astype(BF16))


def _positions_kernel(e_ref, r_ref, start_ref, pos_ref):
    e = e_ref[...]
    start = start_ref[...]
    grp_id = lax.broadcasted_iota(I32, (start.shape[0], e.shape[1]), 0)
    rows = [jnp.sum(jnp.where(grp_id == e[k:k + 1], start, 0), axis=0, keepdims=True)
            for k in range(e.shape[0])]
    pos_ref[...] = jnp.concatenate(rows, axis=0) + r_ref[...]


def _positions(e_idx, rank, starts):
    k, n = e_idx.shape
    tm = min(4 * TOKEN_ROWS, n)
    groups = starts.shape[0]
    tok_spec = pl.BlockSpec((k, tm), lambda i: (0, i))
    return pl.pallas_call(
        _positions_kernel,
        grid=(n // tm,),
        in_specs=[tok_spec, tok_spec, _full((groups, 1))],
        out_specs=tok_spec,
        out_shape=jax.ShapeDtypeStruct((k, n), I32),
        compiler_params=_params("parallel"),
        name="group_positions",
    )(e_idx, rank, starts.reshape(groups, 1))


def _moe_layer(x, xp, router_w, router_b, w_gate, w_up, w_down, s_gate, s_up, s_down,
               ln_g, ln_b, p, ple_w, ple_gate_w):
    n, d = x.shape
    m = EXPERT_ROWS
    k = TOPK_EXPERTS
    e_idx, wgt, rank, cnt = _router(x, router_w, router_b)
    counts = cnt[:, 0]
    padded = (counts + m - 1) // m * m
    pends = jnp.cumsum(padded)
    n_blocks = (n * k) // m + N_EXPERTS
    blk_first = jnp.arange(n_blocks, dtype=I32) * m
    blk_e = jnp.minimum(jnp.sum((pends[None, :] <= blk_first[:, None]).astype(I32), axis=1), N_EXPERTS - 1)
    nblk = (pends[-1] // m).astype(I32).reshape(1)
    pos = _positions(e_idx, rank, (pends - padded).astype(I32))
    chunk = SC_INDEX_CHUNK
    pos_chunks = pos.reshape(k, n // chunk, chunk).transpose(1, 0, 2)
    xs = _sc_scatter_rows(xp, pos_chunks, n_blocks * m)
    ys = _experts(xs, blk_e, nblk, w_gate.astype(BF16), w_up.astype(BF16), w_down.astype(BF16))
    yg = _sc_gather_rows(ys, pos.reshape(-1)).reshape(k, n, d // 2)
    return _combine(x, yg, wgt, s_gate, s_up, s_down, ln_g, ln_b, p, ple_w, ple_gate_w)


def _rope(t, cos_t, sin_t):
    half = ROT_DIM // 2
    width = t.shape[1]
    lane = lax.broadcasted_iota(I32, (1, width), 1) % HEAD_DIM
    partner = jnp.where(lane < half, pltpu.roll(t, width - half, 1), pltpu.roll(t, half, 1))
    cos_f = jnp.concatenate([cos_t] * N_HEADS, axis=1)
    sin_f = jnp.concatenate([sin_t] * N_HEADS, axis=1)
    return t * cos_f + partner * sin_f


def _kv_kernel(x_ref, g_ref, b_ref, wkv_ref, cos_ref, sin_ref, k_ref, v_ref, km_ref):
    h = _layer_norm(x_ref[...], g_ref[...], b_ref[...])
    kv = jnp.dot(h.astype(BF16), wkv_ref[...], preferred_element_type=F32)
    k = _rope(kv[:, :D_MODEL], cos_ref[...], sin_ref[...])
    v = kv[:, D_MODEL:]
    km_ref[0] = jnp.mean(k, axis=0, keepdims=True)
    for hd in range(N_HEADS):
        k_ref[hd, 0] = k[:, hd * HEAD_DIM:(hd + 1) * HEAD_DIM].astype(BF16)
        v_ref[hd, 0] = v[:, hd * HEAD_DIM:(hd + 1) * HEAD_DIM].astype(BF16)


def _shared_kv(x, seq, ln_g, ln_b, w_kv, cos_t, sin_t):
    n, d = x.shape
    blk = MOBA_BLOCK
    nbt = n // blk
    spb = seq // blk
    row = lambda v: v.reshape(1, -1).astype(F32)
    return pl.pallas_call(
        _kv_kernel,
        grid=(nbt,),
        in_specs=[pl.BlockSpec((blk, d), lambda i: (i, 0)), _full((1, d)), _full((1, d)),
                  _full((d, 2 * d)),
                  pl.BlockSpec((blk, HEAD_DIM), lambda i: (i % spb, 0)),
                  pl.BlockSpec((blk, HEAD_DIM), lambda i: (i % spb, 0))],
        out_specs=[pl.BlockSpec((N_HEADS, 1, blk, HEAD_DIM), lambda i: (0, i, 0, 0)),
                   pl.BlockSpec((N_HEADS, 1, blk, HEAD_DIM), lambda i: (0, i, 0, 0)),
                   pl.BlockSpec((1, 1, d), lambda i: (i, 0, 0))],
        out_shape=[jax.ShapeDtypeStruct((N_HEADS, nbt, blk, HEAD_DIM), BF16),
                   jax.ShapeDtypeStruct((N_HEADS, nbt, blk, HEAD_DIM), BF16),
                   jax.ShapeDtypeStruct((nbt, 1, d), F32)],
        compiler_params=_params("parallel"),
        name="shared_kv",
    )(x, row(ln_g), row(ln_b), w_kv.astype(BF16), cos_t, sin_t)


SEL_ROWS = 8


def _query_kernel(x_ref, wq_ref, cos_ref, sin_ref, km_ref, qp_ref, sel_ref, cnt_ref, carry_ref,
                  *, blocks_per_seq):
    i = pl.program_id(0)
    tm = x_ref.shape[0]
    nb = km_ref.shape[0]
    own = i % blocks_per_seq

    @pl.when(own == 0)
    def _():
        carry_ref[...] = jnp.zeros_like(carry_ref)

    q = jnp.dot(x_ref[...].astype(BF16), wq_ref[...], preferred_element_type=F32)
    q = _rope(q, cos_ref[...], sin_ref[...]) * (HEAD_DIM ** -0.5)

    blk_id = lax.broadcasted_iota(I32, (nb, tm), 0)
    t_row = lax.broadcasted_iota(I32, (tm, tm), 0)
    t_col = lax.broadcasted_iota(I32, (tm, tm), 1)
    before = (t_row < t_col).astype(BF16)
    km = km_ref[...]
    for hd in range(N_HEADS):
        lo, hi = hd * HEAD_DIM, (hd + 1) * HEAD_DIM
        q_h = q[:, lo:hi]
        qp_ref[0, hd] = q_h
        gate = lax.dot_general(km[:, lo:hi], q_h, (((1,), (1,)), ((), ())),
                               preferred_element_type=F32, precision=lax.Precision.HIGHEST)
        cur = jnp.where(blk_id < own, gate, NEG_INF)
        sel_rows = []
        chosen = jnp.zeros((nb, tm), F32)
        for _ in range(MOBA_TOPK):
            m, bi = _first_argmax(cur, blk_id, nb, 0)
            valid = m > NEG_INF
            hit = jnp.logical_and(blk_id == bi, valid)
            sel_rows.append(jnp.where(valid, bi, -1))
            chosen = jnp.where(hit, 1.0, chosen)
            cur = jnp.where(blk_id == bi, NEG_INF, cur)
        carry = carry_ref[hd * nb:(hd + 1) * nb, :]
        cum = jnp.dot(chosen.astype(BF16), before, preferred_element_type=F32) + carry
        rank_rows = [jnp.sum(jnp.where(blk_id == s, cum, 0.0), axis=0, keepdims=True).astype(I32)
                     for s in sel_rows]
        carry_ref[hd * nb:(hd + 1) * nb, :] = carry + jnp.sum(chosen, axis=1, keepdims=True)
        pad = jnp.zeros((SEL_ROWS - 2 * MOBA_TOPK, tm), I32)
        sel_ref[hd * SEL_ROWS:(hd + 1) * SEL_ROWS, :] = jnp.concatenate(sel_rows + rank_rows + [pad], axis=0)
    cnt_ref[0] = jnp.broadcast_to(carry_ref[...], cnt_ref.shape[1:]).astype(I32)


def _queries(x, seq, w_q, cos_t, sin_t, k_mean):
    n, d = x.shape
    blk = MOBA_BLOCK
    nb = seq // blk
    return pl.pallas_call(
        functools.partial(_query_kernel, blocks_per_seq=nb),
        grid=(n // blk,),
        in_specs=[pl.BlockSpec((blk, d), lambda i: (i, 0)), _full((d, d)),
                  pl.BlockSpec((blk, HEAD_DIM), lambda i: (i % nb, 0)),
                  pl.BlockSpec((blk, HEAD_DIM), lambda i: (i % nb, 0)),
                  pl.BlockSpec((nb, d), lambda i: (i // nb, 0))],
        out_specs=[pl.BlockSpec((1, N_HEADS, blk, HEAD_DIM), lambda i: (i, 0, 0, 0)),
                   pl.BlockSpec((N_HEADS * SEL_ROWS, blk), lambda i: (0, i)),
                   pl.BlockSpec((1, N_HEADS * nb, 128), lambda i: (i // nb, 0, 0))],
        out_shape=[jax.ShapeDtypeStruct((n // blk, N_HEADS, blk, HEAD_DIM), F32),
                   jax.ShapeDtypeStruct((N_HEADS * SEL_ROWS, n), I32),
                   jax.ShapeDtypeStruct((n // seq, N_HEADS * nb, 128), I32)],
        scratch_shapes=[pltpu.VMEM((N_HEADS * nb, 1), F32)],
        compiler_params=_params("arbitrary"),
        name="moba_queries",
    )(x, w_q.astype(BF16), cos_t, sin_t, k_mean)


def _moba_positions_kernel(sel_ref, start_ref, pos_ref, *, dump_row):
    tm = sel_ref.shape[1]
    nb = start_ref.shape[1] // N_HEADS
    blk_id = lax.broadcasted_iota(I32, (nb, tm), 0)
    dump = dump_row + lax.broadcasted_iota(I32, (1, tm), 1) % SC_INDEX_CHUNK
    rows = []
    for hd in range(N_HEADS):
        start = start_ref[0, hd * nb:(hd + 1) * nb, :]
        for s in range(MOBA_TOPK):
            sel = sel_ref[hd * SEL_ROWS + s:hd * SEL_ROWS + s + 1, :]
            rank = sel_ref[hd * SEL_ROWS + MOBA_TOPK + s:hd * SEL_ROWS + MOBA_TOPK + s + 1, :]
            base = jnp.sum(jnp.where(blk_id == sel, start, 0), axis=0, keepdims=True)
            rows.append(jnp.where(sel >= 0, base + rank, dump))
        rows.extend([dump] * (SEL_ROWS - MOBA_TOPK))
    pos_ref[...] = jnp.concatenate(rows, axis=0)


def _moba_positions(sel, starts, seq, dump_row):
    rows, n = sel.shape
    tm = min(4 * TOKEN_ROWS, seq)
    tps = seq // tm
    groups = starts.shape[1]
    return pl.pallas_call(
        functools.partial(_moba_positions_kernel, dump_row=dump_row),
        grid=(n // tm,),
        in_specs=[pl.BlockSpec((rows, tm), lambda i: (0, i)),
                  pl.BlockSpec((1, groups, 1), lambda i: (i // tps, 0, 0))],
        out_specs=pl.BlockSpec((rows, tm), lambda i: (0, i)),
        out_shape=jax.ShapeDtypeStruct((rows, n), I32),
        compiler_params=_params("parallel"),
        name="moba_positions",
    )(sel, starts)


ATTN_SUB_ROWS = 128
ATTN_SUBS_PER_STEP = 8
ATTN_STEP_ROWS = ATTN_SUB_ROWS * ATTN_SUBS_PER_STEP


def _group_attn_kernel(step_h_ref, step_b_ref, sub_j_ref, nsteps_ref, qs_ref, k_ref, v_ref, part_ref):
    del step_h_ref, step_b_ref
    i = pl.program_id(0)
    half = HEAD_DIM // 2

    @pl.when(i < nsteps_ref[0])
    def _():
        for u in range(ATTN_SUBS_PER_STEP):
            j = sub_j_ref[i * ATTN_SUBS_PER_STEP + u]
            rows = pl.ds(u * ATTN_SUB_ROWS, ATTN_SUB_ROWS)
            q = qs_ref[rows, :].astype(BF16)
            s = lax.dot_general(q, k_ref[0, j], (((1,), (1,)), ((), ())), preferred_element_type=F32)
            m = jnp.max(s, axis=1, keepdims=True)
            p = jnp.exp(s - m)
            l = jnp.sum(p, axis=1, keepdims=True)
            o = jnp.dot(p.astype(BF16), v_ref[0, j], preferred_element_type=F32) / l
            lse = lax.bitcast_convert_type(m + jnp.log(l), U32)
            part_ref[rows, :] = jnp.concatenate(
                [_pack_bf16_pairs(o), jnp.broadcast_to(lse, (ATTN_SUB_ROWS, half))], axis=1)


def _group_attention(qs, k_blk, v_blk, step_h, step_b, sub_j, nsteps, nb):
    n_steps = step_h.shape[0]
    half = HEAD_DIM // 2
    row_map = lambda i, sh, sb, sj, ns: (jnp.minimum(i, ns[0] - 1), 0)
    kv_map = lambda i, sh, sb, sj, ns: (sh[i], sb[i], 0, 0)
    return pl.pallas_call(
        _group_attn_kernel,
        grid_spec=pltpu.PrefetchScalarGridSpec(
            num_scalar_prefetch=4,
            grid=(n_steps,),
            in_specs=[pl.BlockSpec((ATTN_STEP_ROWS, HEAD_DIM), row_map),
                      pl.BlockSpec((1, nb, MOBA_BLOCK, HEAD_DIM), kv_map),
                      pl.BlockSpec((1, nb, MOBA_BLOCK, HEAD_DIM), kv_map)],
            out_specs=pl.BlockSpec((ATTN_STEP_ROWS, HEAD_DIM), row_map)),
        out_shape=jax.ShapeDtypeStruct(((n_steps + 1) * ATTN_STEP_ROWS, HEAD_DIM), U32),
        compiler_params=_params("arbitrary"),
        name="moba_group_attention",
    )(step_h, step_b, sub_j, nsteps, qs, k_blk, v_blk)


def _attn_merge_kernel(x_ref, qp_ref, k_ref, v_ref, pg_ref, sel_ref, wo_ref, lng_ref, lnb_ref,
                       o_ref, op_ref):
    tm = x_ref.shape[0]
    half = HEAD_DIM // 2
    sel_t = sel_ref[...].T
    qpos = lax.broadcasted_iota(I32, (tm, tm), 0)
    kpos = lax.broadcasted_iota(I32, (tm, tm), 1)
    causal = kpos <= qpos
    heads = []
    for hd in range(N_HEADS):
        q = qp_ref[0, hd].astype(BF16)
        s = lax.dot_general(q, k_ref[hd, 0], (((1,), (1,)), ((), ())), preferred_element_type=F32)
        s = jnp.where(causal, s, NEG_INF)
        m_own = jnp.max(s, axis=1, keepdims=True)
        p = jnp.exp(s - m_own)
        l_own = jnp.sum(p, axis=1, keepdims=True)
        acc = jnp.dot(p.astype(BF16), v_ref[hd, 0], preferred_element_type=F32)

        outs, lses = [], []
        m_tot = m_own
        for c in range(MOBA_TOPK):
            part = pg_ref[c, 0, hd]
            lo, hi = _unpack_bf16_pairs(part[:, :half])
            outs.append(jnp.concatenate([lo, hi], axis=1))
            lse = lax.bitcast_convert_type(part[:, half:half + 1], F32)
            valid = sel_t[:, hd * SEL_ROWS + c:hd * SEL_ROWS + c + 1] >= 0
            lse = jnp.where(valid, lse, NEG_INF)
            lses.append(lse)
            m_tot = jnp.maximum(m_tot, lse)
        w_own = jnp.exp(m_own - m_tot)
        num = acc * w_own
        den = l_own * w_own
        for c in range(MOBA_TOPK):
            w_c = jnp.exp(lses[c] - m_tot)
            num = num + jnp.where(w_c > 0.0, outs[c], 0.0) * w_c
            den = den + w_c
        heads.append((num / den).astype(BF16))
    att = jnp.concatenate(heads, axis=1)
    mix = jnp.dot(att, wo_ref[...], preferred_element_type=F32)
    x1 = _layer_norm(DN_ALPHA * x_ref[...] + mix, lng_ref[...], lnb_ref[...])
    o_ref[...] = x1
    op_ref[...] = _pack_bf16_pairs(x1)


def _attn_merge(x, qp, k_blk, v_blk, pg, sel, w_o, ln_g, ln_b):
    n, d = x.shape
    blk = MOBA_BLOCK
    half = HEAD_DIM // 2
    row = lambda v: v.reshape(1, -1).astype(F32)
    kv_spec = pl.BlockSpec((N_HEADS, 1, blk, HEAD_DIM), lambda i: (0, i, 0, 0))
    return pl.pallas_call(
        _attn_merge_kernel,
        grid=(n // blk,),
        in_specs=[pl.BlockSpec((blk, d), lambda i: (i, 0)),
                  pl.BlockSpec((1, N_HEADS, blk, HEAD_DIM), lambda i: (i, 0, 0, 0)),
                  kv_spec, kv_spec,
                  pl.BlockSpec((MOBA_TOPK, 1, N_HEADS, blk, HEAD_DIM), lambda i: (0, i, 0, 0, 0)),
                  pl.BlockSpec((N_HEADS * SEL_ROWS, blk), lambda i: (0, i)),
                  _full((d, d)), _full((1, d)), _full((1, d))],
        out_specs=[pl.BlockSpec((blk, d), lambda i: (i, 0)), pl.BlockSpec((blk, d // 2), lambda i: (i, 0))],
        out_shape=[jax.ShapeDtypeStruct((n, d), F32), jax.ShapeDtypeStruct((n, d // 2), U32)],
        compiler_params=_params("parallel"),
        name="moba_merge",
    )(x, qp, k_blk, v_blk, pg, sel, w_o.astype(BF16), row(ln_g), row(ln_b))


def _moba_layer(x, seq, w_q, w_o, ln_g, ln_b, cos_t, sin_t, k_blk, v_blk, k_mean):
    n, d = x.shape
    batch = n // seq
    blk = MOBA_BLOCK
    nb = seq // blk
    nbt = n // blk
    chunk = SC_INDEX_CHUNK
    qp, sel, cnt = _queries(x, seq, w_q, cos_t, sin_t, k_mean)

    counts = cnt[:, :, 0].reshape(batch * N_HEADS, nb)
    gpad = (counts + ATTN_SUB_ROWS - 1) // ATTN_SUB_ROWS * ATTN_SUB_ROWS
    seg = jnp.sum(gpad, axis=1)
    seg_pad = (seg + ATTN_STEP_ROWS - 1) // ATTN_STEP_ROWS * ATTN_STEP_ROWS
    seg_end = jnp.cumsum(seg_pad)
    gend = (seg_end - seg_pad)[:, None] + jnp.cumsum(gpad, axis=1)
    gstart = (gend - gpad).astype(I32)
    steps_per_seg = -(-(MOBA_TOPK * seq + nb * (ATTN_SUB_ROWS - 1)) // ATTN_STEP_ROWS)
    n_steps = batch * N_HEADS * steps_per_seg
    step_first = jnp.arange(n_steps, dtype=I32) * ATTN_STEP_ROWS
    step_seg = jnp.minimum(jnp.sum((seg_end[None, :] <= step_first[:, None]).astype(I32), axis=1),
                           batch * N_HEADS - 1)
    sub_first = jnp.arange(n_steps * ATTN_SUBS_PER_STEP, dtype=I32) * ATTN_SUB_ROWS
    sub_grp = jnp.minimum(jnp.sum((gend.reshape(-1)[None, :] <= sub_first[:, None]).astype(I32), axis=1),
                          batch * N_HEADS * nb - 1)
    nsteps = (seg_end[-1] // ATTN_STEP_ROWS).astype(I32).reshape(1)
    dump_row = n_steps * ATTN_STEP_ROWS

    pos = _moba_positions(sel, gstart.reshape(batch, N_HEADS * nb, 1), seq, dump_row)
    pos5 = pos.reshape(N_HEADS, SEL_ROWS, nbt, blk // chunk, chunk)[:, :MOBA_TOPK]
    scatter_idx = pos5.transpose(2, 0, 3, 1, 4).reshape(nbt * N_HEADS * (blk // chunk), MOBA_TOPK, chunk)
    gather_idx = pos5.transpose(1, 2, 0, 3, 4).reshape(-1)

    qs = _sc_scatter_rows(qp.reshape(n * N_HEADS, HEAD_DIM), scatter_idx, dump_row + chunk)
    part = _group_attention(qs, k_blk, v_blk, (step_seg % N_HEADS).astype(I32),
                            (step_seg // N_HEADS).astype(I32), (sub_grp % nb).astype(I32), nsteps, nb)
    pg = _sc_gather_rows(part, gather_idx).reshape(MOBA_TOPK, nbt, N_HEADS, blk, HEAD_DIM)
    return _attn_merge(x, qp, k_blk, v_blk, pg, sel, w_o, ln_g, ln_b)


def _rope_tables(seq):
    half = ROT_DIM // 2
    inv = ROPE_THETA ** (-jnp.arange(0, ROT_DIM, 2, dtype=F32) / ROT_DIM)
    ang = jnp.arange(seq, dtype=F32)[:, None] * inv[None, :]
    cos, sin = jnp.cos(ang), jnp.sin(ang)
    rest = HEAD_DIM - ROT_DIM
    cos_t = jnp.concatenate([cos, cos, jnp.ones((seq, rest), F32)], axis=1)
    sin_t = jnp.concatenate([-sin, sin, jnp.zeros((seq, rest), F32)], axis=1)
    del half
    return cos_t, sin_t


def kernel(x, p, ln_g, ln_b, a_w_in, a_b_in, a_conv_w, a_conv_b, a_gate_a_w, a_gate_a_b, a_gate_i_w,
           a_gate_i_b, a_lambda, a_w_out, kv_ln_g, kv_ln_b, w_kv, b_w_q, b_w_o, router_w, router_b,
           exp_w_gate, exp_w_up, exp_w_down, sh_w_gate, sh_w_up, sh_w_down, ple_w, ple_gate_w):
    batch, seq, d = x.shape
    n = batch * seq
    xf = x.reshape(n, d)
    pf = p.reshape(DEPTH, n, PLE_DIM)
    cos_t, sin_t = _rope_tables(seq)
    k_blk = v_blk = k_mean = None
    for i in range(DEPTH):
        if i < N_A_LAYERS:
            x1, x1p = _rglru_layer(xf, seq, a_w_in[i], a_b_in[i], a_conv_w[i], a_conv_b[i],
                                   a_gate_a_w[i], a_gate_a_b[i], a_gate_i_w[i], a_gate_i_b[i],
                                   a_lambda[i], a_w_out[i], ln_g[i, 0], ln_b[i, 0])
        else:
            if i == N_A_LAYERS:
                k_blk, v_blk, k_mean = _shared_kv(xf, seq, kv_ln_g, kv_ln_b, w_kv, cos_t, sin_t)
                k_mean = k_mean.reshape(n // MOBA_BLOCK, d)
            j = i - N_A_LAYERS
            x1, x1p = _moba_layer(xf, seq, b_w_q[j], b_w_o[j], ln_g[i, 0], ln_b[i, 0],
                                  cos_t, sin_t, k_blk, v_blk, k_mean)
        xf = _moe_layer(x1, x1p, router_w[i], router_b[i], exp_w_gate[i], exp_w_up[i], exp_w_down[i],
                        sh_w_gate[i], sh_w_up[i], sh_w_down[i], ln_g[i, 1], ln_b[i, 1],
                        pf[i], ple_w[i], ple_gate_w[i])
    return xf.reshape(batch, seq, d)
```

```python
import functools

import jax
import jax.numpy as jnp
from jax import lax
from jax.experimental import pallas as pl
from jax.experimental.pallas import tpu as pltpu
from jax.experimental.pallas import tpu_sc as plsc

F32 = jnp.float32
BF16 = jnp.bfloat16
I32 = jnp.int32
U32 = jnp.uint32
HIGH_HALF = 0xFFFF0000

SC_CORES = 2
SC_SUBCORES = 16
SC_WORKERS = SC_CORES * SC_SUBCORES
SC_INDEX_CHUNK = 128
SC_STAGE_BYTES = 256 * 1024

D_MODEL = 1024
DEPTH = 4
N_A_LAYERS = DEPTH // 2
D_RNN = D_MODEL
LRU_BLOCKS = 4
LRU_BLOCK_W = D_RNN // LRU_BLOCKS
CONV_W = 4
LRU_C = 8.0
N_HEADS = 8
HEAD_DIM = D_MODEL // N_HEADS
ROT_DIM = HEAD_DIM // 4
ROPE_THETA = 500000.0
MOBA_BLOCK = 256
MOBA_TOPK = 3
N_EXPERTS = 64
N_GROUPS = 8
GROUP_SIZE = N_EXPERTS // N_GROUPS
TOPK_GROUPS = 4
TOPK_EXPERTS = 8
D_EXPERT = 256
D_SHARED = 256
ROUTED_SCALE = 2.5
PLE_DIM = 256
DN_ALPHA = (2 * DEPTH) ** 0.25
LN_EPS = 1e-5

V7X_VMEM_LIMIT_BYTES = 56 * 1024 * 1024

MIXER_ROWS = 256
TOKEN_ROWS = 512
EXPERT_ROWS = 1024
NEG_INF = float("-inf")


def _params(*sem):
    return pltpu.CompilerParams(dimension_semantics=sem, vmem_limit_bytes=V7X_VMEM_LIMIT_BYTES)


def _layer_norm(z, g, b):
    mu = jnp.mean(z, axis=-1, keepdims=True)
    zc = z - mu
    var = jnp.mean(zc * zc, axis=-1, keepdims=True)
    return zc * lax.rsqrt(var + LN_EPS) * g + b


def _silu(x):
    return x * jax.nn.sigmoid(x)


def _gelu_tanh(x):
    return x * jax.nn.sigmoid(x * (1.5957691216057308 + 0.07135481627159855 * (x * x)))


def _full(shape):
    return pl.BlockSpec(shape, lambda *_: (0,) * len(shape))


def _pack_bf16_pairs(x):
    w = x.shape[1] // 2
    bits = lax.bitcast_convert_type(x.astype(BF16).astype(F32), U32)
    return (bits[:, :w] >> 16) | (bits[:, w:] & U32(HIGH_HALF))


def _unpack_bf16_pairs(u):
    lo = lax.bitcast_convert_type(u << 16, F32)
    hi = lax.bitcast_convert_type(u & U32(HIGH_HALF), F32)
    return lo, hi


def _sc_mesh():
    return plsc.VectorSubcoreMesh(core_axis_name="c", subcore_axis_name="s")


def _sc_worker_id():
    return lax.axis_index("s") * SC_CORES + lax.axis_index("c")


def _sc_chunks_per_step(chunks_per_worker, row_words):
    g = max(1, SC_STAGE_BYTES // (SC_INDEX_CHUNK * row_words * 4))
    while chunks_per_worker % g:
        g -= 1
    return g


def _sc_gather_rows(table, idx):
    b = idx.shape[0]
    w = table.shape[1]
    chunk = SC_INDEX_CHUNK
    chunks_per_worker = b // (SC_WORKERS * chunk)
    assert chunks_per_worker * SC_WORKERS * chunk == b
    g = _sc_chunks_per_step(chunks_per_worker, w)

    @functools.partial(
        pl.kernel, mesh=_sc_mesh(), out_type=jax.ShapeDtypeStruct((b, w), table.dtype),
        scratch_types=[pltpu.VMEM((g, chunk), I32), pltpu.VMEM((g * chunk, w), table.dtype),
                       pltpu.SemaphoreType.DMA])
    def gather(table_hbm, idx_hbm, out_hbm, idx_v, rows_v, sem):
        first = _sc_worker_id() * chunks_per_worker

        @pl.loop(0, chunks_per_worker // g)
        def _(j):
            c0 = first + j * g
            pltpu.sync_copy(idx_hbm.at[pl.ds(c0, g)], idx_v)
            copies = [pltpu.async_copy(table_hbm.at[idx_v.at[q]], rows_v.at[pl.ds(q * chunk, chunk)], sem)
                      for q in range(g)]
            for cp in copies:
                cp.wait()
            pltpu.sync_copy(rows_v, out_hbm.at[pl.ds(c0 * chunk, g * chunk)])

    return gather(table, idx.reshape(b // chunk, chunk))


def _sc_scatter_rows(src, idx, out_rows):
    n, w = src.shape
    chunk = SC_INDEX_CHUNK
    fan = idx.shape[1]
    chunks_per_worker = n // (SC_WORKERS * chunk)
    assert idx.shape == (n // chunk, fan, chunk) and chunks_per_worker * SC_WORKERS * chunk == n
    g = _sc_chunks_per_step(chunks_per_worker, w)

    @functools.partial(
        pl.kernel, mesh=_sc_mesh(), out_type=jax.ShapeDtypeStruct((out_rows, w), src.dtype),
        scratch_types=[pltpu.VMEM((g, fan, chunk), I32), pltpu.VMEM((g * chunk, w), src.dtype),
                       pltpu.SemaphoreType.DMA])
    def scatter(src_hbm, idx_hbm, out_hbm, idx_v, rows_v, sem):
        first = _sc_worker_id() * chunks_per_worker

        @pl.loop(0, chunks_per_worker // g)
        def _(j):
            c0 = first + j * g
            pltpu.sync_copy(src_hbm.at[pl.ds(c0 * chunk, g * chunk)], rows_v)
            pltpu.sync_copy(idx_hbm.at[pl.ds(c0, g)], idx_v)
            copies = [pltpu.async_copy(rows_v.at[pl.ds(q * chunk, chunk)], out_hbm.at[idx_v.at[q, f]], sem)
                      for q in range(g) for f in range(fan)]
            for cp in copies:
                cp.wait()

    return scatter(src, idx)


def _rglru_kernel(x_ref, win_ref, bin_ref, cw_ref, cb_ref, gaw_ref, gab_ref, giw_ref, gib_ref,
                  lam_ref, wout_ref, lng_ref, lnb_ref, o_ref, op_ref, tail_ref, h_ref, *, tiles_per_seq):
    i = pl.program_id(0)
    tm = x_ref.shape[0]

    @pl.when(i % tiles_per_seq == 0)
    def _():
        tail_ref[...] = jnp.zeros_like(tail_ref)
        h_ref[...] = jnp.zeros_like(h_ref)

    x = x_ref[...]
    xy = jnp.dot(x.astype(BF16), win_ref[...], preferred_element_type=F32) + bin_ref[...]
    xb = xy[:, :D_RNN]
    y = _gelu_tanh(xy[:, D_RNN:])

    tail = tail_ref[...]
    row8 = lax.broadcasted_iota(I32, (8, 1), 0)
    xc = cb_ref[...] + xb * cw_ref[CONV_W - 1:CONV_W, :]
    for d in range(1, CONV_W):
        rolled = pltpu.roll(xb, d, 0)
        head = jnp.where(row8 < d, pltpu.roll(tail, d, 0), rolled[:8])
        shifted = jnp.concatenate([head, rolled[8:]], axis=0)
        xc = xc + shifted * cw_ref[CONV_W - 1 - d:CONV_W - d, :]
    tail_ref[...] = xb[tm - 8:, :]

    r_parts, i_parts = [], []
    for n in range(LRU_BLOCKS):
        xg = xc[:, n * LRU_BLOCK_W:(n + 1) * LRU_BLOCK_W].astype(BF16)
        r_parts.append(jnp.dot(xg, gaw_ref[n], preferred_element_type=F32))
        i_parts.append(jnp.dot(xg, giw_ref[n], preferred_element_type=F32))
    r = jax.nn.sigmoid(jnp.concatenate(r_parts, axis=1) + gab_ref[...])
    ig = jax.nn.sigmoid(jnp.concatenate(i_parts, axis=1) + gib_ref[...])

    lam = lam_ref[...]
    softplus_neg_lam = jnp.maximum(-lam, 0.0) + jnp.log1p(jnp.exp(-jnp.abs(lam)))
    log_a = (-LRU_C * r) * softplus_neg_lam
    a = jnp.exp(log_a)
    u = jnp.sqrt(1.0 - a * a) * (ig * xc)

    n_grp = tm // 8
    sub = lax.broadcasted_iota(I32, (1, 8, 1), 1)
    acc_a = a.reshape(n_grp, 8, D_RNN)
    acc_h = u.reshape(n_grp, 8, D_RNN)
    for d in (1, 2, 4):
        keep = sub >= d
        sh_a = pltpu.roll(acc_a, d, 1)
        sh_h = pltpu.roll(acc_h, d, 1)
        acc_h = jnp.where(keep, acc_a * sh_h + acc_h, acc_h)
        acc_a = jnp.where(keep, acc_a * sh_a, acc_a)
    state = h_ref[...]
    groups = []
    for g in range(n_grp):
        hg = acc_h[g] + acc_a[g] * state
        state = hg[7:8, :]
        groups.append(hg)
    h = jnp.concatenate(groups, axis=0)
    h_ref[...] = state

    mix = jnp.dot((h * y).astype(BF16), wout_ref[...], preferred_element_type=F32)
    x1 = _layer_norm(DN_ALPHA * x + mix, lng_ref[...], lnb_ref[...])
    o_ref[...] = x1
    op_ref[...] = _pack_bf16_pairs(x1)


def _rglru_layer(x, seq, w_in, b_in, conv_w, conv_b, ga_w, ga_b, gi_w, gi_b, lam, w_out, ln_g, ln_b):
    n, d = x.shape
    tm = min(MIXER_ROWS, seq)
    row = lambda v: v.reshape(1, -1).astype(F32)
    return pl.pallas_call(
        functools.partial(_rglru_kernel, tiles_per_seq=seq // tm),
        grid=(n // tm,),
        in_specs=[pl.BlockSpec((tm, d), lambda i: (i, 0)),
                  _full((d, 2 * D_RNN)), _full((1, 2 * D_RNN)),
                  _full((CONV_W, D_RNN)), _full((1, D_RNN)),
                  _full((LRU_BLOCKS, LRU_BLOCK_W, LRU_BLOCK_W)), _full((1, D_RNN)),
                  _full((LRU_BLOCKS, LRU_BLOCK_W, LRU_BLOCK_W)), _full((1, D_RNN)),
                  _full((1, D_RNN)), _full((D_RNN, d)), _full((1, d)), _full((1, d))],
        out_specs=[pl.BlockSpec((tm, d), lambda i: (i, 0)), pl.BlockSpec((tm, d // 2), lambda i: (i, 0))],
        out_shape=[jax.ShapeDtypeStruct((n, d), F32), jax.ShapeDtypeStruct((n, d // 2), U32)],
        scratch_shapes=[pltpu.VMEM((8, D_RNN), F32), pltpu.VMEM((1, D_RNN), F32)],
        compiler_params=_params("arbitrary"),
        name="rglru_mixer",
    )(x, w_in.astype(BF16), row(b_in), conv_w, row(conv_b), ga_w.astype(BF16), row(ga_b),
      gi_w.astype(BF16), row(gi_b), row(lam), w_out.astype(BF16), row(ln_g), row(ln_b))


def _first_argmax(cur, idx, size, axis):
    m = jnp.max(cur, axis=axis, keepdims=True)
    first = jnp.min(jnp.where(cur == m, idx, size), axis=axis, keepdims=True)
    return m, first


def _router_kernel(x_ref, rw_ref, rb_ref, e_ref, w_ref, r_ref, cnt_ref, carry_ref):
    i = pl.program_id(0)
    tm = x_ref.shape[0]

    @pl.when(i == 0)
    def _():
        carry_ref[...] = jnp.zeros_like(carry_ref)

    logits = jnp.dot(x_ref[...], rw_ref[...], preferred_element_type=F32,
                     precision=lax.Precision.HIGHEST)
    scores = jax.nn.sigmoid(logits.T[:N_EXPERTS, :])
    choice = scores + rb_ref[...]

    c3 = choice.reshape(N_GROUPS, GROUP_SIZE, tm)
    in_grp = lax.broadcasted_iota(I32, c3.shape, 1)
    m1, i1 = _first_argmax(c3, in_grp, GROUP_SIZE, 1)
    m2 = jnp.max(jnp.where(in_grp == i1, NEG_INF, c3), axis=1, keepdims=True)
    grp_score = (m1 + m2)[:, 0, :]

    grp_id = lax.broadcasted_iota(I32, grp_score.shape, 0)
    grp_sel = jnp.zeros(grp_score.shape, jnp.bool_)
    cur = grp_score
    for _ in range(TOPK_GROUPS):
        _, gi = _first_argmax(cur, grp_id, N_GROUPS, 0)
        hit = grp_id == gi
        grp_sel = jnp.logical_or(grp_sel, hit)
        cur = jnp.where(hit, NEG_INF, cur)

    cur = jnp.where(grp_sel[:, None, :], c3, NEG_INF).reshape(N_EXPERTS, tm)
    exp_id = lax.broadcasted_iota(I32, cur.shape, 0)
    sel = jnp.zeros(cur.shape, F32)
    e_rows, s_rows = [], []
    for _ in range(TOPK_EXPERTS):
        _, ei = _first_argmax(cur, exp_id, N_EXPERTS, 0)
        hit = exp_id == ei
        e_rows.append(ei)
        s_rows.append(jnp.sum(jnp.where(hit, scores, 0.0), axis=0, keepdims=True))
        sel = jnp.where(hit, 1.0, sel)
        cur = jnp.where(hit, NEG_INF, cur)
    e_top = jnp.concatenate(e_rows, axis=0)
    s_top = jnp.concatenate(s_rows, axis=0)
    w_ref[...] = s_top / jnp.sum(s_top, axis=0, keepdims=True) * ROUTED_SCALE
    e_ref[...] = e_top

    t_row = lax.broadcasted_iota(I32, (tm, tm), 0)
    t_col = lax.broadcasted_iota(I32, (tm, tm), 1)
    before = (t_row < t_col).astype(BF16)
    cum = jnp.dot(sel.astype(BF16), before, preferred_element_type=F32) + carry_ref[...]
    r_rows = [jnp.sum(jnp.where(exp_id == e_rows[k], cum, 0.0), axis=0, keepdims=True)
              for k in range(TOPK_EXPERTS)]
    r_ref[...] = jnp.concatenate(r_rows, axis=0).astype(I32)
    carry_ref[...] = carry_ref[...] + jnp.sum(sel, axis=1, keepdims=True)
    cnt_ref[...] = jnp.broadcast_to(carry_ref[...], cnt_ref.shape).astype(I32)


def _router(x, router_w, router_b):
    n, d = x.shape
    tm = min(TOKEN_ROWS, n)
    rw = jnp.pad(router_w, ((0, 0), (0, 128 - N_EXPERTS)))
    k = TOPK_EXPERTS
    tok_spec = pl.BlockSpec((k, tm), lambda i: (0, i))
    return pl.pallas_call(
        _router_kernel,
        grid=(n // tm,),
        in_specs=[pl.BlockSpec((tm, d), lambda i: (i, 0)), _full((d, 128)), _full((N_EXPERTS, 1))],
        out_specs=[tok_spec, tok_spec, tok_spec, _full((N_EXPERTS, 128))],
        out_shape=[jax.ShapeDtypeStruct((k, n), I32), jax.ShapeDtypeStruct((k, n), F32),
                   jax.ShapeDtypeStruct((k, n), I32), jax.ShapeDtypeStruct((N_EXPERTS, 128), I32)],
        scratch_shapes=[pltpu.VMEM((N_EXPERTS, 1), F32)],
        compiler_params=_params("arbitrary"),
        name="moe_router",
    )(x, rw, router_b.reshape(N_EXPERTS, 1).astype(F32))


def _expert_kernel(blk_e_ref, nblk_ref, xs_ref, wg_ref, wu_ref, wd_ref, ys_ref, wg_s, wu_s, wd_s):
    i = pl.program_id(0)

    @pl.when(i < nblk_ref[0])
    def _():
        @pl.when(jnp.logical_or(i == 0, blk_e_ref[i] != blk_e_ref[jnp.maximum(i - 1, 0)]))
        def _():
            wg_s[...] = wg_ref[0, 0].astype(BF16)
            wu_s[...] = wu_ref[0, 0].astype(BF16)
            wd_s[...] = wd_ref[0, 0].astype(BF16)

        lo, hi = _unpack_bf16_pairs(xs_ref[...])
        xs = jnp.concatenate([lo.astype(BF16), hi.astype(BF16)], axis=1)
        g = jnp.dot(xs, wg_s[...], preferred_element_type=F32)
        u = jnp.dot(xs, wu_s[...], preferred_element_type=F32)
        hdn = (_silu(g) * u).astype(BF16)
        ys_ref[...] = _pack_bf16_pairs(jnp.dot(hdn, wd_s[...], preferred_element_type=F32))


def _experts(xs, blk_e, nblk, layer, w_gate, w_up, w_down):
    rows, half = xs.shape
    d = 2 * half
    m = EXPERT_ROWS
    row_map = lambda i, be, nb: (jnp.minimum(i, nb[0] - 1), 0)
    w_map = lambda i, be, nb: (layer, be[i], 0, 0)
    return pl.pallas_call(
        _expert_kernel,
        grid_spec=pltpu.PrefetchScalarGridSpec(
            num_scalar_prefetch=2,
            grid=(rows // m,),
            in_specs=[pl.BlockSpec((m, half), row_map),
                      pl.BlockSpec((1, 1, d, D_EXPERT), w_map),
                      pl.BlockSpec((1, 1, d, D_EXPERT), w_map),
                      pl.BlockSpec((1, 1, D_EXPERT, d), w_map)],
            out_specs=pl.BlockSpec((m, half), row_map),
            scratch_shapes=[pltpu.VMEM((d, D_EXPERT), BF16), pltpu.VMEM((d, D_EXPERT), BF16),
                            pltpu.VMEM((D_EXPERT, d), BF16)]),
        out_shape=jax.ShapeDtypeStruct((rows, half), U32),
        compiler_params=_params("arbitrary"),
        name="moe_experts",
    )(blk_e, nblk, xs, w_gate, w_up, w_down)


def _combine_kernel(x_ref, yg_ref, w_ref, sg_ref, su_ref, sd_ref, lng_ref, lnb_ref,
                    p_ref, pw_ref, pg_ref, o_ref):
    x = x_ref[...]
    wt = w_ref[...].T
    moe_lo, moe_hi = None, None
    for k in range(TOPK_EXPERTS):
        lo, hi = _unpack_bf16_pairs(yg_ref[k])
        wk = wt[:, k:k + 1]
        moe_lo = wk * lo if k == 0 else moe_lo + wk * lo
        moe_hi = wk * hi if k == 0 else moe_hi + wk * hi
    moe = jnp.concatenate([moe_lo, moe_hi], axis=1)
    xb = x.astype(BF16)
    hdn = _silu(jnp.dot(xb, sg_ref[...], preferred_element_type=F32)) * \
        jnp.dot(xb, su_ref[...], preferred_element_type=F32)
    shared = jnp.dot(hdn.astype(BF16), sd_ref[...], preferred_element_type=F32)
    x2 = _layer_norm(DN_ALPHA * x + (moe + shared), lng_ref[...], lnb_ref[...])
    emb = jnp.dot(p_ref[...].astype(BF16), pw_ref[...], preferred_element_type=F32)
    gate = jax.nn.sigmoid(jnp.dot(x2.astype(BF16), pg_ref[...], preferred_element_type=F32))
    o_ref[...] = x2 + emb * gate


def _combine(x, yg, wgt, s_gate, s_up, s_down, ln_g, ln_b, p, ple_w, ple_gate_w):
    n, d = x.shape
    tm = min(TOKEN_ROWS // 2, n)
    k = TOPK_EXPERTS
    row = lambda v: v.reshape(1, -1).astype(F32)
    return pl.pallas_call(
        _combine_kernel,
        grid=(n // tm,),
        in_specs=[pl.BlockSpec((tm, d), lambda i: (i, 0)),
                  pl.BlockSpec((k, tm, d // 2), lambda i: (0, i, 0)),
                  pl.BlockSpec((k, tm), lambda i: (0, i)),
                  _full((d, D_SHARED)), _full((d, D_SHARED)), _full((D_SHARED, d)),
                  _full((1, d)), _full((1, d)),
                  pl.BlockSpec((tm, PLE_DIM), lambda i: (i, 0)),
                  _full((PLE_DIM, d)), _full((d, d))],
        out_specs=pl.BlockSpec((tm, d), lambda i: (i, 0)),
        out_shape=jax.ShapeDtypeStruct((n, d), F32),
        compiler_params=_params("parallel"),
        name="moe_combine",
    )(x, yg, wgt, s_gate.astype(BF16), s_up.astype(BF16), s_down.astype(BF16), row(ln_g), row(ln_b),
      p, ple_w.astype(BF16), ple_gate_w.astype(BF16))


def _positions_kernel(e_ref, r_ref, start_ref, pos_ref):
    e = e_ref[...]
    start = start_ref[...]
    grp_id = lax.broadcasted_iota(I32, (start.shape[0], e.shape[1]), 0)
    rows = [jnp.sum(jnp.where(grp_id == e[k:k + 1], start, 0), axis=0, keepdims=True)
            for k in range(e.shape[0])]
    pos_ref[...] = jnp.concatenate(rows, axis=0) + r_ref[...]


def _positions(e_idx, rank, starts):
    k, n = e_idx.shape
    tm = min(4 * TOKEN_ROWS, n)
    groups = starts.shape[0]
    tok_spec = pl.BlockSpec((k, tm), lambda i: (0, i))
    return pl.pallas_call(
        _positions_kernel,
        grid=(n // tm,),
        in_specs=[tok_spec, tok_spec, _full((groups, 1))],
        out_specs=tok_spec,
        out_shape=jax.ShapeDtypeStruct((k, n), I32),
        compiler_params=_params("parallel"),
        name="group_positions",
    )(e_idx, rank, starts.reshape(groups, 1))


def _moe_layer(x, xp, layer, router_w, router_b, w_gate, w_up, w_down, s_gate, s_up, s_down,
               ln_g, ln_b, p, ple_w, ple_gate_w):
    n, d = x.shape
    m = EXPERT_ROWS
    k = TOPK_EXPERTS
    e_idx, wgt, rank, cnt = _router(x, router_w, router_b)
    counts = cnt[:, 0]
    padded = (counts + m - 1) // m * m
    pends = jnp.cumsum(padded)
    n_blocks = (n * k) // m + N_EXPERTS
    blk_first = jnp.arange(n_blocks, dtype=I32) * m
    blk_e = jnp.minimum(jnp.sum((pends[None, :] <= blk_first[:, None]).astype(I32), axis=1), N_EXPERTS - 1)
    nblk = (pends[-1] // m).astype(I32).reshape(1)
    pos = _positions(e_idx, rank, (pends - padded).astype(I32))
    chunk = SC_INDEX_CHUNK
    pos_chunks = pos.reshape(k, n // chunk, chunk).transpose(1, 0, 2)
    xs = _sc_scatter_rows(xp, pos_chunks, n_blocks * m)
    ys = _experts(xs, blk_e, nblk, layer, w_gate, w_up, w_down)
    yg = _sc_gather_rows(ys, pos.reshape(-1)).reshape(k, n, d // 2)
    return _combine(x, yg, wgt, s_gate, s_up, s_down, ln_g, ln_b, p, ple_w, ple_gate_w)


def _rope(t, cos_t, sin_t):
    half = ROT_DIM // 2
    width = t.shape[1]
    lane = lax.broadcasted_iota(I32, (1, width), 1) % HEAD_DIM
    partner = jnp.where(lane < half, pltpu.roll(t, width - half, 1), pltpu.roll(t, half, 1))
    cos_f = jnp.concatenate([cos_t] * N_HEADS, axis=1)
    sin_f = jnp.concatenate([sin_t] * N_HEADS, axis=1)
    return t * cos_f + partner * sin_f


def _kv_kernel(x_ref, g_ref, b_ref, wkv_ref, cos_ref, sin_ref, k_ref, v_ref, km_ref):
    h = _layer_norm(x_ref[...], g_ref[...], b_ref[...])
    kv = jnp.dot(h.astype(BF16), wkv_ref[...], preferred_element_type=F32)
    k = _rope(kv[:, :D_MODEL], cos_ref[...], sin_ref[...])
    v = kv[:, D_MODEL:]
    km_ref[0] = jnp.mean(k, axis=0, keepdims=True)
    for hd in range(N_HEADS):
        k_ref[hd, 0] = k[:, hd * HEAD_DIM:(hd + 1) * HEAD_DIM].astype(BF16)
        v_ref[hd, 0] = v[:, hd * HEAD_DIM:(hd + 1) * HEAD_DIM].astype(BF16)


def _shared_kv(x, seq, ln_g, ln_b, w_kv, cos_t, sin_t):
    n, d = x.shape
    blk = MOBA_BLOCK
    nbt = n // blk
    spb = seq // blk
    row = lambda v: v.reshape(1, -1).astype(F32)
    return pl.pallas_call(
        _kv_kernel,
        grid=(nbt,),
        in_specs=[pl.BlockSpec((blk, d), lambda i: (i, 0)), _full((1, d)), _full((1, d)),
                  _full((d, 2 * d)),
                  pl.BlockSpec((blk, HEAD_DIM), lambda i: (i % spb, 0)),
                  pl.BlockSpec((blk, HEAD_DIM), lambda i: (i % spb, 0))],
        out_specs=[pl.BlockSpec((N_HEADS, 1, blk, HEAD_DIM), lambda i: (0, i, 0, 0)),
                   pl.BlockSpec((N_HEADS, 1, blk, HEAD_DIM), lambda i: (0, i, 0, 0)),
                   pl.BlockSpec((1, 1, d), lambda i: (i, 0, 0))],
        out_shape=[jax.ShapeDtypeStruct((N_HEADS, nbt, blk, HEAD_DIM), BF16),
                   jax.ShapeDtypeStruct((N_HEADS, nbt, blk, HEAD_DIM), BF16),
                   jax.ShapeDtypeStruct((nbt, 1, d), F32)],
        compiler_params=_params("parallel"),
        name="shared_kv",
    )(x, row(ln_g), row(ln_b), w_kv.astype(BF16), cos_t, sin_t)


SEL_ROWS = 8


def _query_kernel(x_ref, wq_ref, cos_ref, sin_ref, km_ref, qp_ref, sel_ref, cnt_ref, carry_ref,
                  *, blocks_per_seq):
    i = pl.program_id(0)
    tm = x_ref.shape[0]
    nb = km_ref.shape[0]
    own = i % blocks_per_seq

    @pl.when(own == 0)
    def _():
        carry_ref[...] = jnp.zeros_like(carry_ref)

    q = jnp.dot(x_ref[...].astype(BF16), wq_ref[...], preferred_element_type=F32)
    q = _rope(q, cos_ref[...], sin_ref[...]) * (HEAD_DIM ** -0.5)

    blk_id = lax.broadcasted_iota(I32, (nb, tm), 0)
    t_row = lax.broadcasted_iota(I32, (tm, tm), 0)
    t_col = lax.broadcasted_iota(I32, (tm, tm), 1)
    before = (t_row < t_col).astype(BF16)
    km = km_ref[...]
    for hd in range(N_HEADS):
        lo, hi = hd * HEAD_DIM, (hd + 1) * HEAD_DIM
        q_h = q[:, lo:hi]
        qp_ref[0, hd] = q_h
        gate = lax.dot_general(km[:, lo:hi], q_h, (((1,), (1,)), ((), ())),
                               preferred_element_type=F32, precision=lax.Precision.HIGHEST)
        cur = jnp.where(blk_id < own, gate, NEG_INF)
        sel_rows = []
        chosen = jnp.zeros((nb, tm), F32)
        for _ in range(MOBA_TOPK):
            m, bi = _first_argmax(cur, blk_id, nb, 0)
            valid = m > NEG_INF
            hit = jnp.logical_and(blk_id == bi, valid)
            sel_rows.append(jnp.where(valid, bi, -1))
            chosen = jnp.where(hit, 1.0, chosen)
            cur = jnp.where(blk_id == bi, NEG_INF, cur)
        carry = carry_ref[hd * nb:(hd + 1) * nb, :]
        cum = jnp.dot(chosen.astype(BF16), before, preferred_element_type=F32) + carry
        rank_rows = [jnp.sum(jnp.where(blk_id == s, cum, 0.0), axis=0, keepdims=True).astype(I32)
                     for s in sel_rows]
        carry_ref[hd * nb:(hd + 1) * nb, :] = carry + jnp.sum(chosen, axis=1, keepdims=True)
        pad = jnp.zeros((SEL_ROWS - 2 * MOBA_TOPK, tm), I32)
        sel_ref[hd * SEL_ROWS:(hd + 1) * SEL_ROWS, :] = jnp.concatenate(sel_rows + rank_rows + [pad], axis=0)
    cnt_ref[0] = jnp.broadcast_to(carry_ref[...], cnt_ref.shape[1:]).astype(I32)


def _queries(x, seq, w_q, cos_t, sin_t, k_mean):
    n, d = x.shape
    blk = MOBA_BLOCK
    nb = seq // blk
    return pl.pallas_call(
        functools.partial(_query_kernel, blocks_per_seq=nb),
        grid=(n // blk,),
        in_specs=[pl.BlockSpec((blk, d), lambda i: (i, 0)), _full((d, d)),
                  pl.BlockSpec((blk, HEAD_DIM), lambda i: (i % nb, 0)),
                  pl.BlockSpec((blk, HEAD_DIM), lambda i: (i % nb, 0)),
                  pl.BlockSpec((nb, d), lambda i: (i // nb, 0))],
        out_specs=[pl.BlockSpec((1, N_HEADS, blk, HEAD_DIM), lambda i: (i, 0, 0, 0)),
                   pl.BlockSpec((N_HEADS * SEL_ROWS, blk), lambda i: (0, i)),
                   pl.BlockSpec((1, N_HEADS * nb, 128), lambda i: (i // nb, 0, 0))],
        out_shape=[jax.ShapeDtypeStruct((n // blk, N_HEADS, blk, HEAD_DIM), F32),
                   jax.ShapeDtypeStruct((N_HEADS * SEL_ROWS, n), I32),
                   jax.ShapeDtypeStruct((n // seq, N_HEADS * nb, 128), I32)],
        scratch_shapes=[pltpu.VMEM((N_HEADS * nb, 1), F32)],
        compiler_params=_params("arbitrary"),
        name="moba_queries",
    )(x, w_q.astype(BF16), cos_t, sin_t, k_mean)


def _moba_positions_kernel(sel_ref, start_ref, pos_ref, *, dump_row):
    tm = sel_ref.shape[1]
    nb = start_ref.shape[1] // N_HEADS
    blk_id = lax.broadcasted_iota(I32, (nb, tm), 0)
    dump = dump_row + lax.broadcasted_iota(I32, (1, tm), 1) % SC_INDEX_CHUNK
    rows = []
    for hd in range(N_HEADS):
        start = start_ref[0, hd * nb:(hd + 1) * nb, :]
        for s in range(MOBA_TOPK):
            sel = sel_ref[hd * SEL_ROWS + s:hd * SEL_ROWS + s + 1, :]
            rank = sel_ref[hd * SEL_ROWS + MOBA_TOPK + s:hd * SEL_ROWS + MOBA_TOPK + s + 1, :]
            base = jnp.sum(jnp.where(blk_id == sel, start, 0), axis=0, keepdims=True)
            rows.append(jnp.where(sel >= 0, base + rank, dump))
        rows.extend([dump] * (SEL_ROWS - MOBA_TOPK))
    pos_ref[...] = jnp.concatenate(rows, axis=0)


def _moba_positions(sel, starts, seq, dump_row):
    rows, n = sel.shape
    tm = min(4 * TOKEN_ROWS, seq)
    tps = seq // tm
    groups = starts.shape[1]
    return pl.pallas_call(
        functools.partial(_moba_positions_kernel, dump_row=dump_row),
        grid=(n // tm,),
        in_specs=[pl.BlockSpec((rows, tm), lambda i: (0, i)),
                  pl.BlockSpec((1, groups, 1), lambda i: (i // tps, 0, 0))],
        out_specs=pl.BlockSpec((rows, tm), lambda i: (0, i)),
        out_shape=jax.ShapeDtypeStruct((rows, n), I32),
        compiler_params=_params("parallel"),
        name="moba_positions",
    )(sel, starts)


ATTN_SUB_ROWS = 128
ATTN_SUBS_PER_STEP = 8
ATTN_STEP_ROWS = ATTN_SUB_ROWS * ATTN_SUBS_PER_STEP


def _softmax_numerator(q, k, keep, keep_t):
    nt = (((1,), (1,)), ((), ()))
    s = lax.dot_general(q, k, nt, preferred_element_type=F32)
    st = lax.dot_general(k, q, nt, preferred_element_type=F32)
    if keep is not None:
        s = jnp.where(keep, s, NEG_INF)
        st = jnp.where(keep_t, st, NEG_INF)
    m_lane = jnp.max(st, axis=0, keepdims=True)
    m = jnp.broadcast_to(m_lane, (8, q.shape[0])).T[:, 0:1]
    return m, jnp.exp(s - m).astype(BF16)


def _group_attn_kernel(step_h_ref, step_b_ref, sub_j_ref, nsteps_ref, qs_ref, k_ref, v_ref, part_ref):
    del step_h_ref, step_b_ref
    i = pl.program_id(0)
    half = HEAD_DIM // 2

    @pl.when(i < nsteps_ref[0])
    def _():
        for u in range(ATTN_SUBS_PER_STEP):
            j = sub_j_ref[i * ATTN_SUBS_PER_STEP + u]
            rows = pl.ds(u * ATTN_SUB_ROWS, ATTN_SUB_ROWS)
            q = qs_ref[rows, :].astype(BF16)
            s = lax.dot_general(q, k_ref[0, j], (((1,), (1,)), ((), ())), preferred_element_type=F32)
            m = jnp.max(s, axis=1, keepdims=True)
            p = jnp.exp(s - m)
            l = jnp.sum(p, axis=1, keepdims=True)
            o = jnp.dot(p.astype(BF16), v_ref[0, j], preferred_element_type=F32) / l
            lse = lax.bitcast_convert_type(m + jnp.log(l), U32)
            part_ref[rows, :] = jnp.concatenate(
                [_pack_bf16_pairs(o), jnp.broadcast_to(lse, (ATTN_SUB_ROWS, half))], axis=1)


def _group_attention(qs, k_blk, v_blk, step_h, step_b, sub_j, nsteps, nb):
    n_steps = step_h.shape[0]
    half = HEAD_DIM // 2
    row_map = lambda i, sh, sb, sj, ns: (jnp.minimum(i, ns[0] - 1), 0)
    kv_map = lambda i, sh, sb, sj, ns: (sh[i], sb[i], 0, 0)
    return pl.pallas_call(
        _group_attn_kernel,
        grid_spec=pltpu.PrefetchScalarGridSpec(
            num_scalar_prefetch=4,
            grid=(n_steps,),
            in_specs=[pl.BlockSpec((ATTN_STEP_ROWS, HEAD_DIM), row_map),
                      pl.BlockSpec((1, nb, MOBA_BLOCK, HEAD_DIM), kv_map),
                      pl.BlockSpec((1, nb, MOBA_BLOCK, HEAD_DIM), kv_map)],
            out_specs=pl.BlockSpec((ATTN_STEP_ROWS, HEAD_DIM), row_map)),
        out_shape=jax.ShapeDtypeStruct(((n_steps + 1) * ATTN_STEP_ROWS, HEAD_DIM), U32),
        compiler_params=_params("arbitrary"),
        name="moba_group_attention",
    )(step_h, step_b, sub_j, nsteps, qs, k_blk, v_blk)


def _attn_merge_kernel(x_ref, qp_ref, k_ref, v_ref, pg_ref, sel_ref, wo_ref, lng_ref, lnb_ref,
                       o_ref, op_ref):
    tm = x_ref.shape[0]
    half = HEAD_DIM // 2
    sel_t = sel_ref[...].T
    row = lax.broadcasted_iota(I32, (tm, tm), 0)
    col = lax.broadcasted_iota(I32, (tm, tm), 1)
    ones = jnp.ones((tm, HEAD_DIM), BF16)
    heads = []
    for hd in range(N_HEADS):
        q = qp_ref[0, hd].astype(BF16)
        k = k_ref[hd, 0]
        m_own, p = _softmax_numerator(q, k, col <= row, row <= col)
        l_own = jnp.dot(p, ones, preferred_element_type=F32)[:, 0:1]
        acc = jnp.dot(p, v_ref[hd, 0], preferred_element_type=F32)

        outs, lses = [], []
        m_tot = m_own
        for c in range(MOBA_TOPK):
            part = pg_ref[c, 0, hd]
            lo, hi = _unpack_bf16_pairs(part[:, :half])
            outs.append(jnp.concatenate([lo, hi], axis=1))
            lse = lax.bitcast_convert_type(part[:, half:half + 1], F32)
            valid = sel_t[:, hd * SEL_ROWS + c:hd * SEL_ROWS + c + 1] >= 0
            lse = jnp.where(valid, lse, NEG_INF)
            lses.append(lse)
            m_tot = jnp.maximum(m_tot, lse)
        w_own = jnp.exp(m_own - m_tot)
        num = acc * w_own
        den = l_own * w_own
        for c in range(MOBA_TOPK):
            w_c = jnp.exp(lses[c] - m_tot)
            num = num + jnp.where(w_c > 0.0, outs[c], 0.0) * w_c
            den = den + w_c
        heads.append((num / den).astype(BF16))
    att = jnp.concatenate(heads, axis=1)
    mix = jnp.dot(att, wo_ref[...], preferred_element_type=F32)
    x1 = _layer_norm(DN_ALPHA * x_ref[...] + mix, lng_ref[...], lnb_ref[...])
    o_ref[...] = x1
    op_ref[...] = _pack_bf16_pairs(x1)


def _attn_merge(x, qp, k_blk, v_blk, pg, sel, w_o, ln_g, ln_b):
    n, d = x.shape
    blk = MOBA_BLOCK
    half = HEAD_DIM // 2
    row = lambda v: v.reshape(1, -1).astype(F32)
    kv_spec = pl.BlockSpec((N_HEADS, 1, blk, HEAD_DIM), lambda i: (0, i, 0, 0))
    return pl.pallas_call(
        _attn_merge_kernel,
        grid=(n // blk,),
        in_specs=[pl.BlockSpec((blk, d), lambda i: (i, 0)),
                  pl.BlockSpec((1, N_HEADS, blk, HEAD_DIM), lambda i: (i, 0, 0, 0)),
                  kv_spec, kv_spec,
                  pl.BlockSpec((MOBA_TOPK, 1, N_HEADS, blk, HEAD_DIM), lambda i: (0, i, 0, 0, 0)),
                  pl.BlockSpec((N_HEADS * SEL_ROWS, blk), lambda i: (0, i)),
                  _full((d, d)), _full((1, d)), _full((1, d))],
        out_specs=[pl.BlockSpec((blk, d), lambda i: (i, 0)), pl.BlockSpec((blk, d // 2), lambda i: (i, 0))],
        out_shape=[jax.ShapeDtypeStruct((n, d), F32), jax.ShapeDtypeStruct((n, d // 2), U32)],
        compiler_params=_params("parallel"),
        name="moba_merge",
    )(x, qp, k_blk, v_blk, pg, sel, w_o.astype(BF16), row(ln_g), row(ln_b))


def _moba_layer(x, seq, w_q, w_o, ln_g, ln_b, cos_t, sin_t, k_blk, v_blk, k_mean):
    n, d = x.shape
    batch = n // seq
    blk = MOBA_BLOCK
    nb = seq // blk
    nbt = n // blk
    chunk = SC_INDEX_CHUNK
    qp, sel, cnt = _queries(x, seq, w_q, cos_t, sin_t, k_mean)

    counts = cnt[:, :, 0].reshape(batch * N_HEADS, nb)
    gpad = (counts + ATTN_SUB_ROWS - 1) // ATTN_SUB_ROWS * ATTN_SUB_ROWS
    seg = jnp.sum(gpad, axis=1)
    seg_pad = (seg + ATTN_STEP_ROWS - 1) // ATTN_STEP_ROWS * ATTN_STEP_ROWS
    seg_end = jnp.cumsum(seg_pad)
    gend = (seg_end - seg_pad)[:, None] + jnp.cumsum(gpad, axis=1)
    gstart = (gend - gpad).astype(I32)
    steps_per_seg = -(-(MOBA_TOPK * seq + nb * (ATTN_SUB_ROWS - 1)) // ATTN_STEP_ROWS)
    n_steps = batch * N_HEADS * steps_per_seg
    step_first = jnp.arange(n_steps, dtype=I32) * ATTN_STEP_ROWS
    step_seg = jnp.minimum(jnp.sum((seg_end[None, :] <= step_first[:, None]).astype(I32), axis=1),
                           batch * N_HEADS - 1)
    sub_first = jnp.arange(n_steps * ATTN_SUBS_PER_STEP, dtype=I32) * ATTN_SUB_ROWS
    sub_grp = jnp.minimum(jnp.sum((gend.reshape(-1)[None, :] <= sub_first[:, None]).astype(I32), axis=1),
                          batch * N_HEADS * nb - 1)
    nsteps = (seg_end[-1] // ATTN_STEP_ROWS).astype(I32).reshape(1)
    dump_row = n_steps * ATTN_STEP_ROWS

    pos = _moba_positions(sel, gstart.reshape(batch, N_HEADS * nb, 1), seq, dump_row)
    pos5 = pos.reshape(N_HEADS, SEL_ROWS, nbt, blk // chunk, chunk)[:, :MOBA_TOPK]
    scatter_idx = pos5.transpose(2, 0, 3, 1, 4).reshape(nbt * N_HEADS * (blk // chunk), MOBA_TOPK, chunk)
    gather_idx = pos5.transpose(1, 2, 0, 3, 4).reshape(-1)

    qs = _sc_scatter_rows(qp.reshape(n * N_HEADS, HEAD_DIM), scatter_idx, dump_row + chunk)
    part = _group_attention(qs, k_blk, v_blk, (step_seg % N_HEADS).astype(I32),
                            (step_seg // N_HEADS).astype(I32), (sub_grp % nb).astype(I32), nsteps, nb)
    pg = _sc_gather_rows(part, gather_idx).reshape(MOBA_TOPK, nbt, N_HEADS, blk, HEAD_DIM)
    return _attn_merge(x, qp, k_blk, v_blk, pg, sel, w_o, ln_g, ln_b)


def _rope_tables(seq):
    half = ROT_DIM // 2
    inv = ROPE_THETA ** (-jnp.arange(0, ROT_DIM, 2, dtype=F32) / ROT_DIM)
    ang = jnp.arange(seq, dtype=F32)[:, None] * inv[None, :]
    cos, sin = jnp.cos(ang), jnp.sin(ang)
    rest = HEAD_DIM - ROT_DIM
    cos_t = jnp.concatenate([cos, cos, jnp.ones((seq, rest), F32)], axis=1)
    sin_t = jnp.concatenate([-sin, sin, jnp.zeros((seq, rest), F32)], axis=1)
    del half
    return cos_t, sin_t


def kernel(x, p, ln_g, ln_b, a_w_in, a_b_in, a_conv_w, a_conv_b, a_gate_a_w, a_gate_a_b, a_gate_i_w,
           a_gate_i_b, a_lambda, a_w_out, kv_ln_g, kv_ln_b, w_kv, b_w_q, b_w_o, router_w, router_b,
           exp_w_gate, exp_w_up, exp_w_down, sh_w_gate, sh_w_up, sh_w_down, ple_w, ple_gate_w):
    batch, seq, d = x.shape
    n = batch * seq
    xf = x.reshape(n, d)
    pf = p.reshape(DEPTH, n, PLE_DIM)
    cos_t, sin_t = _rope_tables(seq)
    k_blk = v_blk = k_mean = None
    for i in range(DEPTH):
        if i < N_A_LAYERS:
            x1, x1p = _rglru_layer(xf, seq, a_w_in[i], a_b_in[i], a_conv_w[i], a_conv_b[i],
                                   a_gate_a_w[i], a_gate_a_b[i], a_gate_i_w[i], a_gate_i_b[i],
                                   a_lambda[i], a_w_out[i], ln_g[i, 0], ln_b[i, 0])
        else:
            if i == N_A_LAYERS:
                k_blk, v_blk, k_mean = _shared_kv(xf, seq, kv_ln_g, kv_ln_b, w_kv, cos_t, sin_t)
                k_mean = k_mean.reshape(n // MOBA_BLOCK, d)
            j = i - N_A_LAYERS
            x1, x1p = _moba_layer(xf, seq, b_w_q[j], b_w_o[j], ln_g[i, 0], ln_b[i, 0],
                                  cos_t, sin_t, k_blk, v_blk, k_mean)
        xf = _moe_layer(x1, x1p, i, router_w[i], router_b[i], exp_w_gate, exp_w_up, exp_w_down,
                        sh_w_gate[i], sh_w_up[i], sh_w_down[i], ln_g[i, 1], ln_b[i, 1],
                        pf[i], ple_w[i], ple_gate_w[i])
    return xf.reshape(batch, seq, d)
```

```python
import functools

import jax
import jax.numpy as jnp
from jax import lax
from jax.experimental import pallas as pl
from jax.experimental.pallas import tpu as pltpu
from jax.experimental.pallas import tpu_sc as plsc

F32 = jnp.float32
BF16 = jnp.bfloat16
I32 = jnp.int32
U32 = jnp.uint32
HIGH_HALF = 0xFFFF0000

SC_CORES = 2
SC_SUBCORES = 16
SC_WORKERS = SC_CORES * SC_SUBCORES
SC_INDEX_CHUNK = 128
SC_STAGE_BYTES = 256 * 1024

D_MODEL = 1024
DEPTH = 4
N_A_LAYERS = DEPTH // 2
D_RNN = D_MODEL
LRU_BLOCKS = 4
LRU_BLOCK_W = D_RNN // LRU_BLOCKS
CONV_W = 4
LRU_C = 8.0
N_HEADS = 8
HEAD_DIM = D_MODEL // N_HEADS
ROT_DIM = HEAD_DIM // 4
ROPE_THETA = 500000.0
MOBA_BLOCK = 256
MOBA_TOPK = 3
N_EXPERTS = 64
N_GROUPS = 8
GROUP_SIZE = N_EXPERTS // N_GROUPS
TOPK_GROUPS = 4
TOPK_EXPERTS = 8
D_EXPERT = 256
D_SHARED = 256
ROUTED_SCALE = 2.5
PLE_DIM = 256
DN_ALPHA = (2 * DEPTH) ** 0.25
LN_EPS = 1e-5

V7X_VMEM_LIMIT_BYTES = 56 * 1024 * 1024

MIXER_ROWS = 256
TOKEN_ROWS = 512
EXPERT_ROWS = 1024
NEG_INF = float("-inf")


def _params(*sem):
    return pltpu.CompilerParams(dimension_semantics=sem, vmem_limit_bytes=V7X_VMEM_LIMIT_BYTES)


def _layer_norm(z, g, b):
    mu = jnp.mean(z, axis=-1, keepdims=True)
    zc = z - mu
    var = jnp.mean(zc * zc, axis=-1, keepdims=True)
    return zc * lax.rsqrt(var + LN_EPS) * g + b


def _silu(x):
    return x * jax.nn.sigmoid(x)


def _gelu_tanh(x):
    return x * jax.nn.sigmoid(x * (1.5957691216057308 + 0.07135481627159855 * (x * x)))


def _full(shape):
    return pl.BlockSpec(shape, lambda *_: (0,) * len(shape))


def _pack_bf16_pairs(x):
    w = x.shape[1] // 2
    bits = lax.bitcast_convert_type(x.astype(BF16).astype(F32), U32)
    return (bits[:, :w] >> 16) | (bits[:, w:] & U32(HIGH_HALF))


def _unpack_bf16_pairs(u):
    lo = lax.bitcast_convert_type(u << 16, F32)
    hi = lax.bitcast_convert_type(u & U32(HIGH_HALF), F32)
    return lo, hi


def _sc_mesh():
    return plsc.VectorSubcoreMesh(core_axis_name="c", subcore_axis_name="s")


def _sc_worker_id():
    return lax.axis_index("s") * SC_CORES + lax.axis_index("c")


def _sc_chunks_per_step(chunks_per_worker, row_words):
    g = max(1, SC_STAGE_BYTES // (SC_INDEX_CHUNK * row_words * 4))
    while chunks_per_worker % g:
        g -= 1
    return g


def _sc_gather_rows(table, idx):
    b = idx.shape[0]
    w = table.shape[1]
    chunk = SC_INDEX_CHUNK
    chunks_per_worker = b // (SC_WORKERS * chunk)
    assert chunks_per_worker * SC_WORKERS * chunk == b
    g = _sc_chunks_per_step(chunks_per_worker, w)

    @functools.partial(
        pl.kernel, mesh=_sc_mesh(), out_type=jax.ShapeDtypeStruct((b, w), table.dtype),
        scratch_types=[pltpu.VMEM((g, chunk), I32), pltpu.VMEM((g * chunk, w), table.dtype),
                       pltpu.SemaphoreType.DMA])
    def gather(table_hbm, idx_hbm, out_hbm, idx_v, rows_v, sem):
        first = _sc_worker_id() * chunks_per_worker

        @pl.loop(0, chunks_per_worker // g)
        def _(j):
            c0 = first + j * g
            pltpu.sync_copy(idx_hbm.at[pl.ds(c0, g)], idx_v)
            copies = [pltpu.async_copy(table_hbm.at[idx_v.at[q]], rows_v.at[pl.ds(q * chunk, chunk)], sem)
                      for q in range(g)]
            for cp in copies:
                cp.wait()
            pltpu.sync_copy(rows_v, out_hbm.at[pl.ds(c0 * chunk, g * chunk)])

    return gather(table, idx.reshape(b // chunk, chunk))


def _sc_scatter_rows(src, idx, out_rows):
    n, w = src.shape
    chunk = SC_INDEX_CHUNK
    fan = idx.shape[1]
    chunks_per_worker = n // (SC_WORKERS * chunk)
    assert idx.shape == (n // chunk, fan, chunk) and chunks_per_worker * SC_WORKERS * chunk == n
    g = _sc_chunks_per_step(chunks_per_worker, w)

    @functools.partial(
        pl.kernel, mesh=_sc_mesh(), out_type=jax.ShapeDtypeStruct((out_rows, w), src.dtype),
        scratch_types=[pltpu.VMEM((g, fan, chunk), I32), pltpu.VMEM((g * chunk, w), src.dtype),
                       pltpu.SemaphoreType.DMA])
    def scatter(src_hbm, idx_hbm, out_hbm, idx_v, rows_v, sem):
        first = _sc_worker_id() * chunks_per_worker

        @pl.loop(0, chunks_per_worker // g)
        def _(j):
            c0 = first + j * g
            pltpu.sync_copy(src_hbm.at[pl.ds(c0 * chunk, g * chunk)], rows_v)
            pltpu.sync_copy(idx_hbm.at[pl.ds(c0, g)], idx_v)
            copies = [pltpu.async_copy(rows_v.at[pl.ds(q * chunk, chunk)], out_hbm.at[idx_v.at[q, f]], sem)
                      for q in range(g) for f in range(fan)]
            for cp in copies:
                cp.wait()

    return scatter(src, idx)


def _rglru_kernel(x_ref, win_ref, bin_ref, cw_ref, cb_ref, gaw_ref, gab_ref, giw_ref, gib_ref,
                  lam_ref, wout_ref, lng_ref, lnb_ref, o_ref, op_ref, tail_ref, h_ref, *, tiles_per_seq):
    i = pl.program_id(0)
    tm = x_ref.shape[0]

    @pl.when(i % tiles_per_seq == 0)
    def _():
        tail_ref[...] = jnp.zeros_like(tail_ref)
        h_ref[...] = jnp.zeros_like(h_ref)

    x = x_ref[...]
    xy = jnp.dot(x.astype(BF16), win_ref[...], preferred_element_type=F32) + bin_ref[...]
    xb = xy[:, :D_RNN]
    y = _gelu_tanh(xy[:, D_RNN:])

    tail = tail_ref[...]
    row8 = lax.broadcasted_iota(I32, (8, 1), 0)
    xc = cb_ref[...] + xb * cw_ref[CONV_W - 1:CONV_W, :]
    for d in range(1, CONV_W):
        rolled = pltpu.roll(xb, d, 0)
        head = jnp.where(row8 < d, pltpu.roll(tail, d, 0), rolled[:8])
        shifted = jnp.concatenate([head, rolled[8:]], axis=0)
        xc = xc + shifted * cw_ref[CONV_W - 1 - d:CONV_W - d, :]
    tail_ref[...] = xb[tm - 8:, :]

    r_parts, i_parts = [], []
    for n in range(LRU_BLOCKS):
        xg = xc[:, n * LRU_BLOCK_W:(n + 1) * LRU_BLOCK_W].astype(BF16)
        r_parts.append(jnp.dot(xg, gaw_ref[n], preferred_element_type=F32))
        i_parts.append(jnp.dot(xg, giw_ref[n], preferred_element_type=F32))
    r = jax.nn.sigmoid(jnp.concatenate(r_parts, axis=1) + gab_ref[...])
    ig = jax.nn.sigmoid(jnp.concatenate(i_parts, axis=1) + gib_ref[...])

    lam = lam_ref[...]
    softplus_neg_lam = jnp.maximum(-lam, 0.0) + jnp.log1p(jnp.exp(-jnp.abs(lam)))
    log_a = (-LRU_C * r) * softplus_neg_lam
    a = jnp.exp(log_a)
    u = jnp.sqrt(1.0 - a * a) * (ig * xc)

    n_grp = tm // 8
    sub = lax.broadcasted_iota(I32, (1, 8, 1), 1)
    acc_a = a.reshape(n_grp, 8, D_RNN)
    acc_h = u.reshape(n_grp, 8, D_RNN)
    for d in (1, 2, 4):
        keep = sub >= d
        sh_a = pltpu.roll(acc_a, d, 1)
        sh_h = pltpu.roll(acc_h, d, 1)
        acc_h = jnp.where(keep, acc_a * sh_h + acc_h, acc_h)
        acc_a = jnp.where(keep, acc_a * sh_a, acc_a)
    state = h_ref[...]
    groups = []
    for g in range(n_grp):
        hg = acc_h[g] + acc_a[g] * state
        state = hg[7:8, :]
        groups.append(hg)
    h = jnp.concatenate(groups, axis=0)
    h_ref[...] = state

    mix = jnp.dot((h * y).astype(BF16), wout_ref[...], preferred_element_type=F32)
    x1 = _layer_norm(DN_ALPHA * x + mix, lng_ref[...], lnb_ref[...])
    o_ref[...] = x1
    op_ref[...] = _pack_bf16_pairs(x1)


def _rglru_layer(x, seq, w_in, b_in, conv_w, conv_b, ga_w, ga_b, gi_w, gi_b, lam, w_out, ln_g, ln_b):
    n, d = x.shape
    tm = min(MIXER_ROWS, seq)
    row = lambda v: v.reshape(1, -1).astype(F32)
    return pl.pallas_call(
        functools.partial(_rglru_kernel, tiles_per_seq=seq // tm),
        grid=(n // tm,),
        in_specs=[pl.BlockSpec((tm, d), lambda i: (i, 0)),
                  _full((d, 2 * D_RNN)), _full((1, 2 * D_RNN)),
                  _full((CONV_W, D_RNN)), _full((1, D_RNN)),
                  _full((LRU_BLOCKS, LRU_BLOCK_W, LRU_BLOCK_W)), _full((1, D_RNN)),
                  _full((LRU_BLOCKS, LRU_BLOCK_W, LRU_BLOCK_W)), _full((1, D_RNN)),
                  _full((1, D_RNN)), _full((D_RNN, d)), _full((1, d)), _full((1, d))],
        out_specs=[pl.BlockSpec((tm, d), lambda i: (i, 0)), pl.BlockSpec((tm, d // 2), lambda i: (i, 0))],
        out_shape=[jax.ShapeDtypeStruct((n, d), F32), jax.ShapeDtypeStruct((n, d // 2), U32)],
        scratch_shapes=[pltpu.VMEM((8, D_RNN), F32), pltpu.VMEM((1, D_RNN), F32)],
        compiler_params=_params("arbitrary"),
        name="rglru_mixer",
    )(x, w_in.astype(BF16), row(b_in), conv_w, row(conv_b), ga_w.astype(BF16), row(ga_b),
      gi_w.astype(BF16), row(gi_b), row(lam), w_out.astype(BF16), row(ln_g), row(ln_b))


def _first_argmax(cur, idx, size, axis):
    m = jnp.max(cur, axis=axis, keepdims=True)
    first = jnp.min(jnp.where(cur == m, idx, size), axis=axis, keepdims=True)
    return m, first


def _router_kernel(x_ref, rw_ref, rb_ref, e_ref, w_ref, r_ref, cnt_ref, carry_ref):
    i = pl.program_id(0)
    tm = x_ref.shape[0]

    @pl.when(i == 0)
    def _():
        carry_ref[...] = jnp.zeros_like(carry_ref)

    logits = jnp.dot(x_ref[...], rw_ref[...], preferred_element_type=F32,
                     precision=lax.Precision.HIGHEST)
    scores = jax.nn.sigmoid(logits.T[:N_EXPERTS, :])
    choice = scores + rb_ref[...]

    c3 = choice.reshape(N_GROUPS, GROUP_SIZE, tm)
    in_grp = lax.broadcasted_iota(I32, c3.shape, 1)
    m1, i1 = _first_argmax(c3, in_grp, GROUP_SIZE, 1)
    m2 = jnp.max(jnp.where(in_grp == i1, NEG_INF, c3), axis=1, keepdims=True)
    grp_score = (m1 + m2)[:, 0, :]

    grp_id = lax.broadcasted_iota(I32, grp_score.shape, 0)
    grp_sel = jnp.zeros(grp_score.shape, jnp.bool_)
    cur = grp_score
    for _ in range(TOPK_GROUPS):
        _, gi = _first_argmax(cur, grp_id, N_GROUPS, 0)
        hit = grp_id == gi
        grp_sel = jnp.logical_or(grp_sel, hit)
        cur = jnp.where(hit, NEG_INF, cur)

    cur = jnp.where(grp_sel[:, None, :], c3, NEG_INF).reshape(N_EXPERTS, tm)
    exp_id = lax.broadcasted_iota(I32, cur.shape, 0)
    sel = jnp.zeros(cur.shape, F32)
    e_rows, s_rows = [], []
    for _ in range(TOPK_EXPERTS):
        _, ei = _first_argmax(cur, exp_id, N_EXPERTS, 0)
        hit = exp_id == ei
        e_rows.append(ei)
        s_rows.append(jnp.sum(jnp.where(hit, scores, 0.0), axis=0, keepdims=True))
        sel = jnp.where(hit, 1.0, sel)
        cur = jnp.where(hit, NEG_INF, cur)
    e_top = jnp.concatenate(e_rows, axis=0)
    s_top = jnp.concatenate(s_rows, axis=0)
    w_ref[...] = s_top / jnp.sum(s_top, axis=0, keepdims=True) * ROUTED_SCALE
    e_ref[...] = e_top

    t_row = lax.broadcasted_iota(I32, (tm, tm), 0)
    t_col = lax.broadcasted_iota(I32, (tm, tm), 1)
    before = (t_row < t_col).astype(BF16)
    cum = jnp.dot(sel.astype(BF16), before, preferred_element_type=F32) + carry_ref[...]
    r_rows = [jnp.sum(jnp.where(exp_id == e_rows[k], cum, 0.0), axis=0, keepdims=True)
              for k in range(TOPK_EXPERTS)]
    r_ref[...] = jnp.concatenate(r_rows, axis=0).astype(I32)
    carry_ref[...] = carry_ref[...] + jnp.sum(sel, axis=1, keepdims=True)
    cnt_ref[...] = jnp.broadcast_to(carry_ref[...], cnt_ref.shape).astype(I32)


def _router(x, router_w, router_b):
    n, d = x.shape
    tm = min(TOKEN_ROWS, n)
    rw = jnp.pad(router_w, ((0, 0), (0, 128 - N_EXPERTS)))
    k = TOPK_EXPERTS
    tok_spec = pl.BlockSpec((k, tm), lambda i: (0, i))
    return pl.pallas_call(
        _router_kernel,
        grid=(n // tm,),
        in_specs=[pl.BlockSpec((tm, d), lambda i: (i, 0)), _full((d, 128)), _full((N_EXPERTS, 1))],
        out_specs=[tok_spec, tok_spec, tok_spec, _full((N_EXPERTS, 128))],
        out_shape=[jax.ShapeDtypeStruct((k, n), I32), jax.ShapeDtypeStruct((k, n), F32),
                   jax.ShapeDtypeStruct((k, n), I32), jax.ShapeDtypeStruct((N_EXPERTS, 128), I32)],
        scratch_shapes=[pltpu.VMEM((N_EXPERTS, 1), F32)],
        compiler_params=_params("arbitrary"),
        name="moe_router",
    )(x, rw, router_b.reshape(N_EXPERTS, 1).astype(F32))


def _expert_kernel(blk_e_ref, nblk_ref, xs_ref, wg_ref, wu_ref, wd_ref, ys_ref, wg_s, wu_s, wd_s):
    i = pl.program_id(0)

    @pl.when(i < nblk_ref[0])
    def _():
        @pl.when(jnp.logical_or(i == 0, blk_e_ref[i] != blk_e_ref[jnp.maximum(i - 1, 0)]))
        def _():
            wg_s[...] = wg_ref[0, 0].astype(BF16)
            wu_s[...] = wu_ref[0, 0].astype(BF16)
            wd_s[...] = wd_ref[0, 0].astype(BF16)

        lo, hi = _unpack_bf16_pairs(xs_ref[...])
        xs = jnp.concatenate([lo.astype(BF16), hi.astype(BF16)], axis=1)
        g = jnp.dot(xs, wg_s[...], preferred_element_type=F32)
        u = jnp.dot(xs, wu_s[...], preferred_element_type=F32)
        hdn = (_silu(g) * u).astype(BF16)
        ys_ref[...] = _pack_bf16_pairs(jnp.dot(hdn, wd_s[...], preferred_element_type=F32))


def _experts(xs, blk_e, nblk, layer, w_gate, w_up, w_down):
    rows, half = xs.shape
    d = 2 * half
    m = EXPERT_ROWS
    row_map = lambda i, be, nb: (jnp.minimum(i, nb[0] - 1), 0)
    w_map = lambda i, be, nb: (layer, be[i], 0, 0)
    return pl.pallas_call(
        _expert_kernel,
        grid_spec=pltpu.PrefetchScalarGridSpec(
            num_scalar_prefetch=2,
            grid=(rows // m,),
            in_specs=[pl.BlockSpec((m, half), row_map),
                      pl.BlockSpec((1, 1, d, D_EXPERT), w_map),
                      pl.BlockSpec((1, 1, d, D_EXPERT), w_map),
                      pl.BlockSpec((1, 1, D_EXPERT, d), w_map)],
            out_specs=pl.BlockSpec((m, half), row_map),
            scratch_shapes=[pltpu.VMEM((d, D_EXPERT), BF16), pltpu.VMEM((d, D_EXPERT), BF16),
                            pltpu.VMEM((D_EXPERT, d), BF16)]),
        out_shape=jax.ShapeDtypeStruct((rows, half), U32),
        compiler_params=_params("arbitrary"),
        name="moe_experts",
    )(blk_e, nblk, xs, w_gate, w_up, w_down)


def _combine_kernel(x_ref, yg_ref, w_ref, sg_ref, su_ref, sd_ref, lng_ref, lnb_ref,
                    p_ref, pw_ref, pg_ref, o_ref):
    x = x_ref[...]
    wt = w_ref[...].T
    moe_lo, moe_hi = None, None
    for k in range(TOPK_EXPERTS):
        lo, hi = _unpack_bf16_pairs(yg_ref[k])
        wk = wt[:, k:k + 1]
        moe_lo = wk * lo if k == 0 else moe_lo + wk * lo
        moe_hi = wk * hi if k == 0 else moe_hi + wk * hi
    moe = jnp.concatenate([moe_lo, moe_hi], axis=1)
    xb = x.astype(BF16)
    hdn = _silu(jnp.dot(xb, sg_ref[...], preferred_element_type=F32)) * \
        jnp.dot(xb, su_ref[...], preferred_element_type=F32)
    shared = jnp.dot(hdn.astype(BF16), sd_ref[...], preferred_element_type=F32)
    x2 = _layer_norm(DN_ALPHA * x + (moe + shared), lng_ref[...], lnb_ref[...])
    emb = jnp.dot(p_ref[...].astype(BF16), pw_ref[...], preferred_element_type=F32)
    gate = jax.nn.sigmoid(jnp.dot(x2.astype(BF16), pg_ref[...], preferred_element_type=F32))
    o_ref[...] = x2 + emb * gate


def _combine(x, yg, wgt, s_gate, s_up, s_down, ln_g, ln_b, p, ple_w, ple_gate_w):
    n, d = x.shape
    tm = min(TOKEN_ROWS // 2, n)
    k = TOPK_EXPERTS
    row = lambda v: v.reshape(1, -1).astype(F32)
    return pl.pallas_call(
        _combine_kernel,
        grid=(n // tm,),
        in_specs=[pl.BlockSpec((tm, d), lambda i: (i, 0)),
                  pl.BlockSpec((k, tm, d // 2), lambda i: (0, i, 0)),
                  pl.BlockSpec((k, tm), lambda i: (0, i)),
                  _full((d, D_SHARED)), _full((d, D_SHARED)), _full((D_SHARED, d)),
                  _full((1, d)), _full((1, d)),
                  pl.BlockSpec((tm, PLE_DIM), lambda i: (i, 0)),
                  _full((PLE_DIM, d)), _full((d, d))],
        out_specs=pl.BlockSpec((tm, d), lambda i: (i, 0)),
        out_shape=jax.ShapeDtypeStruct((n, d), F32),
        compiler_params=_params("parallel"),
        name="moe_combine",
    )(x, yg, wgt, s_gate.astype(BF16), s_up.astype(BF16), s_down.astype(BF16), row(ln_g), row(ln_b),
      p, ple_w.astype(BF16), ple_gate_w.astype(BF16))


def _positions_kernel(e_ref, r_ref, start_ref, pos_ref):
    e = e_ref[...]
    start = start_ref[...]
    grp_id = lax.broadcasted_iota(I32, (start.shape[0], e.shape[1]), 0)
    rows = [jnp.sum(jnp.where(grp_id == e[k:k + 1], start, 0), axis=0, keepdims=True)
            for k in range(e.shape[0])]
    pos_ref[...] = jnp.concatenate(rows, axis=0) + r_ref[...]


def _positions(e_idx, rank, starts):
    k, n = e_idx.shape
    tm = min(4 * TOKEN_ROWS, n)
    groups = starts.shape[0]
    tok_spec = pl.BlockSpec((k, tm), lambda i: (0, i))
    return pl.pallas_call(
        _positions_kernel,
        grid=(n // tm,),
        in_specs=[tok_spec, tok_spec, _full((groups, 1))],
        out_specs=tok_spec,
        out_shape=jax.ShapeDtypeStruct((k, n), I32),
        compiler_params=_params("parallel"),
        name="group_positions",
    )(e_idx, rank, starts.reshape(groups, 1))


def _moe_layer(x, xp, layer, router_w, router_b, w_gate, w_up, w_down, s_gate, s_up, s_down,
               ln_g, ln_b, p, ple_w, ple_gate_w):
    n, d = x.shape
    m = EXPERT_ROWS
    k = TOPK_EXPERTS
    e_idx, wgt, rank, cnt = _router(x, router_w, router_b)
    counts = cnt[:, 0]
    padded = (counts + m - 1) // m * m
    pends = jnp.cumsum(padded)
    n_blocks = (n * k) // m + N_EXPERTS
    blk_first = jnp.arange(n_blocks, dtype=I32) * m
    blk_e = jnp.minimum(jnp.sum((pends[None, :] <= blk_first[:, None]).astype(I32), axis=1), N_EXPERTS - 1)
    nblk = (pends[-1] // m).astype(I32).reshape(1)
    pos = _positions(e_idx, rank, (pends - padded).astype(I32))
    chunk = SC_INDEX_CHUNK
    pos_chunks = pos.reshape(k, n // chunk, chunk).transpose(1, 0, 2)
    xs = _sc_scatter_rows(xp, pos_chunks, n_blocks * m)
    ys = _experts(xs, blk_e, nblk, layer, w_gate, w_up, w_down)
    yg = _sc_gather_rows(ys, pos.reshape(-1)).reshape(k, n, d // 2)
    return _combine(x, yg, wgt, s_gate, s_up, s_down, ln_g, ln_b, p, ple_w, ple_gate_w)


def _rope(t, cos_t, sin_t):
    half = ROT_DIM // 2
    width = t.shape[1]
    lane = lax.broadcasted_iota(I32, (1, width), 1) % HEAD_DIM
    partner = jnp.where(lane < half, pltpu.roll(t, width - half, 1), pltpu.roll(t, half, 1))
    cos_f = jnp.concatenate([cos_t] * N_HEADS, axis=1)
    sin_f = jnp.concatenate([sin_t] * N_HEADS, axis=1)
    return t * cos_f + partner * sin_f


def _kv_kernel(x_ref, g_ref, b_ref, wkv_ref, cos_ref, sin_ref, k_ref, vt_ref, km_ref):
    h = _layer_norm(x_ref[...], g_ref[...], b_ref[...])
    kv = jnp.dot(h.astype(BF16), wkv_ref[...], preferred_element_type=F32)
    k = _rope(kv[:, :D_MODEL], cos_ref[...], sin_ref[...])
    vt = kv[:, D_MODEL:].T
    km_ref[0] = jnp.mean(k, axis=0, keepdims=True)
    for hd in range(N_HEADS):
        k_ref[hd, 0] = k[:, hd * HEAD_DIM:(hd + 1) * HEAD_DIM].astype(BF16)
        vt_ref[hd, 0] = vt[hd * HEAD_DIM:(hd + 1) * HEAD_DIM, :].astype(BF16)


def _shared_kv(x, seq, ln_g, ln_b, w_kv, cos_t, sin_t):
    n, d = x.shape
    blk = MOBA_BLOCK
    nbt = n // blk
    spb = seq // blk
    row = lambda v: v.reshape(1, -1).astype(F32)
    return pl.pallas_call(
        _kv_kernel,
        grid=(nbt,),
        in_specs=[pl.BlockSpec((blk, d), lambda i: (i, 0)), _full((1, d)), _full((1, d)),
                  _full((d, 2 * d)),
                  pl.BlockSpec((blk, HEAD_DIM), lambda i: (i % spb, 0)),
                  pl.BlockSpec((blk, HEAD_DIM), lambda i: (i % spb, 0))],
        out_specs=[pl.BlockSpec((N_HEADS, 1, blk, HEAD_DIM), lambda i: (0, i, 0, 0)),
                   pl.BlockSpec((N_HEADS, 1, HEAD_DIM, blk), lambda i: (0, i, 0, 0)),
                   pl.BlockSpec((1, 1, d), lambda i: (i, 0, 0))],
        out_shape=[jax.ShapeDtypeStruct((N_HEADS, nbt, blk, HEAD_DIM), BF16),
                   jax.ShapeDtypeStruct((N_HEADS, nbt, HEAD_DIM, blk), BF16),
                   jax.ShapeDtypeStruct((nbt, 1, d), F32)],
        compiler_params=_params("parallel"),
        name="shared_kv",
    )(x, row(ln_g), row(ln_b), w_kv.astype(BF16), cos_t, sin_t)


SEL_ROWS = 8


def _query_kernel(x_ref, wq_ref, cos_ref, sin_ref, km_ref, qp_ref, sel_ref, cnt_ref, carry_ref,
                  *, blocks_per_seq):
    i = pl.program_id(0)
    tm = x_ref.shape[0]
    nb = km_ref.shape[0]
    own = i % blocks_per_seq

    @pl.when(own == 0)
    def _():
        carry_ref[...] = jnp.zeros_like(carry_ref)

    q = jnp.dot(x_ref[...].astype(BF16), wq_ref[...], preferred_element_type=F32)
    q = _rope(q, cos_ref[...], sin_ref[...]) * (HEAD_DIM ** -0.5)

    blk_id = lax.broadcasted_iota(I32, (nb, tm), 0)
    t_row = lax.broadcasted_iota(I32, (tm, tm), 0)
    t_col = lax.broadcasted_iota(I32, (tm, tm), 1)
    before = (t_row < t_col).astype(BF16)
    km = km_ref[...]
    for hd in range(N_HEADS):
        lo, hi = hd * HEAD_DIM, (hd + 1) * HEAD_DIM
        q_h = q[:, lo:hi]
        qp_ref[0, hd] = q_h
        gate = lax.dot_general(km[:, lo:hi], q_h, (((1,), (1,)), ((), ())),
                               preferred_element_type=F32, precision=lax.Precision.HIGHEST)
        cur = jnp.where(blk_id < own, gate, NEG_INF)
        sel_rows = []
        chosen = jnp.zeros((nb, tm), F32)
        for _ in range(MOBA_TOPK):
            m, bi = _first_argmax(cur, blk_id, nb, 0)
            valid = m > NEG_INF
            hit = jnp.logical_and(blk_id == bi, valid)
            sel_rows.append(jnp.where(valid, bi, -1))
            chosen = jnp.where(hit, 1.0, chosen)
            cur = jnp.where(blk_id == bi, NEG_INF, cur)
        carry = carry_ref[hd * nb:(hd + 1) * nb, :]
        cum = jnp.dot(chosen.astype(BF16), before, preferred_element_type=F32) + carry
        rank_rows = [jnp.sum(jnp.where(blk_id == s, cum, 0.0), axis=0, keepdims=True).astype(I32)
                     for s in sel_rows]
        carry_ref[hd * nb:(hd + 1) * nb, :] = carry + jnp.sum(chosen, axis=1, keepdims=True)
        pad = jnp.zeros((SEL_ROWS - 2 * MOBA_TOPK, tm), I32)
        sel_ref[hd * SEL_ROWS:(hd + 1) * SEL_ROWS, :] = jnp.concatenate(sel_rows + rank_rows + [pad], axis=0)
    cnt_ref[0] = jnp.broadcast_to(carry_ref[...], cnt_ref.shape[1:]).astype(I32)


def _queries(x, seq, w_q, cos_t, sin_t, k_mean):
    n, d = x.shape
    blk = MOBA_BLOCK
    nb = seq // blk
    return pl.pallas_call(
        functools.partial(_query_kernel, blocks_per_seq=nb),
        grid=(n // blk,),
        in_specs=[pl.BlockSpec((blk, d), lambda i: (i, 0)), _full((d, d)),
                  pl.BlockSpec((blk, HEAD_DIM), lambda i: (i % nb, 0)),
                  pl.BlockSpec((blk, HEAD_DIM), lambda i: (i % nb, 0)),
                  pl.BlockSpec((nb, d), lambda i: (i // nb, 0))],
        out_specs=[pl.BlockSpec((1, N_HEADS, blk, HEAD_DIM), lambda i: (i, 0, 0, 0)),
                   pl.BlockSpec((N_HEADS * SEL_ROWS, blk), lambda i: (0, i)),
                   pl.BlockSpec((1, N_HEADS * nb, 128), lambda i: (i // nb, 0, 0))],
        out_shape=[jax.ShapeDtypeStruct((n // blk, N_HEADS, blk, HEAD_DIM), F32),
                   jax.ShapeDtypeStruct((N_HEADS * SEL_ROWS, n), I32),
                   jax.ShapeDtypeStruct((n // seq, N_HEADS * nb, 128), I32)],
        scratch_shapes=[pltpu.VMEM((N_HEADS * nb, 1), F32)],
        compiler_params=_params("arbitrary"),
        name="moba_queries",
    )(x, w_q.astype(BF16), cos_t, sin_t, k_mean)


def _moba_positions_kernel(sel_ref, start_ref, pos_ref, *, dump_row):
    tm = sel_ref.shape[1]
    nb = start_ref.shape[1] // N_HEADS
    blk_id = lax.broadcasted_iota(I32, (nb, tm), 0)
    dump = dump_row + lax.broadcasted_iota(I32, (1, tm), 1) % SC_INDEX_CHUNK
    rows = []
    for hd in range(N_HEADS):
        start = start_ref[0, hd * nb:(hd + 1) * nb, :]
        for s in range(MOBA_TOPK):
            sel = sel_ref[hd * SEL_ROWS + s:hd * SEL_ROWS + s + 1, :]
            rank = sel_ref[hd * SEL_ROWS + MOBA_TOPK + s:hd * SEL_ROWS + MOBA_TOPK + s + 1, :]
            base = jnp.sum(jnp.where(blk_id == sel, start, 0), axis=0, keepdims=True)
            rows.append(jnp.where(sel >= 0, base + rank, dump))
        rows.extend([dump] * (SEL_ROWS - MOBA_TOPK))
    pos_ref[...] = jnp.concatenate(rows, axis=0)


def _moba_positions(sel, starts, seq, dump_row):
    rows, n = sel.shape
    tm = min(4 * TOKEN_ROWS, seq)
    tps = seq // tm
    groups = starts.shape[1]
    return pl.pallas_call(
        functools.partial(_moba_positions_kernel, dump_row=dump_row),
        grid=(n // tm,),
        in_specs=[pl.BlockSpec((rows, tm), lambda i: (0, i)),
                  pl.BlockSpec((1, groups, 1), lambda i: (i // tps, 0, 0))],
        out_specs=pl.BlockSpec((rows, tm), lambda i: (0, i)),
        out_shape=jax.ShapeDtypeStruct((rows, n), I32),
        compiler_params=_params("parallel"),
        name="moba_positions",
    )(sel, starts)


ATTN_SUB_ROWS = 128
ATTN_SUBS_PER_STEP = 32
ATTN_STEP_ROWS = ATTN_SUB_ROWS * ATTN_SUBS_PER_STEP


def _pack_bf16_row_pairs(x):
    h = x.shape[0] // 2
    bits = lax.bitcast_convert_type(x.astype(BF16).astype(F32), U32)
    return (bits[:h] >> 16) | (bits[h:] & U32(HIGH_HALF))


def _transpose_u32(x):
    return lax.bitcast_convert_type(lax.bitcast_convert_type(x, I32).T, U32)


def _scores_t(k, q, keep):
    st = lax.dot_general(k, q, (((1,), (1,)), ((), ())), preferred_element_type=F32)
    if keep is not None:
        st = jnp.where(keep, st, NEG_INF)
    m = jnp.max(st, axis=0, keepdims=True)
    p = jnp.exp(st - m)
    return m, p, jnp.sum(p, axis=0, keepdims=True)


def _group_attn_kernel(step_h_ref, step_b_ref, sub_j_ref, nsteps_ref, qs_ref, k_ref, vt_ref, part_ref):
    del step_h_ref, step_b_ref
    i = pl.program_id(0)
    half = HEAD_DIM // 2

    @pl.when(i < nsteps_ref[0])
    def _():
        for u in range(ATTN_SUBS_PER_STEP):
            j = sub_j_ref[i * ATTN_SUBS_PER_STEP + u]
            rows = pl.ds(u * ATTN_SUB_ROWS, ATTN_SUB_ROWS)
            q = qs_ref[rows, :].astype(BF16)
            m, p, l = _scores_t(k_ref[0, j], q, None)
            ot = jnp.dot(vt_ref[0, j], p.astype(BF16), preferred_element_type=F32) * (1.0 / l)
            lse = lax.bitcast_convert_type(m + jnp.log(l), U32)
            packed_t = jnp.concatenate(
                [_pack_bf16_row_pairs(ot), jnp.broadcast_to(lse, (half, ATTN_SUB_ROWS))], axis=0)
            part_ref[rows, :] = _transpose_u32(packed_t)


def _group_attention(qs, k_blk, vt_blk, step_h, step_b, sub_j, nsteps, nb):
    n_steps = step_h.shape[0]
    row_map = lambda i, sh, sb, sj, ns: (jnp.minimum(i, ns[0] - 1), 0)
    kv_map = lambda i, sh, sb, sj, ns: (sh[i], sb[i], 0, 0)
    return pl.pallas_call(
        _group_attn_kernel,
        grid_spec=pltpu.PrefetchScalarGridSpec(
            num_scalar_prefetch=4,
            grid=(n_steps,),
            in_specs=[pl.BlockSpec((ATTN_STEP_ROWS, HEAD_DIM), row_map),
                      pl.BlockSpec((1, nb, MOBA_BLOCK, HEAD_DIM), kv_map),
                      pl.BlockSpec((1, nb, HEAD_DIM, MOBA_BLOCK), kv_map)],
            out_specs=pl.BlockSpec((ATTN_STEP_ROWS, HEAD_DIM), row_map)),
        out_shape=jax.ShapeDtypeStruct(((n_steps + 1) * ATTN_STEP_ROWS, HEAD_DIM), U32),
        compiler_params=_params("arbitrary"),
        name="moba_group_attention",
    )(step_h, step_b, sub_j, nsteps, qs, k_blk, vt_blk)


def _attn_merge_kernel(x_ref, qp_ref, k_ref, vt_ref, pg_ref, sel_ref, wo_ref, lng_ref, lnb_ref,
                       o_ref, op_ref):
    tm = x_ref.shape[0]
    half = HEAD_DIM // 2
    sel = sel_ref[...]
    key = lax.broadcasted_iota(I32, (tm, tm), 0)
    qry = lax.broadcasted_iota(I32, (tm, tm), 1)
    causal = key <= qry
    heads = []
    for hd in range(N_HEADS):
        q = qp_ref[0, hd].astype(BF16)
        m_own, p, l_own = _scores_t(k_ref[hd, 0], q, causal)
        acc = jnp.dot(vt_ref[hd, 0], p.astype(BF16), preferred_element_type=F32)

        outs, lses = [], []
        m_tot = m_own
        for c in range(MOBA_TOPK):
            part_t = _transpose_u32(pg_ref[c, 0, hd])
            lo, hi = _unpack_bf16_pairs(part_t[:half])
            valid = sel[hd * SEL_ROWS + c:hd * SEL_ROWS + c + 1, :] >= 0
            outs.append(jnp.where(valid, jnp.concatenate([lo, hi], axis=0), 0.0))
            lse = jnp.where(valid, lax.bitcast_convert_type(part_t[half:half + 1], F32), NEG_INF)
            lses.append(lse)
            m_tot = jnp.maximum(m_tot, lse)
        w_own = jnp.exp(m_own - m_tot)
        num = acc * w_own
        den = l_own * w_own
        for c in range(MOBA_TOPK):
            w_c = jnp.exp(lses[c] - m_tot)
            num = num + outs[c] * w_c
            den = den + w_c
        heads.append(num * (1.0 / den))
    att = jnp.concatenate(heads, axis=0).T.astype(BF16)
    mix = jnp.dot(att, wo_ref[...], preferred_element_type=F32)
    x1 = _layer_norm(DN_ALPHA * x_ref[...] + mix, lng_ref[...], lnb_ref[...])
    o_ref[...] = x1
    op_ref[...] = _pack_bf16_pairs(x1)


def _attn_merge(x, qp, k_blk, vt_blk, pg, sel, w_o, ln_g, ln_b):
    n, d = x.shape
    blk = MOBA_BLOCK
    row = lambda v: v.reshape(1, -1).astype(F32)
    return pl.pallas_call(
        _attn_merge_kernel,
        grid=(n // blk,),
        in_specs=[pl.BlockSpec((blk, d), lambda i: (i, 0)),
                  pl.BlockSpec((1, N_HEADS, blk, HEAD_DIM), lambda i: (i, 0, 0, 0)),
                  pl.BlockSpec((N_HEADS, 1, blk, HEAD_DIM), lambda i: (0, i, 0, 0)),
                  pl.BlockSpec((N_HEADS, 1, HEAD_DIM, blk), lambda i: (0, i, 0, 0)),
                  pl.BlockSpec((MOBA_TOPK, 1, N_HEADS, blk, HEAD_DIM), lambda i: (0, i, 0, 0, 0)),
                  pl.BlockSpec((N_HEADS * SEL_ROWS, blk), lambda i: (0, i)),
                  _full((d, d)), _full((1, d)), _full((1, d))],
        out_specs=[pl.BlockSpec((blk, d), lambda i: (i, 0)), pl.BlockSpec((blk, d // 2), lambda i: (i, 0))],
        out_shape=[jax.ShapeDtypeStruct((n, d), F32), jax.ShapeDtypeStruct((n, d // 2), U32)],
        compiler_params=_params("parallel"),
        name="moba_merge",
    )(x, qp, k_blk, vt_blk, pg, sel, w_o.astype(BF16), row(ln_g), row(ln_b))


def _moba_layer(x, seq, w_q, w_o, ln_g, ln_b, cos_t, sin_t, k_blk, vt_blk, k_mean):
    n, d = x.shape
    batch = n // seq
    blk = MOBA_BLOCK
    nb = seq // blk
    nbt = n // blk
    chunk = SC_INDEX_CHUNK
    qp, sel, cnt = _queries(x, seq, w_q, cos_t, sin_t, k_mean)

    counts = cnt[:, :, 0].reshape(batch * N_HEADS, nb)
    gpad = (counts + ATTN_SUB_ROWS - 1) // ATTN_SUB_ROWS * ATTN_SUB_ROWS
    seg = jnp.sum(gpad, axis=1)
    seg_pad = (seg + ATTN_STEP_ROWS - 1) // ATTN_STEP_ROWS * ATTN_STEP_ROWS
    seg_end = jnp.cumsum(seg_pad)
    gend = (seg_end - seg_pad)[:, None] + jnp.cumsum(gpad, axis=1)
    gstart = (gend - gpad).astype(I32)
    steps_per_seg = -(-(MOBA_TOPK * seq + nb * (ATTN_SUB_ROWS - 1)) // ATTN_STEP_ROWS)
    n_steps = batch * N_HEADS * steps_per_seg
    step_first = jnp.arange(n_steps, dtype=I32) * ATTN_STEP_ROWS
    step_seg = jnp.minimum(jnp.sum((seg_end[None, :] <= step_first[:, None]).astype(I32), axis=1),
                           batch * N_HEADS - 1)
    sub_first = jnp.arange(n_steps * ATTN_SUBS_PER_STEP, dtype=I32) * ATTN_SUB_ROWS
    sub_grp = jnp.minimum(jnp.sum((gend.reshape(-1)[None, :] <= sub_first[:, None]).astype(I32), axis=1),
                          batch * N_HEADS * nb - 1)
    nsteps = (seg_end[-1] // ATTN_STEP_ROWS).astype(I32).reshape(1)
    dump_row = n_steps * ATTN_STEP_ROWS

    pos = _moba_positions(sel, gstart.reshape(batch, N_HEADS * nb, 1), seq, dump_row)
    pos5 = pos.reshape(N_HEADS, SEL_ROWS, nbt, blk // chunk, chunk)[:, :MOBA_TOPK]
    scatter_idx = pos5.transpose(2, 0, 3, 1, 4).reshape(nbt * N_HEADS * (blk // chunk), MOBA_TOPK, chunk)
    gather_idx = pos5.transpose(1, 2, 0, 3, 4).reshape(-1)

    qs = _sc_scatter_rows(qp.reshape(n * N_HEADS, HEAD_DIM), scatter_idx, dump_row + chunk)
    part = _group_attention(qs, k_blk, vt_blk, (step_seg % N_HEADS).astype(I32),
                            (step_seg // N_HEADS).astype(I32), (sub_grp % nb).astype(I32), nsteps, nb)
    pg = _sc_gather_rows(part, gather_idx).reshape(MOBA_TOPK, nbt, N_HEADS, blk, HEAD_DIM)
    return _attn_merge(x, qp, k_blk, vt_blk, pg, sel, w_o, ln_g, ln_b)


def _rope_tables(seq):
    half = ROT_DIM // 2
    inv = ROPE_THETA ** (-jnp.arange(0, ROT_DIM, 2, dtype=F32) / ROT_DIM)
    ang = jnp.arange(seq, dtype=F32)[:, None] * inv[None, :]
    cos, sin = jnp.cos(ang), jnp.sin(ang)
    rest = HEAD_DIM - ROT_DIM
    cos_t = jnp.concatenate([cos, cos, jnp.ones((seq, rest), F32)], axis=1)
    sin_t = jnp.concatenate([-sin, sin, jnp.zeros((seq, rest), F32)], axis=1)
    del half
    return cos_t, sin_t


def kernel(x, p, ln_g, ln_b, a_w_in, a_b_in, a_conv_w, a_conv_b, a_gate_a_w, a_gate_a_b, a_gate_i_w,
           a_gate_i_b, a_lambda, a_w_out, kv_ln_g, kv_ln_b, w_kv, b_w_q, b_w_o, router_w, router_b,
           exp_w_gate, exp_w_up, exp_w_down, sh_w_gate, sh_w_up, sh_w_down, ple_w, ple_gate_w):
    batch, seq, d = x.shape
    n = batch * seq
    xf = x.reshape(n, d)
    pf = p.reshape(DEPTH, n, PLE_DIM)
    cos_t, sin_t = _rope_tables(seq)
    k_blk = v_blk = k_mean = None
    for i in range(DEPTH):
        if i < N_A_LAYERS:
            x1, x1p = _rglru_layer(xf, seq, a_w_in[i], a_b_in[i], a_conv_w[i], a_conv_b[i],
                                   a_gate_a_w[i], a_gate_a_b[i], a_gate_i_w[i], a_gate_i_b[i],
                                   a_lambda[i], a_w_out[i], ln_g[i, 0], ln_b[i, 0])
        else:
            if i == N_A_LAYERS:
                k_blk, v_blk, k_mean = _shared_kv(xf, seq, kv_ln_g, kv_ln_b, w_kv, cos_t, sin_t)
                k_mean = k_mean.reshape(n // MOBA_BLOCK, d)
            j = i - N_A_LAYERS
            x1, x1p = _moba_layer(xf, seq, b_w_q[j], b_w_o[j], ln_g[i, 0], ln_b[i, 0],
                                  cos_t, sin_t, k_blk, v_blk, k_mean)
        xf = _moe_layer(x1, x1p, i, router_w[i], router_b[i], exp_w_gate, exp_w_up, exp_w_down,
                        sh_w_gate[i], sh_w_up[i], sh_w_down[i], ln_g[i, 1], ln_b[i, 1],
                        pf[i], ple_w[i], ple_gate_w[i])
    return xf.reshape(batch, seq, d)
```

```python
import functools

import jax
import jax.numpy as jnp
from jax import lax
from jax.experimental import pallas as pl
from jax.experimental.pallas import tpu as pltpu
from jax.experimental.pallas import tpu_sc as plsc

F32 = jnp.float32
BF16 = jnp.bfloat16
I32 = jnp.int32
U32 = jnp.uint32
HIGH_HALF = 0xFFFF0000

SC_CORES = 2
SC_SUBCORES = 16
SC_WORKERS = SC_CORES * SC_SUBCORES
SC_INDEX_CHUNK = 128
SC_STAGE_BYTES = 256 * 1024

D_MODEL = 1024
DEPTH = 4
N_A_LAYERS = DEPTH // 2
D_RNN = D_MODEL
LRU_BLOCKS = 4
LRU_BLOCK_W = D_RNN // LRU_BLOCKS
CONV_W = 4
LRU_C = 8.0
N_HEADS = 8
HEAD_DIM = D_MODEL // N_HEADS
ROT_DIM = HEAD_DIM // 4
ROPE_THETA = 500000.0
MOBA_BLOCK = 256
MOBA_TOPK = 3
N_EXPERTS = 64
N_GROUPS = 8
GROUP_SIZE = N_EXPERTS // N_GROUPS
TOPK_GROUPS = 4
TOPK_EXPERTS = 8
D_EXPERT = 256
D_SHARED = 256
ROUTED_SCALE = 2.5
PLE_DIM = 256
DN_ALPHA = (2 * DEPTH) ** 0.25
LN_EPS = 1e-5

V7X_VMEM_LIMIT_BYTES = 56 * 1024 * 1024

MIXER_ROWS = 256
TOKEN_ROWS = 512
EXPERT_ROWS = 1024
NEG_INF = float("-inf")


def _params(*sem):
    return pltpu.CompilerParams(dimension_semantics=sem, vmem_limit_bytes=V7X_VMEM_LIMIT_BYTES)


def _layer_norm(z, g, b):
    mu = jnp.mean(z, axis=-1, keepdims=True)
    zc = z - mu
    var = jnp.mean(zc * zc, axis=-1, keepdims=True)
    return zc * lax.rsqrt(var + LN_EPS) * g + b


def _silu(x):
    return x * jax.nn.sigmoid(x)


def _gelu_tanh(x):
    return x * jax.nn.sigmoid(x * (1.5957691216057308 + 0.07135481627159855 * (x * x)))


def _full(shape):
    return pl.BlockSpec(shape, lambda *_: (0,) * len(shape))


def _pack_bf16_pairs(x):
    w = x.shape[1] // 2
    bits = lax.bitcast_convert_type(x.astype(BF16).astype(F32), U32)
    return (bits[:, :w] >> 16) | (bits[:, w:] & U32(HIGH_HALF))


def _unpack_bf16_pairs(u):
    lo = lax.bitcast_convert_type(u << 16, F32)
    hi = lax.bitcast_convert_type(u & U32(HIGH_HALF), F32)
    return lo, hi


def _sc_mesh():
    return plsc.VectorSubcoreMesh(core_axis_name="c", subcore_axis_name="s")


def _sc_worker_id():
    return lax.axis_index("s") * SC_CORES + lax.axis_index("c")


def _sc_chunks_per_step(chunks_per_worker, row_words):
    g = max(1, SC_STAGE_BYTES // (SC_INDEX_CHUNK * row_words * 4))
    while chunks_per_worker % g:
        g -= 1
    return g


def _sc_gather_rows(table, idx):
    b = idx.shape[0]
    w = table.shape[1]
    chunk = SC_INDEX_CHUNK
    chunks_per_worker = b // (SC_WORKERS * chunk)
    assert chunks_per_worker * SC_WORKERS * chunk == b
    g = _sc_chunks_per_step(chunks_per_worker, w)

    @functools.partial(
        pl.kernel, mesh=_sc_mesh(), out_type=jax.ShapeDtypeStruct((b, w), table.dtype),
        scratch_types=[pltpu.VMEM((g, chunk), I32), pltpu.VMEM((g * chunk, w), table.dtype),
                       pltpu.SemaphoreType.DMA])
    def gather(table_hbm, idx_hbm, out_hbm, idx_v, rows_v, sem):
        first = _sc_worker_id() * chunks_per_worker

        @pl.loop(0, chunks_per_worker // g)
        def _(j):
            c0 = first + j * g
            pltpu.sync_copy(idx_hbm.at[pl.ds(c0, g)], idx_v)
            copies = [pltpu.async_copy(table_hbm.at[idx_v.at[q]], rows_v.at[pl.ds(q * chunk, chunk)], sem)
                      for q in range(g)]
            for cp in copies:
                cp.wait()
            pltpu.sync_copy(rows_v, out_hbm.at[pl.ds(c0 * chunk, g * chunk)])

    return gather(table, idx.reshape(b // chunk, chunk))


def _sc_scatter_rows(src, idx, out_rows):
    n, w = src.shape
    chunk = SC_INDEX_CHUNK
    fan = idx.shape[1]
    chunks_per_worker = n // (SC_WORKERS * chunk)
    assert idx.shape == (n // chunk, fan, chunk) and chunks_per_worker * SC_WORKERS * chunk == n
    g = _sc_chunks_per_step(chunks_per_worker, w)

    @functools.partial(
        pl.kernel, mesh=_sc_mesh(), out_type=jax.ShapeDtypeStruct((out_rows, w), src.dtype),
        scratch_types=[pltpu.VMEM((g, fan, chunk), I32), pltpu.VMEM((g * chunk, w), src.dtype),
                       pltpu.SemaphoreType.DMA])
    def scatter(src_hbm, idx_hbm, out_hbm, idx_v, rows_v, sem):
        first = _sc_worker_id() * chunks_per_worker

        @pl.loop(0, chunks_per_worker // g)
        def _(j):
            c0 = first + j * g
            pltpu.sync_copy(src_hbm.at[pl.ds(c0 * chunk, g * chunk)], rows_v)
            pltpu.sync_copy(idx_hbm.at[pl.ds(c0, g)], idx_v)
            copies = [pltpu.async_copy(rows_v.at[pl.ds(q * chunk, chunk)], out_hbm.at[idx_v.at[q, f]], sem)
                      for q in range(g) for f in range(fan)]
            for cp in copies:
                cp.wait()

    return scatter(src, idx)


def _rglru_kernel(x_ref, win_ref, bin_ref, cw_ref, cb_ref, gaw_ref, gab_ref, giw_ref, gib_ref,
                  lam_ref, wout_ref, lng_ref, lnb_ref, o_ref, op_ref, tail_ref, h_ref, *, tiles_per_seq):
    i = pl.program_id(0)
    tm = x_ref.shape[0]

    @pl.when(i % tiles_per_seq == 0)
    def _():
        tail_ref[...] = jnp.zeros_like(tail_ref)
        h_ref[...] = jnp.zeros_like(h_ref)

    x = x_ref[...]
    xy = jnp.dot(x.astype(BF16), win_ref[...], preferred_element_type=F32) + bin_ref[...]
    xb = xy[:, :D_RNN]
    y = _gelu_tanh(xy[:, D_RNN:])

    tail = tail_ref[...]
    row8 = lax.broadcasted_iota(I32, (8, 1), 0)
    xc = cb_ref[...] + xb * cw_ref[CONV_W - 1:CONV_W, :]
    for d in range(1, CONV_W):
        rolled = pltpu.roll(xb, d, 0)
        head = jnp.where(row8 < d, pltpu.roll(tail, d, 0), rolled[:8])
        shifted = jnp.concatenate([head, rolled[8:]], axis=0)
        xc = xc + shifted * cw_ref[CONV_W - 1 - d:CONV_W - d, :]
    tail_ref[...] = xb[tm - 8:, :]

    r_parts, i_parts = [], []
    for n in range(LRU_BLOCKS):
        xg = xc[:, n * LRU_BLOCK_W:(n + 1) * LRU_BLOCK_W].astype(BF16)
        r_parts.append(jnp.dot(xg, gaw_ref[n], preferred_element_type=F32))
        i_parts.append(jnp.dot(xg, giw_ref[n], preferred_element_type=F32))
    r = jax.nn.sigmoid(jnp.concatenate(r_parts, axis=1) + gab_ref[...])
    ig = jax.nn.sigmoid(jnp.concatenate(i_parts, axis=1) + gib_ref[...])

    lam = lam_ref[...]
    softplus_neg_lam = jnp.maximum(-lam, 0.0) + jnp.log1p(jnp.exp(-jnp.abs(lam)))
    log_a = (-LRU_C * r) * softplus_neg_lam
    a = jnp.exp(log_a)
    u = jnp.sqrt(1.0 - a * a) * (ig * xc)

    n_grp = tm // 8
    sub = lax.broadcasted_iota(I32, (1, 8, 1), 1)
    acc_a = a.reshape(n_grp, 8, D_RNN)
    acc_h = u.reshape(n_grp, 8, D_RNN)
    for d in (1, 2, 4):
        keep = sub >= d
        sh_a = pltpu.roll(acc_a, d, 1)
        sh_h = pltpu.roll(acc_h, d, 1)
        acc_h = jnp.where(keep, acc_a * sh_h + acc_h, acc_h)
        acc_a = jnp.where(keep, acc_a * sh_a, acc_a)
    state = h_ref[...]
    groups = []
    for g in range(n_grp):
        hg = acc_h[g] + acc_a[g] * state
        state = hg[7:8, :]
        groups.append(hg)
    h = jnp.concatenate(groups, axis=0)
    h_ref[...] = state

    mix = jnp.dot((h * y).astype(BF16), wout_ref[...], preferred_element_type=F32)
    x1 = _layer_norm(DN_ALPHA * x + mix, lng_ref[...], lnb_ref[...])
    o_ref[...] = x1
    op_ref[...] = _pack_bf16_pairs(x1)


def _rglru_layer(x, seq, w_in, b_in, conv_w, conv_b, ga_w, ga_b, gi_w, gi_b, lam, w_out, ln_g, ln_b):
    n, d = x.shape
    tm = min(MIXER_ROWS, seq)
    row = lambda v: v.reshape(1, -1).astype(F32)
    return pl.pallas_call(
        functools.partial(_rglru_kernel, tiles_per_seq=seq // tm),
        grid=(n // tm,),
        in_specs=[pl.BlockSpec((tm, d), lambda i: (i, 0)),
                  _full((d, 2 * D_RNN)), _full((1, 2 * D_RNN)),
                  _full((CONV_W, D_RNN)), _full((1, D_RNN)),
                  _full((LRU_BLOCKS, LRU_BLOCK_W, LRU_BLOCK_W)), _full((1, D_RNN)),
                  _full((LRU_BLOCKS, LRU_BLOCK_W, LRU_BLOCK_W)), _full((1, D_RNN)),
                  _full((1, D_RNN)), _full((D_RNN, d)), _full((1, d)), _full((1, d))],
        out_specs=[pl.BlockSpec((tm, d), lambda i: (i, 0)), pl.BlockSpec((tm, d // 2), lambda i: (i, 0))],
        out_shape=[jax.ShapeDtypeStruct((n, d), F32), jax.ShapeDtypeStruct((n, d // 2), U32)],
        scratch_shapes=[pltpu.VMEM((8, D_RNN), F32), pltpu.VMEM((1, D_RNN), F32)],
        compiler_params=_params("arbitrary"),
        name="rglru_mixer",
    )(x, w_in.astype(BF16), row(b_in), conv_w, row(conv_b), ga_w.astype(BF16), row(ga_b),
      gi_w.astype(BF16), row(gi_b), row(lam), w_out.astype(BF16), row(ln_g), row(ln_b))


def _first_argmax(cur, idx, size, axis):
    m = jnp.max(cur, axis=axis, keepdims=True)
    first = jnp.min(jnp.where(cur == m, idx, size), axis=axis, keepdims=True)
    return m, first


def _router_kernel(x_ref, rw_ref, rb_ref, e_ref, w_ref, r_ref, cnt_ref, carry_ref):
    i = pl.program_id(0)
    tm = x_ref.shape[0]

    @pl.when(i == 0)
    def _():
        carry_ref[...] = jnp.zeros_like(carry_ref)

    logits = jnp.dot(x_ref[...], rw_ref[...], preferred_element_type=F32,
                     precision=lax.Precision.HIGHEST)
    scores = jax.nn.sigmoid(logits.T[:N_EXPERTS, :])
    choice = scores + rb_ref[...]

    c3 = choice.reshape(N_GROUPS, GROUP_SIZE, tm)
    in_grp = lax.broadcasted_iota(I32, c3.shape, 1)
    m1, i1 = _first_argmax(c3, in_grp, GROUP_SIZE, 1)
    m2 = jnp.max(jnp.where(in_grp == i1, NEG_INF, c3), axis=1, keepdims=True)
    grp_score = (m1 + m2)[:, 0, :]

    grp_id = lax.broadcasted_iota(I32, grp_score.shape, 0)
    grp_sel = jnp.zeros(grp_score.shape, jnp.bool_)
    cur = grp_score
    for _ in range(TOPK_GROUPS):
        _, gi = _first_argmax(cur, grp_id, N_GROUPS, 0)
        hit = grp_id == gi
        grp_sel = jnp.logical_or(grp_sel, hit)
        cur = jnp.where(hit, NEG_INF, cur)

    cur = jnp.where(grp_sel[:, None, :], c3, NEG_INF).reshape(N_EXPERTS, tm)
    exp_id = lax.broadcasted_iota(I32, cur.shape, 0)
    sel = jnp.zeros(cur.shape, F32)
    e_rows, s_rows = [], []
    for _ in range(TOPK_EXPERTS):
        _, ei = _first_argmax(cur, exp_id, N_EXPERTS, 0)
        hit = exp_id == ei
        e_rows.append(ei)
        s_rows.append(jnp.sum(jnp.where(hit, scores, 0.0), axis=0, keepdims=True))
        sel = jnp.where(hit, 1.0, sel)
        cur = jnp.where(hit, NEG_INF, cur)
    e_top = jnp.concatenate(e_rows, axis=0)
    s_top = jnp.concatenate(s_rows, axis=0)
    w_ref[...] = s_top / jnp.sum(s_top, axis=0, keepdims=True) * ROUTED_SCALE
    e_ref[...] = e_top

    t_row = lax.broadcasted_iota(I32, (tm, tm), 0)
    t_col = lax.broadcasted_iota(I32, (tm, tm), 1)
    before = (t_row < t_col).astype(BF16)
    cum = jnp.dot(sel.astype(BF16), before, preferred_element_type=F32) + carry_ref[...]
    r_rows = [jnp.sum(jnp.where(exp_id == e_rows[k], cum, 0.0), axis=0, keepdims=True)
              for k in range(TOPK_EXPERTS)]
    r_ref[...] = jnp.concatenate(r_rows, axis=0).astype(I32)
    carry_ref[...] = carry_ref[...] + jnp.sum(sel, axis=1, keepdims=True)
    cnt_ref[...] = jnp.broadcast_to(carry_ref[...], cnt_ref.shape).astype(I32)


def _router(x, router_w, router_b):
    n, d = x.shape
    tm = min(TOKEN_ROWS, n)
    rw = jnp.pad(router_w, ((0, 0), (0, 128 - N_EXPERTS)))
    k = TOPK_EXPERTS
    tok_spec = pl.BlockSpec((k, tm), lambda i: (0, i))
    return pl.pallas_call(
        _router_kernel,
        grid=(n // tm,),
        in_specs=[pl.BlockSpec((tm, d), lambda i: (i, 0)), _full((d, 128)), _full((N_EXPERTS, 1))],
        out_specs=[tok_spec, tok_spec, tok_spec, _full((N_EXPERTS, 128))],
        out_shape=[jax.ShapeDtypeStruct((k, n), I32), jax.ShapeDtypeStruct((k, n), F32),
                   jax.ShapeDtypeStruct((k, n), I32), jax.ShapeDtypeStruct((N_EXPERTS, 128), I32)],
        scratch_shapes=[pltpu.VMEM((N_EXPERTS, 1), F32)],
        compiler_params=_params("arbitrary"),
        name="moe_router",
    )(x, rw, router_b.reshape(N_EXPERTS, 1).astype(F32))


def _expert_kernel(blk_e_ref, nblk_ref, xs_ref, wg_ref, wu_ref, wd_ref, ys_ref, wg_s, wu_s, wd_s):
    i = pl.program_id(0)

    @pl.when(i < nblk_ref[0])
    def _():
        @pl.when(jnp.logical_or(i == 0, blk_e_ref[i] != blk_e_ref[jnp.maximum(i - 1, 0)]))
        def _():
            wg_s[...] = wg_ref[0, 0].astype(BF16)
            wu_s[...] = wu_ref[0, 0].astype(BF16)
            wd_s[...] = wd_ref[0, 0].astype(BF16)

        lo, hi = _unpack_bf16_pairs(xs_ref[...])
        xs = jnp.concatenate([lo.astype(BF16), hi.astype(BF16)], axis=1)
        g = jnp.dot(xs, wg_s[...], preferred_element_type=F32)
        u = jnp.dot(xs, wu_s[...], preferred_element_type=F32)
        hdn = (_silu(g) * u).astype(BF16)
        ys_ref[...] = _pack_bf16_pairs(jnp.dot(hdn, wd_s[...], preferred_element_type=F32))


def _experts(xs, blk_e, nblk, layer, w_gate, w_up, w_down):
    rows, half = xs.shape
    d = 2 * half
    m = EXPERT_ROWS
    row_map = lambda i, be, nb: (jnp.minimum(i, nb[0] - 1), 0)
    w_map = lambda i, be, nb: (layer, be[i], 0, 0)
    return pl.pallas_call(
        _expert_kernel,
        grid_spec=pltpu.PrefetchScalarGridSpec(
            num_scalar_prefetch=2,
            grid=(rows // m,),
            in_specs=[pl.BlockSpec((m, half), row_map),
                      pl.BlockSpec((1, 1, d, D_EXPERT), w_map),
                      pl.BlockSpec((1, 1, d, D_EXPERT), w_map),
                      pl.BlockSpec((1, 1, D_EXPERT, d), w_map)],
            out_specs=pl.BlockSpec((m, half), row_map),
            scratch_shapes=[pltpu.VMEM((d, D_EXPERT), BF16), pltpu.VMEM((d, D_EXPERT), BF16),
                            pltpu.VMEM((D_EXPERT, d), BF16)]),
        out_shape=jax.ShapeDtypeStruct((rows, half), U32),
        compiler_params=_params("arbitrary"),
        name="moe_experts",
    )(blk_e, nblk, xs, w_gate, w_up, w_down)


def _combine_kernel(x_ref, yg_ref, w_ref, sg_ref, su_ref, sd_ref, lng_ref, lnb_ref,
                    p_ref, pw_ref, pg_ref, o_ref):
    x = x_ref[...]
    wt = w_ref[...].T
    moe_lo, moe_hi = None, None
    for k in range(TOPK_EXPERTS):
        lo, hi = _unpack_bf16_pairs(yg_ref[k])
        wk = wt[:, k:k + 1]
        moe_lo = wk * lo if k == 0 else moe_lo + wk * lo
        moe_hi = wk * hi if k == 0 else moe_hi + wk * hi
    moe = jnp.concatenate([moe_lo, moe_hi], axis=1)
    xb = x.astype(BF16)
    hdn = _silu(jnp.dot(xb, sg_ref[...], preferred_element_type=F32)) * \
        jnp.dot(xb, su_ref[...], preferred_element_type=F32)
    shared = jnp.dot(hdn.astype(BF16), sd_ref[...], preferred_element_type=F32)
    x2 = _layer_norm(DN_ALPHA * x + (moe + shared), lng_ref[...], lnb_ref[...])
    emb = jnp.dot(p_ref[0].astype(BF16), pw_ref[...], preferred_element_type=F32)
    gate = jax.nn.sigmoid(jnp.dot(x2.astype(BF16), pg_ref[...], preferred_element_type=F32))
    o_ref[...] = x2 + emb * gate


def _combine(x, yg, wgt, s_gate, s_up, s_down, ln_g, ln_b, p_all, p_index, ple_w, ple_gate_w):
    n, d = x.shape
    tm = min(TOKEN_ROWS // 2, n)
    k = TOPK_EXPERTS
    row = lambda v: v.reshape(1, -1).astype(F32)
    return pl.pallas_call(
        _combine_kernel,
        grid=(n // tm,),
        in_specs=[pl.BlockSpec((tm, d), lambda i: (i, 0)),
                  pl.BlockSpec((k, tm, d // 2), lambda i: (0, i, 0)),
                  pl.BlockSpec((k, tm), lambda i: (0, i)),
                  _full((d, D_SHARED)), _full((d, D_SHARED)), _full((D_SHARED, d)),
                  _full((1, d)), _full((1, d)),
                  pl.BlockSpec((1, tm, PLE_DIM), lambda i: (p_index, i, 0)),
                  _full((PLE_DIM, d)), _full((d, d))],
        out_specs=pl.BlockSpec((tm, d), lambda i: (i, 0)),
        out_shape=jax.ShapeDtypeStruct((n, d), F32),
        compiler_params=_params("parallel"),
        name="moe_combine",
    )(x, yg, wgt, s_gate.astype(BF16), s_up.astype(BF16), s_down.astype(BF16), row(ln_g), row(ln_b),
      p_all, ple_w.astype(BF16), ple_gate_w.astype(BF16))


def _positions_kernel(e_ref, r_ref, start_ref, pos_ref):
    e = e_ref[...]
    start = start_ref[...]
    grp_id = lax.broadcasted_iota(I32, (start.shape[0], e.shape[1]), 0)
    rows = [jnp.sum(jnp.where(grp_id == e[k:k + 1], start, 0), axis=0, keepdims=True)
            for k in range(e.shape[0])]
    pos_ref[...] = jnp.concatenate(rows, axis=0) + r_ref[...]


def _positions(e_idx, rank, starts):
    k, n = e_idx.shape
    tm = min(4 * TOKEN_ROWS, n)
    groups = starts.shape[0]
    tok_spec = pl.BlockSpec((k, tm), lambda i: (0, i))
    return pl.pallas_call(
        _positions_kernel,
        grid=(n // tm,),
        in_specs=[tok_spec, tok_spec, _full((groups, 1))],
        out_specs=tok_spec,
        out_shape=jax.ShapeDtypeStruct((k, n), I32),
        compiler_params=_params("parallel"),
        name="group_positions",
    )(e_idx, rank, starts.reshape(groups, 1))


def _moe_layer(streams, layer, router_w, router_b, w_gate, w_up, w_down, s_gate, s_up, s_down,
               ln_g, ln_b, p_all, p_index, ple_w, ple_gate_w):
    m = EXPERT_ROWS
    k = TOPK_EXPERTS
    chunk = SC_INDEX_CHUNK
    routed = []
    for x, xp in streams:
        n, d = x.shape
        e_idx, wgt, rank, cnt = _router(x, router_w, router_b)
        counts = cnt[:, 0]
        padded = (counts + m - 1) // m * m
        pends = jnp.cumsum(padded)
        n_blocks = (n * k) // m + N_EXPERTS
        blk_first = jnp.arange(n_blocks, dtype=I32) * m
        blk_e = jnp.minimum(jnp.sum((pends[None, :] <= blk_first[:, None]).astype(I32), axis=1),
                            N_EXPERTS - 1)
        nblk = (pends[-1] // m).astype(I32).reshape(1)
        pos = _positions(e_idx, rank, (pends - padded).astype(I32))
        pos_chunks = pos.reshape(k, n // chunk, chunk).transpose(1, 0, 2)
        xs = _sc_scatter_rows(xp, pos_chunks, n_blocks * m)
        routed.append((xs, blk_e, nblk, pos, wgt))
    gathered = []
    for (x, _), (xs, blk_e, nblk, pos, wgt) in zip(streams, routed):
        n, d = x.shape
        ys = _experts(xs, blk_e, nblk, layer, w_gate, w_up, w_down)
        gathered.append(_sc_gather_rows(ys, pos.reshape(-1)).reshape(k, n, d // 2))
    return [_combine(x, yg, wgt, s_gate, s_up, s_down, ln_g, ln_b, p_all, p_index + s, ple_w, ple_gate_w)
            for s, ((x, _), (_, _, _, _, wgt), yg) in enumerate(zip(streams, routed, gathered))]


def _rope(t, cos_t, sin_t):
    half = ROT_DIM // 2
    width = t.shape[1]
    lane = lax.broadcasted_iota(I32, (1, width), 1) % HEAD_DIM
    partner = jnp.where(lane < half, pltpu.roll(t, width - half, 1), pltpu.roll(t, half, 1))
    cos_f = jnp.concatenate([cos_t] * N_HEADS, axis=1)
    sin_f = jnp.concatenate([sin_t] * N_HEADS, axis=1)
    return t * cos_f + partner * sin_f


def _kv_kernel(x_ref, g_ref, b_ref, wkv_ref, cos_ref, sin_ref, k_ref, vt_ref, km_ref):
    h = _layer_norm(x_ref[...], g_ref[...], b_ref[...])
    kv = jnp.dot(h.astype(BF16), wkv_ref[...], preferred_element_type=F32)
    k = _rope(kv[:, :D_MODEL], cos_ref[...], sin_ref[...])
    vt = kv[:, D_MODEL:].T
    km_ref[0] = jnp.mean(k, axis=0, keepdims=True)
    for hd in range(N_HEADS):
        k_ref[hd, 0] = k[:, hd * HEAD_DIM:(hd + 1) * HEAD_DIM].astype(BF16)
        vt_ref[hd, 0] = vt[hd * HEAD_DIM:(hd + 1) * HEAD_DIM, :].astype(BF16)


def _shared_kv(x, seq, ln_g, ln_b, w_kv, cos_t, sin_t):
    n, d = x.shape
    blk = MOBA_BLOCK
    nbt = n // blk
    spb = seq // blk
    row = lambda v: v.reshape(1, -1).astype(F32)
    return pl.pallas_call(
        _kv_kernel,
        grid=(nbt,),
        in_specs=[pl.BlockSpec((blk, d), lambda i: (i, 0)), _full((1, d)), _full((1, d)),
                  _full((d, 2 * d)),
                  pl.BlockSpec((blk, HEAD_DIM), lambda i: (i % spb, 0)),
                  pl.BlockSpec((blk, HEAD_DIM), lambda i: (i % spb, 0))],
        out_specs=[pl.BlockSpec((N_HEADS, 1, blk, HEAD_DIM), lambda i: (0, i, 0, 0)),
                   pl.BlockSpec((N_HEADS, 1, HEAD_DIM, blk), lambda i: (0, i, 0, 0)),
                   pl.BlockSpec((1, 1, d), lambda i: (i, 0, 0))],
        out_shape=[jax.ShapeDtypeStruct((N_HEADS, nbt, blk, HEAD_DIM), BF16),
                   jax.ShapeDtypeStruct((N_HEADS, nbt, HEAD_DIM, blk), BF16),
                   jax.ShapeDtypeStruct((nbt, 1, d), F32)],
        compiler_params=_params("parallel"),
        name="shared_kv",
    )(x, row(ln_g), row(ln_b), w_kv.astype(BF16), cos_t, sin_t)


SEL_ROWS = 8


def _query_kernel(x_ref, wq_ref, cos_ref, sin_ref, km_ref, qp_ref, sel_ref, cnt_ref, carry_ref,
                  *, blocks_per_seq):
    i = pl.program_id(0)
    tm = x_ref.shape[0]
    nb = km_ref.shape[0]
    own = i % blocks_per_seq

    @pl.when(own == 0)
    def _():
        carry_ref[...] = jnp.zeros_like(carry_ref)

    q = jnp.dot(x_ref[...].astype(BF16), wq_ref[...], preferred_element_type=F32)
    q = _rope(q, cos_ref[...], sin_ref[...]) * (HEAD_DIM ** -0.5)

    blk_id = lax.broadcasted_iota(I32, (nb, tm), 0)
    t_row = lax.broadcasted_iota(I32, (tm, tm), 0)
    t_col = lax.broadcasted_iota(I32, (tm, tm), 1)
    before = (t_row < t_col).astype(BF16)
    km = km_ref[...]
    for hd in range(N_HEADS):
        lo, hi = hd * HEAD_DIM, (hd + 1) * HEAD_DIM
        q_h = q[:, lo:hi]
        qp_ref[0, hd] = q_h
        gate = lax.dot_general(km[:, lo:hi], q_h, (((1,), (1,)), ((), ())),
                               preferred_element_type=F32, precision=lax.Precision.HIGHEST)
        cur = jnp.where(blk_id < own, gate, NEG_INF)
        sel_rows = []
        chosen = jnp.zeros((nb, tm), F32)
        for _ in range(MOBA_TOPK):
            m, bi = _first_argmax(cur, blk_id, nb, 0)
            valid = m > NEG_INF
            hit = jnp.logical_and(blk_id == bi, valid)
            sel_rows.append(jnp.where(valid, bi, -1))
            chosen = jnp.where(hit, 1.0, chosen)
            cur = jnp.where(blk_id == bi, NEG_INF, cur)
        carry = carry_ref[hd * nb:(hd + 1) * nb, :]
        cum = jnp.dot(chosen.astype(BF16), before, preferred_element_type=F32) + carry
        rank_rows = [jnp.sum(jnp.where(blk_id == s, cum, 0.0), axis=0, keepdims=True).astype(I32)
                     for s in sel_rows]
        carry_ref[hd * nb:(hd + 1) * nb, :] = carry + jnp.sum(chosen, axis=1, keepdims=True)
        pad = jnp.zeros((SEL_ROWS - 2 * MOBA_TOPK, tm), I32)
        sel_ref[hd * SEL_ROWS:(hd + 1) * SEL_ROWS, :] = jnp.concatenate(sel_rows + rank_rows + [pad], axis=0)
    cnt_ref[0] = jnp.broadcast_to(carry_ref[...], cnt_ref.shape[1:]).astype(I32)


def _queries(x, seq, w_q, cos_t, sin_t, k_mean):
    n, d = x.shape
    blk = MOBA_BLOCK
    nb = seq // blk
    return pl.pallas_call(
        functools.partial(_query_kernel, blocks_per_seq=nb),
        grid=(n // blk,),
        in_specs=[pl.BlockSpec((blk, d), lambda i: (i, 0)), _full((d, d)),
                  pl.BlockSpec((blk, HEAD_DIM), lambda i: (i % nb, 0)),
                  pl.BlockSpec((blk, HEAD_DIM), lambda i: (i % nb, 0)),
                  pl.BlockSpec((nb, d), lambda i: (i // nb, 0))],
        out_specs=[pl.BlockSpec((1, N_HEADS, blk, HEAD_DIM), lambda i: (i, 0, 0, 0)),
                   pl.BlockSpec((N_HEADS * SEL_ROWS, blk), lambda i: (0, i)),
                   pl.BlockSpec((1, N_HEADS * nb, 128), lambda i: (i // nb, 0, 0))],
        out_shape=[jax.ShapeDtypeStruct((n // blk, N_HEADS, blk, HEAD_DIM), F32),
                   jax.ShapeDtypeStruct((N_HEADS * SEL_ROWS, n), I32),
                   jax.ShapeDtypeStruct((n // seq, N_HEADS * nb, 128), I32)],
        scratch_shapes=[pltpu.VMEM((N_HEADS * nb, 1), F32)],
        compiler_params=_params("arbitrary"),
        name="moba_queries",
    )(x, w_q.astype(BF16), cos_t, sin_t, k_mean)


def _moba_positions_kernel(sel_ref, start_ref, pos_ref, *, dump_row):
    tm = sel_ref.shape[1]
    nb = start_ref.shape[1] // N_HEADS
    blk_id = lax.broadcasted_iota(I32, (nb, tm), 0)
    dump = dump_row + lax.broadcasted_iota(I32, (1, tm), 1) % SC_INDEX_CHUNK
    rows = []
    for hd in range(N_HEADS):
        start = start_ref[0, hd * nb:(hd + 1) * nb, :]
        for s in range(MOBA_TOPK):
            sel = sel_ref[hd * SEL_ROWS + s:hd * SEL_ROWS + s + 1, :]
            rank = sel_ref[hd * SEL_ROWS + MOBA_TOPK + s:hd * SEL_ROWS + MOBA_TOPK + s + 1, :]
            base = jnp.sum(jnp.where(blk_id == sel, start, 0), axis=0, keepdims=True)
            rows.append(jnp.where(sel >= 0, base + rank, dump))
        rows.extend([dump] * (SEL_ROWS - MOBA_TOPK))
    pos_ref[...] = jnp.concatenate(rows, axis=0)


def _moba_positions(sel, starts, seq, dump_row):
    rows, n = sel.shape
    tm = min(4 * TOKEN_ROWS, seq)
    tps = seq // tm
    groups = starts.shape[1]
    return pl.pallas_call(
        functools.partial(_moba_positions_kernel, dump_row=dump_row),
        grid=(n // tm,),
        in_specs=[pl.BlockSpec((rows, tm), lambda i: (0, i)),
                  pl.BlockSpec((1, groups, 1), lambda i: (i // tps, 0, 0))],
        out_specs=pl.BlockSpec((rows, tm), lambda i: (0, i)),
        out_shape=jax.ShapeDtypeStruct((rows, n), I32),
        compiler_params=_params("parallel"),
        name="moba_positions",
    )(sel, starts)


ATTN_SUB_ROWS = 128
ATTN_SUBS_PER_STEP = 32
ATTN_STEP_ROWS = ATTN_SUB_ROWS * ATTN_SUBS_PER_STEP


def _pack_bf16_row_pairs(x):
    h = x.shape[0] // 2
    bits = lax.bitcast_convert_type(x.astype(BF16).astype(F32), U32)
    return (bits[:h] >> 16) | (bits[h:] & U32(HIGH_HALF))


def _transpose_u32(x):
    return lax.bitcast_convert_type(lax.bitcast_convert_type(x, I32).T, U32)


def _scores_t(k, q, keep):
    st = lax.dot_general(k, q, (((1,), (1,)), ((), ())), preferred_element_type=F32)
    if keep is not None:
        st = jnp.where(keep, st, NEG_INF)
    m = jnp.max(st, axis=0, keepdims=True)
    p = jnp.exp(st - m)
    return m, p, jnp.sum(p, axis=0, keepdims=True)


def _group_attn_kernel(step_h_ref, step_b_ref, sub_j_ref, nsteps_ref, qs_ref, k_ref, vt_ref, part_ref):
    del step_h_ref, step_b_ref
    i = pl.program_id(0)
    half = HEAD_DIM // 2

    @pl.when(i < nsteps_ref[0])
    def _():
        for u in range(ATTN_SUBS_PER_STEP):
            j = sub_j_ref[i * ATTN_SUBS_PER_STEP + u]
            rows = pl.ds(u * ATTN_SUB_ROWS, ATTN_SUB_ROWS)
            q = qs_ref[rows, :].astype(BF16)
            m, p, l = _scores_t(k_ref[0, j], q, None)
            ot = jnp.dot(vt_ref[0, j], p.astype(BF16), preferred_element_type=F32) * (1.0 / l)
            lse = lax.bitcast_convert_type(m + jnp.log(l), U32)
            packed_t = jnp.concatenate(
                [_pack_bf16_row_pairs(ot), jnp.broadcast_to(lse, (half, ATTN_SUB_ROWS))], axis=0)
            part_ref[rows, :] = _transpose_u32(packed_t)


def _group_attention(qs, k_blk, vt_blk, step_h, step_b, sub_j, nsteps, nb):
    n_steps = step_h.shape[0]
    row_map = lambda i, sh, sb, sj, ns: (jnp.minimum(i, ns[0] - 1), 0)
    kv_map = lambda i, sh, sb, sj, ns: (sh[i], sb[i], 0, 0)
    return pl.pallas_call(
        _group_attn_kernel,
        grid_spec=pltpu.PrefetchScalarGridSpec(
            num_scalar_prefetch=4,
            grid=(n_steps,),
            in_specs=[pl.BlockSpec((ATTN_STEP_ROWS, HEAD_DIM), row_map),
                      pl.BlockSpec((1, nb, MOBA_BLOCK, HEAD_DIM), kv_map),
                      pl.BlockSpec((1, nb, HEAD_DIM, MOBA_BLOCK), kv_map)],
            out_specs=pl.BlockSpec((ATTN_STEP_ROWS, HEAD_DIM), row_map)),
        out_shape=jax.ShapeDtypeStruct(((n_steps + 1) * ATTN_STEP_ROWS, HEAD_DIM), U32),
        compiler_params=_params("arbitrary"),
        name="moba_group_attention",
    )(step_h, step_b, sub_j, nsteps, qs, k_blk, vt_blk)


def _attn_merge_kernel(x_ref, qp_ref, k_ref, vt_ref, pg_ref, sel_ref, wo_ref, lng_ref, lnb_ref,
                       o_ref, op_ref):
    tm = x_ref.shape[0]
    half = HEAD_DIM // 2
    sel = sel_ref[...]
    key = lax.broadcasted_iota(I32, (tm, tm), 0)
    qry = lax.broadcasted_iota(I32, (tm, tm), 1)
    causal = key <= qry
    heads = []
    for hd in range(N_HEADS):
        q = qp_ref[0, hd].astype(BF16)
        m_own, p, l_own = _scores_t(k_ref[hd, 0], q, causal)
        acc = jnp.dot(vt_ref[hd, 0], p.astype(BF16), preferred_element_type=F32)

        outs, lses = [], []
        m_tot = m_own
        for c in range(MOBA_TOPK):
            part_t = _transpose_u32(pg_ref[c, 0, hd])
            lo, hi = _unpack_bf16_pairs(part_t[:half])
            valid = sel[hd * SEL_ROWS + c:hd * SEL_ROWS + c + 1, :] >= 0
            outs.append(jnp.where(valid, jnp.concatenate([lo, hi], axis=0), 0.0))
            lse = jnp.where(valid, lax.bitcast_convert_type(part_t[half:half + 1], F32), NEG_INF)
            lses.append(lse)
            m_tot = jnp.maximum(m_tot, lse)
        w_own = jnp.exp(m_own - m_tot)
        num = acc * w_own
        den = l_own * w_own
        for c in range(MOBA_TOPK):
            w_c = jnp.exp(lses[c] - m_tot)
            num = num + outs[c] * w_c
            den = den + w_c
        heads.append(num * (1.0 / den))
    att = jnp.concatenate(heads, axis=0).T.astype(BF16)
    mix = jnp.dot(att, wo_ref[...], preferred_element_type=F32)
    x1 = _layer_norm(DN_ALPHA * x_ref[...] + mix, lng_ref[...], lnb_ref[...])
    o_ref[...] = x1
    op_ref[...] = _pack_bf16_pairs(x1)


def _attn_merge(x, qp, k_blk, vt_blk, pg, sel, w_o, ln_g, ln_b):
    n, d = x.shape
    blk = MOBA_BLOCK
    row = lambda v: v.reshape(1, -1).astype(F32)
    return pl.pallas_call(
        _attn_merge_kernel,
        grid=(n // blk,),
        in_specs=[pl.BlockSpec((blk, d), lambda i: (i, 0)),
                  pl.BlockSpec((1, N_HEADS, blk, HEAD_DIM), lambda i: (i, 0, 0, 0)),
                  pl.BlockSpec((N_HEADS, 1, blk, HEAD_DIM), lambda i: (0, i, 0, 0)),
                  pl.BlockSpec((N_HEADS, 1, HEAD_DIM, blk), lambda i: (0, i, 0, 0)),
                  pl.BlockSpec((MOBA_TOPK, 1, N_HEADS, blk, HEAD_DIM), lambda i: (0, i, 0, 0, 0)),
                  pl.BlockSpec((N_HEADS * SEL_ROWS, blk), lambda i: (0, i)),
                  _full((d, d)), _full((1, d)), _full((1, d))],
        out_specs=[pl.BlockSpec((blk, d), lambda i: (i, 0)), pl.BlockSpec((blk, d // 2), lambda i: (i, 0))],
        out_shape=[jax.ShapeDtypeStruct((n, d), F32), jax.ShapeDtypeStruct((n, d // 2), U32)],
        compiler_params=_params("parallel"),
        name="moba_merge",
    )(x, qp, k_blk, vt_blk, pg, sel, w_o.astype(BF16), row(ln_g), row(ln_b))


def _moba_layer(streams, seq, w_q, w_o, ln_g, ln_b, cos_t, sin_t):
    staged = [_moba_regroup(x, seq, w_q, cos_t, sin_t, k_mean) for x, _, _, k_mean in streams]
    gathered = []
    for (x, k_blk, vt_blk, _), (qp, sel, qs, tables, gather_idx) in zip(streams, staged):
        nbt = x.shape[0] // MOBA_BLOCK
        part = _group_attention(qs, k_blk, vt_blk, *tables, seq // MOBA_BLOCK)
        gathered.append(_sc_gather_rows(part, gather_idx).reshape(MOBA_TOPK, nbt, N_HEADS, MOBA_BLOCK, HEAD_DIM))
    return [_attn_merge(x, qp, k_blk, vt_blk, pg, sel, w_o, ln_g, ln_b)
            for (x, k_blk, vt_blk, _), (qp, sel, _, _, _), pg in zip(streams, staged, gathered)]


def _moba_regroup(x, seq, w_q, cos_t, sin_t, k_mean):
    n, d = x.shape
    batch = n // seq
    blk = MOBA_BLOCK
    nb = seq // blk
    nbt = n // blk
    chunk = SC_INDEX_CHUNK
    qp, sel, cnt = _queries(x, seq, w_q, cos_t, sin_t, k_mean)

    counts = cnt[:, :, 0].reshape(batch * N_HEADS, nb)
    gpad = (counts + ATTN_SUB_ROWS - 1) // ATTN_SUB_ROWS * ATTN_SUB_ROWS
    seg = jnp.sum(gpad, axis=1)
    seg_pad = (seg + ATTN_STEP_ROWS - 1) // ATTN_STEP_ROWS * ATTN_STEP_ROWS
    seg_end = jnp.cumsum(seg_pad)
    gend = (seg_end - seg_pad)[:, None] + jnp.cumsum(gpad, axis=1)
    gstart = (gend - gpad).astype(I32)
    steps_per_seg = -(-(MOBA_TOPK * seq + nb * (ATTN_SUB_ROWS - 1)) // ATTN_STEP_ROWS)
    n_steps = batch * N_HEADS * steps_per_seg
    step_first = jnp.arange(n_steps, dtype=I32) * ATTN_STEP_ROWS
    step_seg = jnp.minimum(jnp.sum((seg_end[None, :] <= step_first[:, None]).astype(I32), axis=1),
                           batch * N_HEADS - 1)
    sub_first = jnp.arange(n_steps * ATTN_SUBS_PER_STEP, dtype=I32) * ATTN_SUB_ROWS
    sub_grp = jnp.minimum(jnp.sum((gend.reshape(-1)[None, :] <= sub_first[:, None]).astype(I32), axis=1),
                          batch * N_HEADS * nb - 1)
    nsteps = (seg_end[-1] // ATTN_STEP_ROWS).astype(I32).reshape(1)
    dump_row = n_steps * ATTN_STEP_ROWS

    pos = _moba_positions(sel, gstart.reshape(batch, N_HEADS * nb, 1), seq, dump_row)
    pos5 = pos.reshape(N_HEADS, SEL_ROWS, nbt, blk // chunk, chunk)[:, :MOBA_TOPK]
    scatter_idx = pos5.transpose(2, 0, 3, 1, 4).reshape(nbt * N_HEADS * (blk // chunk), MOBA_TOPK, chunk)
    gather_idx = pos5.transpose(1, 2, 0, 3, 4).reshape(-1)

    qs = _sc_scatter_rows(qp.reshape(n * N_HEADS, HEAD_DIM), scatter_idx, dump_row + chunk)
    tables = ((step_seg % N_HEADS).astype(I32), (step_seg // N_HEADS).astype(I32),
              (sub_grp % nb).astype(I32), nsteps)
    return qp, sel, qs, tables, gather_idx


def _rope_tables(seq):
    half = ROT_DIM // 2
    inv = ROPE_THETA ** (-jnp.arange(0, ROT_DIM, 2, dtype=F32) / ROT_DIM)
    ang = jnp.arange(seq, dtype=F32)[:, None] * inv[None, :]
    cos, sin = jnp.cos(ang), jnp.sin(ang)
    rest = HEAD_DIM - ROT_DIM
    cos_t = jnp.concatenate([cos, cos, jnp.ones((seq, rest), F32)], axis=1)
    sin_t = jnp.concatenate([-sin, sin, jnp.zeros((seq, rest), F32)], axis=1)
    del half
    return cos_t, sin_t


def kernel(x, p, ln_g, ln_b, a_w_in, a_b_in, a_conv_w, a_conv_b, a_gate_a_w, a_gate_a_b, a_gate_i_w,
           a_gate_i_b, a_lambda, a_w_out, kv_ln_g, kv_ln_b, w_kv, b_w_q, b_w_o, router_w, router_b,
           exp_w_gate, exp_w_up, exp_w_down, sh_w_gate, sh_w_up, sh_w_down, ple_w, ple_gate_w):
    batch, seq, d = x.shape
    xs = [x[b] for b in range(batch)]
    p_all = p.reshape(DEPTH * batch, seq, PLE_DIM)
    cos_t, sin_t = _rope_tables(seq)
    kv = None
    for i in range(DEPTH):
        if i < N_A_LAYERS:
            mixed = [_rglru_layer(xb, seq, a_w_in[i], a_b_in[i], a_conv_w[i], a_conv_b[i],
                                  a_gate_a_w[i], a_gate_a_b[i], a_gate_i_w[i], a_gate_i_b[i],
                                  a_lambda[i], a_w_out[i], ln_g[i, 0], ln_b[i, 0]) for xb in xs]
        else:
            if i == N_A_LAYERS:
                kv = []
                for xb in xs:
                    k_blk, vt_blk, k_mean = _shared_kv(xb, seq, kv_ln_g, kv_ln_b, w_kv, cos_t, sin_t)
                    kv.append((k_blk, vt_blk, k_mean.reshape(seq // MOBA_BLOCK, d)))
            j = i - N_A_LAYERS
            mixed = _moba_layer([(xb,) + kvb for xb, kvb in zip(xs, kv)], seq, b_w_q[j], b_w_o[j],
                                ln_g[i, 0], ln_b[i, 0], cos_t, sin_t)
        xs = _moe_layer(mixed, i, router_w[i], router_b[i], exp_w_gate, exp_w_up, exp_w_down,
                        sh_w_gate[i], sh_w_up[i], sh_w_down[i], ln_g[i, 1], ln_b[i, 1],
                        p_all, i * batch, ple_w[i], ple_gate_w[i])
    return jnp.stack(xs, axis=0)
```

```python
import functools

import jax
import jax.numpy as jnp
from jax import lax
from jax.experimental import pallas as pl
from jax.experimental.pallas import tpu as pltpu
from jax.experimental.pallas import tpu_sc as plsc

F32 = jnp.float32
BF16 = jnp.bfloat16
I32 = jnp.int32
U32 = jnp.uint32
HIGH_HALF = 0xFFFF0000

SC_CORES = 2
SC_SUBCORES = 16
SC_WORKERS = SC_CORES * SC_SUBCORES
SC_INDEX_CHUNK = 128
SC_STAGE_BYTES = 256 * 1024

D_MODEL = 1024
DEPTH = 4
N_A_LAYERS = DEPTH // 2
D_RNN = D_MODEL
LRU_BLOCKS = 4
LRU_BLOCK_W = D_RNN // LRU_BLOCKS
CONV_W = 4
LRU_C = 8.0
N_HEADS = 8
HEAD_DIM = D_MODEL // N_HEADS
ROT_DIM = HEAD_DIM // 4
ROPE_THETA = 500000.0
MOBA_BLOCK = 256
MOBA_TOPK = 3
N_EXPERTS = 64
N_GROUPS = 8
GROUP_SIZE = N_EXPERTS // N_GROUPS
TOPK_GROUPS = 4
TOPK_EXPERTS = 8
D_EXPERT = 256
D_SHARED = 256
ROUTED_SCALE = 2.5
PLE_DIM = 256
DN_ALPHA = (2 * DEPTH) ** 0.25
LN_EPS = 1e-5

V7X_VMEM_LIMIT_BYTES = 56 * 1024 * 1024

MIXER_ROWS = 256
TOKEN_ROWS = 512
EXPERT_ROWS = 1024
NEG_INF = float("-inf")


def _params(*sem):
    return pltpu.CompilerParams(dimension_semantics=sem, vmem_limit_bytes=V7X_VMEM_LIMIT_BYTES)


def _layer_norm(z, g, b):
    mu = jnp.mean(z, axis=-1, keepdims=True)
    zc = z - mu
    var = jnp.mean(zc * zc, axis=-1, keepdims=True)
    return zc * lax.rsqrt(var + LN_EPS) * g + b


def _silu(x):
    return x * jax.nn.sigmoid(x)


def _gelu_tanh(x):
    return x * jax.nn.sigmoid(x * (1.5957691216057308 + 0.07135481627159855 * (x * x)))


def _full(shape):
    return pl.BlockSpec(shape, lambda *_: (0,) * len(shape))


def _pack_bf16_pairs(x):
    w = x.shape[1] // 2
    bits = lax.bitcast_convert_type(x.astype(BF16).astype(F32), U32)
    return (bits[:, :w] >> 16) | (bits[:, w:] & U32(HIGH_HALF))


def _unpack_bf16_pairs(u):
    lo = lax.bitcast_convert_type(u << 16, F32)
    hi = lax.bitcast_convert_type(u & U32(HIGH_HALF), F32)
    return lo, hi


def _sc_mesh():
    return plsc.VectorSubcoreMesh(core_axis_name="c", subcore_axis_name="s")


def _sc_worker_id():
    return lax.axis_index("s") * SC_CORES + lax.axis_index("c")


def _sc_chunks_per_step(chunks_per_worker, row_words):
    g = max(1, SC_STAGE_BYTES // (SC_INDEX_CHUNK * row_words * 4))
    while chunks_per_worker % g:
        g -= 1
    return g


def _sc_gather_rows(table, idx):
    b = idx.shape[0]
    w = table.shape[1]
    chunk = SC_INDEX_CHUNK
    chunks_per_worker = b // (SC_WORKERS * chunk)
    assert chunks_per_worker * SC_WORKERS * chunk == b
    g = _sc_chunks_per_step(chunks_per_worker, w)

    @functools.partial(
        pl.kernel, mesh=_sc_mesh(), out_type=jax.ShapeDtypeStruct((b, w), table.dtype),
        scratch_types=[pltpu.VMEM((g, chunk), I32), pltpu.VMEM((g * chunk, w), table.dtype),
                       pltpu.SemaphoreType.DMA])
    def gather(table_hbm, idx_hbm, out_hbm, idx_v, rows_v, sem):
        first = _sc_worker_id() * chunks_per_worker

        @pl.loop(0, chunks_per_worker // g)
        def _(j):
            c0 = first + j * g
            pltpu.sync_copy(idx_hbm.at[pl.ds(c0, g)], idx_v)
            copies = [pltpu.async_copy(table_hbm.at[idx_v.at[q]], rows_v.at[pl.ds(q * chunk, chunk)], sem)
                      for q in range(g)]
            for cp in copies:
                cp.wait()
            pltpu.sync_copy(rows_v, out_hbm.at[pl.ds(c0 * chunk, g * chunk)])

    return gather(table, idx.reshape(b // chunk, chunk))


def _sc_scatter_rows(src, idx, out_rows):
    n, w = src.shape
    chunk = SC_INDEX_CHUNK
    fan = idx.shape[1]
    chunks_per_worker = n // (SC_WORKERS * chunk)
    assert idx.shape == (n // chunk, fan, chunk) and chunks_per_worker * SC_WORKERS * chunk == n
    g = _sc_chunks_per_step(chunks_per_worker, w)

    @functools.partial(
        pl.kernel, mesh=_sc_mesh(), out_type=jax.ShapeDtypeStruct((out_rows, w), src.dtype),
        scratch_types=[pltpu.VMEM((g, fan, chunk), I32), pltpu.VMEM((g * chunk, w), src.dtype),
                       pltpu.SemaphoreType.DMA])
    def scatter(src_hbm, idx_hbm, out_hbm, idx_v, rows_v, sem):
        first = _sc_worker_id() * chunks_per_worker

        @pl.loop(0, chunks_per_worker // g)
        def _(j):
            c0 = first + j * g
            pltpu.sync_copy(src_hbm.at[pl.ds(c0 * chunk, g * chunk)], rows_v)
            pltpu.sync_copy(idx_hbm.at[pl.ds(c0, g)], idx_v)
            copies = [pltpu.async_copy(rows_v.at[pl.ds(q * chunk, chunk)], out_hbm.at[idx_v.at[q, f]], sem)
                      for q in range(g) for f in range(fan)]
            for cp in copies:
                cp.wait()

    return scatter(src, idx)


def _rglru_kernel(x_ref, win_ref, bin_ref, cw_ref, cb_ref, gaw_ref, gab_ref, giw_ref, gib_ref,
                  lam_ref, wout_ref, lng_ref, lnb_ref, o_ref, op_ref, tail_ref, h_ref, *, tiles_per_seq):
    i = pl.program_id(0)
    tm = x_ref.shape[0]

    @pl.when(i % tiles_per_seq == 0)
    def _():
        tail_ref[...] = jnp.zeros_like(tail_ref)
        h_ref[...] = jnp.zeros_like(h_ref)

    x = x_ref[...]
    xy = jnp.dot(x.astype(BF16), win_ref[...], preferred_element_type=F32) + bin_ref[...]
    xb = xy[:, :D_RNN]
    y = _gelu_tanh(xy[:, D_RNN:])

    tail = tail_ref[...]
    row8 = lax.broadcasted_iota(I32, (8, 1), 0)
    xc = cb_ref[...] + xb * cw_ref[CONV_W - 1:CONV_W, :]
    for d in range(1, CONV_W):
        rolled = pltpu.roll(xb, d, 0)
        head = jnp.where(row8 < d, pltpu.roll(tail, d, 0), rolled[:8])
        shifted = jnp.concatenate([head, rolled[8:]], axis=0)
        xc = xc + shifted * cw_ref[CONV_W - 1 - d:CONV_W - d, :]
    tail_ref[...] = xb[tm - 8:, :]

    r_parts, i_parts = [], []
    for n in range(LRU_BLOCKS):
        xg = xc[:, n * LRU_BLOCK_W:(n + 1) * LRU_BLOCK_W].astype(BF16)
        r_parts.append(jnp.dot(xg, gaw_ref[n], preferred_element_type=F32))
        i_parts.append(jnp.dot(xg, giw_ref[n], preferred_element_type=F32))
    r = jax.nn.sigmoid(jnp.concatenate(r_parts, axis=1) + gab_ref[...])
    ig = jax.nn.sigmoid(jnp.concatenate(i_parts, axis=1) + gib_ref[...])

    lam = lam_ref[...]
    softplus_neg_lam = jnp.maximum(-lam, 0.0) + jnp.log1p(jnp.exp(-jnp.abs(lam)))
    log_a = (-LRU_C * r) * softplus_neg_lam
    a = jnp.exp(log_a)
    u = jnp.sqrt(1.0 - a * a) * (ig * xc)

    n_grp = tm // 8
    sub = lax.broadcasted_iota(I32, (1, 8, 1), 1)
    acc_a = a.reshape(n_grp, 8, D_RNN)
    acc_h = u.reshape(n_grp, 8, D_RNN)
    for d in (1, 2, 4):
        keep = sub >= d
        sh_a = pltpu.roll(acc_a, d, 1)
        sh_h = pltpu.roll(acc_h, d, 1)
        acc_h = jnp.where(keep, acc_a * sh_h + acc_h, acc_h)
        acc_a = jnp.where(keep, acc_a * sh_a, acc_a)
    state = h_ref[...]
    groups = []
    for g in range(n_grp):
        hg = acc_h[g] + acc_a[g] * state
        state = hg[7:8, :]
        groups.append(hg)
    h = jnp.concatenate(groups, axis=0)
    h_ref[...] = state

    mix = jnp.dot((h * y).astype(BF16), wout_ref[...], preferred_element_type=F32)
    x1 = _layer_norm(DN_ALPHA * x + mix, lng_ref[...], lnb_ref[...])
    o_ref[...] = x1
    op_ref[...] = _pack_bf16_pairs(x1)


def _rglru_layer(x, seq, w_in, b_in, conv_w, conv_b, ga_w, ga_b, gi_w, gi_b, lam, w_out, ln_g, ln_b):
    n, d = x.shape
    tm = min(MIXER_ROWS, seq)
    row = lambda v: v.reshape(1, -1).astype(F32)
    return pl.pallas_call(
        functools.partial(_rglru_kernel, tiles_per_seq=seq // tm),
        grid=(n // tm,),
        in_specs=[pl.BlockSpec((tm, d), lambda i: (i, 0)),
                  _full((d, 2 * D_RNN)), _full((1, 2 * D_RNN)),
                  _full((CONV_W, D_RNN)), _full((1, D_RNN)),
                  _full((LRU_BLOCKS, LRU_BLOCK_W, LRU_BLOCK_W)), _full((1, D_RNN)),
                  _full((LRU_BLOCKS, LRU_BLOCK_W, LRU_BLOCK_W)), _full((1, D_RNN)),
                  _full((1, D_RNN)), _full((D_RNN, d)), _full((1, d)), _full((1, d))],
        out_specs=[pl.BlockSpec((tm, d), lambda i: (i, 0)), pl.BlockSpec((tm, d // 2), lambda i: (i, 0))],
        out_shape=[jax.ShapeDtypeStruct((n, d), F32), jax.ShapeDtypeStruct((n, d // 2), U32)],
        scratch_shapes=[pltpu.VMEM((8, D_RNN), F32), pltpu.VMEM((1, D_RNN), F32)],
        compiler_params=_params("arbitrary"),
        name="rglru_mixer",
    )(x, w_in.astype(BF16), row(b_in), conv_w, row(conv_b), ga_w.astype(BF16), row(ga_b),
      gi_w.astype(BF16), row(gi_b), row(lam), w_out.astype(BF16), row(ln_g), row(ln_b))


def _first_argmax(cur, idx, size, axis):
    m = jnp.max(cur, axis=axis, keepdims=True)
    first = jnp.min(jnp.where(cur == m, idx, size), axis=axis, keepdims=True)
    return m, first


def _router_kernel(x_ref, rw_ref, rb_ref, e_ref, w_ref, r_ref, cnt_ref, carry_ref):
    i = pl.program_id(0)
    tm = x_ref.shape[0]

    @pl.when(i == 0)
    def _():
        carry_ref[...] = jnp.zeros_like(carry_ref)

    logits = jnp.dot(x_ref[...], rw_ref[...], preferred_element_type=F32,
                     precision=lax.Precision.HIGHEST)
    scores = jax.nn.sigmoid(logits.T[:N_EXPERTS, :])
    choice = scores + rb_ref[...]

    c3 = choice.reshape(N_GROUPS, GROUP_SIZE, tm)
    in_grp = lax.broadcasted_iota(I32, c3.shape, 1)
    m1, i1 = _first_argmax(c3, in_grp, GROUP_SIZE, 1)
    m2 = jnp.max(jnp.where(in_grp == i1, NEG_INF, c3), axis=1, keepdims=True)
    grp_score = (m1 + m2)[:, 0, :]

    grp_id = lax.broadcasted_iota(I32, grp_score.shape, 0)
    grp_sel = jnp.zeros(grp_score.shape, jnp.bool_)
    cur = grp_score
    for _ in range(TOPK_GROUPS):
        _, gi = _first_argmax(cur, grp_id, N_GROUPS, 0)
        hit = grp_id == gi
        grp_sel = jnp.logical_or(grp_sel, hit)
        cur = jnp.where(hit, NEG_INF, cur)

    cur = jnp.where(grp_sel[:, None, :], c3, NEG_INF).reshape(N_EXPERTS, tm)
    exp_id = lax.broadcasted_iota(I32, cur.shape, 0)
    sel = jnp.zeros(cur.shape, F32)
    e_rows, s_rows = [], []
    for _ in range(TOPK_EXPERTS):
        _, ei = _first_argmax(cur, exp_id, N_EXPERTS, 0)
        hit = exp_id == ei
        e_rows.append(ei)
        s_rows.append(jnp.sum(jnp.where(hit, scores, 0.0), axis=0, keepdims=True))
        sel = jnp.where(hit, 1.0, sel)
        cur = jnp.where(hit, NEG_INF, cur)
    e_top = jnp.concatenate(e_rows, axis=0)
    s_top = jnp.concatenate(s_rows, axis=0)
    w_ref[...] = s_top / jnp.sum(s_top, axis=0, keepdims=True) * ROUTED_SCALE
    e_ref[...] = e_top

    t_row = lax.broadcasted_iota(I32, (tm, tm), 0)
    t_col = lax.broadcasted_iota(I32, (tm, tm), 1)
    before = (t_row < t_col).astype(BF16)
    cum = jnp.dot(sel.astype(BF16), before, preferred_element_type=F32) + carry_ref[...]
    r_rows = [jnp.sum(jnp.where(exp_id == e_rows[k], cum, 0.0), axis=0, keepdims=True)
              for k in range(TOPK_EXPERTS)]
    r_ref[...] = jnp.concatenate(r_rows, axis=0).astype(I32)
    carry_ref[...] = carry_ref[...] + jnp.sum(sel, axis=1, keepdims=True)
    cnt_ref[...] = jnp.broadcast_to(carry_ref[...], cnt_ref.shape).astype(I32)


def _router(x, router_w, router_b):
    n, d = x.shape
    tm = min(TOKEN_ROWS, n)
    rw = jnp.pad(router_w, ((0, 0), (0, 128 - N_EXPERTS)))
    k = TOPK_EXPERTS
    tok_spec = pl.BlockSpec((k, tm), lambda i: (0, i))
    return pl.pallas_call(
        _router_kernel,
        grid=(n // tm,),
        in_specs=[pl.BlockSpec((tm, d), lambda i: (i, 0)), _full((d, 128)), _full((N_EXPERTS, 1))],
        out_specs=[tok_spec, tok_spec, tok_spec, _full((N_EXPERTS, 128))],
        out_shape=[jax.ShapeDtypeStruct((k, n), I32), jax.ShapeDtypeStruct((k, n), F32),
                   jax.ShapeDtypeStruct((k, n), I32), jax.ShapeDtypeStruct((N_EXPERTS, 128), I32)],
        scratch_shapes=[pltpu.VMEM((N_EXPERTS, 1), F32)],
        compiler_params=_params("arbitrary"),
        name="moe_router",
    )(x, rw, router_b.reshape(N_EXPERTS, 1).astype(F32))


def _expert_kernel(blk_e_ref, nblk_ref, xs_ref, wg_ref, wu_ref, wd_ref, after_ref, ys_ref,
                   wg_s, wu_s, wd_s):
    del after_ref
    i = pl.program_id(0)

    @pl.when(i < nblk_ref[0])
    def _():
        @pl.when(jnp.logical_or(i == 0, blk_e_ref[i] != blk_e_ref[jnp.maximum(i - 1, 0)]))
        def _():
            wg_s[...] = wg_ref[0, 0].astype(BF16)
            wu_s[...] = wu_ref[0, 0].astype(BF16)
            wd_s[...] = wd_ref[0, 0].astype(BF16)

        lo, hi = _unpack_bf16_pairs(xs_ref[...])
        xs = jnp.concatenate([lo.astype(BF16), hi.astype(BF16)], axis=1)
        g = jnp.dot(xs, wg_s[...], preferred_element_type=F32)
        u = jnp.dot(xs, wu_s[...], preferred_element_type=F32)
        hdn = (_silu(g) * u).astype(BF16)
        ys_ref[...] = _pack_bf16_pairs(jnp.dot(hdn, wd_s[...], preferred_element_type=F32))


_ORDER_ONLY = pl.BlockSpec(memory_space=pl.ANY)


def _experts(xs, blk_e, nblk, layer, w_gate, w_up, w_down, after):
    rows, half = xs.shape
    d = 2 * half
    m = EXPERT_ROWS
    row_map = lambda i, be, nb: (jnp.minimum(i, nb[0] - 1), 0)
    w_map = lambda i, be, nb: (layer, be[i], 0, 0)
    return pl.pallas_call(
        _expert_kernel,
        grid_spec=pltpu.PrefetchScalarGridSpec(
            num_scalar_prefetch=2,
            grid=(rows // m,),
            in_specs=[pl.BlockSpec((m, half), row_map),
                      pl.BlockSpec((1, 1, d, D_EXPERT), w_map),
                      pl.BlockSpec((1, 1, d, D_EXPERT), w_map),
                      pl.BlockSpec((1, 1, D_EXPERT, d), w_map), _ORDER_ONLY],
            out_specs=pl.BlockSpec((m, half), row_map),
            scratch_shapes=[pltpu.VMEM((d, D_EXPERT), BF16), pltpu.VMEM((d, D_EXPERT), BF16),
                            pltpu.VMEM((D_EXPERT, d), BF16)]),
        out_shape=jax.ShapeDtypeStruct((rows, half), U32),
        compiler_params=_params("arbitrary"),
        name="moe_experts",
    )(blk_e, nblk, xs, w_gate, w_up, w_down, after)


def _combine_kernel(x_ref, yg_ref, w_ref, sg_ref, su_ref, sd_ref, lng_ref, lnb_ref,
                    p_ref, pw_ref, pg_ref, after_ref, o_ref):
    del after_ref
    x = x_ref[...]
    wt = w_ref[...].T
    moe_lo, moe_hi = None, None
    for k in range(TOPK_EXPERTS):
        lo, hi = _unpack_bf16_pairs(yg_ref[k])
        wk = wt[:, k:k + 1]
        moe_lo = wk * lo if k == 0 else moe_lo + wk * lo
        moe_hi = wk * hi if k == 0 else moe_hi + wk * hi
    moe = jnp.concatenate([moe_lo, moe_hi], axis=1)
    xb = x.astype(BF16)
    hdn = _silu(jnp.dot(xb, sg_ref[...], preferred_element_type=F32)) * \
        jnp.dot(xb, su_ref[...], preferred_element_type=F32)
    shared = jnp.dot(hdn.astype(BF16), sd_ref[...], preferred_element_type=F32)
    x2 = _layer_norm(DN_ALPHA * x + (moe + shared), lng_ref[...], lnb_ref[...])
    emb = jnp.dot(p_ref[0].astype(BF16), pw_ref[...], preferred_element_type=F32)
    gate = jax.nn.sigmoid(jnp.dot(x2.astype(BF16), pg_ref[...], preferred_element_type=F32))
    o_ref[...] = x2 + emb * gate


def _combine(x, yg, wgt, s_gate, s_up, s_down, ln_g, ln_b, p_all, p_index, ple_w, ple_gate_w, after):
    n, d = x.shape
    tm = min(TOKEN_ROWS // 2, n)
    k = TOPK_EXPERTS
    row = lambda v: v.reshape(1, -1).astype(F32)
    return pl.pallas_call(
        _combine_kernel,
        grid=(n // tm,),
        in_specs=[pl.BlockSpec((tm, d), lambda i: (i, 0)),
                  pl.BlockSpec((k, tm, d // 2), lambda i: (0, i, 0)),
                  pl.BlockSpec((k, tm), lambda i: (0, i)),
                  _full((d, D_SHARED)), _full((d, D_SHARED)), _full((D_SHARED, d)),
                  _full((1, d)), _full((1, d)),
                  pl.BlockSpec((1, tm, PLE_DIM), lambda i: (p_index, i, 0)),
                  _full((PLE_DIM, d)), _full((d, d)), _ORDER_ONLY],
        out_specs=pl.BlockSpec((tm, d), lambda i: (i, 0)),
        out_shape=jax.ShapeDtypeStruct((n, d), F32),
        compiler_params=_params("parallel"),
        name="moe_combine",
    )(x, yg, wgt, s_gate.astype(BF16), s_up.astype(BF16), s_down.astype(BF16), row(ln_g), row(ln_b),
      p_all, ple_w.astype(BF16), ple_gate_w.astype(BF16), after)


def _positions_kernel(e_ref, r_ref, start_ref, pos_ref):
    e = e_ref[...]
    start = start_ref[...]
    grp_id = lax.broadcasted_iota(I32, (start.shape[0], e.shape[1]), 0)
    rows = [jnp.sum(jnp.where(grp_id == e[k:k + 1], start, 0), axis=0, keepdims=True)
            for k in range(e.shape[0])]
    pos_ref[...] = jnp.concatenate(rows, axis=0) + r_ref[...]


def _positions(e_idx, rank, starts):
    k, n = e_idx.shape
    tm = min(4 * TOKEN_ROWS, n)
    groups = starts.shape[0]
    tok_spec = pl.BlockSpec((k, tm), lambda i: (0, i))
    return pl.pallas_call(
        _positions_kernel,
        grid=(n // tm,),
        in_specs=[tok_spec, tok_spec, _full((groups, 1))],
        out_specs=tok_spec,
        out_shape=jax.ShapeDtypeStruct((k, n), I32),
        compiler_params=_params("parallel"),
        name="group_positions",
    )(e_idx, rank, starts.reshape(groups, 1))


def _moe_layer(streams, layer, router_w, router_b, w_gate, w_up, w_down, s_gate, s_up, s_down,
               ln_g, ln_b, p_all, p_index, ple_w, ple_gate_w):
    m = EXPERT_ROWS
    k = TOPK_EXPERTS
    chunk = SC_INDEX_CHUNK
    routed = []
    for x, xp in streams:
        n, d = x.shape
        e_idx, wgt, rank, cnt = _router(x, router_w, router_b)
        counts = cnt[:, 0]
        padded = (counts + m - 1) // m * m
        pends = jnp.cumsum(padded)
        n_blocks = (n * k) // m + N_EXPERTS
        blk_first = jnp.arange(n_blocks, dtype=I32) * m
        blk_e = jnp.minimum(jnp.sum((pends[None, :] <= blk_first[:, None]).astype(I32), axis=1),
                            N_EXPERTS - 1)
        nblk = (pends[-1] // m).astype(I32).reshape(1)
        pos = _positions(e_idx, rank, (pends - padded).astype(I32))
        pos_chunks = pos.reshape(k, n // chunk, chunk).transpose(1, 0, 2)
        xs = _sc_scatter_rows(xp, pos_chunks, n_blocks * m)
        routed.append((xs, blk_e, nblk, pos, wgt))
    gathered, order = [], routed[-1][3]
    for (x, _), (xs, blk_e, nblk, pos, wgt) in zip(streams, routed):
        n, d = x.shape
        ys = _experts(xs, blk_e, nblk, layer, w_gate, w_up, w_down, order)
        gathered.append(_sc_gather_rows(ys, pos.reshape(-1)).reshape(k, n, d // 2))
        order = ys
    outs = []
    for s, ((x, _), (_, _, _, _, wgt), yg) in enumerate(zip(streams, routed, gathered)):
        outs.append(_combine(x, yg, wgt, s_gate, s_up, s_down, ln_g, ln_b, p_all, p_index + s,
                             ple_w, ple_gate_w, order))
        order = outs[-1]
    return outs


def _rope(t, cos_t, sin_t):
    half = ROT_DIM // 2
    width = t.shape[1]
    lane = lax.broadcasted_iota(I32, (1, width), 1) % HEAD_DIM
    partner = jnp.where(lane < half, pltpu.roll(t, width - half, 1), pltpu.roll(t, half, 1))
    cos_f = jnp.concatenate([cos_t] * N_HEADS, axis=1)
    sin_f = jnp.concatenate([sin_t] * N_HEADS, axis=1)
    return t * cos_f + partner * sin_f


def _kv_kernel(x_ref, g_ref, b_ref, wkv_ref, cos_ref, sin_ref, k_ref, vt_ref, km_ref):
    h = _layer_norm(x_ref[...], g_ref[...], b_ref[...])
    kv = jnp.dot(h.astype(BF16), wkv_ref[...], preferred_element_type=F32)
    k = _rope(kv[:, :D_MODEL], cos_ref[...], sin_ref[...])
    vt = kv[:, D_MODEL:].T
    km_ref[0] = jnp.mean(k, axis=0, keepdims=True)
    for hd in range(N_HEADS):
        k_ref[hd, 0] = k[:, hd * HEAD_DIM:(hd + 1) * HEAD_DIM].astype(BF16)
        vt_ref[hd, 0] = vt[hd * HEAD_DIM:(hd + 1) * HEAD_DIM, :].astype(BF16)


def _shared_kv(x, seq, ln_g, ln_b, w_kv, cos_t, sin_t):
    n, d = x.shape
    blk = MOBA_BLOCK
    nbt = n // blk
    spb = seq // blk
    row = lambda v: v.reshape(1, -1).astype(F32)
    return pl.pallas_call(
        _kv_kernel,
        grid=(nbt,),
        in_specs=[pl.BlockSpec((blk, d), lambda i: (i, 0)), _full((1, d)), _full((1, d)),
                  _full((d, 2 * d)),
                  pl.BlockSpec((blk, HEAD_DIM), lambda i: (i % spb, 0)),
                  pl.BlockSpec((blk, HEAD_DIM), lambda i: (i % spb, 0))],
        out_specs=[pl.BlockSpec((N_HEADS, 1, blk, HEAD_DIM), lambda i: (0, i, 0, 0)),
                   pl.BlockSpec((N_HEADS, 1, HEAD_DIM, blk), lambda i: (0, i, 0, 0)),
                   pl.BlockSpec((1, 1, d), lambda i: (i, 0, 0))],
        out_shape=[jax.ShapeDtypeStruct((N_HEADS, nbt, blk, HEAD_DIM), BF16),
                   jax.ShapeDtypeStruct((N_HEADS, nbt, HEAD_DIM, blk), BF16),
                   jax.ShapeDtypeStruct((nbt, 1, d), F32)],
        compiler_params=_params("parallel"),
        name="shared_kv",
    )(x, row(ln_g), row(ln_b), w_kv.astype(BF16), cos_t, sin_t)


SEL_ROWS = 8


def _query_kernel(x_ref, wq_ref, cos_ref, sin_ref, km_ref, qp_ref, sel_ref, cnt_ref, carry_ref,
                  *, blocks_per_seq):
    i = pl.program_id(0)
    tm = x_ref.shape[0]
    nb = km_ref.shape[0]
    own = i % blocks_per_seq

    @pl.when(own == 0)
    def _():
        carry_ref[...] = jnp.zeros_like(carry_ref)

    q = jnp.dot(x_ref[...].astype(BF16), wq_ref[...], preferred_element_type=F32)
    q = _rope(q, cos_ref[...], sin_ref[...]) * (HEAD_DIM ** -0.5)

    blk_id = lax.broadcasted_iota(I32, (nb, tm), 0)
    t_row = lax.broadcasted_iota(I32, (tm, tm), 0)
    t_col = lax.broadcasted_iota(I32, (tm, tm), 1)
    before = (t_row < t_col).astype(BF16)
    km = km_ref[...]
    for hd in range(N_HEADS):
        lo, hi = hd * HEAD_DIM, (hd + 1) * HEAD_DIM
        q_h = q[:, lo:hi]
        qp_ref[0, hd] = q_h
        gate = lax.dot_general(km[:, lo:hi], q_h, (((1,), (1,)), ((), ())),
                               preferred_element_type=F32, precision=lax.Precision.HIGHEST)
        cur = jnp.where(blk_id < own, gate, NEG_INF)
        sel_rows = []
        chosen = jnp.zeros((nb, tm), F32)
        for _ in range(MOBA_TOPK):
            m, bi = _first_argmax(cur, blk_id, nb, 0)
            valid = m > NEG_INF
            hit = jnp.logical_and(blk_id == bi, valid)
            sel_rows.append(jnp.where(valid, bi, -1))
            chosen = jnp.where(hit, 1.0, chosen)
            cur = jnp.where(blk_id == bi, NEG_INF, cur)
        carry = carry_ref[hd * nb:(hd + 1) * nb, :]
        cum = jnp.dot(chosen.astype(BF16), before, preferred_element_type=F32) + carry
        rank_rows = [jnp.sum(jnp.where(blk_id == s, cum, 0.0), axis=0, keepdims=True).astype(I32)
                     for s in sel_rows]
        carry_ref[hd * nb:(hd + 1) * nb, :] = carry + jnp.sum(chosen, axis=1, keepdims=True)
        pad = jnp.zeros((SEL_ROWS - 2 * MOBA_TOPK, tm), I32)
        sel_ref[hd * SEL_ROWS:(hd + 1) * SEL_ROWS, :] = jnp.concatenate(sel_rows + rank_rows + [pad], axis=0)
    cnt_ref[0] = jnp.broadcast_to(carry_ref[...], cnt_ref.shape[1:]).astype(I32)


def _queries(x, seq, w_q, cos_t, sin_t, k_mean):
    n, d = x.shape
    blk = MOBA_BLOCK
    nb = seq // blk
    return pl.pallas_call(
        functools.partial(_query_kernel, blocks_per_seq=nb),
        grid=(n // blk,),
        in_specs=[pl.BlockSpec((blk, d), lambda i: (i, 0)), _full((d, d)),
                  pl.BlockSpec((blk, HEAD_DIM), lambda i: (i % nb, 0)),
                  pl.BlockSpec((blk, HEAD_DIM), lambda i: (i % nb, 0)),
                  pl.BlockSpec((nb, d), lambda i: (i // nb, 0))],
        out_specs=[pl.BlockSpec((1, N_HEADS, blk, HEAD_DIM), lambda i: (i, 0, 0, 0)),
                   pl.BlockSpec((N_HEADS * SEL_ROWS, blk), lambda i: (0, i)),
                   pl.BlockSpec((1, N_HEADS * nb, 128), lambda i: (i // nb, 0, 0))],
        out_shape=[jax.ShapeDtypeStruct((n // blk, N_HEADS, blk, HEAD_DIM), F32),
                   jax.ShapeDtypeStruct((N_HEADS * SEL_ROWS, n), I32),
                   jax.ShapeDtypeStruct((n // seq, N_HEADS * nb, 128), I32)],
        scratch_shapes=[pltpu.VMEM((N_HEADS * nb, 1), F32)],
        compiler_params=_params("arbitrary"),
        name="moba_queries",
    )(x, w_q.astype(BF16), cos_t, sin_t, k_mean)


def _moba_positions_kernel(sel_ref, start_ref, pos_ref, *, dump_row):
    tm = sel_ref.shape[1]
    nb = start_ref.shape[1] // N_HEADS
    blk_id = lax.broadcasted_iota(I32, (nb, tm), 0)
    dump = dump_row + lax.broadcasted_iota(I32, (1, tm), 1) % SC_INDEX_CHUNK
    rows = []
    for hd in range(N_HEADS):
        start = start_ref[0, hd * nb:(hd + 1) * nb, :]
        for s in range(MOBA_TOPK):
            sel = sel_ref[hd * SEL_ROWS + s:hd * SEL_ROWS + s + 1, :]
            rank = sel_ref[hd * SEL_ROWS + MOBA_TOPK + s:hd * SEL_ROWS + MOBA_TOPK + s + 1, :]
            base = jnp.sum(jnp.where(blk_id == sel, start, 0), axis=0, keepdims=True)
            rows.append(jnp.where(sel >= 0, base + rank, dump))
        rows.extend([dump] * (SEL_ROWS - MOBA_TOPK))
    pos_ref[...] = jnp.concatenate(rows, axis=0)


def _moba_positions(sel, starts, seq, dump_row):
    rows, n = sel.shape
    tm = min(4 * TOKEN_ROWS, seq)
    tps = seq // tm
    groups = starts.shape[1]
    return pl.pallas_call(
        functools.partial(_moba_positions_kernel, dump_row=dump_row),
        grid=(n // tm,),
        in_specs=[pl.BlockSpec((rows, tm), lambda i: (0, i)),
                  pl.BlockSpec((1, groups, 1), lambda i: (i // tps, 0, 0))],
        out_specs=pl.BlockSpec((rows, tm), lambda i: (0, i)),
        out_shape=jax.ShapeDtypeStruct((rows, n), I32),
        compiler_params=_params("parallel"),
        name="moba_positions",
    )(sel, starts)


ATTN_SUB_ROWS = 128
ATTN_SUBS_PER_STEP = 32
ATTN_STEP_ROWS = ATTN_SUB_ROWS * ATTN_SUBS_PER_STEP


def _pack_bf16_row_pairs(x):
    h = x.shape[0] // 2
    bits = lax.bitcast_convert_type(x.astype(BF16).astype(F32), U32)
    return (bits[:h] >> 16) | (bits[h:] & U32(HIGH_HALF))


def _transpose_u32(x):
    return lax.bitcast_convert_type(lax.bitcast_convert_type(x, I32).T, U32)


def _scores_t(k, q, keep):
    st = lax.dot_general(k, q, (((1,), (1,)), ((), ())), preferred_element_type=F32)
    if keep is not None:
        st = jnp.where(keep, st, NEG_INF)
    m = jnp.max(st, axis=0, keepdims=True)
    p = jnp.exp(st - m)
    return m, p, jnp.sum(p, axis=0, keepdims=True)


def _group_attn_kernel(step_h_ref, step_b_ref, sub_j_ref, nsteps_ref, qs_ref, k_ref, vt_ref, after_ref,
                       part_ref):
    del step_h_ref, step_b_ref, after_ref
    i = pl.program_id(0)
    half = HEAD_DIM // 2

    @pl.when(i < nsteps_ref[0])
    def _():
        for u in range(ATTN_SUBS_PER_STEP):
            j = sub_j_ref[i * ATTN_SUBS_PER_STEP + u]
            rows = pl.ds(u * ATTN_SUB_ROWS, ATTN_SUB_ROWS)
            q = qs_ref[rows, :].astype(BF16)
            m, p, l = _scores_t(k_ref[0, j], q, None)
            ot = jnp.dot(vt_ref[0, j], p.astype(BF16), preferred_element_type=F32) * (1.0 / l)
            lse = lax.bitcast_convert_type(m + jnp.log(l), U32)
            packed_t = jnp.concatenate(
                [_pack_bf16_row_pairs(ot), jnp.broadcast_to(lse, (half, ATTN_SUB_ROWS))], axis=0)
            part_ref[rows, :] = _transpose_u32(packed_t)


def _group_attention(qs, k_blk, vt_blk, step_h, step_b, sub_j, nsteps, nb, after):
    n_steps = step_h.shape[0]
    row_map = lambda i, sh, sb, sj, ns: (jnp.minimum(i, ns[0] - 1), 0)
    kv_map = lambda i, sh, sb, sj, ns: (sh[i], sb[i], 0, 0)
    return pl.pallas_call(
        _group_attn_kernel,
        grid_spec=pltpu.PrefetchScalarGridSpec(
            num_scalar_prefetch=4,
            grid=(n_steps,),
            in_specs=[pl.BlockSpec((ATTN_STEP_ROWS, HEAD_DIM), row_map),
                      pl.BlockSpec((1, nb, MOBA_BLOCK, HEAD_DIM), kv_map),
                      pl.BlockSpec((1, nb, HEAD_DIM, MOBA_BLOCK), kv_map), _ORDER_ONLY],
            out_specs=pl.BlockSpec((ATTN_STEP_ROWS, HEAD_DIM), row_map)),
        out_shape=jax.ShapeDtypeStruct(((n_steps + 1) * ATTN_STEP_ROWS, HEAD_DIM), U32),
        compiler_params=_params("arbitrary"),
        name="moba_group_attention",
    )(step_h, step_b, sub_j, nsteps, qs, k_blk, vt_blk, after)


def _attn_merge_kernel(x_ref, qp_ref, k_ref, vt_ref, pg_ref, sel_ref, wo_ref, lng_ref, lnb_ref,
                       after_ref, o_ref, op_ref):
    del after_ref
    tm = x_ref.shape[0]
    half = HEAD_DIM // 2
    sel = sel_ref[...]
    key = lax.broadcasted_iota(I32, (tm, tm), 0)
    qry = lax.broadcasted_iota(I32, (tm, tm), 1)
    causal = key <= qry
    heads = []
    for hd in range(N_HEADS):
        q = qp_ref[0, hd].astype(BF16)
        m_own, p, l_own = _scores_t(k_ref[hd, 0], q, causal)
        acc = jnp.dot(vt_ref[hd, 0], p.astype(BF16), preferred_element_type=F32)

        outs, lses = [], []
        m_tot = m_own
        for c in range(MOBA_TOPK):
            part_t = _transpose_u32(pg_ref[c, 0, hd])
            lo, hi = _unpack_bf16_pairs(part_t[:half])
            valid = sel[hd * SEL_ROWS + c:hd * SEL_ROWS + c + 1, :] >= 0
            outs.append(jnp.where(valid, jnp.concatenate([lo, hi], axis=0), 0.0))
            lse = jnp.where(valid, lax.bitcast_convert_type(part_t[half:half + 1], F32), NEG_INF)
            lses.append(lse)
            m_tot = jnp.maximum(m_tot, lse)
        w_own = jnp.exp(m_own - m_tot)
        num = acc * w_own
        den = l_own * w_own
        for c in range(MOBA_TOPK):
            w_c = jnp.exp(lses[c] - m_tot)
            num = num + outs[c] * w_c
            den = den + w_c
        heads.append(num * (1.0 / den))
    att = jnp.concatenate(heads, axis=0).T.astype(BF16)
    mix = jnp.dot(att, wo_ref[...], preferred_element_type=F32)
    x1 = _layer_norm(DN_ALPHA * x_ref[...] + mix, lng_ref[...], lnb_ref[...])
    o_ref[...] = x1
    op_ref[...] = _pack_bf16_pairs(x1)


def _attn_merge(x, qp, k_blk, vt_blk, pg, sel, w_o, ln_g, ln_b, after):
    n, d = x.shape
    blk = MOBA_BLOCK
    row = lambda v: v.reshape(1, -1).astype(F32)
    return pl.pallas_call(
        _attn_merge_kernel,
        grid=(n // blk,),
        in_specs=[pl.BlockSpec((blk, d), lambda i: (i, 0)),
                  pl.BlockSpec((1, N_HEADS, blk, HEAD_DIM), lambda i: (i, 0, 0, 0)),
                  pl.BlockSpec((N_HEADS, 1, blk, HEAD_DIM), lambda i: (0, i, 0, 0)),
                  pl.BlockSpec((N_HEADS, 1, HEAD_DIM, blk), lambda i: (0, i, 0, 0)),
                  pl.BlockSpec((MOBA_TOPK, 1, N_HEADS, blk, HEAD_DIM), lambda i: (0, i, 0, 0, 0)),
                  pl.BlockSpec((N_HEADS * SEL_ROWS, blk), lambda i: (0, i)),
                  _full((d, d)), _full((1, d)), _full((1, d)), _ORDER_ONLY],
        out_specs=[pl.BlockSpec((blk, d), lambda i: (i, 0)), pl.BlockSpec((blk, d // 2), lambda i: (i, 0))],
        out_shape=[jax.ShapeDtypeStruct((n, d), F32), jax.ShapeDtypeStruct((n, d // 2), U32)],
        compiler_params=_params("parallel"),
        name="moba_merge",
    )(x, qp, k_blk, vt_blk, pg, sel, w_o.astype(BF16), row(ln_g), row(ln_b), after)


def _moba_layer(streams, seq, w_q, w_o, ln_g, ln_b, cos_t, sin_t):
    staged = [_moba_regroup(x, seq, w_q, cos_t, sin_t, k_mean) for x, _, _, k_mean in streams]
    gathered, order = [], staged[-1][1]
    for (x, k_blk, vt_blk, _), (qp, sel, qs, tables, gather_idx) in zip(streams, staged):
        nbt = x.shape[0] // MOBA_BLOCK
        part = _group_attention(qs, k_blk, vt_blk, *tables, seq // MOBA_BLOCK, order)
        gathered.append(_sc_gather_rows(part, gather_idx).reshape(MOBA_TOPK, nbt, N_HEADS, MOBA_BLOCK, HEAD_DIM))
        order = part
    outs = []
    for (x, k_blk, vt_blk, _), (qp, sel, _, _, _), pg in zip(streams, staged, gathered):
        outs.append(_attn_merge(x, qp, k_blk, vt_blk, pg, sel, w_o, ln_g, ln_b, order))
        order = outs[-1][0]
    return outs


def _moba_regroup(x, seq, w_q, cos_t, sin_t, k_mean):
    n, d = x.shape
    batch = n // seq
    blk = MOBA_BLOCK
    nb = seq // blk
    nbt = n // blk
    chunk = SC_INDEX_CHUNK
    qp, sel, cnt = _queries(x, seq, w_q, cos_t, sin_t, k_mean)

    counts = cnt[:, :, 0].reshape(batch * N_HEADS, nb)
    gpad = (counts + ATTN_SUB_ROWS - 1) // ATTN_SUB_ROWS * ATTN_SUB_ROWS
    seg = jnp.sum(gpad, axis=1)
    seg_pad = (seg + ATTN_STEP_ROWS - 1) // ATTN_STEP_ROWS * ATTN_STEP_ROWS
    seg_end = jnp.cumsum(seg_pad)
    gend = (seg_end - seg_pad)[:, None] + jnp.cumsum(gpad, axis=1)
    gstart = (gend - gpad).astype(I32)
    steps_per_seg = -(-(MOBA_TOPK * seq + nb * (ATTN_SUB_ROWS - 1)) // ATTN_STEP_ROWS)
    n_steps = batch * N_HEADS * steps_per_seg
    step_first = jnp.arange(n_steps, dtype=I32) * ATTN_STEP_ROWS
    step_seg = jnp.minimum(jnp.sum((seg_end[None, :] <= step_first[:, None]).astype(I32), axis=1),
                           batch * N_HEADS - 1)
    sub_first = jnp.arange(n_steps * ATTN_SUBS_PER_STEP, dtype=I32) * ATTN_SUB_ROWS
    sub_grp = jnp.minimum(jnp.sum((gend.reshape(-1)[None, :] <= sub_first[:, None]).astype(I32), axis=1),
                          batch * N_HEADS * nb - 1)
    nsteps = (seg_end[-1] // ATTN_STEP_ROWS).astype(I32).reshape(1)
    dump_row = n_steps * ATTN_STEP_ROWS

    pos = _moba_positions(sel, gstart.reshape(batch, N_HEADS * nb, 1), seq, dump_row)
    pos5 = pos.reshape(N_HEADS, SEL_ROWS, nbt, blk // chunk, chunk)[:, :MOBA_TOPK]
    scatter_idx = pos5.transpose(2, 0, 3, 1, 4).reshape(nbt * N_HEADS * (blk // chunk), MOBA_TOPK, chunk)
    gather_idx = pos5.transpose(1, 2, 0, 3, 4).reshape(-1)

    qs = _sc_scatter_rows(qp.reshape(n * N_HEADS, HEAD_DIM), scatter_idx, dump_row + chunk)
    tables = ((step_seg % N_HEADS).astype(I32), (step_seg // N_HEADS).astype(I32),
              (sub_grp % nb).astype(I32), nsteps)
    return qp, sel, qs, tables, gather_idx


def _rope_tables(seq):
    half = ROT_DIM // 2
    inv = ROPE_THETA ** (-jnp.arange(0, ROT_DIM, 2, dtype=F32) / ROT_DIM)
    ang = jnp.arange(seq, dtype=F32)[:, None] * inv[None, :]
    cos, sin = jnp.cos(ang), jnp.sin(ang)
    rest = HEAD_DIM - ROT_DIM
    cos_t = jnp.concatenate([cos, cos, jnp.ones((seq, rest), F32)], axis=1)
    sin_t = jnp.concatenate([-sin, sin, jnp.zeros((seq, rest), F32)], axis=1)
    del half
    return cos_t, sin_t


def kernel(x, p, ln_g, ln_b, a_w_in, a_b_in, a_conv_w, a_conv_b, a_gate_a_w, a_gate_a_b, a_gate_i_w,
           a_gate_i_b, a_lambda, a_w_out, kv_ln_g, kv_ln_b, w_kv, b_w_q, b_w_o, router_w, router_b,
           exp_w_gate, exp_w_up, exp_w_down, sh_w_gate, sh_w_up, sh_w_down, ple_w, ple_gate_w):
    batch, seq, d = x.shape
    xs = [x[b] for b in range(batch)]
    p_all = p.reshape(DEPTH * batch, seq, PLE_DIM)
    cos_t, sin_t = _rope_tables(seq)
    kv = None
    for i in range(DEPTH):
        if i < N_A_LAYERS:
            mixed = [_rglru_layer(xb, seq, a_w_in[i], a_b_in[i], a_conv_w[i], a_conv_b[i],
                                  a_gate_a_w[i], a_gate_a_b[i], a_gate_i_w[i], a_gate_i_b[i],
                                  a_lambda[i], a_w_out[i], ln_g[i, 0], ln_b[i, 0]) for xb in xs]
        else:
            if i == N_A_LAYERS:
                kv = []
                for xb in xs:
                    k_blk, vt_blk, k_mean = _shared_kv(xb, seq, kv_ln_g, kv_ln_b, w_kv, cos_t, sin_t)
                    kv.append((k_blk, vt_blk, k_mean.reshape(seq // MOBA_BLOCK, d)))
            j = i - N_A_LAYERS
            mixed = _moba_layer([(xb,) + kvb for xb, kvb in zip(xs, kv)], seq, b_w_q[j], b_w_o[j],
                                ln_g[i, 0], ln_b[i, 0], cos_t, sin_t)
        xs = _moe_layer(mixed, i, router_w[i], router_b[i], exp_w_gate, exp_w_up, exp_w_down,
                        sh_w_gate[i], sh_w_up[i], sh_w_down[i], ln_g[i, 1], ln_b[i, 1],
                        p_all, i * batch, ple_w[i], ple_gate_w[i])
    return jnp.stack(xs, axis=0)
```

```python
import functools

import jax
import jax.numpy as jnp
from jax import lax
from jax.experimental import pallas as pl
from jax.experimental.pallas import tpu as pltpu
from jax.experimental.pallas import tpu_sc as plsc

F32 = jnp.float32
BF16 = jnp.bfloat16
I32 = jnp.int32
U32 = jnp.uint32
HIGH_HALF = 0xFFFF0000

SC_CORES = 2
SC_SUBCORES = 16
SC_WORKERS = SC_CORES * SC_SUBCORES
SC_INDEX_CHUNK = 128
SC_STAGE_BYTES = 256 * 1024

D_MODEL = 1024
DEPTH = 4
N_A_LAYERS = DEPTH // 2
D_RNN = D_MODEL
LRU_BLOCKS = 4
LRU_BLOCK_W = D_RNN // LRU_BLOCKS
CONV_W = 4
LRU_C = 8.0
N_HEADS = 8
HEAD_DIM = D_MODEL // N_HEADS
ROT_DIM = HEAD_DIM // 4
ROPE_THETA = 500000.0
MOBA_BLOCK = 256
MOBA_TOPK = 3
N_EXPERTS = 64
N_GROUPS = 8
GROUP_SIZE = N_EXPERTS // N_GROUPS
TOPK_GROUPS = 4
TOPK_EXPERTS = 8
D_EXPERT = 256
D_SHARED = 256
ROUTED_SCALE = 2.5
PLE_DIM = 256
DN_ALPHA = (2 * DEPTH) ** 0.25
LN_EPS = 1e-5

V7X_VMEM_LIMIT_BYTES = 56 * 1024 * 1024

MIXER_ROWS = 256
TOKEN_ROWS = 512
EXPERT_ROWS = 1024
NEG_INF = float("-inf")


def _params(*sem):
    return pltpu.CompilerParams(dimension_semantics=sem, vmem_limit_bytes=V7X_VMEM_LIMIT_BYTES)


def _layer_norm(z, g, b):
    mu = jnp.mean(z, axis=-1, keepdims=True)
    zc = z - mu
    var = jnp.mean(zc * zc, axis=-1, keepdims=True)
    return zc * lax.rsqrt(var + LN_EPS) * g + b


def _silu(x):
    return x * jax.nn.sigmoid(x)


def _gelu_tanh(x):
    return x * jax.nn.sigmoid(x * (1.5957691216057308 + 0.07135481627159855 * (x * x)))


def _full(shape):
    return pl.BlockSpec(shape, lambda *_: (0,) * len(shape))


def _pack_bf16_pairs(x):
    w = x.shape[1] // 2
    bits = lax.bitcast_convert_type(x.astype(BF16).astype(F32), U32)
    return (bits[:, :w] >> 16) | (bits[:, w:] & U32(HIGH_HALF))


def _unpack_bf16_pairs(u):
    lo = lax.bitcast_convert_type(u << 16, F32)
    hi = lax.bitcast_convert_type(u & U32(HIGH_HALF), F32)
    return lo, hi


def _sc_mesh():
    return plsc.VectorSubcoreMesh(core_axis_name="c", subcore_axis_name="s")


def _sc_worker_id():
    return lax.axis_index("s") * SC_CORES + lax.axis_index("c")


def _sc_chunks_per_step(chunks_per_worker, row_words):
    g = max(1, SC_STAGE_BYTES // (SC_INDEX_CHUNK * row_words * 4))
    while chunks_per_worker % g:
        g -= 1
    return g


def _sc_gather_rows(table, idx):
    b = idx.shape[0]
    w = table.shape[1]
    chunk = SC_INDEX_CHUNK
    chunks_per_worker = b // (SC_WORKERS * chunk)
    assert chunks_per_worker * SC_WORKERS * chunk == b
    g = _sc_chunks_per_step(chunks_per_worker, w)

    @functools.partial(
        pl.kernel, mesh=_sc_mesh(), out_type=jax.ShapeDtypeStruct((b, w), table.dtype),
        scratch_types=[pltpu.VMEM((g, chunk), I32), pltpu.VMEM((g * chunk, w), table.dtype),
                       pltpu.SemaphoreType.DMA])
    def gather(table_hbm, idx_hbm, out_hbm, idx_v, rows_v, sem):
        first = _sc_worker_id() * chunks_per_worker

        @pl.loop(0, chunks_per_worker // g)
        def _(j):
            c0 = first + j * g
            pltpu.sync_copy(idx_hbm.at[pl.ds(c0, g)], idx_v)
            copies = [pltpu.async_copy(table_hbm.at[idx_v.at[q]], rows_v.at[pl.ds(q * chunk, chunk)], sem)
                      for q in range(g)]
            for cp in copies:
                cp.wait()
            pltpu.sync_copy(rows_v, out_hbm.at[pl.ds(c0 * chunk, g * chunk)])

    return gather(table, idx.reshape(b // chunk, chunk))


def _sc_scatter_rows(src, idx, out_rows):
    n, w = src.shape
    chunk = SC_INDEX_CHUNK
    fan = idx.shape[1]
    chunks_per_worker = n // (SC_WORKERS * chunk)
    assert idx.shape == (n // chunk, fan, chunk) and chunks_per_worker * SC_WORKERS * chunk == n
    g = _sc_chunks_per_step(chunks_per_worker, w)

    @functools.partial(
        pl.kernel, mesh=_sc_mesh(), out_type=jax.ShapeDtypeStruct((out_rows, w), src.dtype),
        scratch_types=[pltpu.VMEM((g, fan, chunk), I32), pltpu.VMEM((g * chunk, w), src.dtype),
                       pltpu.SemaphoreType.DMA])
    def scatter(src_hbm, idx_hbm, out_hbm, idx_v, rows_v, sem):
        first = _sc_worker_id() * chunks_per_worker

        @pl.loop(0, chunks_per_worker // g)
        def _(j):
            c0 = first + j * g
            pltpu.sync_copy(src_hbm.at[pl.ds(c0 * chunk, g * chunk)], rows_v)
            pltpu.sync_copy(idx_hbm.at[pl.ds(c0, g)], idx_v)
            copies = [pltpu.async_copy(rows_v.at[pl.ds(q * chunk, chunk)], out_hbm.at[idx_v.at[q, f]], sem)
                      for q in range(g) for f in range(fan)]
            for cp in copies:
                cp.wait()

    return scatter(src, idx)


def _rglru_kernel(x_ref, win_ref, bin_ref, cw_ref, cb_ref, gaw_ref, gab_ref, giw_ref, gib_ref,
                  lam_ref, wout_ref, lng_ref, lnb_ref, o_ref, op_ref, tail_ref, h_ref, *, tiles_per_seq):
    i = pl.program_id(0)
    tm = x_ref.shape[0]

    @pl.when(i % tiles_per_seq == 0)
    def _():
        tail_ref[...] = jnp.zeros_like(tail_ref)
        h_ref[...] = jnp.zeros_like(h_ref)

    x = x_ref[...]
    xy = jnp.dot(x.astype(BF16), win_ref[...], preferred_element_type=F32) + bin_ref[...]
    xb = xy[:, :D_RNN]
    y = _gelu_tanh(xy[:, D_RNN:])

    tail = tail_ref[...]
    row8 = lax.broadcasted_iota(I32, (8, 1), 0)
    xc = cb_ref[...] + xb * cw_ref[CONV_W - 1:CONV_W, :]
    for d in range(1, CONV_W):
        rolled = pltpu.roll(xb, d, 0)
        head = jnp.where(row8 < d, pltpu.roll(tail, d, 0), rolled[:8])
        shifted = jnp.concatenate([head, rolled[8:]], axis=0)
        xc = xc + shifted * cw_ref[CONV_W - 1 - d:CONV_W - d, :]
    tail_ref[...] = xb[tm - 8:, :]

    r_parts, i_parts = [], []
    for n in range(LRU_BLOCKS):
        xg = xc[:, n * LRU_BLOCK_W:(n + 1) * LRU_BLOCK_W].astype(BF16)
        r_parts.append(jnp.dot(xg, gaw_ref[n], preferred_element_type=F32))
        i_parts.append(jnp.dot(xg, giw_ref[n], preferred_element_type=F32))
    r = jax.nn.sigmoid(jnp.concatenate(r_parts, axis=1) + gab_ref[...])
    ig = jax.nn.sigmoid(jnp.concatenate(i_parts, axis=1) + gib_ref[...])

    lam = lam_ref[...]
    softplus_neg_lam = jnp.maximum(-lam, 0.0) + jnp.log1p(jnp.exp(-jnp.abs(lam)))
    log_a = (-LRU_C * r) * softplus_neg_lam
    a = jnp.exp(log_a)
    u = jnp.sqrt(1.0 - a * a) * (ig * xc)

    n_grp = tm // 8
    sub = lax.broadcasted_iota(I32, (1, 8, 1), 1)
    acc_a = a.reshape(n_grp, 8, D_RNN)
    acc_h = u.reshape(n_grp, 8, D_RNN)
    for d in (1, 2, 4):
        keep = sub >= d
        sh_a = pltpu.roll(acc_a, d, 1)
        sh_h = pltpu.roll(acc_h, d, 1)
        acc_h = jnp.where(keep, acc_a * sh_h + acc_h, acc_h)
        acc_a = jnp.where(keep, acc_a * sh_a, acc_a)
    state = h_ref[...]
    groups = []
    for g in range(n_grp):
        hg = acc_h[g] + acc_a[g] * state
        state = hg[7:8, :]
        groups.append(hg)
    h = jnp.concatenate(groups, axis=0)
    h_ref[...] = state

    mix = jnp.dot((h * y).astype(BF16), wout_ref[...], preferred_element_type=F32)
    x1 = _layer_norm(DN_ALPHA * x + mix, lng_ref[...], lnb_ref[...])
    o_ref[...] = x1
    op_ref[...] = _pack_bf16_pairs(x1)


def _rglru_layer(x, seq, w_in, b_in, conv_w, conv_b, ga_w, ga_b, gi_w, gi_b, lam, w_out, ln_g, ln_b):
    n, d = x.shape
    tm = min(MIXER_ROWS, seq)
    row = lambda v: v.reshape(1, -1).astype(F32)
    return pl.pallas_call(
        functools.partial(_rglru_kernel, tiles_per_seq=seq // tm),
        grid=(n // tm,),
        in_specs=[pl.BlockSpec((tm, d), lambda i: (i, 0)),
                  _full((d, 2 * D_RNN)), _full((1, 2 * D_RNN)),
                  _full((CONV_W, D_RNN)), _full((1, D_RNN)),
                  _full((LRU_BLOCKS, LRU_BLOCK_W, LRU_BLOCK_W)), _full((1, D_RNN)),
                  _full((LRU_BLOCKS, LRU_BLOCK_W, LRU_BLOCK_W)), _full((1, D_RNN)),
                  _full((1, D_RNN)), _full((D_RNN, d)), _full((1, d)), _full((1, d))],
        out_specs=[pl.BlockSpec((tm, d), lambda i: (i, 0)), pl.BlockSpec((tm, d // 2), lambda i: (i, 0))],
        out_shape=[jax.ShapeDtypeStruct((n, d), F32), jax.ShapeDtypeStruct((n, d // 2), U32)],
        scratch_shapes=[pltpu.VMEM((8, D_RNN), F32), pltpu.VMEM((1, D_RNN), F32)],
        compiler_params=_params("arbitrary"),
        name="rglru_mixer",
    )(x, w_in.astype(BF16), row(b_in), conv_w, row(conv_b), ga_w.astype(BF16), row(ga_b),
      gi_w.astype(BF16), row(gi_b), row(lam), w_out.astype(BF16), row(ln_g), row(ln_b))


def _first_argmax(cur, idx, size, axis):
    m = jnp.max(cur, axis=axis, keepdims=True)
    first = jnp.min(jnp.where(cur == m, idx, size), axis=axis, keepdims=True)
    return m, first


def _router_kernel(x_ref, rw_ref, rb_ref, e_ref, w_ref, r_ref, cnt_ref, carry_ref):
    i = pl.program_id(0)
    tm = x_ref.shape[0]

    @pl.when(i == 0)
    def _():
        carry_ref[...] = jnp.zeros_like(carry_ref)

    x = x_ref[...]
    x_hi = x.astype(BF16)
    x_lo = (x - x_hi.astype(F32)).astype(BF16)
    rw = rw_ref[...]
    hi_both = jnp.dot(x_hi, rw, preferred_element_type=F32)
    logits = (hi_both[:, :128] + hi_both[:, 128:]
              + jnp.dot(x_lo, rw[:, :128], preferred_element_type=F32))
    scores = jax.nn.sigmoid(logits.T[:N_EXPERTS, :])
    choice = scores + rb_ref[...]

    c3 = choice.reshape(N_GROUPS, GROUP_SIZE, tm)
    in_grp = lax.broadcasted_iota(I32, c3.shape, 1)
    m1, i1 = _first_argmax(c3, in_grp, GROUP_SIZE, 1)
    m2 = jnp.max(jnp.where(in_grp == i1, NEG_INF, c3), axis=1, keepdims=True)
    grp_score = (m1 + m2)[:, 0, :]

    grp_id = lax.broadcasted_iota(I32, grp_score.shape, 0)
    grp_sel = jnp.zeros(grp_score.shape, jnp.bool_)
    cur = grp_score
    for _ in range(TOPK_GROUPS):
        _, gi = _first_argmax(cur, grp_id, N_GROUPS, 0)
        hit = grp_id == gi
        grp_sel = jnp.logical_or(grp_sel, hit)
        cur = jnp.where(hit, NEG_INF, cur)

    cur = jnp.where(grp_sel[:, None, :], c3, NEG_INF).reshape(N_EXPERTS, tm)
    exp_id = lax.broadcasted_iota(I32, cur.shape, 0)
    sel = jnp.zeros(cur.shape, F32)
    e_rows, s_rows = [], []
    for _ in range(TOPK_EXPERTS):
        _, ei = _first_argmax(cur, exp_id, N_EXPERTS, 0)
        hit = exp_id == ei
        e_rows.append(ei)
        s_rows.append(jnp.sum(jnp.where(hit, scores, 0.0), axis=0, keepdims=True))
        sel = jnp.where(hit, 1.0, sel)
        cur = jnp.where(hit, NEG_INF, cur)
    e_top = jnp.concatenate(e_rows, axis=0)
    s_top = jnp.concatenate(s_rows, axis=0)
    w_ref[...] = s_top / jnp.sum(s_top, axis=0, keepdims=True) * ROUTED_SCALE
    e_ref[...] = e_top

    t_row = lax.broadcasted_iota(I32, (tm, tm), 0)
    t_col = lax.broadcasted_iota(I32, (tm, tm), 1)
    before = (t_row < t_col).astype(BF16)
    cum = jnp.dot(sel.astype(BF16), before, preferred_element_type=F32) + carry_ref[...]
    r_rows = [jnp.sum(jnp.where(exp_id == e_rows[k], cum, 0.0), axis=0, keepdims=True)
              for k in range(TOPK_EXPERTS)]
    r_ref[...] = jnp.concatenate(r_rows, axis=0).astype(I32)
    carry_ref[...] = carry_ref[...] + jnp.sum(sel, axis=1, keepdims=True)
    cnt_ref[...] = jnp.broadcast_to(carry_ref[...], cnt_ref.shape).astype(I32)


def _router(x, router_w, router_b):
    n, d = x.shape
    tm = min(TOKEN_ROWS, n)
    rw = jnp.pad(router_w, ((0, 0), (0, 128 - N_EXPERTS)))
    rw_hi = rw.astype(BF16)
    rw = jnp.concatenate([rw_hi, (rw - rw_hi.astype(F32)).astype(BF16)], axis=1)
    k = TOPK_EXPERTS
    tok_spec = pl.BlockSpec((k, tm), lambda i: (0, i))
    return pl.pallas_call(
        _router_kernel,
        grid=(n // tm,),
        in_specs=[pl.BlockSpec((tm, d), lambda i: (i, 0)), _full((d, 256)), _full((N_EXPERTS, 1))],
        out_specs=[tok_spec, tok_spec, tok_spec, _full((N_EXPERTS, 128))],
        out_shape=[jax.ShapeDtypeStruct((k, n), I32), jax.ShapeDtypeStruct((k, n), F32),
                   jax.ShapeDtypeStruct((k, n), I32), jax.ShapeDtypeStruct((N_EXPERTS, 128), I32)],
        scratch_shapes=[pltpu.VMEM((N_EXPERTS, 1), F32)],
        compiler_params=_params("arbitrary"),
        name="moe_router",
    )(x, rw, router_b.reshape(N_EXPERTS, 1).astype(F32))


def _expert_kernel(blk_e_ref, nblk_ref, xs_ref, wg_ref, wu_ref, wd_ref, after_ref, ys_ref,
                   wg_s, wu_s, wd_s):
    del after_ref
    i = pl.program_id(0)

    @pl.when(i < nblk_ref[0])
    def _():
        @pl.when(jnp.logical_or(i == 0, blk_e_ref[i] != blk_e_ref[jnp.maximum(i - 1, 0)]))
        def _():
            wg_s[...] = wg_ref[0, 0].astype(BF16)
            wu_s[...] = wu_ref[0, 0].astype(BF16)
            wd_s[...] = wd_ref[0, 0].astype(BF16)

        lo, hi = _unpack_bf16_pairs(xs_ref[...])
        xs = jnp.concatenate([lo.astype(BF16), hi.astype(BF16)], axis=1)
        g = jnp.dot(xs, wg_s[...], preferred_element_type=F32)
        u = jnp.dot(xs, wu_s[...], preferred_element_type=F32)
        hdn = (_silu(g) * u).astype(BF16)
        ys_ref[...] = _pack_bf16_pairs(jnp.dot(hdn, wd_s[...], preferred_element_type=F32))


_ORDER_ONLY = pl.BlockSpec(memory_space=pl.ANY)


def _experts(xs, blk_e, nblk, layer, w_gate, w_up, w_down, after):
    rows, half = xs.shape
    d = 2 * half
    m = EXPERT_ROWS
    row_map = lambda i, be, nb: (jnp.minimum(i, nb[0] - 1), 0)
    w_map = lambda i, be, nb: (layer, be[i], 0, 0)
    return pl.pallas_call(
        _expert_kernel,
        grid_spec=pltpu.PrefetchScalarGridSpec(
            num_scalar_prefetch=2,
            grid=(rows // m,),
            in_specs=[pl.BlockSpec((m, half), row_map),
                      pl.BlockSpec((1, 1, d, D_EXPERT), w_map),
                      pl.BlockSpec((1, 1, d, D_EXPERT), w_map),
                      pl.BlockSpec((1, 1, D_EXPERT, d), w_map), _ORDER_ONLY],
            out_specs=pl.BlockSpec((m, half), row_map),
            scratch_shapes=[pltpu.VMEM((d, D_EXPERT), BF16), pltpu.VMEM((d, D_EXPERT), BF16),
                            pltpu.VMEM((D_EXPERT, d), BF16)]),
        out_shape=jax.ShapeDtypeStruct((rows, half), U32),
        compiler_params=_params("arbitrary"),
        name="moe_experts",
    )(blk_e, nblk, xs, w_gate, w_up, w_down, after)


def _combine_kernel(x_ref, yg_ref, w_ref, sg_ref, su_ref, sd_ref, lng_ref, lnb_ref,
                    p_ref, pw_ref, pg_ref, after_ref, o_ref):
    del after_ref
    x = x_ref[...]
    wt = w_ref[...].T
    moe_lo, moe_hi = None, None
    for k in range(TOPK_EXPERTS):
        lo, hi = _unpack_bf16_pairs(yg_ref[k])
        wk = wt[:, k:k + 1]
        moe_lo = wk * lo if k == 0 else moe_lo + wk * lo
        moe_hi = wk * hi if k == 0 else moe_hi + wk * hi
    moe = jnp.concatenate([moe_lo, moe_hi], axis=1)
    xb = x.astype(BF16)
    hdn = _silu(jnp.dot(xb, sg_ref[...], preferred_element_type=F32)) * \
        jnp.dot(xb, su_ref[...], preferred_element_type=F32)
    shared = jnp.dot(hdn.astype(BF16), sd_ref[...], preferred_element_type=F32)
    x2 = _layer_norm(DN_ALPHA * x + (moe + shared), lng_ref[...], lnb_ref[...])
    emb = jnp.dot(p_ref[0].astype(BF16), pw_ref[...], preferred_element_type=F32)
    gate = jax.nn.sigmoid(jnp.dot(x2.astype(BF16), pg_ref[...], preferred_element_type=F32))
    o_ref[...] = x2 + emb * gate


def _combine(x, yg, wgt, s_gate, s_up, s_down, ln_g, ln_b, p_all, p_index, ple_w, ple_gate_w, after):
    n, d = x.shape
    tm = min(TOKEN_ROWS // 2, n)
    k = TOPK_EXPERTS
    row = lambda v: v.reshape(1, -1).astype(F32)
    return pl.pallas_call(
        _combine_kernel,
        grid=(n // tm,),
        in_specs=[pl.BlockSpec((tm, d), lambda i: (i, 0)),
                  pl.BlockSpec((k, tm, d // 2), lambda i: (0, i, 0)),
                  pl.BlockSpec((k, tm), lambda i: (0, i)),
                  _full((d, D_SHARED)), _full((d, D_SHARED)), _full((D_SHARED, d)),
                  _full((1, d)), _full((1, d)),
                  pl.BlockSpec((1, tm, PLE_DIM), lambda i: (p_index, i, 0)),
                  _full((PLE_DIM, d)), _full((d, d)), _ORDER_ONLY],
        out_specs=pl.BlockSpec((tm, d), lambda i: (i, 0)),
        out_shape=jax.ShapeDtypeStruct((n, d), F32),
        compiler_params=_params("parallel"),
        name="moe_combine",
    )(x, yg, wgt, s_gate.astype(BF16), s_up.astype(BF16), s_down.astype(BF16), row(ln_g), row(ln_b),
      p_all, ple_w.astype(BF16), ple_gate_w.astype(BF16), after)


def _positions_kernel(e_ref, r_ref, start_ref, pos_ref):
    e = e_ref[...]
    start = start_ref[...]
    grp_id = lax.broadcasted_iota(I32, (start.shape[0], e.shape[1]), 0)
    rows = [jnp.sum(jnp.where(grp_id == e[k:k + 1], start, 0), axis=0, keepdims=True)
            for k in range(e.shape[0])]
    pos_ref[...] = jnp.concatenate(rows, axis=0) + r_ref[...]


def _positions(e_idx, rank, starts):
    k, n = e_idx.shape
    tm = min(4 * TOKEN_ROWS, n)
    groups = starts.shape[0]
    tok_spec = pl.BlockSpec((k, tm), lambda i: (0, i))
    return pl.pallas_call(
        _positions_kernel,
        grid=(n // tm,),
        in_specs=[tok_spec, tok_spec, _full((groups, 1))],
        out_specs=tok_spec,
        out_shape=jax.ShapeDtypeStruct((k, n), I32),
        compiler_params=_params("parallel"),
        name="group_positions",
    )(e_idx, rank, starts.reshape(groups, 1))


def _moe_layer(streams, layer, router_w, router_b, w_gate, w_up, w_down, s_gate, s_up, s_down,
               ln_g, ln_b, p_all, p_index, ple_w, ple_gate_w):
    m = EXPERT_ROWS
    k = TOPK_EXPERTS
    chunk = SC_INDEX_CHUNK
    routed = []
    for x, xp in streams:
        n, d = x.shape
        e_idx, wgt, rank, cnt = _router(x, router_w, router_b)
        counts = cnt[:, 0]
        padded = (counts + m - 1) // m * m
        pends = jnp.cumsum(padded)
        n_blocks = (n * k) // m + N_EXPERTS
        blk_first = jnp.arange(n_blocks, dtype=I32) * m
        blk_e = jnp.minimum(jnp.sum((pends[None, :] <= blk_first[:, None]).astype(I32), axis=1),
                            N_EXPERTS - 1)
        nblk = (pends[-1] // m).astype(I32).reshape(1)
        pos = _positions(e_idx, rank, (pends - padded).astype(I32))
        pos_chunks = pos.reshape(k, n // chunk, chunk).transpose(1, 0, 2)
        xs = _sc_scatter_rows(xp, pos_chunks, n_blocks * m)
        routed.append((xs, blk_e, nblk, pos, wgt))
    gathered, order = [], routed[-1][3]
    for (x, _), (xs, blk_e, nblk, pos, wgt) in zip(streams, routed):
        n, d = x.shape
        ys = _experts(xs, blk_e, nblk, layer, w_gate, w_up, w_down, order)
        gathered.append(_sc_gather_rows(ys, pos.reshape(-1)).reshape(k, n, d // 2))
        order = ys
    outs = []
    for s, ((x, _), (_, _, _, _, wgt), yg) in enumerate(zip(streams, routed, gathered)):
        outs.append(_combine(x, yg, wgt, s_gate, s_up, s_down, ln_g, ln_b, p_all, p_index + s,
                             ple_w, ple_gate_w, order))
        order = outs[-1]
    return outs


def _rope(t, cos_t, sin_t):
    half = ROT_DIM // 2
    width = t.shape[1]
    lane = lax.broadcasted_iota(I32, (1, width), 1) % HEAD_DIM
    partner = jnp.where(lane < half, pltpu.roll(t, width - half, 1), pltpu.roll(t, half, 1))
    cos_f = jnp.concatenate([cos_t] * N_HEADS, axis=1)
    sin_f = jnp.concatenate([sin_t] * N_HEADS, axis=1)
    return t * cos_f + partner * sin_f


def _kv_kernel(x_ref, g_ref, b_ref, wkv_ref, cos_ref, sin_ref, k_ref, vt_ref, km_ref):
    h = _layer_norm(x_ref[...], g_ref[...], b_ref[...])
    kv = jnp.dot(h.astype(BF16), wkv_ref[...], preferred_element_type=F32)
    k = _rope(kv[:, :D_MODEL], cos_ref[...], sin_ref[...])
    vt = kv[:, D_MODEL:].T
    km_ref[0] = jnp.mean(k, axis=0, keepdims=True)
    for hd in range(N_HEADS):
        k_ref[hd, 0] = k[:, hd * HEAD_DIM:(hd + 1) * HEAD_DIM].astype(BF16)
        vt_ref[hd, 0] = vt[hd * HEAD_DIM:(hd + 1) * HEAD_DIM, :].astype(BF16)


def _shared_kv(x, seq, ln_g, ln_b, w_kv, cos_t, sin_t):
    n, d = x.shape
    blk = MOBA_BLOCK
    nbt = n // blk
    spb = seq // blk
    row = lambda v: v.reshape(1, -1).astype(F32)
    return pl.pallas_call(
        _kv_kernel,
        grid=(nbt,),
        in_specs=[pl.BlockSpec((blk, d), lambda i: (i, 0)), _full((1, d)), _full((1, d)),
                  _full((d, 2 * d)),
                  pl.BlockSpec((blk, HEAD_DIM), lambda i: (i % spb, 0)),
                  pl.BlockSpec((blk, HEAD_DIM), lambda i: (i % spb, 0))],
        out_specs=[pl.BlockSpec((N_HEADS, 1, blk, HEAD_DIM), lambda i: (0, i, 0, 0)),
                   pl.BlockSpec((N_HEADS, 1, HEAD_DIM, blk), lambda i: (0, i, 0, 0)),
                   pl.BlockSpec((1, 1, d), lambda i: (i, 0, 0))],
        out_shape=[jax.ShapeDtypeStruct((N_HEADS, nbt, blk, HEAD_DIM), BF16),
                   jax.ShapeDtypeStruct((N_HEADS, nbt, HEAD_DIM, blk), BF16),
                   jax.ShapeDtypeStruct((nbt, 1, d), F32)],
        compiler_params=_params("parallel"),
        name="shared_kv",
    )(x, row(ln_g), row(ln_b), w_kv.astype(BF16), cos_t, sin_t)


SEL_ROWS = 8


def _query_kernel(x_ref, wq_ref, cos_ref, sin_ref, km_ref, qp_ref, sel_ref, cnt_ref, carry_ref,
                  *, blocks_per_seq):
    i = pl.program_id(0)
    tm = x_ref.shape[0]
    nb = km_ref.shape[0]
    own = i % blocks_per_seq

    @pl.when(own == 0)
    def _():
        carry_ref[...] = jnp.zeros_like(carry_ref)

    q = jnp.dot(x_ref[...].astype(BF16), wq_ref[...], preferred_element_type=F32)
    q = _rope(q, cos_ref[...], sin_ref[...]) * (HEAD_DIM ** -0.5)

    t_row = lax.broadcasted_iota(I32, (tm, tm), 0)
    t_col = lax.broadcasted_iota(I32, (tm, tm), 1)
    before = (t_row < t_col).astype(BF16)
    km = km_ref[...]
    km_hi = km.astype(BF16)
    km_both = jnp.concatenate([km_hi, (km - km_hi.astype(F32)).astype(BF16)], axis=0)
    q_hi = q.astype(BF16)
    q_lo = (q - q_hi.astype(F32)).astype(BF16)
    nt = (((1,), (1,)), ((), ()))
    gates = []
    for hd in range(N_HEADS):
        lo, hi = hd * HEAD_DIM, (hd + 1) * HEAD_DIM
        qp_ref[0, hd] = q[:, lo:hi]
        both = lax.dot_general(km_both[:, lo:hi], q_hi[:, lo:hi], nt, preferred_element_type=F32)
        gates.append(both[:nb] + both[nb:]
                     + lax.dot_general(km_hi[:, lo:hi], q_lo[:, lo:hi], nt, preferred_element_type=F32))

    gate = jnp.concatenate(gates, axis=0).reshape(N_HEADS, nb, tm)
    blk_id = lax.broadcasted_iota(I32, (N_HEADS, nb, tm), 1)
    cur = jnp.where(blk_id < own, gate, NEG_INF)
    sel_rows = []
    chosen = jnp.zeros((N_HEADS, nb, tm), F32)
    for _ in range(MOBA_TOPK):
        m, bi = _first_argmax(cur, blk_id, nb, 1)
        valid = m > NEG_INF
        hit = jnp.logical_and(blk_id == bi, valid)
        sel_rows.append(jnp.where(valid, bi, -1))
        chosen = jnp.where(hit, 1.0, chosen)
        cur = jnp.where(blk_id == bi, NEG_INF, cur)
    chosen2 = chosen.reshape(N_HEADS * nb, tm)
    carry = carry_ref[...]
    cum = (jnp.dot(chosen2.astype(BF16), before, preferred_element_type=F32) + carry).reshape(N_HEADS, nb, tm)
    rank_rows = [jnp.sum(jnp.where(blk_id == s, cum, 0.0), axis=1, keepdims=True).astype(I32)
                 for s in sel_rows]
    carry = carry + jnp.sum(chosen2, axis=1, keepdims=True)
    carry_ref[...] = carry
    pad = jnp.zeros((N_HEADS, SEL_ROWS - 2 * MOBA_TOPK, tm), I32)
    table = jnp.concatenate(sel_rows + rank_rows + [pad], axis=1)
    sel_ref[...] = table.reshape(N_HEADS * SEL_ROWS, tm)
    cnt_ref[0] = jnp.broadcast_to(carry, cnt_ref.shape[1:]).astype(I32)


def _queries(x, seq, w_q, cos_t, sin_t, k_mean):
    n, d = x.shape
    blk = MOBA_BLOCK
    nb = seq // blk
    return pl.pallas_call(
        functools.partial(_query_kernel, blocks_per_seq=nb),
        grid=(n // blk,),
        in_specs=[pl.BlockSpec((blk, d), lambda i: (i, 0)), _full((d, d)),
                  pl.BlockSpec((blk, HEAD_DIM), lambda i: (i % nb, 0)),
                  pl.BlockSpec((blk, HEAD_DIM), lambda i: (i % nb, 0)),
                  pl.BlockSpec((nb, d), lambda i: (i // nb, 0))],
        out_specs=[pl.BlockSpec((1, N_HEADS, blk, HEAD_DIM), lambda i: (i, 0, 0, 0)),
                   pl.BlockSpec((N_HEADS * SEL_ROWS, blk), lambda i: (0, i)),
                   pl.BlockSpec((1, N_HEADS * nb, 128), lambda i: (i // nb, 0, 0))],
        out_shape=[jax.ShapeDtypeStruct((n // blk, N_HEADS, blk, HEAD_DIM), F32),
                   jax.ShapeDtypeStruct((N_HEADS * SEL_ROWS, n), I32),
                   jax.ShapeDtypeStruct((n // seq, N_HEADS * nb, 128), I32)],
        scratch_shapes=[pltpu.VMEM((N_HEADS * nb, 1), F32)],
        compiler_params=_params("arbitrary"),
        name="moba_queries",
    )(x, w_q.astype(BF16), cos_t, sin_t, k_mean)


def _moba_positions_kernel(sel_ref, start_ref, pos_ref, *, dump_row):
    tm = sel_ref.shape[1]
    nb = start_ref.shape[1] // N_HEADS
    blk_id = lax.broadcasted_iota(I32, (nb, tm), 0)
    dump = dump_row + lax.broadcasted_iota(I32, (1, tm), 1) % SC_INDEX_CHUNK
    rows = []
    for hd in range(N_HEADS):
        start = start_ref[0, hd * nb:(hd + 1) * nb, :]
        for s in range(MOBA_TOPK):
            sel = sel_ref[hd * SEL_ROWS + s:hd * SEL_ROWS + s + 1, :]
            rank = sel_ref[hd * SEL_ROWS + MOBA_TOPK + s:hd * SEL_ROWS + MOBA_TOPK + s + 1, :]
            base = jnp.sum(jnp.where(blk_id == sel, start, 0), axis=0, keepdims=True)
            rows.append(jnp.where(sel >= 0, base + rank, dump))
        rows.extend([dump] * (SEL_ROWS - MOBA_TOPK))
    pos_ref[...] = jnp.concatenate(rows, axis=0)


def _moba_positions(sel, starts, seq, dump_row):
    rows, n = sel.shape
    tm = min(4 * TOKEN_ROWS, seq)
    tps = seq // tm
    groups = starts.shape[1]
    return pl.pallas_call(
        functools.partial(_moba_positions_kernel, dump_row=dump_row),
        grid=(n // tm,),
        in_specs=[pl.BlockSpec((rows, tm), lambda i: (0, i)),
                  pl.BlockSpec((1, groups, 1), lambda i: (i // tps, 0, 0))],
        out_specs=pl.BlockSpec((rows, tm), lambda i: (0, i)),
        out_shape=jax.ShapeDtypeStruct((rows, n), I32),
        compiler_params=_params("parallel"),
        name="moba_positions",
    )(sel, starts)


ATTN_SUB_ROWS = 128
ATTN_SUBS_PER_STEP = 32
ATTN_STEP_ROWS = ATTN_SUB_ROWS * ATTN_SUBS_PER_STEP


def _pack_bf16_row_pairs(x):
    h = x.shape[0] // 2
    bits = lax.bitcast_convert_type(x.astype(BF16).astype(F32), U32)
    return (bits[:h] >> 16) | (bits[h:] & U32(HIGH_HALF))


def _transpose_u32(x):
    return lax.bitcast_convert_type(lax.bitcast_convert_type(x, I32).T, U32)


def _scores_t(k, q, keep):
    st = lax.dot_general(k, q, (((1,), (1,)), ((), ())), preferred_element_type=F32)
    if keep is not None:
        st = jnp.where(keep, st, NEG_INF)
    m = jnp.max(st, axis=0, keepdims=True)
    p = jnp.exp(st - m)
    return m, p, jnp.sum(p, axis=0, keepdims=True)


def _group_attn_kernel(step_h_ref, step_b_ref, sub_j_ref, nsteps_ref, qs_ref, k_ref, vt_ref, after_ref,
                       part_ref):
    del step_h_ref, step_b_ref, after_ref
    i = pl.program_id(0)
    half = HEAD_DIM // 2

    @pl.when(i < nsteps_ref[0])
    def _():
        for u in range(ATTN_SUBS_PER_STEP):
            j = sub_j_ref[i * ATTN_SUBS_PER_STEP + u]
            rows = pl.ds(u * ATTN_SUB_ROWS, ATTN_SUB_ROWS)
            q = qs_ref[rows, :].astype(BF16)
            m, p, l = _scores_t(k_ref[0, j], q, None)
            ot = jnp.dot(vt_ref[0, j], p.astype(BF16), preferred_element_type=F32) * (1.0 / l)
            lse = lax.bitcast_convert_type(m + jnp.log(l), U32)
            packed_t = jnp.concatenate(
                [_pack_bf16_row_pairs(ot), jnp.broadcast_to(lse, (half, ATTN_SUB_ROWS))], axis=0)
            part_ref[rows, :] = _transpose_u32(packed_t)


def _group_attention(qs, k_blk, vt_blk, step_h, step_b, sub_j, nsteps, nb, after):
    n_steps = step_h.shape[0]
    row_map = lambda i, sh, sb, sj, ns: (jnp.minimum(i, ns[0] - 1), 0)
    kv_map = lambda i, sh, sb, sj, ns: (sh[i], sb[i], 0, 0)
    return pl.pallas_call(
        _group_attn_kernel,
        grid_spec=pltpu.PrefetchScalarGridSpec(
            num_scalar_prefetch=4,
            grid=(n_steps,),
            in_specs=[pl.BlockSpec((ATTN_STEP_ROWS, HEAD_DIM), row_map),
                      pl.BlockSpec((1, nb, MOBA_BLOCK, HEAD_DIM), kv_map),
                      pl.BlockSpec((1, nb, HEAD_DIM, MOBA_BLOCK), kv_map), _ORDER_ONLY],
            out_specs=pl.BlockSpec((ATTN_STEP_ROWS, HEAD_DIM), row_map)),
        out_shape=jax.ShapeDtypeStruct(((n_steps + 1) * ATTN_STEP_ROWS, HEAD_DIM), U32),
        compiler_params=_params("arbitrary"),
        name="moba_group_attention",
    )(step_h, step_b, sub_j, nsteps, qs, k_blk, vt_blk, after)


def _attn_merge_kernel(x_ref, qp_ref, k_ref, vt_ref, pg_ref, sel_ref, wo_ref, lng_ref, lnb_ref,
                       after_ref, o_ref, op_ref):
    del after_ref
    tm = x_ref.shape[0]
    half = HEAD_DIM // 2
    sel = sel_ref[...]
    key = lax.broadcasted_iota(I32, (tm, tm), 0)
    qry = lax.broadcasted_iota(I32, (tm, tm), 1)
    causal = key <= qry
    heads = []
    for hd in range(N_HEADS):
        q = qp_ref[0, hd].astype(BF16)
        m_own, p, l_own = _scores_t(k_ref[hd, 0], q, causal)
        acc = jnp.dot(vt_ref[hd, 0], p.astype(BF16), preferred_element_type=F32)

        outs, lses = [], []
        m_tot = m_own
        for c in range(MOBA_TOPK):
            part_t = _transpose_u32(pg_ref[c, 0, hd])
            lo, hi = _unpack_bf16_pairs(part_t[:half])
            valid = sel[hd * SEL_ROWS + c:hd * SEL_ROWS + c + 1, :] >= 0
            outs.append(jnp.where(valid, jnp.concatenate([lo, hi], axis=0), 0.0))
            lse = jnp.where(valid, lax.bitcast_convert_type(part_t[half:half + 1], F32), NEG_INF)
            lses.append(lse)
            m_tot = jnp.maximum(m_tot, lse)
        w_own = jnp.exp(m_own - m_tot)
        num = acc * w_own
        den = l_own * w_own
        for c in range(MOBA_TOPK):
            w_c = jnp.exp(lses[c] - m_tot)
            num = num + outs[c] * w_c
            den = den + w_c
        heads.append(num * (1.0 / den))
    att = jnp.concatenate(heads, axis=0).T.astype(BF16)
    mix = jnp.dot(att, wo_ref[...], preferred_element_type=F32)
    x1 = _layer_norm(DN_ALPHA * x_ref[...] + mix, lng_ref[...], lnb_ref[...])
    o_ref[...] = x1
    op_ref[...] = _pack_bf16_pairs(x1)


def _attn_merge(x, qp, k_blk, vt_blk, pg, sel, w_o, ln_g, ln_b, after):
    n, d = x.shape
    blk = MOBA_BLOCK
    row = lambda v: v.reshape(1, -1).astype(F32)
    return pl.pallas_call(
        _attn_merge_kernel,
        grid=(n // blk,),
        in_specs=[pl.BlockSpec((blk, d), lambda i: (i, 0)),
                  pl.BlockSpec((1, N_HEADS, blk, HEAD_DIM), lambda i: (i, 0, 0, 0)),
                  pl.BlockSpec((N_HEADS, 1, blk, HEAD_DIM), lambda i: (0, i, 0, 0)),
                  pl.BlockSpec((N_HEADS, 1, HEAD_DIM, blk), lambda i: (0, i, 0, 0)),
                  pl.BlockSpec((MOBA_TOPK, 1, N_HEADS, blk, HEAD_DIM), lambda i: (0, i, 0, 0, 0)),
                  pl.BlockSpec((N_HEADS * SEL_ROWS, blk), lambda i: (0, i)),
                  _full((d, d)), _full((1, d)), _full((1, d)), _ORDER_ONLY],
        out_specs=[pl.BlockSpec((blk, d), lambda i: (i, 0)), pl.BlockSpec((blk, d // 2), lambda i: (i, 0))],
        out_shape=[jax.ShapeDtypeStruct((n, d), F32), jax.ShapeDtypeStruct((n, d // 2), U32)],
        compiler_params=_params("parallel"),
        name="moba_merge",
    )(x, qp, k_blk, vt_blk, pg, sel, w_o.astype(BF16), row(ln_g), row(ln_b), after)


def _moba_layer(streams, seq, w_q, w_o, ln_g, ln_b, cos_t, sin_t):
    staged = [_moba_regroup(x, seq, w_q, cos_t, sin_t, k_mean) for x, _, _, k_mean in streams]
    gathered, order = [], staged[-1][1]
    for (x, k_blk, vt_blk, _), (qp, sel, qs, tables, gather_idx) in zip(streams, staged):
        nbt = x.shape[0] // MOBA_BLOCK
        part = _group_attention(qs, k_blk, vt_blk, *tables, seq // MOBA_BLOCK, order)
        gathered.append(_sc_gather_rows(part, gather_idx).reshape(MOBA_TOPK, nbt, N_HEADS, MOBA_BLOCK, HEAD_DIM))
        order = part
    outs = []
    for (x, k_blk, vt_blk, _), (qp, sel, _, _, _), pg in zip(streams, staged, gathered):
        outs.append(_attn_merge(x, qp, k_blk, vt_blk, pg, sel, w_o, ln_g, ln_b, order))
        order = outs[-1][0]
    return outs


def _moba_regroup(x, seq, w_q, cos_t, sin_t, k_mean):
    n, d = x.shape
    batch = n // seq
    blk = MOBA_BLOCK
    nb = seq // blk
    nbt = n // blk
    chunk = SC_INDEX_CHUNK
    qp, sel, cnt = _queries(x, seq, w_q, cos_t, sin_t, k_mean)

    counts = cnt[:, :, 0].reshape(batch * N_HEADS, nb)
    gpad = (counts + ATTN_SUB_ROWS - 1) // ATTN_SUB_ROWS * ATTN_SUB_ROWS
    seg = jnp.sum(gpad, axis=1)
    seg_pad = (seg + ATTN_STEP_ROWS - 1) // ATTN_STEP_ROWS * ATTN_STEP_ROWS
    seg_end = jnp.cumsum(seg_pad)
    gend = (seg_end - seg_pad)[:, None] + jnp.cumsum(gpad, axis=1)
    gstart = (gend - gpad).astype(I32)
    steps_per_seg = -(-(MOBA_TOPK * seq + nb * (ATTN_SUB_ROWS - 1)) // ATTN_STEP_ROWS)
    n_steps = batch * N_HEADS * steps_per_seg
    step_first = jnp.arange(n_steps, dtype=I32) * ATTN_STEP_ROWS
    step_seg = jnp.minimum(jnp.sum((seg_end[None, :] <= step_first[:, None]).astype(I32), axis=1),
                           batch * N_HEADS - 1)
    sub_first = jnp.arange(n_steps * ATTN_SUBS_PER_STEP, dtype=I32) * ATTN_SUB_ROWS
    sub_grp = jnp.minimum(jnp.sum((gend.reshape(-1)[None, :] <= sub_first[:, None]).astype(I32), axis=1),
                          batch * N_HEADS * nb - 1)
    nsteps = (seg_end[-1] // ATTN_STEP_ROWS).astype(I32).reshape(1)
    dump_row = n_steps * ATTN_STEP_ROWS

    pos = _moba_positions(sel, gstart.reshape(batch, N_HEADS * nb, 1), seq, dump_row)
    pos5 = pos.reshape(N_HEADS, SEL_ROWS, nbt, blk // chunk, chunk)[:, :MOBA_TOPK]
    scatter_idx = pos5.transpose(2, 0, 3, 1, 4).reshape(nbt * N_HEADS * (blk // chunk), MOBA_TOPK, chunk)
    gather_idx = pos5.transpose(1, 2, 0, 3, 4).reshape(-1)

    qs = _sc_scatter_rows(qp.reshape(n * N_HEADS, HEAD_DIM), scatter_idx, dump_row + chunk)
    tables = ((step_seg % N_HEADS).astype(I32), (step_seg // N_HEADS).astype(I32),
              (sub_grp % nb).astype(I32), nsteps)
    return qp, sel, qs, tables, gather_idx


def _rope_tables(seq):
    half = ROT_DIM // 2
    inv = ROPE_THETA ** (-jnp.arange(0, ROT_DIM, 2, dtype=F32) / ROT_DIM)
    ang = jnp.arange(seq, dtype=F32)[:, None] * inv[None, :]
    cos, sin = jnp.cos(ang), jnp.sin(ang)
    rest = HEAD_DIM - ROT_DIM
    cos_t = jnp.concatenate([cos, cos, jnp.ones((seq, rest), F32)], axis=1)
    sin_t = jnp.concatenate([-sin, sin, jnp.zeros((seq, rest), F32)], axis=1)
    del half
    return cos_t, sin_t


def kernel(x, p, ln_g, ln_b, a_w_in, a_b_in, a_conv_w, a_conv_b, a_gate_a_w, a_gate_a_b, a_gate_i_w,
           a_gate_i_b, a_lambda, a_w_out, kv_ln_g, kv_ln_b, w_kv, b_w_q, b_w_o, router_w, router_b,
           exp_w_gate, exp_w_up, exp_w_down, sh_w_gate, sh_w_up, sh_w_down, ple_w, ple_gate_w):
    batch, seq, d = x.shape
    xs = [x[b] for b in range(batch)]
    p_all = p.reshape(DEPTH * batch, seq, PLE_DIM)
    cos_t, sin_t = _rope_tables(seq)
    kv = None
    for i in range(DEPTH):
        if i < N_A_LAYERS:
            mixed = [_rglru_layer(xb, seq, a_w_in[i], a_b_in[i], a_conv_w[i], a_conv_b[i],
                                  a_gate_a_w[i], a_gate_a_b[i], a_gate_i_w[i], a_gate_i_b[i],
                                  a_lambda[i], a_w_out[i], ln_g[i, 0], ln_b[i, 0]) for xb in xs]
        else:
            if i == N_A_LAYERS:
                kv = []
                for xb in xs:
                    k_blk, vt_blk, k_mean = _shared_kv(xb, seq, kv_ln_g, kv_ln_b, w_kv, cos_t, sin_t)
                    kv.append((k_blk, vt_blk, k_mean.reshape(seq // MOBA_BLOCK, d)))
            j = i - N_A_LAYERS
            mixed = _moba_layer([(xb,) + kvb for xb, kvb in zip(xs, kv)], seq, b_w_q[j], b_w_o[j],
                                ln_g[i, 0], ln_b[i, 0], cos_t, sin_t)
        xs = _moe_layer(mixed, i, router_w[i], router_b[i], exp_w_gate, exp_w_up, exp_w_down,
                        sh_w_gate[i], sh_w_up[i], sh_w_down[i], ln_g[i, 1], ln_b[i, 1],
                        p_all, i * batch, ple_w[i], ple_gate_w[i])
    return jnp.stack(xs, axis=0)
```

```python
import functools

import jax
import jax.numpy as jnp
from jax import lax
from jax.experimental import pallas as pl
from jax.experimental.pallas import tpu as pltpu
from jax.experimental.pallas import tpu_sc as plsc

F32 = jnp.float32
BF16 = jnp.bfloat16
I32 = jnp.int32
U32 = jnp.uint32
HIGH_HALF = 0xFFFF0000

SC_CORES = 2
SC_SUBCORES = 16
SC_WORKERS = SC_CORES * SC_SUBCORES
SC_INDEX_CHUNK = 128
SC_STAGE_BYTES = 256 * 1024

D_MODEL = 1024
DEPTH = 4
N_A_LAYERS = DEPTH // 2
D_RNN = D_MODEL
LRU_BLOCKS = 4
LRU_BLOCK_W = D_RNN // LRU_BLOCKS
CONV_W = 4
LRU_C = 8.0
N_HEADS = 8
HEAD_DIM = D_MODEL // N_HEADS
ROT_DIM = HEAD_DIM // 4
ROPE_THETA = 500000.0
MOBA_BLOCK = 256
MOBA_TOPK = 3
N_EXPERTS = 64
N_GROUPS = 8
GROUP_SIZE = N_EXPERTS // N_GROUPS
TOPK_GROUPS = 4
TOPK_EXPERTS = 8
D_EXPERT = 256
D_SHARED = 256
ROUTED_SCALE = 2.5
PLE_DIM = 256
DN_ALPHA = (2 * DEPTH) ** 0.25
LN_EPS = 1e-5

V7X_VMEM_LIMIT_BYTES = 56 * 1024 * 1024

MIXER_ROWS = 256
TOKEN_ROWS = 512
EXPERT_ROWS = 1024
NEG_INF = float("-inf")


def _params(*sem):
    return pltpu.CompilerParams(dimension_semantics=sem, vmem_limit_bytes=V7X_VMEM_LIMIT_BYTES)


def _layer_norm(z, g, b):
    mu = jnp.mean(z, axis=-1, keepdims=True)
    zc = z - mu
    var = jnp.mean(zc * zc, axis=-1, keepdims=True)
    return zc * lax.rsqrt(var + LN_EPS) * g + b


def _silu(x):
    return x * jax.nn.sigmoid(x)


def _gelu_tanh(x):
    return x * jax.nn.sigmoid(x * (1.5957691216057308 + 0.07135481627159855 * (x * x)))


def _full(shape):
    return pl.BlockSpec(shape, lambda *_: (0,) * len(shape))


def _pack_bf16_pairs(x):
    w = x.shape[1] // 2
    bits = lax.bitcast_convert_type(x.astype(BF16).astype(F32), U32)
    return (bits[:, :w] >> 16) | (bits[:, w:] & U32(HIGH_HALF))


def _unpack_bf16_pairs(u):
    lo = lax.bitcast_convert_type(u << 16, F32)
    hi = lax.bitcast_convert_type(u & U32(HIGH_HALF), F32)
    return lo, hi


def _sc_mesh():
    return plsc.VectorSubcoreMesh(core_axis_name="c", subcore_axis_name="s")


def _sc_worker_id():
    return lax.axis_index("s") * SC_CORES + lax.axis_index("c")


def _sc_chunks_per_step(chunks_per_worker, row_words):
    g = max(1, SC_STAGE_BYTES // (SC_INDEX_CHUNK * row_words * 4))
    while chunks_per_worker % g:
        g -= 1
    return g


def _sc_gather_rows(table, idx):
    b = idx.shape[0]
    w = table.shape[1]
    chunk = SC_INDEX_CHUNK
    chunks_per_worker = b // (SC_WORKERS * chunk)
    assert chunks_per_worker * SC_WORKERS * chunk == b
    g = _sc_chunks_per_step(chunks_per_worker, w)

    @functools.partial(
        pl.kernel, mesh=_sc_mesh(), out_type=jax.ShapeDtypeStruct((b, w), table.dtype),
        scratch_types=[pltpu.VMEM((g, chunk), I32), pltpu.VMEM((g * chunk, w), table.dtype),
                       pltpu.SemaphoreType.DMA])
    def gather(table_hbm, idx_hbm, out_hbm, idx_v, rows_v, sem):
        first = _sc_worker_id() * chunks_per_worker

        @pl.loop(0, chunks_per_worker // g)
        def _(j):
            c0 = first + j * g
            pltpu.sync_copy(idx_hbm.at[pl.ds(c0, g)], idx_v)
            copies = [pltpu.async_copy(table_hbm.at[idx_v.at[q]], rows_v.at[pl.ds(q * chunk, chunk)], sem)
                      for q in range(g)]
            for cp in copies:
                cp.wait()
            pltpu.sync_copy(rows_v, out_hbm.at[pl.ds(c0 * chunk, g * chunk)])

    return gather(table, idx.reshape(b // chunk, chunk))


def _sc_scatter_rows(src, idx, out_rows):
    n, w = src.shape
    chunk = SC_INDEX_CHUNK
    fan = idx.shape[1]
    chunks_per_worker = n // (SC_WORKERS * chunk)
    assert idx.shape == (n // chunk, fan, chunk) and chunks_per_worker * SC_WORKERS * chunk == n
    g = _sc_chunks_per_step(chunks_per_worker, w)

    @functools.partial(
        pl.kernel, mesh=_sc_mesh(), out_type=jax.ShapeDtypeStruct((out_rows, w), src.dtype),
        scratch_types=[pltpu.VMEM((g, fan, chunk), I32), pltpu.VMEM((g * chunk, w), src.dtype),
                       pltpu.SemaphoreType.DMA])
    def scatter(src_hbm, idx_hbm, out_hbm, idx_v, rows_v, sem):
        first = _sc_worker_id() * chunks_per_worker

        @pl.loop(0, chunks_per_worker // g)
        def _(j):
            c0 = first + j * g
            pltpu.sync_copy(src_hbm.at[pl.ds(c0 * chunk, g * chunk)], rows_v)
            pltpu.sync_copy(idx_hbm.at[pl.ds(c0, g)], idx_v)
            copies = [pltpu.async_copy(rows_v.at[pl.ds(q * chunk, chunk)], out_hbm.at[idx_v.at[q, f]], sem)
                      for q in range(g) for f in range(fan)]
            for cp in copies:
                cp.wait()

    return scatter(src, idx)


def _rglru_kernel(x_ref, win_ref, bin_ref, cw_ref, cb_ref, gaw_ref, gab_ref, giw_ref, gib_ref,
                  lam_ref, wout_ref, lng_ref, lnb_ref, o_ref, op_ref, tail_ref, h_ref, *, tiles_per_seq):
    i = pl.program_id(0)
    tm = x_ref.shape[0]

    @pl.when(i % tiles_per_seq == 0)
    def _():
        tail_ref[...] = jnp.zeros_like(tail_ref)
        h_ref[...] = jnp.zeros_like(h_ref)

    x = x_ref[...]
    xy = jnp.dot(x.astype(BF16), win_ref[...], preferred_element_type=F32) + bin_ref[...]
    xb = xy[:, :D_RNN]
    y = _gelu_tanh(xy[:, D_RNN:])

    tail = tail_ref[...]
    row8 = lax.broadcasted_iota(I32, (8, 1), 0)
    xc = cb_ref[...] + xb * cw_ref[CONV_W - 1:CONV_W, :]
    for d in range(1, CONV_W):
        rolled = pltpu.roll(xb, d, 0)
        head = jnp.where(row8 < d, pltpu.roll(tail, d, 0), rolled[:8])
        shifted = jnp.concatenate([head, rolled[8:]], axis=0)
        xc = xc + shifted * cw_ref[CONV_W - 1 - d:CONV_W - d, :]
    tail_ref[...] = xb[tm - 8:, :]

    r_parts, i_parts = [], []
    for n in range(LRU_BLOCKS):
        xg = xc[:, n * LRU_BLOCK_W:(n + 1) * LRU_BLOCK_W].astype(BF16)
        r_parts.append(jnp.dot(xg, gaw_ref[n], preferred_element_type=F32))
        i_parts.append(jnp.dot(xg, giw_ref[n], preferred_element_type=F32))
    r = jax.nn.sigmoid(jnp.concatenate(r_parts, axis=1) + gab_ref[...])
    ig = jax.nn.sigmoid(jnp.concatenate(i_parts, axis=1) + gib_ref[...])

    lam = lam_ref[...]
    softplus_neg_lam = jnp.maximum(-lam, 0.0) + jnp.log1p(jnp.exp(-jnp.abs(lam)))
    log_a = (-LRU_C * r) * softplus_neg_lam
    a = jnp.exp(log_a)
    u = jnp.sqrt(1.0 - a * a) * (ig * xc)

    n_grp = tm // 8
    sub = lax.broadcasted_iota(I32, (1, 8, 1), 1)
    acc_a = a.reshape(n_grp, 8, D_RNN)
    acc_h = u.reshape(n_grp, 8, D_RNN)
    for d in (1, 2, 4):
        keep = sub >= d
        sh_a = pltpu.roll(acc_a, d, 1)
        sh_h = pltpu.roll(acc_h, d, 1)
        acc_h = jnp.where(keep, acc_a * sh_h + acc_h, acc_h)
        acc_a = jnp.where(keep, acc_a * sh_a, acc_a)
    state = h_ref[...]
    groups = []
    for g in range(n_grp):
        hg = acc_h[g] + acc_a[g] * state
        state = hg[7:8, :]
        groups.append(hg)
    h = jnp.concatenate(groups, axis=0)
    h_ref[...] = state

    mix = jnp.dot((h * y).astype(BF16), wout_ref[...], preferred_element_type=F32)
    x1 = _layer_norm(DN_ALPHA * x + mix, lng_ref[...], lnb_ref[...])
    o_ref[...] = x1
    op_ref[...] = _pack_bf16_pairs(x1)


def _rglru_layer(x, seq, stream, w_in, b_in, conv_w, conv_b, ga_w, ga_b, gi_w, gi_b, lam, w_out,
                 ln_g, ln_b):
    n, d = seq, x.shape[1]
    tm = min(MIXER_ROWS, seq)
    first_tile = stream * (seq // tm)
    row = lambda v: v.reshape(1, -1).astype(F32)
    return pl.pallas_call(
        functools.partial(_rglru_kernel, tiles_per_seq=seq // tm),
        grid=(n // tm,),
        in_specs=[pl.BlockSpec((tm, d), lambda i: (i + first_tile, 0)),
                  _full((d, 2 * D_RNN)), _full((1, 2 * D_RNN)),
                  _full((CONV_W, D_RNN)), _full((1, D_RNN)),
                  _full((LRU_BLOCKS, LRU_BLOCK_W, LRU_BLOCK_W)), _full((1, D_RNN)),
                  _full((LRU_BLOCKS, LRU_BLOCK_W, LRU_BLOCK_W)), _full((1, D_RNN)),
                  _full((1, D_RNN)), _full((D_RNN, d)), _full((1, d)), _full((1, d))],
        out_specs=[pl.BlockSpec((tm, d), lambda i: (i, 0)), pl.BlockSpec((tm, d // 2), lambda i: (i, 0))],
        out_shape=[jax.ShapeDtypeStruct((n, d), F32), jax.ShapeDtypeStruct((n, d // 2), U32)],
        scratch_shapes=[pltpu.VMEM((8, D_RNN), F32), pltpu.VMEM((1, D_RNN), F32)],
        compiler_params=_params("arbitrary"),
        name="rglru_mixer",
    )(x, w_in.astype(BF16), row(b_in), conv_w, row(conv_b), ga_w.astype(BF16), row(ga_b),
      gi_w.astype(BF16), row(gi_b), row(lam), w_out.astype(BF16), row(ln_g), row(ln_b))


def _first_argmax(cur, idx, size, axis):
    m = jnp.max(cur, axis=axis, keepdims=True)
    first = jnp.min(jnp.where(cur == m, idx, size), axis=axis, keepdims=True)
    return m, first


def _router_kernel(x_ref, rw_ref, rb_ref, e_ref, w_ref, r_ref, cnt_ref, carry_ref):
    i = pl.program_id(0)
    tm = x_ref.shape[0]

    @pl.when(i == 0)
    def _():
        carry_ref[...] = jnp.zeros_like(carry_ref)

    x = x_ref[...]
    x_hi = x.astype(BF16)
    x_lo = (x - x_hi.astype(F32)).astype(BF16)
    rw = rw_ref[...]
    hi_both = jnp.dot(x_hi, rw, preferred_element_type=F32)
    logits = (hi_both[:, :128] + hi_both[:, 128:]
              + jnp.dot(x_lo, rw[:, :128], preferred_element_type=F32))
    scores = jax.nn.sigmoid(logits.T[:N_EXPERTS, :])
    choice = scores + rb_ref[...]

    c3 = choice.reshape(N_GROUPS, GROUP_SIZE, tm)
    in_grp = lax.broadcasted_iota(I32, c3.shape, 1)
    m1, i1 = _first_argmax(c3, in_grp, GROUP_SIZE, 1)
    m2 = jnp.max(jnp.where(in_grp == i1, NEG_INF, c3), axis=1, keepdims=True)
    grp_score = (m1 + m2)[:, 0, :]

    grp_id = lax.broadcasted_iota(I32, grp_score.shape, 0)
    grp_sel = jnp.zeros(grp_score.shape, jnp.bool_)
    cur = grp_score
    for _ in range(TOPK_GROUPS):
        _, gi = _first_argmax(cur, grp_id, N_GROUPS, 0)
        hit = grp_id == gi
        grp_sel = jnp.logical_or(grp_sel, hit)
        cur = jnp.where(hit, NEG_INF, cur)

    cur = jnp.where(grp_sel[:, None, :], c3, NEG_INF).reshape(N_EXPERTS, tm)
    exp_id = lax.broadcasted_iota(I32, cur.shape, 0)
    sel = jnp.zeros(cur.shape, F32)
    e_rows, s_rows = [], []
    for _ in range(TOPK_EXPERTS):
        _, ei = _first_argmax(cur, exp_id, N_EXPERTS, 0)
        hit = exp_id == ei
        e_rows.append(ei)
        s_rows.append(jnp.sum(jnp.where(hit, scores, 0.0), axis=0, keepdims=True))
        sel = jnp.where(hit, 1.0, sel)
        cur = jnp.where(hit, NEG_INF, cur)
    e_top = jnp.concatenate(e_rows, axis=0)
    s_top = jnp.concatenate(s_rows, axis=0)
    w_ref[...] = s_top / jnp.sum(s_top, axis=0, keepdims=True) * ROUTED_SCALE
    e_ref[...] = e_top

    t_row = lax.broadcasted_iota(I32, (tm, tm), 0)
    t_col = lax.broadcasted_iota(I32, (tm, tm), 1)
    before = (t_row < t_col).astype(BF16)
    cum = jnp.dot(sel.astype(BF16), before, preferred_element_type=F32) + carry_ref[...]
    r_rows = [jnp.sum(jnp.where(exp_id == e_rows[k], cum, 0.0), axis=0, keepdims=True)
              for k in range(TOPK_EXPERTS)]
    r_ref[...] = jnp.concatenate(r_rows, axis=0).astype(I32)
    carry_ref[...] = carry_ref[...] + jnp.sum(sel, axis=1, keepdims=True)
    cnt_ref[...] = jnp.broadcast_to(carry_ref[...], cnt_ref.shape).astype(I32)


def _router(x, router_w, router_b):
    n, d = x.shape
    tm = min(TOKEN_ROWS, n)
    rw = jnp.pad(router_w, ((0, 0), (0, 128 - N_EXPERTS)))
    rw_hi = rw.astype(BF16)
    rw = jnp.concatenate([rw_hi, (rw - rw_hi.astype(F32)).astype(BF16)], axis=1)
    k = TOPK_EXPERTS
    tok_spec = pl.BlockSpec((k, tm), lambda i: (0, i))
    return pl.pallas_call(
        _router_kernel,
        grid=(n // tm,),
        in_specs=[pl.BlockSpec((tm, d), lambda i: (i, 0)), _full((d, 256)), _full((N_EXPERTS, 1))],
        out_specs=[tok_spec, tok_spec, tok_spec, _full((N_EXPERTS, 128))],
        out_shape=[jax.ShapeDtypeStruct((k, n), I32), jax.ShapeDtypeStruct((k, n), F32),
                   jax.ShapeDtypeStruct((k, n), I32), jax.ShapeDtypeStruct((N_EXPERTS, 128), I32)],
        scratch_shapes=[pltpu.VMEM((N_EXPERTS, 1), F32)],
        compiler_params=_params("arbitrary"),
        name="moe_router",
    )(x, rw, router_b.reshape(N_EXPERTS, 1).astype(F32))


def _expert_kernel(blk_e_ref, nblk_ref, first_ref, slot_ref, next_ref, xs_ref, wg_hbm, wu_hbm, wd_hbm,
                   after_ref, ys_ref, wg_f, wu_f, wd_f, wg_s, wu_s, wd_s, sem, *, layer):
    del after_ref
    i = pl.program_id(0)

    def fetch(expert, slot):
        return (pltpu.make_async_copy(wg_hbm.at[layer, expert], wg_f.at[slot], sem.at[slot, 0]),
                pltpu.make_async_copy(wu_hbm.at[layer, expert], wu_f.at[slot], sem.at[slot, 1]),
                pltpu.make_async_copy(wd_hbm.at[layer, expert], wd_f.at[slot], sem.at[slot, 2]))

    @pl.when(i < nblk_ref[0])
    def _():
        slot = slot_ref[i]

        @pl.when(i == 0)
        def _():
            for copy in fetch(blk_e_ref[0], 0):
                copy.start()

        @pl.when(first_ref[i] == 1)
        def _():
            for copy in fetch(blk_e_ref[i], slot):
                copy.wait()
            wg_s[...] = wg_f[slot].astype(BF16)
            wu_s[...] = wu_f[slot].astype(BF16)
            wd_s[...] = wd_f[slot].astype(BF16)

            @pl.when(next_ref[i] >= 0)
            def _():
                for copy in fetch(next_ref[i], 1 - slot):
                    copy.start()

        lo, hi = _unpack_bf16_pairs(xs_ref[...])
        xs = jnp.concatenate([lo.astype(BF16), hi.astype(BF16)], axis=1)
        g = jnp.dot(xs, wg_s[...], preferred_element_type=F32)
        u = jnp.dot(xs, wu_s[...], preferred_element_type=F32)
        hdn = (_silu(g) * u).astype(BF16)
        ys_ref[...] = _pack_bf16_pairs(jnp.dot(hdn, wd_s[...], preferred_element_type=F32))


_ORDER_ONLY = pl.BlockSpec(memory_space=pl.ANY)


def _experts(xs, blk_e, nblk, layer, w_gate, w_up, w_down, after):
    rows, half = xs.shape
    d = 2 * half
    m = EXPERT_ROWS
    n_blocks = rows // m
    idx = jnp.arange(n_blocks, dtype=I32)
    first = jnp.logical_and(idx < nblk[0], jnp.logical_or(idx == 0, blk_e != jnp.roll(blk_e, 1)))
    slot = (jnp.cumsum(first.astype(I32)) - 1) % 2
    later_first = lax.cummin(jnp.where(first, idx, n_blocks), axis=0, reverse=True)
    next_pos = jnp.concatenate([later_first[1:], jnp.full((1,), n_blocks, I32)])
    next_e = jnp.where(next_pos < n_blocks, blk_e[jnp.minimum(next_pos, n_blocks - 1)], -1)
    row_map = lambda i, be, nb, fi, sl, nx: (jnp.minimum(i, nb[0] - 1), 0)
    hbm = pl.BlockSpec(memory_space=pl.ANY)
    return pl.pallas_call(
        functools.partial(_expert_kernel, layer=layer),
        grid_spec=pltpu.PrefetchScalarGridSpec(
            num_scalar_prefetch=5,
            grid=(n_blocks,),
            in_specs=[pl.BlockSpec((m, half), row_map), hbm, hbm, hbm, _ORDER_ONLY],
            out_specs=pl.BlockSpec((m, half), row_map),
            scratch_shapes=[pltpu.VMEM((2, d, D_EXPERT), F32), pltpu.VMEM((2, d, D_EXPERT), F32),
                            pltpu.VMEM((2, D_EXPERT, d), F32),
                            pltpu.VMEM((d, D_EXPERT), BF16), pltpu.VMEM((d, D_EXPERT), BF16),
                            pltpu.VMEM((D_EXPERT, d), BF16), pltpu.SemaphoreType.DMA((2, 3))]),
        out_shape=jax.ShapeDtypeStruct((rows, half), U32),
        compiler_params=_params("arbitrary"),
        name="moe_experts",
    )(blk_e, nblk, first.astype(I32), slot.astype(I32), next_e.astype(I32), xs, w_gate, w_up, w_down, after)


def _combine_kernel(x_ref, yg_ref, w_ref, sg_ref, su_ref, sd_ref, lng_ref, lnb_ref,
                    p_ref, pw_ref, pg_ref, after_ref, o_ref):
    del after_ref
    x = x_ref[...]
    wt = w_ref[...].T
    moe_lo, moe_hi = None, None
    for k in range(TOPK_EXPERTS):
        lo, hi = _unpack_bf16_pairs(yg_ref[k])
        wk = wt[:, k:k + 1]
        moe_lo = wk * lo if k == 0 else moe_lo + wk * lo
        moe_hi = wk * hi if k == 0 else moe_hi + wk * hi
    moe = jnp.concatenate([moe_lo, moe_hi], axis=1)
    xb = x.astype(BF16)
    hdn = _silu(jnp.dot(xb, sg_ref[...], preferred_element_type=F32)) * \
        jnp.dot(xb, su_ref[...], preferred_element_type=F32)
    shared = jnp.dot(hdn.astype(BF16), sd_ref[...], preferred_element_type=F32)
    x2 = _layer_norm(DN_ALPHA * x + (moe + shared), lng_ref[...], lnb_ref[...])
    emb = jnp.dot(p_ref[0].astype(BF16), pw_ref[...], preferred_element_type=F32)
    gate = jax.nn.sigmoid(jnp.dot(x2.astype(BF16), pg_ref[...], preferred_element_type=F32))
    o_ref[...] = x2 + emb * gate


def _combine(x, yg, wgt, s_gate, s_up, s_down, ln_g, ln_b, p_all, p_index, ple_w, ple_gate_w, after):
    n, d = x.shape
    tm = min(TOKEN_ROWS // 2, n)
    k = TOPK_EXPERTS
    row = lambda v: v.reshape(1, -1).astype(F32)
    return pl.pallas_call(
        _combine_kernel,
        grid=(n // tm,),
        in_specs=[pl.BlockSpec((tm, d), lambda i: (i, 0)),
                  pl.BlockSpec((k, tm, d // 2), lambda i: (0, i, 0)),
                  pl.BlockSpec((k, tm), lambda i: (0, i)),
                  _full((d, D_SHARED)), _full((d, D_SHARED)), _full((D_SHARED, d)),
                  _full((1, d)), _full((1, d)),
                  pl.BlockSpec((1, tm, PLE_DIM), lambda i: (p_index, i, 0)),
                  _full((PLE_DIM, d)), _full((d, d)), _ORDER_ONLY],
        out_specs=pl.BlockSpec((tm, d), lambda i: (i, 0)),
        out_shape=jax.ShapeDtypeStruct((n, d), F32),
        compiler_params=_params("parallel"),
        name="moe_combine",
    )(x, yg, wgt, s_gate.astype(BF16), s_up.astype(BF16), s_down.astype(BF16), row(ln_g), row(ln_b),
      p_all, ple_w.astype(BF16), ple_gate_w.astype(BF16), after)


def _positions_kernel(e_ref, r_ref, start_ref, pos_ref):
    e = e_ref[...]
    start = start_ref[...]
    grp_id = lax.broadcasted_iota(I32, (start.shape[0], e.shape[1]), 0)
    rows = [jnp.sum(jnp.where(grp_id == e[k:k + 1], start, 0), axis=0, keepdims=True)
            for k in range(e.shape[0])]
    pos_ref[...] = jnp.concatenate(rows, axis=0) + r_ref[...]


def _positions(e_idx, rank, starts):
    k, n = e_idx.shape
    tm = min(4 * TOKEN_ROWS, n)
    groups = starts.shape[0]
    tok_spec = pl.BlockSpec((k, tm), lambda i: (0, i))
    return pl.pallas_call(
        _positions_kernel,
        grid=(n // tm,),
        in_specs=[tok_spec, tok_spec, _full((groups, 1))],
        out_specs=tok_spec,
        out_shape=jax.ShapeDtypeStruct((k, n), I32),
        compiler_params=_params("parallel"),
        name="group_positions",
    )(e_idx, rank, starts.reshape(groups, 1))


def _moe_layer(streams, layer, router_w, router_b, w_gate, w_up, w_down, s_gate, s_up, s_down,
               ln_g, ln_b, p_all, p_index, ple_w, ple_gate_w):
    m = EXPERT_ROWS
    k = TOPK_EXPERTS
    chunk = SC_INDEX_CHUNK
    routed = []
    for x, xp in streams:
        n, d = x.shape
        e_idx, wgt, rank, cnt = _router(x, router_w, router_b)
        counts = cnt[:, 0]
        padded = (counts + m - 1) // m * m
        pends = jnp.cumsum(padded)
        n_blocks = (n * k) // m + N_EXPERTS
        blk_first = jnp.arange(n_blocks, dtype=I32) * m
        blk_e = jnp.minimum(jnp.searchsorted(pends, blk_first, side="right"), N_EXPERTS - 1).astype(I32)
        nblk = (pends[-1] // m).astype(I32).reshape(1)
        pos = _positions(e_idx, rank, (pends - padded).astype(I32))
        pos_chunks = pos.reshape(k, n // chunk, chunk).transpose(1, 0, 2)
        xs = _sc_scatter_rows(xp, pos_chunks, n_blocks * m)
        routed.append((xs, blk_e, nblk, pos, wgt))
    gathered, order = [], routed[-1][3]
    for (x, _), (xs, blk_e, nblk, pos, wgt) in zip(streams, routed):
        n, d = x.shape
        ys = _experts(xs, blk_e, nblk, layer, w_gate, w_up, w_down, order)
        gathered.append(_sc_gather_rows(ys, pos.reshape(-1)).reshape(k, n, d // 2))
        order = ys
    outs = []
    for s, ((x, _), (_, _, _, _, wgt), yg) in enumerate(zip(streams, routed, gathered)):
        outs.append(_combine(x, yg, wgt, s_gate, s_up, s_down, ln_g, ln_b, p_all, p_index + s,
                             ple_w, ple_gate_w, order))
        order = outs[-1]
    return outs


def _rope(t, cos_t, sin_t):
    half = ROT_DIM // 2
    width = t.shape[1]
    lane = lax.broadcasted_iota(I32, (1, width), 1) % HEAD_DIM
    partner = jnp.where(lane < half, pltpu.roll(t, width - half, 1), pltpu.roll(t, half, 1))
    cos_f = jnp.concatenate([cos_t] * N_HEADS, axis=1)
    sin_f = jnp.concatenate([sin_t] * N_HEADS, axis=1)
    return t * cos_f + partner * sin_f


def _kv_kernel(x_ref, g_ref, b_ref, wkv_ref, cos_ref, sin_ref, k_ref, vt_ref, km_ref):
    h = _layer_norm(x_ref[...], g_ref[...], b_ref[...])
    kv = jnp.dot(h.astype(BF16), wkv_ref[...], preferred_element_type=F32)
    k = _rope(kv[:, :D_MODEL], cos_ref[...], sin_ref[...])
    vt = kv[:, D_MODEL:].T
    km_ref[0] = jnp.mean(k, axis=0, keepdims=True)
    for hd in range(N_HEADS):
        k_ref[hd, 0] = k[:, hd * HEAD_DIM:(hd + 1) * HEAD_DIM].astype(BF16)
        vt_ref[hd, 0] = vt[hd * HEAD_DIM:(hd + 1) * HEAD_DIM, :].astype(BF16)


def _shared_kv(x, seq, ln_g, ln_b, w_kv, cos_t, sin_t):
    n, d = x.shape
    blk = MOBA_BLOCK
    nbt = n // blk
    spb = seq // blk
    row = lambda v: v.reshape(1, -1).astype(F32)
    return pl.pallas_call(
        _kv_kernel,
        grid=(nbt,),
        in_specs=[pl.BlockSpec((blk, d), lambda i: (i, 0)), _full((1, d)), _full((1, d)),
                  _full((d, 2 * d)),
                  pl.BlockSpec((blk, HEAD_DIM), lambda i: (i % spb, 0)),
                  pl.BlockSpec((blk, HEAD_DIM), lambda i: (i % spb, 0))],
        out_specs=[pl.BlockSpec((N_HEADS, 1, blk, HEAD_DIM), lambda i: (0, i, 0, 0)),
                   pl.BlockSpec((N_HEADS, 1, HEAD_DIM, blk), lambda i: (0, i, 0, 0)),
                   pl.BlockSpec((1, 1, d), lambda i: (i, 0, 0))],
        out_shape=[jax.ShapeDtypeStruct((N_HEADS, nbt, blk, HEAD_DIM), BF16),
                   jax.ShapeDtypeStruct((N_HEADS, nbt, HEAD_DIM, blk), BF16),
                   jax.ShapeDtypeStruct((nbt, 1, d), F32)],
        compiler_params=_params("parallel"),
        name="shared_kv",
    )(x, row(ln_g), row(ln_b), w_kv.astype(BF16), cos_t, sin_t)


SEL_ROWS = 8


def _query_kernel(x_ref, wq_ref, cos_ref, sin_ref, km_ref, qp_ref, sel_ref, cnt_ref, carry_ref,
                  *, blocks_per_seq):
    i = pl.program_id(0)
    tm = x_ref.shape[0]
    nb = km_ref.shape[0]
    own = i % blocks_per_seq

    @pl.when(own == 0)
    def _():
        carry_ref[...] = jnp.zeros_like(carry_ref)

    q = jnp.dot(x_ref[...].astype(BF16), wq_ref[...], preferred_element_type=F32)
    q = _rope(q, cos_ref[...], sin_ref[...]) * (HEAD_DIM ** -0.5)

    t_row = lax.broadcasted_iota(I32, (tm, tm), 0)
    t_col = lax.broadcasted_iota(I32, (tm, tm), 1)
    before = (t_row < t_col).astype(BF16)
    km = km_ref[...]
    km_hi = km.astype(BF16)
    km_both = jnp.concatenate([km_hi, (km - km_hi.astype(F32)).astype(BF16)], axis=0)
    q_hi = q.astype(BF16)
    q_lo = (q - q_hi.astype(F32)).astype(BF16)
    nt = (((1,), (1,)), ((), ()))
    gates = []
    for hd in range(N_HEADS):
        lo, hi = hd * HEAD_DIM, (hd + 1) * HEAD_DIM
        qp_ref[0, hd] = q[:, lo:hi]
        both = lax.dot_general(km_both[:, lo:hi], q_hi[:, lo:hi], nt, preferred_element_type=F32)
        gates.append(both[:nb] + both[nb:]
                     + lax.dot_general(km_hi[:, lo:hi], q_lo[:, lo:hi], nt, preferred_element_type=F32))

    gate = jnp.concatenate(gates, axis=0).reshape(N_HEADS, nb, tm)
    blk_id = lax.broadcasted_iota(I32, (N_HEADS, nb, tm), 1)
    cur = jnp.where(blk_id < own, gate, NEG_INF)
    sel_rows = []
    chosen = jnp.zeros((N_HEADS, nb, tm), F32)
    for _ in range(MOBA_TOPK):
        m, bi = _first_argmax(cur, blk_id, nb, 1)
        valid = m > NEG_INF
        hit = jnp.logical_and(blk_id == bi, valid)
        sel_rows.append(jnp.where(valid, bi, -1))
        chosen = jnp.where(hit, 1.0, chosen)
        cur = jnp.where(blk_id == bi, NEG_INF, cur)
    chosen2 = chosen.reshape(N_HEADS * nb, tm)
    carry = carry_ref[...]
    cum = (jnp.dot(chosen2.astype(BF16), before, preferred_element_type=F32) + carry).reshape(N_HEADS, nb, tm)
    rank_rows = [jnp.sum(jnp.where(blk_id == s, cum, 0.0), axis=1, keepdims=True).astype(I32)
                 for s in sel_rows]
    carry = carry + jnp.sum(chosen2, axis=1, keepdims=True)
    carry_ref[...] = carry
    pad = jnp.zeros((N_HEADS, SEL_ROWS - 2 * MOBA_TOPK, tm), I32)
    table = jnp.concatenate(sel_rows + rank_rows + [pad], axis=1)
    sel_ref[...] = table.reshape(N_HEADS * SEL_ROWS, tm)
    cnt_ref[0] = jnp.broadcast_to(carry, cnt_ref.shape[1:]).astype(I32)


def _queries(x, seq, w_q, cos_t, sin_t, k_mean):
    n, d = x.shape
    blk = MOBA_BLOCK
    nb = seq // blk
    return pl.pallas_call(
        functools.partial(_query_kernel, blocks_per_seq=nb),
        grid=(n // blk,),
        in_specs=[pl.BlockSpec((blk, d), lambda i: (i, 0)), _full((d, d)),
                  pl.BlockSpec((blk, HEAD_DIM), lambda i: (i % nb, 0)),
                  pl.BlockSpec((blk, HEAD_DIM), lambda i: (i % nb, 0)),
                  pl.BlockSpec((nb, d), lambda i: (i // nb, 0))],
        out_specs=[pl.BlockSpec((1, N_HEADS, blk, HEAD_DIM), lambda i: (i, 0, 0, 0)),
                   pl.BlockSpec((N_HEADS * SEL_ROWS, blk), lambda i: (0, i)),
                   pl.BlockSpec((1, N_HEADS * nb, 128), lambda i: (i // nb, 0, 0))],
        out_shape=[jax.ShapeDtypeStruct((n // blk, N_HEADS, blk, HEAD_DIM), F32),
                   jax.ShapeDtypeStruct((N_HEADS * SEL_ROWS, n), I32),
                   jax.ShapeDtypeStruct((n // seq, N_HEADS * nb, 128), I32)],
        scratch_shapes=[pltpu.VMEM((N_HEADS * nb, 1), F32)],
        compiler_params=_params("arbitrary"),
        name="moba_queries",
    )(x, w_q.astype(BF16), cos_t, sin_t, k_mean)


def _moba_positions_kernel(sel_ref, start_ref, pos_ref, *, dump_row):
    tm = sel_ref.shape[1]
    nb = start_ref.shape[1] // N_HEADS
    blk_id = lax.broadcasted_iota(I32, (nb, tm), 0)
    dump = dump_row + lax.broadcasted_iota(I32, (1, tm), 1) % SC_INDEX_CHUNK
    rows = []
    for hd in range(N_HEADS):
        start = start_ref[0, hd * nb:(hd + 1) * nb, :]
        for s in range(MOBA_TOPK):
            sel = sel_ref[hd * SEL_ROWS + s:hd * SEL_ROWS + s + 1, :]
            rank = sel_ref[hd * SEL_ROWS + MOBA_TOPK + s:hd * SEL_ROWS + MOBA_TOPK + s + 1, :]
            base = jnp.sum(jnp.where(blk_id == sel, start, 0), axis=0, keepdims=True)
            rows.append(jnp.where(sel >= 0, base + rank, dump))
        rows.extend([dump] * (SEL_ROWS - MOBA_TOPK))
    pos_ref[...] = jnp.concatenate(rows, axis=0)


def _moba_positions(sel, starts, seq, dump_row):
    rows, n = sel.shape
    tm = min(4 * TOKEN_ROWS, seq)
    tps = seq // tm
    groups = starts.shape[1]
    return pl.pallas_call(
        functools.partial(_moba_positions_kernel, dump_row=dump_row),
        grid=(n // tm,),
        in_specs=[pl.BlockSpec((rows, tm), lambda i: (0, i)),
                  pl.BlockSpec((1, groups, 1), lambda i: (i // tps, 0, 0))],
        out_specs=pl.BlockSpec((rows, tm), lambda i: (0, i)),
        out_shape=jax.ShapeDtypeStruct((rows, n), I32),
        compiler_params=_params("parallel"),
        name="moba_positions",
    )(sel, starts)


ATTN_SUB_ROWS = 128
ATTN_SUBS_PER_STEP = 32
ATTN_STEP_ROWS = ATTN_SUB_ROWS * ATTN_SUBS_PER_STEP


def _pack_bf16_row_pairs(x):
    h = x.shape[0] // 2
    bits = lax.bitcast_convert_type(x.astype(BF16).astype(F32), U32)
    return (bits[:h] >> 16) | (bits[h:] & U32(HIGH_HALF))


def _transpose_u32(x):
    return lax.bitcast_convert_type(lax.bitcast_convert_type(x, I32).T, U32)


def _scores_t(k, q, keep):
    st = lax.dot_general(k, q, (((1,), (1,)), ((), ())), preferred_element_type=F32)
    if keep is not None:
        st = jnp.where(keep, st, NEG_INF)
    m = jnp.max(st, axis=0, keepdims=True)
    p = jnp.exp(st - m)
    return m, p, jnp.sum(p, axis=0, keepdims=True)


def _group_attn_kernel(step_h_ref, step_b_ref, sub_j_ref, nsteps_ref, qs_ref, k_ref, vt_ref, after_ref,
                       part_ref):
    del step_h_ref, step_b_ref, after_ref
    i = pl.program_id(0)
    half = HEAD_DIM // 2

    @pl.when(i < nsteps_ref[0])
    def _():
        for u in range(ATTN_SUBS_PER_STEP):
            j = sub_j_ref[i * ATTN_SUBS_PER_STEP + u]
            rows = pl.ds(u * ATTN_SUB_ROWS, ATTN_SUB_ROWS)
            q = qs_ref[rows, :].astype(BF16)
            m, p, l = _scores_t(k_ref[0, j], q, None)
            ot = jnp.dot(vt_ref[0, j], p.astype(BF16), preferred_element_type=F32) * (1.0 / l)
            lse = lax.bitcast_convert_type(m + jnp.log(l), U32)
            packed_t = jnp.concatenate(
                [_pack_bf16_row_pairs(ot), jnp.broadcast_to(lse, (half, ATTN_SUB_ROWS))], axis=0)
            part_ref[rows, :] = _transpose_u32(packed_t)


def _group_attention(qs, k_blk, vt_blk, step_h, step_b, sub_j, nsteps, nb, after):
    n_steps = step_h.shape[0]
    row_map = lambda i, sh, sb, sj, ns: (jnp.minimum(i, ns[0] - 1), 0)
    kv_map = lambda i, sh, sb, sj, ns: (sh[i], sb[i], 0, 0)
    return pl.pallas_call(
        _group_attn_kernel,
        grid_spec=pltpu.PrefetchScalarGridSpec(
            num_scalar_prefetch=4,
            grid=(n_steps,),
            in_specs=[pl.BlockSpec((ATTN_STEP_ROWS, HEAD_DIM), row_map),
                      pl.BlockSpec((1, nb, MOBA_BLOCK, HEAD_DIM), kv_map),
                      pl.BlockSpec((1, nb, HEAD_DIM, MOBA_BLOCK), kv_map), _ORDER_ONLY],
            out_specs=pl.BlockSpec((ATTN_STEP_ROWS, HEAD_DIM), row_map)),
        out_shape=jax.ShapeDtypeStruct(((n_steps + 1) * ATTN_STEP_ROWS, HEAD_DIM), U32),
        compiler_params=_params("arbitrary"),
        name="moba_group_attention",
    )(step_h, step_b, sub_j, nsteps, qs, k_blk, vt_blk, after)


def _attn_merge_kernel(x_ref, qp_ref, k_ref, vt_ref, pg_ref, sel_ref, wo_ref, lng_ref, lnb_ref,
                       after_ref, o_ref, op_ref):
    del after_ref
    tm = x_ref.shape[0]
    half = HEAD_DIM // 2
    sel = sel_ref[...]
    key = lax.broadcasted_iota(I32, (tm, tm), 0)
    qry = lax.broadcasted_iota(I32, (tm, tm), 1)
    causal = key <= qry
    heads = []
    for hd in range(N_HEADS):
        q = qp_ref[0, hd].astype(BF16)
        m_own, p, l_own = _scores_t(k_ref[hd, 0], q, causal)
        acc = jnp.dot(vt_ref[hd, 0], p.astype(BF16), preferred_element_type=F32)

        outs, lses = [], []
        m_tot = m_own
        for c in range(MOBA_TOPK):
            part_t = _transpose_u32(pg_ref[c, 0, hd])
            lo, hi = _unpack_bf16_pairs(part_t[:half])
            valid = sel[hd * SEL_ROWS + c:hd * SEL_ROWS + c + 1, :] >= 0
            outs.append(jnp.where(valid, jnp.concatenate([lo, hi], axis=0), 0.0))
            lse = jnp.where(valid, lax.bitcast_convert_type(part_t[half:half + 1], F32), NEG_INF)
            lses.append(lse)
            m_tot = jnp.maximum(m_tot, lse)
        w_own = jnp.exp(m_own - m_tot)
        num = acc * w_own
        den = l_own * w_own
        for c in range(MOBA_TOPK):
            w_c = jnp.exp(lses[c] - m_tot)
            num = num + outs[c] * w_c
            den = den + w_c
        heads.append(num * (1.0 / den))
    att = jnp.concatenate(heads, axis=0).T.astype(BF16)
    mix = jnp.dot(att, wo_ref[...], preferred_element_type=F32)
    x1 = _layer_norm(DN_ALPHA * x_ref[...] + mix, lng_ref[...], lnb_ref[...])
    o_ref[...] = x1
    op_ref[...] = _pack_bf16_pairs(x1)


def _attn_merge(x, qp, k_blk, vt_blk, pg, sel, w_o, ln_g, ln_b, after):
    n, d = x.shape
    blk = MOBA_BLOCK
    row = lambda v: v.reshape(1, -1).astype(F32)
    return pl.pallas_call(
        _attn_merge_kernel,
        grid=(n // blk,),
        in_specs=[pl.BlockSpec((blk, d), lambda i: (i, 0)),
                  pl.BlockSpec((1, N_HEADS, blk, HEAD_DIM), lambda i: (i, 0, 0, 0)),
                  pl.BlockSpec((N_HEADS, 1, blk, HEAD_DIM), lambda i: (0, i, 0, 0)),
                  pl.BlockSpec((N_HEADS, 1, HEAD_DIM, blk), lambda i: (0, i, 0, 0)),
                  pl.BlockSpec((MOBA_TOPK, 1, N_HEADS, blk, HEAD_DIM), lambda i: (0, i, 0, 0, 0)),
                  pl.BlockSpec((N_HEADS * SEL_ROWS, blk), lambda i: (0, i)),
                  _full((d, d)), _full((1, d)), _full((1, d)), _ORDER_ONLY],
        out_specs=[pl.BlockSpec((blk, d), lambda i: (i, 0)), pl.BlockSpec((blk, d // 2), lambda i: (i, 0))],
        out_shape=[jax.ShapeDtypeStruct((n, d), F32), jax.ShapeDtypeStruct((n, d // 2), U32)],
        compiler_params=_params("parallel"),
        name="moba_merge",
    )(x, qp, k_blk, vt_blk, pg, sel, w_o.astype(BF16), row(ln_g), row(ln_b), after)


def _moba_layer(streams, seq, w_q, w_o, ln_g, ln_b, cos_t, sin_t):
    staged = [_moba_regroup(x, seq, w_q, cos_t, sin_t, k_mean) for x, _, _, k_mean in streams]
    gathered, order = [], staged[-1][1]
    for (x, k_blk, vt_blk, _), (qp, sel, qs, tables, gather_idx) in zip(streams, staged):
        nbt = x.shape[0] // MOBA_BLOCK
        part = _group_attention(qs, k_blk, vt_blk, *tables, seq // MOBA_BLOCK, order)
        gathered.append(_sc_gather_rows(part, gather_idx).reshape(MOBA_TOPK, nbt, N_HEADS, MOBA_BLOCK, HEAD_DIM))
        order = part
    outs = []
    for (x, k_blk, vt_blk, _), (qp, sel, _, _, _), pg in zip(streams, staged, gathered):
        outs.append(_attn_merge(x, qp, k_blk, vt_blk, pg, sel, w_o, ln_g, ln_b, order))
        order = outs[-1][0]
    return outs


def _moba_regroup(x, seq, w_q, cos_t, sin_t, k_mean):
    n, d = x.shape
    batch = n // seq
    blk = MOBA_BLOCK
    nb = seq // blk
    nbt = n // blk
    chunk = SC_INDEX_CHUNK
    qp, sel, cnt = _queries(x, seq, w_q, cos_t, sin_t, k_mean)

    counts = cnt[:, :, 0].reshape(batch * N_HEADS, nb)
    gpad = (counts + ATTN_SUB_ROWS - 1) // ATTN_SUB_ROWS * ATTN_SUB_ROWS
    seg = jnp.sum(gpad, axis=1)
    seg_pad = (seg + ATTN_STEP_ROWS - 1) // ATTN_STEP_ROWS * ATTN_STEP_ROWS
    seg_end = jnp.cumsum(seg_pad)
    gend = (seg_end - seg_pad)[:, None] + jnp.cumsum(gpad, axis=1)
    gstart = (gend - gpad).astype(I32)
    steps_per_seg = -(-(MOBA_TOPK * seq + nb * (ATTN_SUB_ROWS - 1)) // ATTN_STEP_ROWS)
    n_steps = batch * N_HEADS * steps_per_seg
    step_first = jnp.arange(n_steps, dtype=I32) * ATTN_STEP_ROWS
    step_seg = jnp.minimum(jnp.searchsorted(seg_end, step_first, side="right"), batch * N_HEADS - 1)
    sub_first = jnp.arange(n_steps * ATTN_SUBS_PER_STEP, dtype=I32) * ATTN_SUB_ROWS
    sub_grp = jnp.minimum(jnp.searchsorted(gend.reshape(-1), sub_first, side="right"),
                          batch * N_HEADS * nb - 1)
    nsteps = (seg_end[-1] // ATTN_STEP_ROWS).astype(I32).reshape(1)
    dump_row = n_steps * ATTN_STEP_ROWS

    pos = _moba_positions(sel, gstart.reshape(batch, N_HEADS * nb, 1), seq, dump_row)
    pos5 = pos.reshape(N_HEADS, SEL_ROWS, nbt, blk // chunk, chunk)[:, :MOBA_TOPK]
    scatter_idx = pos5.transpose(2, 0, 3, 1, 4).reshape(nbt * N_HEADS * (blk // chunk), MOBA_TOPK, chunk)
    gather_idx = pos5.transpose(1, 2, 0, 3, 4).reshape(-1)

    qs = _sc_scatter_rows(qp.reshape(n * N_HEADS, HEAD_DIM), scatter_idx, dump_row + chunk)
    tables = ((step_seg % N_HEADS).astype(I32), (step_seg // N_HEADS).astype(I32),
              (sub_grp % nb).astype(I32), nsteps)
    return qp, sel, qs, tables, gather_idx


def _rope_tables(seq):
    half = ROT_DIM // 2
    inv = ROPE_THETA ** (-jnp.arange(0, ROT_DIM, 2, dtype=F32) / ROT_DIM)
    ang = jnp.arange(seq, dtype=F32)[:, None] * inv[None, :]
    cos, sin = jnp.cos(ang), jnp.sin(ang)
    rest = HEAD_DIM - ROT_DIM
    cos_t = jnp.concatenate([cos, cos, jnp.ones((seq, rest), F32)], axis=1)
    sin_t = jnp.concatenate([-sin, sin, jnp.zeros((seq, rest), F32)], axis=1)
    del half
    return cos_t, sin_t


def kernel(x, p, ln_g, ln_b, a_w_in, a_b_in, a_conv_w, a_conv_b, a_gate_a_w, a_gate_a_b, a_gate_i_w,
           a_gate_i_b, a_lambda, a_w_out, kv_ln_g, kv_ln_b, w_kv, b_w_q, b_w_o, router_w, router_b,
           exp_w_gate, exp_w_up, exp_w_down, sh_w_gate, sh_w_up, sh_w_down, ple_w, ple_gate_w):
    batch, seq, d = x.shape
    xs = [x.reshape(batch * seq, d)] * batch
    p_all = p.reshape(DEPTH * batch, seq, PLE_DIM)
    cos_t, sin_t = _rope_tables(seq)
    kv = None
    for i in range(DEPTH):
        if i < N_A_LAYERS:
            mixed = [_rglru_layer(xb, seq, b if i == 0 else 0, a_w_in[i], a_b_in[i], a_conv_w[i],
                                  a_conv_b[i], a_gate_a_w[i], a_gate_a_b[i], a_gate_i_w[i], a_gate_i_b[i],
                                  a_lambda[i], a_w_out[i], ln_g[i, 0], ln_b[i, 0])
                     for b, xb in enumerate(xs)]
        else:
            if i == N_A_LAYERS:
                kv = []
                for xb in xs:
                    k_blk, vt_blk, k_mean = _shared_kv(xb, seq, kv_ln_g, kv_ln_b, w_kv, cos_t, sin_t)
                    kv.append((k_blk, vt_blk, k_mean.reshape(seq // MOBA_BLOCK, d)))
            j = i - N_A_LAYERS
            mixed = _moba_layer([(xb,) + kvb for xb, kvb in zip(xs, kv)], seq, b_w_q[j], b_w_o[j],
                                ln_g[i, 0], ln_b[i, 0], cos_t, sin_t)
        xs = _moe_layer(mixed, i, router_w[i], router_b[i], exp_w_gate, exp_w_up, exp_w_down,
                        sh_w_gate[i], sh_w_up[i], sh_w_down[i], ln_g[i, 1], ln_b[i, 1],
                        p_all, i * batch, ple_w[i], ple_gate_w[i])
    return jnp.stack(xs, axis=0)
```

```python
import functools

import jax
import jax.numpy as jnp
from jax import lax
from jax.experimental import pallas as pl
from jax.experimental.pallas import tpu as pltpu
from jax.experimental.pallas import tpu_sc as plsc

F32 = jnp.float32
BF16 = jnp.bfloat16
I32 = jnp.int32
U32 = jnp.uint32
HIGH_HALF = 0xFFFF0000

SC_CORES = 2
SC_SUBCORES = 16
SC_WORKERS = SC_CORES * SC_SUBCORES
SC_INDEX_CHUNK = 128
SC_STAGE_BYTES = 256 * 1024

D_MODEL = 1024
DEPTH = 4
N_A_LAYERS = DEPTH // 2
D_RNN = D_MODEL
LRU_BLOCKS = 4
LRU_BLOCK_W = D_RNN // LRU_BLOCKS
CONV_W = 4
LRU_C = 8.0
N_HEADS = 8
HEAD_DIM = D_MODEL // N_HEADS
ROT_DIM = HEAD_DIM // 4
ROPE_THETA = 500000.0
MOBA_BLOCK = 256
MOBA_TOPK = 3
N_EXPERTS = 64
N_GROUPS = 8
GROUP_SIZE = N_EXPERTS // N_GROUPS
TOPK_GROUPS = 4
TOPK_EXPERTS = 8
D_EXPERT = 256
D_SHARED = 256
ROUTED_SCALE = 2.5
PLE_DIM = 256
DN_ALPHA = (2 * DEPTH) ** 0.25
LN_EPS = 1e-5

V7X_VMEM_LIMIT_BYTES = 56 * 1024 * 1024

MIXER_ROWS = 256
TOKEN_ROWS = 512
EXPERT_ROWS = 1024
NEG_INF = float("-inf")


def _params(*sem):
    return pltpu.CompilerParams(dimension_semantics=sem, vmem_limit_bytes=V7X_VMEM_LIMIT_BYTES)


def _layer_norm(z, g, b):
    mu = jnp.mean(z, axis=-1, keepdims=True)
    zc = z - mu
    var = jnp.mean(zc * zc, axis=-1, keepdims=True)
    return zc * lax.rsqrt(var + LN_EPS) * g + b


def _silu(x):
    return x * jax.nn.sigmoid(x)


def _gelu_tanh(x):
    return x * jax.nn.sigmoid(x * (1.5957691216057308 + 0.07135481627159855 * (x * x)))


def _full(shape):
    return pl.BlockSpec(shape, lambda *_: (0,) * len(shape))


def _count_le(ends, values):
    return jnp.sum((ends[None, :] <= values[:, None]).astype(I32), axis=1)


def _pack_bf16_pairs(x):
    w = x.shape[1] // 2
    bits = lax.bitcast_convert_type(x.astype(BF16).astype(F32), U32)
    return (bits[:, :w] >> 16) | (bits[:, w:] & U32(HIGH_HALF))


def _unpack_bf16_pairs(u):
    lo = lax.bitcast_convert_type(u << 16, F32)
    hi = lax.bitcast_convert_type(u & U32(HIGH_HALF), F32)
    return lo, hi


def _sc_mesh():
    return plsc.VectorSubcoreMesh(core_axis_name="c", subcore_axis_name="s")


def _sc_worker_id():
    return lax.axis_index("s") * SC_CORES + lax.axis_index("c")


def _sc_chunks_per_step(chunks_per_worker, row_words):
    g = max(1, SC_STAGE_BYTES // (SC_INDEX_CHUNK * row_words * 4))
    while chunks_per_worker % g:
        g -= 1
    return g


def _sc_gather_rows(table, idx):
    b = idx.shape[0]
    w = table.shape[1]
    chunk = SC_INDEX_CHUNK
    chunks_per_worker = b // (SC_WORKERS * chunk)
    assert chunks_per_worker * SC_WORKERS * chunk == b
    g = _sc_chunks_per_step(chunks_per_worker, w)

    @functools.partial(
        pl.kernel, mesh=_sc_mesh(), out_type=jax.ShapeDtypeStruct((b, w), table.dtype),
        scratch_types=[pltpu.VMEM((g, chunk), I32), pltpu.VMEM((g * chunk, w), table.dtype),
                       pltpu.SemaphoreType.DMA])
    def gather(table_hbm, idx_hbm, out_hbm, idx_v, rows_v, sem):
        first = _sc_worker_id() * chunks_per_worker

        @pl.loop(0, chunks_per_worker // g)
        def _(j):
            c0 = first + j * g
            pltpu.sync_copy(idx_hbm.at[pl.ds(c0, g)], idx_v)
            copies = [pltpu.async_copy(table_hbm.at[idx_v.at[q]], rows_v.at[pl.ds(q * chunk, chunk)], sem)
                      for q in range(g)]
            for cp in copies:
                cp.wait()
            pltpu.sync_copy(rows_v, out_hbm.at[pl.ds(c0 * chunk, g * chunk)])

    return gather(table, idx.reshape(b // chunk, chunk))


def _sc_scatter_rows(src, idx, out_rows):
    n, w = src.shape
    chunk = SC_INDEX_CHUNK
    fan = idx.shape[1]
    chunks_per_worker = n // (SC_WORKERS * chunk)
    assert idx.shape == (n // chunk, fan, chunk) and chunks_per_worker * SC_WORKERS * chunk == n
    g = _sc_chunks_per_step(chunks_per_worker, w)

    @functools.partial(
        pl.kernel, mesh=_sc_mesh(), out_type=jax.ShapeDtypeStruct((out_rows, w), src.dtype),
        scratch_types=[pltpu.VMEM((g, fan, chunk), I32), pltpu.VMEM((g * chunk, w), src.dtype),
                       pltpu.SemaphoreType.DMA])
    def scatter(src_hbm, idx_hbm, out_hbm, idx_v, rows_v, sem):
        first = _sc_worker_id() * chunks_per_worker

        @pl.loop(0, chunks_per_worker // g)
        def _(j):
            c0 = first + j * g
            pltpu.sync_copy(src_hbm.at[pl.ds(c0 * chunk, g * chunk)], rows_v)
            pltpu.sync_copy(idx_hbm.at[pl.ds(c0, g)], idx_v)
            copies = [pltpu.async_copy(rows_v.at[pl.ds(q * chunk, chunk)], out_hbm.at[idx_v.at[q, f]], sem)
                      for q in range(g) for f in range(fan)]
            for cp in copies:
                cp.wait()

    return scatter(src, idx)


def _rglru_kernel(x_ref, win_ref, bin_ref, cw_ref, cb_ref, gaw_ref, gab_ref, giw_ref, gib_ref,
                  lam_ref, wout_ref, lng_ref, lnb_ref, o_ref, op_ref, tail_ref, h_ref, *, tiles_per_seq):
    i = pl.program_id(0)
    tm = x_ref.shape[0]

    @pl.when(i % tiles_per_seq == 0)
    def _():
        tail_ref[...] = jnp.zeros_like(tail_ref)
        h_ref[...] = jnp.zeros_like(h_ref)

    x = x_ref[...]
    xy = jnp.dot(x.astype(BF16), win_ref[...], preferred_element_type=F32) + bin_ref[...]
    xb = xy[:, :D_RNN]
    y = _gelu_tanh(xy[:, D_RNN:])

    tail = tail_ref[...]
    row8 = lax.broadcasted_iota(I32, (8, 1), 0)
    xc = cb_ref[...] + xb * cw_ref[CONV_W - 1:CONV_W, :]
    for d in range(1, CONV_W):
        rolled = pltpu.roll(xb, d, 0)
        head = jnp.where(row8 < d, pltpu.roll(tail, d, 0), rolled[:8])
        shifted = jnp.concatenate([head, rolled[8:]], axis=0)
        xc = xc + shifted * cw_ref[CONV_W - 1 - d:CONV_W - d, :]
    tail_ref[...] = xb[tm - 8:, :]

    r_parts, i_parts = [], []
    for n in range(LRU_BLOCKS):
        xg = xc[:, n * LRU_BLOCK_W:(n + 1) * LRU_BLOCK_W].astype(BF16)
        r_parts.append(jnp.dot(xg, gaw_ref[n], preferred_element_type=F32))
        i_parts.append(jnp.dot(xg, giw_ref[n], preferred_element_type=F32))
    r = jax.nn.sigmoid(jnp.concatenate(r_parts, axis=1) + gab_ref[...])
    ig = jax.nn.sigmoid(jnp.concatenate(i_parts, axis=1) + gib_ref[...])

    lam = lam_ref[...]
    softplus_neg_lam = jnp.maximum(-lam, 0.0) + jnp.log1p(jnp.exp(-jnp.abs(lam)))
    log_a = (-LRU_C * r) * softplus_neg_lam
    a = jnp.exp(log_a)
    u = jnp.sqrt(1.0 - a * a) * (ig * xc)

    n_grp = tm // 8
    sub = lax.broadcasted_iota(I32, (1, 8, 1), 1)
    acc_a = a.reshape(n_grp, 8, D_RNN)
    acc_h = u.reshape(n_grp, 8, D_RNN)
    for d in (1, 2, 4):
        keep = sub >= d
        sh_a = pltpu.roll(acc_a, d, 1)
        sh_h = pltpu.roll(acc_h, d, 1)
        acc_h = jnp.where(keep, acc_a * sh_h + acc_h, acc_h)
        acc_a = jnp.where(keep, acc_a * sh_a, acc_a)
    state = h_ref[...]
    groups = []
    for g in range(n_grp):
        hg = acc_h[g] + acc_a[g] * state
        state = hg[7:8, :]
        groups.append(hg)
    h = jnp.concatenate(groups, axis=0)
    h_ref[...] = state

    mix = jnp.dot((h * y).astype(BF16), wout_ref[...], preferred_element_type=F32)
    x1 = _layer_norm(DN_ALPHA * x + mix, lng_ref[...], lnb_ref[...])
    o_ref[...] = x1
    op_ref[...] = _pack_bf16_pairs(x1)


def _rglru_layer(x, seq, stream, w_in, b_in, conv_w, conv_b, ga_w, ga_b, gi_w, gi_b, lam, w_out,
                 ln_g, ln_b):
    n, d = seq, x.shape[1]
    tm = min(MIXER_ROWS, seq)
    first_tile = stream * (seq // tm)
    row = lambda v: v.reshape(1, -1).astype(F32)
    return pl.pallas_call(
        functools.partial(_rglru_kernel, tiles_per_seq=seq // tm),
        grid=(n // tm,),
        in_specs=[pl.BlockSpec((tm, d), lambda i: (i + first_tile, 0)),
                  _full((d, 2 * D_RNN)), _full((1, 2 * D_RNN)),
                  _full((CONV_W, D_RNN)), _full((1, D_RNN)),
                  _full((LRU_BLOCKS, LRU_BLOCK_W, LRU_BLOCK_W)), _full((1, D_RNN)),
                  _full((LRU_BLOCKS, LRU_BLOCK_W, LRU_BLOCK_W)), _full((1, D_RNN)),
                  _full((1, D_RNN)), _full((D_RNN, d)), _full((1, d)), _full((1, d))],
        out_specs=[pl.BlockSpec((tm, d), lambda i: (i, 0)), pl.BlockSpec((tm, d // 2), lambda i: (i, 0))],
        out_shape=[jax.ShapeDtypeStruct((n, d), F32), jax.ShapeDtypeStruct((n, d // 2), U32)],
        scratch_shapes=[pltpu.VMEM((8, D_RNN), F32), pltpu.VMEM((1, D_RNN), F32)],
        compiler_params=_params("arbitrary"),
        name="rglru_mixer",
    )(x, w_in.astype(BF16), row(b_in), conv_w, row(conv_b), ga_w.astype(BF16), row(ga_b),
      gi_w.astype(BF16), row(gi_b), row(lam), w_out.astype(BF16), row(ln_g), row(ln_b))


def _first_argmax(cur, idx, size, axis):
    m = jnp.max(cur, axis=axis, keepdims=True)
    first = jnp.min(jnp.where(cur == m, idx, size), axis=axis, keepdims=True)
    return m, first


def _router_kernel(x_ref, rw_ref, rb_ref, e_ref, w_ref, r_ref, cnt_ref, carry_ref):
    i = pl.program_id(0)
    tm = x_ref.shape[0]

    @pl.when(i == 0)
    def _():
        carry_ref[...] = jnp.zeros_like(carry_ref)

    x = x_ref[...]
    x_hi = x.astype(BF16)
    x_lo = (x - x_hi.astype(F32)).astype(BF16)
    rw = rw_ref[...]
    hi_both = jnp.dot(x_hi, rw, preferred_element_type=F32)
    logits = (hi_both[:, :128] + hi_both[:, 128:]
              + jnp.dot(x_lo, rw[:, :128], preferred_element_type=F32))
    scores = jax.nn.sigmoid(logits.T[:N_EXPERTS, :])
    choice = scores + rb_ref[...]

    c3 = choice.reshape(N_GROUPS, GROUP_SIZE, tm)
    in_grp = lax.broadcasted_iota(I32, c3.shape, 1)
    m1, i1 = _first_argmax(c3, in_grp, GROUP_SIZE, 1)
    m2 = jnp.max(jnp.where(in_grp == i1, NEG_INF, c3), axis=1, keepdims=True)
    grp_score = (m1 + m2)[:, 0, :]

    grp_id = lax.broadcasted_iota(I32, grp_score.shape, 0)
    grp_sel = jnp.zeros(grp_score.shape, jnp.bool_)
    cur = grp_score
    for _ in range(TOPK_GROUPS):
        _, gi = _first_argmax(cur, grp_id, N_GROUPS, 0)
        hit = grp_id == gi
        grp_sel = jnp.logical_or(grp_sel, hit)
        cur = jnp.where(hit, NEG_INF, cur)

    cur = jnp.where(grp_sel[:, None, :], c3, NEG_INF).reshape(N_EXPERTS, tm)
    exp_id = lax.broadcasted_iota(I32, cur.shape, 0)
    sel = jnp.zeros(cur.shape, F32)
    e_rows, s_rows = [], []
    for _ in range(TOPK_EXPERTS):
        _, ei = _first_argmax(cur, exp_id, N_EXPERTS, 0)
        hit = exp_id == ei
        e_rows.append(ei)
        s_rows.append(jnp.sum(jnp.where(hit, scores, 0.0), axis=0, keepdims=True))
        sel = jnp.where(hit, 1.0, sel)
        cur = jnp.where(hit, NEG_INF, cur)
    e_top = jnp.concatenate(e_rows, axis=0)
    s_top = jnp.concatenate(s_rows, axis=0)
    w_ref[...] = s_top / jnp.sum(s_top, axis=0, keepdims=True) * ROUTED_SCALE
    e_ref[...] = e_top

    t_row = lax.broadcasted_iota(I32, (tm, tm), 0)
    t_col = lax.broadcasted_iota(I32, (tm, tm), 1)
    before = (t_row < t_col).astype(BF16)
    cum = jnp.dot(sel.astype(BF16), before, preferred_element_type=F32) + carry_ref[...]
    r_rows = [jnp.sum(jnp.where(exp_id == e_rows[k], cum, 0.0), axis=0, keepdims=True)
              for k in range(TOPK_EXPERTS)]
    r_ref[...] = jnp.concatenate(r_rows, axis=0).astype(I32)
    carry_ref[...] = carry_ref[...] + jnp.sum(sel, axis=1, keepdims=True)
    cnt_ref[...] = jnp.broadcast_to(carry_ref[...], cnt_ref.shape).astype(I32)


def _router(x, router_w, router_b):
    n, d = x.shape
    tm = min(TOKEN_ROWS, n)
    rw = jnp.pad(router_w, ((0, 0), (0, 128 - N_EXPERTS)))
    rw_hi = rw.astype(BF16)
    rw = jnp.concatenate([rw_hi, (rw - rw_hi.astype(F32)).astype(BF16)], axis=1)
    k = TOPK_EXPERTS
    tok_spec = pl.BlockSpec((k, tm), lambda i: (0, i))
    return pl.pallas_call(
        _router_kernel,
        grid=(n // tm,),
        in_specs=[pl.BlockSpec((tm, d), lambda i: (i, 0)), _full((d, 256)), _full((N_EXPERTS, 1))],
        out_specs=[tok_spec, tok_spec, tok_spec, _full((N_EXPERTS, 128))],
        out_shape=[jax.ShapeDtypeStruct((k, n), I32), jax.ShapeDtypeStruct((k, n), F32),
                   jax.ShapeDtypeStruct((k, n), I32), jax.ShapeDtypeStruct((N_EXPERTS, 128), I32)],
        scratch_shapes=[pltpu.VMEM((N_EXPERTS, 1), F32)],
        compiler_params=_params("arbitrary"),
        name="moe_router",
    )(x, rw, router_b.reshape(N_EXPERTS, 1).astype(F32))


def _expert_kernel(blk_e_ref, nblk_ref, first_ref, slot_ref, next_ref, xs_ref, wg_hbm, wu_hbm, wd_hbm,
                   after_ref, ys_ref, wg_f, wu_f, wd_f, wg_s, wu_s, wd_s, sem, *, layer):
    del after_ref
    i = pl.program_id(0)

    def fetch(expert, slot):
        return (pltpu.make_async_copy(wg_hbm.at[layer, expert], wg_f.at[slot], sem.at[slot, 0]),
                pltpu.make_async_copy(wu_hbm.at[layer, expert], wu_f.at[slot], sem.at[slot, 1]),
                pltpu.make_async_copy(wd_hbm.at[layer, expert], wd_f.at[slot], sem.at[slot, 2]))

    @pl.when(i < nblk_ref[0])
    def _():
        slot = slot_ref[i]

        @pl.when(i == 0)
        def _():
            for copy in fetch(blk_e_ref[0], 0):
                copy.start()

        @pl.when(first_ref[i] == 1)
        def _():
            for copy in fetch(blk_e_ref[i], slot):
                copy.wait()
            wg_s[...] = wg_f[slot].astype(BF16)
            wu_s[...] = wu_f[slot].astype(BF16)
            wd_s[...] = wd_f[slot].astype(BF16)

            @pl.when(next_ref[i] >= 0)
            def _():
                for copy in fetch(next_ref[i], 1 - slot):
                    copy.start()

        lo, hi = _unpack_bf16_pairs(xs_ref[...])
        xs = jnp.concatenate([lo.astype(BF16), hi.astype(BF16)], axis=1)
        g = jnp.dot(xs, wg_s[...], preferred_element_type=F32)
        u = jnp.dot(xs, wu_s[...], preferred_element_type=F32)
        hdn = (_silu(g) * u).astype(BF16)
        ys_ref[...] = _pack_bf16_pairs(jnp.dot(hdn, wd_s[...], preferred_element_type=F32))


_ORDER_ONLY = pl.BlockSpec(memory_space=pl.ANY)


def _experts(xs, blk_e, nblk, layer, w_gate, w_up, w_down, after):
    rows, half = xs.shape
    d = 2 * half
    m = EXPERT_ROWS
    n_blocks = rows // m
    idx = jnp.arange(n_blocks, dtype=I32)
    used = idx < nblk[0]
    first = jnp.logical_and(used, jnp.logical_or(idx == 0, blk_e != jnp.roll(blk_e, 1)))
    slot = (jnp.cumsum(first.astype(I32)) - 1) % 2
    larger = jnp.logical_and(blk_e[None, :] > blk_e[:, None], used[None, :])
    next_e = jnp.min(jnp.where(larger, blk_e[None, :], N_EXPERTS), axis=1)
    next_e = jnp.where(next_e < N_EXPERTS, next_e, -1)
    row_map = lambda i, be, nb, fi, sl, nx: (jnp.minimum(i, nb[0] - 1), 0)
    hbm = pl.BlockSpec(memory_space=pl.ANY)
    return pl.pallas_call(
        functools.partial(_expert_kernel, layer=layer),
        grid_spec=pltpu.PrefetchScalarGridSpec(
            num_scalar_prefetch=5,
            grid=(n_blocks,),
            in_specs=[pl.BlockSpec((m, half), row_map), hbm, hbm, hbm, _ORDER_ONLY],
            out_specs=pl.BlockSpec((m, half), row_map),
            scratch_shapes=[pltpu.VMEM((2, d, D_EXPERT), F32), pltpu.VMEM((2, d, D_EXPERT), F32),
                            pltpu.VMEM((2, D_EXPERT, d), F32),
                            pltpu.VMEM((d, D_EXPERT), BF16), pltpu.VMEM((d, D_EXPERT), BF16),
                            pltpu.VMEM((D_EXPERT, d), BF16), pltpu.SemaphoreType.DMA((2, 3))]),
        out_shape=jax.ShapeDtypeStruct((rows, half), U32),
        compiler_params=_params("arbitrary"),
        name="moe_experts",
    )(blk_e, nblk, first.astype(I32), slot.astype(I32), next_e.astype(I32), xs, w_gate, w_up, w_down, after)


def _combine_kernel(x_ref, yg_ref, w_ref, sg_ref, su_ref, sd_ref, lng_ref, lnb_ref,
                    p_ref, pw_ref, pg_ref, after_ref, o_ref):
    del after_ref
    x = x_ref[...]
    wt = w_ref[...].T
    moe_lo, moe_hi = None, None
    for k in range(TOPK_EXPERTS):
        lo, hi = _unpack_bf16_pairs(yg_ref[k])
        wk = wt[:, k:k + 1]
        moe_lo = wk * lo if k == 0 else moe_lo + wk * lo
        moe_hi = wk * hi if k == 0 else moe_hi + wk * hi
    moe = jnp.concatenate([moe_lo, moe_hi], axis=1)
    xb = x.astype(BF16)
    hdn = _silu(jnp.dot(xb, sg_ref[...], preferred_element_type=F32)) * \
        jnp.dot(xb, su_ref[...], preferred_element_type=F32)
    shared = jnp.dot(hdn.astype(BF16), sd_ref[...], preferred_element_type=F32)
    x2 = _layer_norm(DN_ALPHA * x + (moe + shared), lng_ref[...], lnb_ref[...])
    emb = jnp.dot(p_ref[0].astype(BF16), pw_ref[...], preferred_element_type=F32)
    gate = jax.nn.sigmoid(jnp.dot(x2.astype(BF16), pg_ref[...], preferred_element_type=F32))
    o_ref[...] = x2 + emb * gate


def _combine(x, yg, wgt, s_gate, s_up, s_down, ln_g, ln_b, p_all, p_index, ple_w, ple_gate_w, after):
    n, d = x.shape
    tm = min(TOKEN_ROWS // 2, n)
    k = TOPK_EXPERTS
    row = lambda v: v.reshape(1, -1).astype(F32)
    return pl.pallas_call(
        _combine_kernel,
        grid=(n // tm,),
        in_specs=[pl.BlockSpec((tm, d), lambda i: (i, 0)),
                  pl.BlockSpec((k, tm, d // 2), lambda i: (0, i, 0)),
                  pl.BlockSpec((k, tm), lambda i: (0, i)),
                  _full((d, D_SHARED)), _full((d, D_SHARED)), _full((D_SHARED, d)),
                  _full((1, d)), _full((1, d)),
                  pl.BlockSpec((1, tm, PLE_DIM), lambda i: (p_index, i, 0)),
                  _full((PLE_DIM, d)), _full((d, d)), _ORDER_ONLY],
        out_specs=pl.BlockSpec((tm, d), lambda i: (i, 0)),
        out_shape=jax.ShapeDtypeStruct((n, d), F32),
        compiler_params=_params("parallel"),
        name="moe_combine",
    )(x, yg, wgt, s_gate.astype(BF16), s_up.astype(BF16), s_down.astype(BF16), row(ln_g), row(ln_b),
      p_all, ple_w.astype(BF16), ple_gate_w.astype(BF16), after)


def _positions_kernel(e_ref, r_ref, start_ref, pos_ref):
    e = e_ref[...]
    start = start_ref[...]
    grp_id = lax.broadcasted_iota(I32, (start.shape[0], e.shape[1]), 0)
    rows = [jnp.sum(jnp.where(grp_id == e[k:k + 1], start, 0), axis=0, keepdims=True)
            for k in range(e.shape[0])]
    pos_ref[...] = jnp.concatenate(rows, axis=0) + r_ref[...]


def _positions(e_idx, rank, starts):
    k, n = e_idx.shape
    tm = min(4 * TOKEN_ROWS, n)
    groups = starts.shape[0]
    tok_spec = pl.BlockSpec((k, tm), lambda i: (0, i))
    return pl.pallas_call(
        _positions_kernel,
        grid=(n // tm,),
        in_specs=[tok_spec, tok_spec, _full((groups, 1))],
        out_specs=tok_spec,
        out_shape=jax.ShapeDtypeStruct((k, n), I32),
        compiler_params=_params("parallel"),
        name="group_positions",
    )(e_idx, rank, starts.reshape(groups, 1))


def _moe_layer(streams, layer, router_w, router_b, w_gate, w_up, w_down, s_gate, s_up, s_down,
               ln_g, ln_b, p_all, p_index, ple_w, ple_gate_w):
    m = EXPERT_ROWS
    k = TOPK_EXPERTS
    chunk = SC_INDEX_CHUNK
    routed = []
    for x, xp in streams:
        n, d = x.shape
        e_idx, wgt, rank, cnt = _router(x, router_w, router_b)
        counts = cnt[:, 0]
        padded = (counts + m - 1) // m * m
        pends = jnp.cumsum(padded)
        n_blocks = (n * k) // m + N_EXPERTS
        blk_first = jnp.arange(n_blocks, dtype=I32) * m
        blk_e = jnp.minimum(_count_le(pends, blk_first), N_EXPERTS - 1)
        nblk = (pends[-1] // m).astype(I32).reshape(1)
        pos = _positions(e_idx, rank, (pends - padded).astype(I32))
        pos_chunks = pos.reshape(k, n // chunk, chunk).transpose(1, 0, 2)
        xs = _sc_scatter_rows(xp, pos_chunks, n_blocks * m)
        routed.append((xs, blk_e, nblk, pos, wgt))
    gathered, order = [], routed[-1][3]
    for (x, _), (xs, blk_e, nblk, pos, wgt) in zip(streams, routed):
        n, d = x.shape
        ys = _experts(xs, blk_e, nblk, layer, w_gate, w_up, w_down, order)
        gathered.append(_sc_gather_rows(ys, pos.reshape(-1)).reshape(k, n, d // 2))
        order = ys
    outs = []
    for s, ((x, _), (_, _, _, _, wgt), yg) in enumerate(zip(streams, routed, gathered)):
        outs.append(_combine(x, yg, wgt, s_gate, s_up, s_down, ln_g, ln_b, p_all, p_index + s,
                             ple_w, ple_gate_w, order))
        order = outs[-1]
    return outs


def _rope(t, cos_t, sin_t):
    half = ROT_DIM // 2
    width = t.shape[1]
    lane = lax.broadcasted_iota(I32, (1, width), 1) % HEAD_DIM
    partner = jnp.where(lane < half, pltpu.roll(t, width - half, 1), pltpu.roll(t, half, 1))
    cos_f = jnp.concatenate([cos_t] * N_HEADS, axis=1)
    sin_f = jnp.concatenate([sin_t] * N_HEADS, axis=1)
    return t * cos_f + partner * sin_f


def _kv_kernel(x_ref, g_ref, b_ref, wkv_ref, cos_ref, sin_ref, k_ref, vt_ref, km_ref):
    h = _layer_norm(x_ref[...], g_ref[...], b_ref[...])
    kv = jnp.dot(h.astype(BF16), wkv_ref[...], preferred_element_type=F32)
    k = _rope(kv[:, :D_MODEL], cos_ref[...], sin_ref[...])
    vt = kv[:, D_MODEL:].T
    km_ref[0] = jnp.mean(k, axis=0, keepdims=True)
    for hd in range(N_HEADS):
        k_ref[hd, 0] = k[:, hd * HEAD_DIM:(hd + 1) * HEAD_DIM].astype(BF16)
        vt_ref[hd, 0] = vt[hd * HEAD_DIM:(hd + 1) * HEAD_DIM, :].astype(BF16)


def _shared_kv(x, seq, ln_g, ln_b, w_kv, cos_t, sin_t):
    n, d = x.shape
    blk = MOBA_BLOCK
    nbt = n // blk
    spb = seq // blk
    row = lambda v: v.reshape(1, -1).astype(F32)
    return pl.pallas_call(
        _kv_kernel,
        grid=(nbt,),
        in_specs=[pl.BlockSpec((blk, d), lambda i: (i, 0)), _full((1, d)), _full((1, d)),
                  _full((d, 2 * d)),
                  pl.BlockSpec((blk, HEAD_DIM), lambda i: (i % spb, 0)),
                  pl.BlockSpec((blk, HEAD_DIM), lambda i: (i % spb, 0))],
        out_specs=[pl.BlockSpec((N_HEADS, 1, blk, HEAD_DIM), lambda i: (0, i, 0, 0)),
                   pl.BlockSpec((N_HEADS, 1, HEAD_DIM, blk), lambda i: (0, i, 0, 0)),
                   pl.BlockSpec((1, 1, d), lambda i: (i, 0, 0))],
        out_shape=[jax.ShapeDtypeStruct((N_HEADS, nbt, blk, HEAD_DIM), BF16),
                   jax.ShapeDtypeStruct((N_HEADS, nbt, HEAD_DIM, blk), BF16),
                   jax.ShapeDtypeStruct((nbt, 1, d), F32)],
        compiler_params=_params("parallel"),
        name="shared_kv",
    )(x, row(ln_g), row(ln_b), w_kv.astype(BF16), cos_t, sin_t)


SEL_ROWS = 8


def _query_kernel(x_ref, wq_ref, cos_ref, sin_ref, km_ref, qp_ref, sel_ref, cnt_ref, carry_ref,
                  *, blocks_per_seq):
    i = pl.program_id(0)
    tm = x_ref.shape[0]
    nb = km_ref.shape[0]
    own = i % blocks_per_seq

    @pl.when(own == 0)
    def _():
        carry_ref[...] = jnp.zeros_like(carry_ref)

    q = jnp.dot(x_ref[...].astype(BF16), wq_ref[...], preferred_element_type=F32)
    q = _rope(q, cos_ref[...], sin_ref[...]) * (HEAD_DIM ** -0.5)

    t_row = lax.broadcasted_iota(I32, (tm, tm), 0)
    t_col = lax.broadcasted_iota(I32, (tm, tm), 1)
    before = (t_row < t_col).astype(BF16)
    km = km_ref[...]
    km_hi = km.astype(BF16)
    km_both = jnp.concatenate([km_hi, (km - km_hi.astype(F32)).astype(BF16)], axis=0)
    q_hi = q.astype(BF16)
    q_lo = (q - q_hi.astype(F32)).astype(BF16)
    nt = (((1,), (1,)), ((), ()))
    gates = []
    for hd in range(N_HEADS):
        lo, hi = hd * HEAD_DIM, (hd + 1) * HEAD_DIM
        qp_ref[0, hd] = q[:, lo:hi]
        both = lax.dot_general(km_both[:, lo:hi], q_hi[:, lo:hi], nt, preferred_element_type=F32)
        gates.append(both[:nb] + both[nb:]
                     + lax.dot_general(km_hi[:, lo:hi], q_lo[:, lo:hi], nt, preferred_element_type=F32))

    gate = jnp.concatenate(gates, axis=0).reshape(N_HEADS, nb, tm)
    blk_id = lax.broadcasted_iota(I32, (N_HEADS, nb, tm), 1)
    cur = jnp.where(blk_id < own, gate, NEG_INF)
    sel_rows = []
    chosen = jnp.zeros((N_HEADS, nb, tm), F32)
    for _ in range(MOBA_TOPK):
        m, bi = _first_argmax(cur, blk_id, nb, 1)
        valid = m > NEG_INF
        hit = jnp.logical_and(blk_id == bi, valid)
        sel_rows.append(jnp.where(valid, bi, -1))
        chosen = jnp.where(hit, 1.0, chosen)
        cur = jnp.where(blk_id == bi, NEG_INF, cur)
    chosen2 = chosen.reshape(N_HEADS * nb, tm)
    carry = carry_ref[...]
    cum = (jnp.dot(chosen2.astype(BF16), before, preferred_element_type=F32) + carry).reshape(N_HEADS, nb, tm)
    rank_rows = [jnp.sum(jnp.where(blk_id == s, cum, 0.0), axis=1, keepdims=True).astype(I32)
                 for s in sel_rows]
    carry = carry + jnp.sum(chosen2, axis=1, keepdims=True)
    carry_ref[...] = carry
    pad = jnp.zeros((N_HEADS, SEL_ROWS - 2 * MOBA_TOPK, tm), I32)
    table = jnp.concatenate(sel_rows + rank_rows + [pad], axis=1)
    sel_ref[...] = table.reshape(N_HEADS * SEL_ROWS, tm)
    cnt_ref[0] = jnp.broadcast_to(carry, cnt_ref.shape[1:]).astype(I32)


def _queries(x, seq, w_q, cos_t, sin_t, k_mean):
    n, d = x.shape
    blk = MOBA_BLOCK
    nb = seq // blk
    return pl.pallas_call(
        functools.partial(_query_kernel, blocks_per_seq=nb),
        grid=(n // blk,),
        in_specs=[pl.BlockSpec((blk, d), lambda i: (i, 0)), _full((d, d)),
                  pl.BlockSpec((blk, HEAD_DIM), lambda i: (i % nb, 0)),
                  pl.BlockSpec((blk, HEAD_DIM), lambda i: (i % nb, 0)),
                  pl.BlockSpec((nb, d), lambda i: (i // nb, 0))],
        out_specs=[pl.BlockSpec((1, N_HEADS, blk, HEAD_DIM), lambda i: (i, 0, 0, 0)),
                   pl.BlockSpec((N_HEADS * SEL_ROWS, blk), lambda i: (0, i)),
                   pl.BlockSpec((1, N_HEADS * nb, 128), lambda i: (i // nb, 0, 0))],
        out_shape=[jax.ShapeDtypeStruct((n // blk, N_HEADS, blk, HEAD_DIM), F32),
                   jax.ShapeDtypeStruct((N_HEADS * SEL_ROWS, n), I32),
                   jax.ShapeDtypeStruct((n // seq, N_HEADS * nb, 128), I32)],
        scratch_shapes=[pltpu.VMEM((N_HEADS * nb, 1), F32)],
        compiler_params=_params("arbitrary"),
        name="moba_queries",
    )(x, w_q.astype(BF16), cos_t, sin_t, k_mean)


def _moba_positions_kernel(sel_ref, start_ref, pos_ref, *, dump_row):
    tm = sel_ref.shape[1]
    nb = start_ref.shape[1] // N_HEADS
    blk_id = lax.broadcasted_iota(I32, (nb, tm), 0)
    dump = dump_row + lax.broadcasted_iota(I32, (1, tm), 1) % SC_INDEX_CHUNK
    rows = []
    for hd in range(N_HEADS):
        start = start_ref[0, hd * nb:(hd + 1) * nb, :]
        for s in range(MOBA_TOPK):
            sel = sel_ref[hd * SEL_ROWS + s:hd * SEL_ROWS + s + 1, :]
            rank = sel_ref[hd * SEL_ROWS + MOBA_TOPK + s:hd * SEL_ROWS + MOBA_TOPK + s + 1, :]
            base = jnp.sum(jnp.where(blk_id == sel, start, 0), axis=0, keepdims=True)
            rows.append(jnp.where(sel >= 0, base + rank, dump))
        rows.extend([dump] * (SEL_ROWS - MOBA_TOPK))
    pos_ref[...] = jnp.concatenate(rows, axis=0)


def _moba_positions(sel, starts, seq, dump_row):
    rows, n = sel.shape
    tm = min(4 * TOKEN_ROWS, seq)
    tps = seq // tm
    groups = starts.shape[1]
    return pl.pallas_call(
        functools.partial(_moba_positions_kernel, dump_row=dump_row),
        grid=(n // tm,),
        in_specs=[pl.BlockSpec((rows, tm), lambda i: (0, i)),
                  pl.BlockSpec((1, groups, 1), lambda i: (i // tps, 0, 0))],
        out_specs=pl.BlockSpec((rows, tm), lambda i: (0, i)),
        out_shape=jax.ShapeDtypeStruct((rows, n), I32),
        compiler_params=_params("parallel"),
        name="moba_positions",
    )(sel, starts)


ATTN_SUB_ROWS = 128
ATTN_SUBS_PER_STEP = 32
ATTN_STEP_ROWS = ATTN_SUB_ROWS * ATTN_SUBS_PER_STEP


def _pack_bf16_row_pairs(x):
    h = x.shape[0] // 2
    bits = lax.bitcast_convert_type(x.astype(BF16).astype(F32), U32)
    return (bits[:h] >> 16) | (bits[h:] & U32(HIGH_HALF))


def _transpose_u32(x):
    return lax.bitcast_convert_type(lax.bitcast_convert_type(x, I32).T, U32)


def _scores_t(k, q, keep):
    st = lax.dot_general(k, q, (((1,), (1,)), ((), ())), preferred_element_type=F32)
    if keep is not None:
        st = jnp.where(keep, st, NEG_INF)
    m = jnp.max(st, axis=0, keepdims=True)
    p = jnp.exp(st - m)
    return m, p, jnp.sum(p, axis=0, keepdims=True)


def _group_attn_kernel(step_h_ref, step_b_ref, sub_j_ref, nsteps_ref, qs_ref, k_ref, vt_ref, after_ref,
                       part_ref):
    del step_h_ref, step_b_ref, after_ref
    i = pl.program_id(0)
    half = HEAD_DIM // 2

    @pl.when(i < nsteps_ref[0])
    def _():
        for u in range(ATTN_SUBS_PER_STEP):
            j = sub_j_ref[i * ATTN_SUBS_PER_STEP + u]
            rows = pl.ds(u * ATTN_SUB_ROWS, ATTN_SUB_ROWS)
            q = qs_ref[rows, :].astype(BF16)
            m, p, l = _scores_t(k_ref[0, j], q, None)
            ot = jnp.dot(vt_ref[0, j], p.astype(BF16), preferred_element_type=F32) * (1.0 / l)
            lse = lax.bitcast_convert_type(m + jnp.log(l), U32)
            packed_t = jnp.concatenate(
                [_pack_bf16_row_pairs(ot), jnp.broadcast_to(lse, (half, ATTN_SUB_ROWS))], axis=0)
            part_ref[rows, :] = _transpose_u32(packed_t)


def _group_attention(qs, k_blk, vt_blk, step_h, step_b, sub_j, nsteps, nb, after):
    n_steps = step_h.shape[0]
    row_map = lambda i, sh, sb, sj, ns: (jnp.minimum(i, ns[0] - 1), 0)
    kv_map = lambda i, sh, sb, sj, ns: (sh[i], sb[i], 0, 0)
    return pl.pallas_call(
        _group_attn_kernel,
        grid_spec=pltpu.PrefetchScalarGridSpec(
            num_scalar_prefetch=4,
            grid=(n_steps,),
            in_specs=[pl.BlockSpec((ATTN_STEP_ROWS, HEAD_DIM), row_map),
                      pl.BlockSpec((1, nb, MOBA_BLOCK, HEAD_DIM), kv_map),
                      pl.BlockSpec((1, nb, HEAD_DIM, MOBA_BLOCK), kv_map), _ORDER_ONLY],
            out_specs=pl.BlockSpec((ATTN_STEP_ROWS, HEAD_DIM), row_map)),
        out_shape=jax.ShapeDtypeStruct(((n_steps + 1) * ATTN_STEP_ROWS, HEAD_DIM), U32),
        compiler_params=_params("arbitrary"),
        name="moba_group_attention",
    )(step_h, step_b, sub_j, nsteps, qs, k_blk, vt_blk, after)


def _attn_merge_kernel(x_ref, qp_ref, k_ref, vt_ref, pg_ref, sel_ref, wo_ref, lng_ref, lnb_ref,
                       after_ref, o_ref, op_ref):
    del after_ref
    tm = x_ref.shape[0]
    half = HEAD_DIM // 2
    sel = sel_ref[...]
    key = lax.broadcasted_iota(I32, (tm, tm), 0)
    qry = lax.broadcasted_iota(I32, (tm, tm), 1)
    causal = key <= qry
    heads = []
    for hd in range(N_HEADS):
        q = qp_ref[0, hd].astype(BF16)
        m_own, p, l_own = _scores_t(k_ref[hd, 0], q, causal)
        acc = jnp.dot(vt_ref[hd, 0], p.astype(BF16), preferred_element_type=F32)

        outs, lses = [], []
        m_tot = m_own
        for c in range(MOBA_TOPK):
            part_t = _transpose_u32(pg_ref[c, 0, hd])
            lo, hi = _unpack_bf16_pairs(part_t[:half])
            valid = sel[hd * SEL_ROWS + c:hd * SEL_ROWS + c + 1, :] >= 0
            outs.append(jnp.where(valid, jnp.concatenate([lo, hi], axis=0), 0.0))
            lse = jnp.where(valid, lax.bitcast_convert_type(part_t[half:half + 1], F32), NEG_INF)
            lses.append(lse)
            m_tot = jnp.maximum(m_tot, lse)
        w_own = jnp.exp(m_own - m_tot)
        num = acc * w_own
        den = l_own * w_own
        for c in range(MOBA_TOPK):
            w_c = jnp.exp(lses[c] - m_tot)
            num = num + outs[c] * w_c
            den = den + w_c
        heads.append(num * (1.0 / den))
    att = jnp.concatenate(heads, axis=0).T.astype(BF16)
    mix = jnp.dot(att, wo_ref[...], preferred_element_type=F32)
    x1 = _layer_norm(DN_ALPHA * x_ref[...] + mix, lng_ref[...], lnb_ref[...])
    o_ref[...] = x1
    op_ref[...] = _pack_bf16_pairs(x1)


def _attn_merge(x, qp, k_blk, vt_blk, pg, sel, w_o, ln_g, ln_b, after):
    n, d = x.shape
    blk = MOBA_BLOCK
    row = lambda v: v.reshape(1, -1).astype(F32)
    return pl.pallas_call(
        _attn_merge_kernel,
        grid=(n // blk,),
        in_specs=[pl.BlockSpec((blk, d), lambda i: (i, 0)),
                  pl.BlockSpec((1, N_HEADS, blk, HEAD_DIM), lambda i: (i, 0, 0, 0)),
                  pl.BlockSpec((N_HEADS, 1, blk, HEAD_DIM), lambda i: (0, i, 0, 0)),
                  pl.BlockSpec((N_HEADS, 1, HEAD_DIM, blk), lambda i: (0, i, 0, 0)),
                  pl.BlockSpec((MOBA_TOPK, 1, N_HEADS, blk, HEAD_DIM), lambda i: (0, i, 0, 0, 0)),
                  pl.BlockSpec((N_HEADS * SEL_ROWS, blk), lambda i: (0, i)),
                  _full((d, d)), _full((1, d)), _full((1, d)), _ORDER_ONLY],
        out_specs=[pl.BlockSpec((blk, d), lambda i: (i, 0)), pl.BlockSpec((blk, d // 2), lambda i: (i, 0))],
        out_shape=[jax.ShapeDtypeStruct((n, d), F32), jax.ShapeDtypeStruct((n, d // 2), U32)],
        compiler_params=_params("parallel"),
        name="moba_merge",
    )(x, qp, k_blk, vt_blk, pg, sel, w_o.astype(BF16), row(ln_g), row(ln_b), after)


def _moba_layer(streams, seq, w_q, w_o, ln_g, ln_b, cos_t, sin_t):
    staged = [_moba_regroup(x, seq, w_q, cos_t, sin_t, k_mean) for x, _, _, k_mean in streams]
    gathered, order = [], staged[-1][1]
    for (x, k_blk, vt_blk, _), (qp, sel, qs, tables, gather_idx) in zip(streams, staged):
        nbt = x.shape[0] // MOBA_BLOCK
        part = _group_attention(qs, k_blk, vt_blk, *tables, seq // MOBA_BLOCK, order)
        gathered.append(_sc_gather_rows(part, gather_idx).reshape(MOBA_TOPK, nbt, N_HEADS, MOBA_BLOCK, HEAD_DIM))
        order = part
    outs = []
    for (x, k_blk, vt_blk, _), (qp, sel, _, _, _), pg in zip(streams, staged, gathered):
        outs.append(_attn_merge(x, qp, k_blk, vt_blk, pg, sel, w_o, ln_g, ln_b, order))
        order = outs[-1][0]
    return outs


def _moba_regroup(x, seq, w_q, cos_t, sin_t, k_mean):
    n, d = x.shape
    batch = n // seq
    blk = MOBA_BLOCK
    nb = seq // blk
    nbt = n // blk
    chunk = SC_INDEX_CHUNK
    qp, sel, cnt = _queries(x, seq, w_q, cos_t, sin_t, k_mean)

    counts = cnt[:, :, 0].reshape(batch * N_HEADS, nb)
    gpad = (counts + ATTN_SUB_ROWS - 1) // ATTN_SUB_ROWS * ATTN_SUB_ROWS
    seg = jnp.sum(gpad, axis=1)
    seg_pad = (seg + ATTN_STEP_ROWS - 1) // ATTN_STEP_ROWS * ATTN_STEP_ROWS
    seg_end = jnp.cumsum(seg_pad)
    gend = (seg_end - seg_pad)[:, None] + jnp.cumsum(gpad, axis=1)
    gstart = (gend - gpad).astype(I32)
    steps_per_seg = -(-(MOBA_TOPK * seq + nb * (ATTN_SUB_ROWS - 1)) // ATTN_STEP_ROWS)
    n_steps = batch * N_HEADS * steps_per_seg
    step_first = jnp.arange(n_steps, dtype=I32) * ATTN_STEP_ROWS
    step_seg = jnp.minimum(_count_le(seg_end, step_first), batch * N_HEADS - 1)
    sub_first = jnp.arange(n_steps * ATTN_SUBS_PER_STEP, dtype=I32) * ATTN_SUB_ROWS
    sub_grp = jnp.minimum(_count_le(gend.reshape(-1), sub_first), batch * N_HEADS * nb - 1)
    nsteps = (seg_end[-1] // ATTN_STEP_ROWS).astype(I32).reshape(1)
    dump_row = n_steps * ATTN_STEP_ROWS

    pos = _moba_positions(sel, gstart.reshape(batch, N_HEADS * nb, 1), seq, dump_row)
    pos5 = pos.reshape(N_HEADS, SEL_ROWS, nbt, blk // chunk, chunk)[:, :MOBA_TOPK]
    scatter_idx = pos5.transpose(2, 0, 3, 1, 4).reshape(nbt * N_HEADS * (blk // chunk), MOBA_TOPK, chunk)
    gather_idx = pos5.transpose(1, 2, 0, 3, 4).reshape(-1)

    qs = _sc_scatter_rows(qp.reshape(n * N_HEADS, HEAD_DIM), scatter_idx, dump_row + chunk)
    tables = ((step_seg % N_HEADS).astype(I32), (step_seg // N_HEADS).astype(I32),
              (sub_grp % nb).astype(I32), nsteps)
    return qp, sel, qs, tables, gather_idx


def _rope_tables(seq):
    half = ROT_DIM // 2
    inv = ROPE_THETA ** (-jnp.arange(0, ROT_DIM, 2, dtype=F32) / ROT_DIM)
    ang = jnp.arange(seq, dtype=F32)[:, None] * inv[None, :]
    cos, sin = jnp.cos(ang), jnp.sin(ang)
    rest = HEAD_DIM - ROT_DIM
    cos_t = jnp.concatenate([cos, cos, jnp.ones((seq, rest), F32)], axis=1)
    sin_t = jnp.concatenate([-sin, sin, jnp.zeros((seq, rest), F32)], axis=1)
    del half
    return cos_t, sin_t


def kernel(x, p, ln_g, ln_b, a_w_in, a_b_in, a_conv_w, a_conv_b, a_gate_a_w, a_gate_a_b, a_gate_i_w,
           a_gate_i_b, a_lambda, a_w_out, kv_ln_g, kv_ln_b, w_kv, b_w_q, b_w_o, router_w, router_b,
           exp_w_gate, exp_w_up, exp_w_down, sh_w_gate, sh_w_up, sh_w_down, ple_w, ple_gate_w):
    batch, seq, d = x.shape
    xs = [x.reshape(batch * seq, d)] * batch
    p_all = p.reshape(DEPTH * batch, seq, PLE_DIM)
    cos_t, sin_t = _rope_tables(seq)
    kv = None
    for i in range(DEPTH):
        if i < N_A_LAYERS:
            mixed = [_rglru_layer(xb, seq, b if i == 0 else 0, a_w_in[i], a_b_in[i], a_conv_w[i],
                                  a_conv_b[i], a_gate_a_w[i], a_gate_a_b[i], a_gate_i_w[i], a_gate_i_b[i],
                                  a_lambda[i], a_w_out[i], ln_g[i, 0], ln_b[i, 0])
                     for b, xb in enumerate(xs)]
        else:
            if i == N_A_LAYERS:
                kv = []
                for xb in xs:
                    k_blk, vt_blk, k_mean = _shared_kv(xb, seq, kv_ln_g, kv_ln_b, w_kv, cos_t, sin_t)
                    kv.append((k_blk, vt_blk, k_mean.reshape(seq // MOBA_BLOCK, d)))
            j = i - N_A_LAYERS
            mixed = _moba_layer([(xb,) + kvb for xb, kvb in zip(xs, kv)], seq, b_w_q[j], b_w_o[j],
                                ln_g[i, 0], ln_b[i, 0], cos_t, sin_t)
        xs = _moe_layer(mixed, i, router_w[i], router_b[i], exp_w_gate, exp_w_up, exp_w_down,
                        sh_w_gate[i], sh_w_up[i], sh_w_down[i], ln_g[i, 1], ln_b[i, 1],
                        p_all, i * batch, ple_w[i], ple_gate_w[i])
    return jnp.stack(xs, axis=0)
```

```python
import functools

import jax
import jax.numpy as jnp
from jax import lax
from jax.experimental import pallas as pl
from jax.experimental.pallas import tpu as pltpu
from jax.experimental.pallas import tpu_sc as plsc

F32 = jnp.float32
BF16 = jnp.bfloat16
I32 = jnp.int32
U32 = jnp.uint32
HIGH_HALF = 0xFFFF0000

SC_CORES = 2
SC_SUBCORES = 16
SC_WORKERS = SC_CORES * SC_SUBCORES
SC_INDEX_CHUNK = 128
SC_STAGE_BYTES = 256 * 1024

D_MODEL = 1024
DEPTH = 4
N_A_LAYERS = DEPTH // 2
D_RNN = D_MODEL
LRU_BLOCKS = 4
LRU_BLOCK_W = D_RNN // LRU_BLOCKS
CONV_W = 4
LRU_C = 8.0
N_HEADS = 8
HEAD_DIM = D_MODEL // N_HEADS
ROT_DIM = HEAD_DIM // 4
ROPE_THETA = 500000.0
MOBA_BLOCK = 256
MOBA_TOPK = 3
N_EXPERTS = 64
N_GROUPS = 8
GROUP_SIZE = N_EXPERTS // N_GROUPS
TOPK_GROUPS = 4
TOPK_EXPERTS = 8
D_EXPERT = 256
D_SHARED = 256
ROUTED_SCALE = 2.5
PLE_DIM = 256
DN_ALPHA = (2 * DEPTH) ** 0.25
LN_EPS = 1e-5

V7X_VMEM_LIMIT_BYTES = 56 * 1024 * 1024

MIXER_ROWS = 256
TOKEN_ROWS = 512
EXPERT_ROWS = 2304
NEG_INF = float("-inf")


def _params(*sem):
    return pltpu.CompilerParams(dimension_semantics=sem, vmem_limit_bytes=V7X_VMEM_LIMIT_BYTES)


def _layer_norm(z, g, b):
    mu = jnp.mean(z, axis=-1, keepdims=True)
    zc = z - mu
    var = jnp.mean(zc * zc, axis=-1, keepdims=True)
    return zc * lax.rsqrt(var + LN_EPS) * g + b


def _silu(x):
    return x * jax.nn.sigmoid(x)


def _gelu_tanh(x):
    return x * jax.nn.sigmoid(x * (1.5957691216057308 + 0.07135481627159855 * (x * x)))


def _full(shape):
    return pl.BlockSpec(shape, lambda *_: (0,) * len(shape))


def _count_le(ends, values):
    return jnp.sum((ends[None, :] <= values[:, None]).astype(I32), axis=1)


def _pack_bf16_pairs(x):
    w = x.shape[1] // 2
    bits = lax.bitcast_convert_type(x.astype(BF16).astype(F32), U32)
    return (bits[:, :w] >> 16) | (bits[:, w:] & U32(HIGH_HALF))


def _unpack_bf16_pairs(u):
    lo = lax.bitcast_convert_type(u << 16, F32)
    hi = lax.bitcast_convert_type(u & U32(HIGH_HALF), F32)
    return lo, hi


def _sc_mesh():
    return plsc.VectorSubcoreMesh(core_axis_name="c", subcore_axis_name="s")


def _sc_worker_id():
    return lax.axis_index("s") * SC_CORES + lax.axis_index("c")


def _sc_chunks_per_step(chunks_per_worker, row_words):
    g = max(1, SC_STAGE_BYTES // (SC_INDEX_CHUNK * row_words * 4))
    while chunks_per_worker % g:
        g -= 1
    return g


def _sc_gather_rows(table, idx):
    b = idx.shape[0]
    w = table.shape[1]
    chunk = SC_INDEX_CHUNK
    chunks_per_worker = b // (SC_WORKERS * chunk)
    assert chunks_per_worker * SC_WORKERS * chunk == b
    g = _sc_chunks_per_step(chunks_per_worker, w)

    @functools.partial(
        pl.kernel, mesh=_sc_mesh(), out_type=jax.ShapeDtypeStruct((b, w), table.dtype),
        scratch_types=[pltpu.VMEM((g, chunk), I32), pltpu.VMEM((g * chunk, w), table.dtype),
                       pltpu.SemaphoreType.DMA])
    def gather(table_hbm, idx_hbm, out_hbm, idx_v, rows_v, sem):
        first = _sc_worker_id() * chunks_per_worker

        @pl.loop(0, chunks_per_worker // g)
        def _(j):
            c0 = first + j * g
            pltpu.sync_copy(idx_hbm.at[pl.ds(c0, g)], idx_v)
            copies = [pltpu.async_copy(table_hbm.at[idx_v.at[q]], rows_v.at[pl.ds(q * chunk, chunk)], sem)
                      for q in range(g)]
            for cp in copies:
                cp.wait()
            pltpu.sync_copy(rows_v, out_hbm.at[pl.ds(c0 * chunk, g * chunk)])

    return gather(table, idx.reshape(b // chunk, chunk))


def _sc_scatter_rows(src, idx, out_rows):
    n, w = src.shape
    chunk = SC_INDEX_CHUNK
    fan = idx.shape[1]
    chunks_per_worker = n // (SC_WORKERS * chunk)
    assert idx.shape == (n // chunk, fan, chunk) and chunks_per_worker * SC_WORKERS * chunk == n
    g = _sc_chunks_per_step(chunks_per_worker, w)

    @functools.partial(
        pl.kernel, mesh=_sc_mesh(), out_type=jax.ShapeDtypeStruct((out_rows, w), src.dtype),
        scratch_types=[pltpu.VMEM((g, fan, chunk), I32), pltpu.VMEM((g * chunk, w), src.dtype),
                       pltpu.SemaphoreType.DMA])
    def scatter(src_hbm, idx_hbm, out_hbm, idx_v, rows_v, sem):
        first = _sc_worker_id() * chunks_per_worker

        @pl.loop(0, chunks_per_worker // g)
        def _(j):
            c0 = first + j * g
            pltpu.sync_copy(src_hbm.at[pl.ds(c0 * chunk, g * chunk)], rows_v)
            pltpu.sync_copy(idx_hbm.at[pl.ds(c0, g)], idx_v)
            copies = [pltpu.async_copy(rows_v.at[pl.ds(q * chunk, chunk)], out_hbm.at[idx_v.at[q, f]], sem)
                      for q in range(g) for f in range(fan)]
            for cp in copies:
                cp.wait()

    return scatter(src, idx)


def _rglru_kernel(x_ref, win_ref, bin_ref, cw_ref, cb_ref, gaw_ref, gab_ref, giw_ref, gib_ref,
                  lam_ref, wout_ref, lng_ref, lnb_ref, o_ref, op_ref, tail_ref, h_ref, *, tiles_per_seq):
    i = pl.program_id(0)
    tm = x_ref.shape[0]

    @pl.when(i % tiles_per_seq == 0)
    def _():
        tail_ref[...] = jnp.zeros_like(tail_ref)
        h_ref[...] = jnp.zeros_like(h_ref)

    x = x_ref[...]
    xy = jnp.dot(x.astype(BF16), win_ref[...], preferred_element_type=F32) + bin_ref[...]
    xb = xy[:, :D_RNN]
    y = _gelu_tanh(xy[:, D_RNN:])

    tail = tail_ref[...]
    row8 = lax.broadcasted_iota(I32, (8, 1), 0)
    xc = cb_ref[...] + xb * cw_ref[CONV_W - 1:CONV_W, :]
    for d in range(1, CONV_W):
        rolled = pltpu.roll(xb, d, 0)
        head = jnp.where(row8 < d, pltpu.roll(tail, d, 0), rolled[:8])
        shifted = jnp.concatenate([head, rolled[8:]], axis=0)
        xc = xc + shifted * cw_ref[CONV_W - 1 - d:CONV_W - d, :]
    tail_ref[...] = xb[tm - 8:, :]

    r_parts, i_parts = [], []
    for n in range(LRU_BLOCKS):
        xg = xc[:, n * LRU_BLOCK_W:(n + 1) * LRU_BLOCK_W].astype(BF16)
        r_parts.append(jnp.dot(xg, gaw_ref[n], preferred_element_type=F32))
        i_parts.append(jnp.dot(xg, giw_ref[n], preferred_element_type=F32))
    r = jax.nn.sigmoid(jnp.concatenate(r_parts, axis=1) + gab_ref[...])
    ig = jax.nn.sigmoid(jnp.concatenate(i_parts, axis=1) + gib_ref[...])

    lam = lam_ref[...]
    softplus_neg_lam = jnp.maximum(-lam, 0.0) + jnp.log1p(jnp.exp(-jnp.abs(lam)))
    log_a = (-LRU_C * r) * softplus_neg_lam
    a = jnp.exp(log_a)
    u = jnp.sqrt(1.0 - a * a) * (ig * xc)

    n_grp = tm // 8
    sub = lax.broadcasted_iota(I32, (1, 8, 1), 1)
    acc_a = a.reshape(n_grp, 8, D_RNN)
    acc_h = u.reshape(n_grp, 8, D_RNN)
    for d in (1, 2, 4):
        keep = sub >= d
        sh_a = pltpu.roll(acc_a, d, 1)
        sh_h = pltpu.roll(acc_h, d, 1)
        acc_h = jnp.where(keep, acc_a * sh_h + acc_h, acc_h)
        acc_a = jnp.where(keep, acc_a * sh_a, acc_a)
    state = h_ref[...]
    groups = []
    for g in range(n_grp):
        hg = acc_h[g] + acc_a[g] * state
        state = hg[7:8, :]
        groups.append(hg)
    h = jnp.concatenate(groups, axis=0)
    h_ref[...] = state

    mix = jnp.dot((h * y).astype(BF16), wout_ref[...], preferred_element_type=F32)
    x1 = _layer_norm(DN_ALPHA * x + mix, lng_ref[...], lnb_ref[...])
    o_ref[...] = x1
    op_ref[...] = _pack_bf16_pairs(x1)


def _rglru_layer(x, seq, stream, w_in, b_in, conv_w, conv_b, ga_w, ga_b, gi_w, gi_b, lam, w_out,
                 ln_g, ln_b):
    n, d = seq, x.shape[1]
    tm = min(MIXER_ROWS, seq)
    first_tile = stream * (seq // tm)
    row = lambda v: v.reshape(1, -1).astype(F32)
    return pl.pallas_call(
        functools.partial(_rglru_kernel, tiles_per_seq=seq // tm),
        grid=(n // tm,),
        in_specs=[pl.BlockSpec((tm, d), lambda i: (i + first_tile, 0)),
                  _full((d, 2 * D_RNN)), _full((1, 2 * D_RNN)),
                  _full((CONV_W, D_RNN)), _full((1, D_RNN)),
                  _full((LRU_BLOCKS, LRU_BLOCK_W, LRU_BLOCK_W)), _full((1, D_RNN)),
                  _full((LRU_BLOCKS, LRU_BLOCK_W, LRU_BLOCK_W)), _full((1, D_RNN)),
                  _full((1, D_RNN)), _full((D_RNN, d)), _full((1, d)), _full((1, d))],
        out_specs=[pl.BlockSpec((tm, d), lambda i: (i, 0)), pl.BlockSpec((tm, d // 2), lambda i: (i, 0))],
        out_shape=[jax.ShapeDtypeStruct((n, d), F32), jax.ShapeDtypeStruct((n, d // 2), U32)],
        scratch_shapes=[pltpu.VMEM((8, D_RNN), F32), pltpu.VMEM((1, D_RNN), F32)],
        compiler_params=_params("arbitrary"),
        name="rglru_mixer",
    )(x, w_in.astype(BF16), row(b_in), conv_w, row(conv_b), ga_w.astype(BF16), row(ga_b),
      gi_w.astype(BF16), row(gi_b), row(lam), w_out.astype(BF16), row(ln_g), row(ln_b))


def _first_argmax(cur, idx, size, axis):
    m = jnp.max(cur, axis=axis, keepdims=True)
    first = jnp.min(jnp.where(cur == m, idx, size), axis=axis, keepdims=True)
    return m, first


def _router_kernel(x_ref, rw_ref, rb_ref, e_ref, w_ref, r_ref, cnt_ref, carry_ref):
    i = pl.program_id(0)
    tm = x_ref.shape[0]

    @pl.when(i == 0)
    def _():
        carry_ref[...] = jnp.zeros_like(carry_ref)

    x = x_ref[...]
    x_hi = x.astype(BF16)
    x_lo = (x - x_hi.astype(F32)).astype(BF16)
    rw = rw_ref[...]
    hi_both = jnp.dot(x_hi, rw, preferred_element_type=F32)
    logits = (hi_both[:, :128] + hi_both[:, 128:]
              + jnp.dot(x_lo, rw[:, :128], preferred_element_type=F32))
    scores = jax.nn.sigmoid(logits.T[:N_EXPERTS, :])
    choice = scores + rb_ref[...]

    c3 = choice.reshape(N_GROUPS, GROUP_SIZE, tm)
    in_grp = lax.broadcasted_iota(I32, c3.shape, 1)
    m1, i1 = _first_argmax(c3, in_grp, GROUP_SIZE, 1)
    m2 = jnp.max(jnp.where(in_grp == i1, NEG_INF, c3), axis=1, keepdims=True)
    grp_score = (m1 + m2)[:, 0, :]

    grp_id = lax.broadcasted_iota(I32, grp_score.shape, 0)
    grp_sel = jnp.zeros(grp_score.shape, jnp.bool_)
    cur = grp_score
    for _ in range(TOPK_GROUPS):
        _, gi = _first_argmax(cur, grp_id, N_GROUPS, 0)
        hit = grp_id == gi
        grp_sel = jnp.logical_or(grp_sel, hit)
        cur = jnp.where(hit, NEG_INF, cur)

    cur = jnp.where(grp_sel[:, None, :], c3, NEG_INF).reshape(N_EXPERTS, tm)
    exp_id = lax.broadcasted_iota(I32, cur.shape, 0)
    sel = jnp.zeros(cur.shape, F32)
    e_rows, s_rows = [], []
    for _ in range(TOPK_EXPERTS):
        _, ei = _first_argmax(cur, exp_id, N_EXPERTS, 0)
        hit = exp_id == ei
        e_rows.append(ei)
        s_rows.append(jnp.sum(jnp.where(hit, scores, 0.0), axis=0, keepdims=True))
        sel = jnp.where(hit, 1.0, sel)
        cur = jnp.where(hit, NEG_INF, cur)
    e_top = jnp.concatenate(e_rows, axis=0)
    s_top = jnp.concatenate(s_rows, axis=0)
    w_ref[...] = s_top / jnp.sum(s_top, axis=0, keepdims=True) * ROUTED_SCALE
    e_ref[...] = e_top

    t_row = lax.broadcasted_iota(I32, (tm, tm), 0)
    t_col = lax.broadcasted_iota(I32, (tm, tm), 1)
    before = (t_row < t_col).astype(BF16)
    cum = jnp.dot(sel.astype(BF16), before, preferred_element_type=F32) + carry_ref[...]
    r_rows = [jnp.sum(jnp.where(exp_id == e_rows[k], cum, 0.0), axis=0, keepdims=True)
              for k in range(TOPK_EXPERTS)]
    r_ref[...] = jnp.concatenate(r_rows, axis=0).astype(I32)
    carry_ref[...] = carry_ref[...] + jnp.sum(sel, axis=1, keepdims=True)
    cnt_ref[...] = jnp.broadcast_to(carry_ref[...], cnt_ref.shape).astype(I32)


def _router(x, router_w, router_b):
    n, d = x.shape
    tm = min(TOKEN_ROWS, n)
    rw = jnp.pad(router_w, ((0, 0), (0, 128 - N_EXPERTS)))
    rw_hi = rw.astype(BF16)
    rw = jnp.concatenate([rw_hi, (rw - rw_hi.astype(F32)).astype(BF16)], axis=1)
    k = TOPK_EXPERTS
    tok_spec = pl.BlockSpec((k, tm), lambda i: (0, i))
    return pl.pallas_call(
        _router_kernel,
        grid=(n // tm,),
        in_specs=[pl.BlockSpec((tm, d), lambda i: (i, 0)), _full((d, 256)), _full((N_EXPERTS, 1))],
        out_specs=[tok_spec, tok_spec, tok_spec, _full((N_EXPERTS, 128))],
        out_shape=[jax.ShapeDtypeStruct((k, n), I32), jax.ShapeDtypeStruct((k, n), F32),
                   jax.ShapeDtypeStruct((k, n), I32), jax.ShapeDtypeStruct((N_EXPERTS, 128), I32)],
        scratch_shapes=[pltpu.VMEM((N_EXPERTS, 1), F32)],
        compiler_params=_params("arbitrary"),
        name="moe_router",
    )(x, rw, router_b.reshape(N_EXPERTS, 1).astype(F32))


def _expert_kernel(blk_e_ref, nblk_ref, first_ref, slot_ref, next_ref, xs_ref, wg_hbm, wu_hbm, wd_hbm,
                   after_ref, ys_ref, wg_f, wu_f, wd_f, wg_s, wu_s, wd_s, sem, *, layer):
    del after_ref
    i = pl.program_id(0)

    def fetch(expert, slot):
        return (pltpu.make_async_copy(wg_hbm.at[layer, expert], wg_f.at[slot], sem.at[slot, 0]),
                pltpu.make_async_copy(wu_hbm.at[layer, expert], wu_f.at[slot], sem.at[slot, 1]),
                pltpu.make_async_copy(wd_hbm.at[layer, expert], wd_f.at[slot], sem.at[slot, 2]))

    @pl.when(i < nblk_ref[0])
    def _():
        slot = slot_ref[i]

        @pl.when(i == 0)
        def _():
            for copy in fetch(blk_e_ref[0], 0):
                copy.start()

        @pl.when(first_ref[i] == 1)
        def _():
            for copy in fetch(blk_e_ref[i], slot):
                copy.wait()
            wg_s[...] = wg_f[slot].astype(BF16)
            wu_s[...] = wu_f[slot].astype(BF16)
            wd_s[...] = wd_f[slot].astype(BF16)

            @pl.when(next_ref[i] >= 0)
            def _():
                for copy in fetch(next_ref[i], 1 - slot):
                    copy.start()

        lo, hi = _unpack_bf16_pairs(xs_ref[...])
        xs = jnp.concatenate([lo.astype(BF16), hi.astype(BF16)], axis=1)
        g = jnp.dot(xs, wg_s[...], preferred_element_type=F32)
        u = jnp.dot(xs, wu_s[...], preferred_element_type=F32)
        hdn = (_silu(g) * u).astype(BF16)
        ys_ref[...] = _pack_bf16_pairs(jnp.dot(hdn, wd_s[...], preferred_element_type=F32))


_ORDER_ONLY = pl.BlockSpec(memory_space=pl.ANY)


def _experts(xs, blk_e, nblk, layer, w_gate, w_up, w_down, after):
    rows, half = xs.shape
    d = 2 * half
    m = EXPERT_ROWS
    n_blocks = rows // m
    idx = jnp.arange(n_blocks, dtype=I32)
    used = idx < nblk[0]
    first = jnp.logical_and(used, jnp.logical_or(idx == 0, blk_e != jnp.roll(blk_e, 1)))
    slot = (jnp.cumsum(first.astype(I32)) - 1) % 2
    larger = jnp.logical_and(blk_e[None, :] > blk_e[:, None], used[None, :])
    next_e = jnp.min(jnp.where(larger, blk_e[None, :], N_EXPERTS), axis=1)
    next_e = jnp.where(next_e < N_EXPERTS, next_e, -1)
    row_map = lambda i, be, nb, fi, sl, nx: (jnp.minimum(i, nb[0] - 1), 0)
    hbm = pl.BlockSpec(memory_space=pl.ANY)
    return pl.pallas_call(
        functools.partial(_expert_kernel, layer=layer),
        grid_spec=pltpu.PrefetchScalarGridSpec(
            num_scalar_prefetch=5,
            grid=(n_blocks,),
            in_specs=[pl.BlockSpec((m, half), row_map), hbm, hbm, hbm, _ORDER_ONLY],
            out_specs=pl.BlockSpec((m, half), row_map),
            scratch_shapes=[pltpu.VMEM((2, d, D_EXPERT), F32), pltpu.VMEM((2, d, D_EXPERT), F32),
                            pltpu.VMEM((2, D_EXPERT, d), F32),
                            pltpu.VMEM((d, D_EXPERT), BF16), pltpu.VMEM((d, D_EXPERT), BF16),
                            pltpu.VMEM((D_EXPERT, d), BF16), pltpu.SemaphoreType.DMA((2, 3))]),
        out_shape=jax.ShapeDtypeStruct((rows, half), U32),
        compiler_params=_params("arbitrary"),
        name="moe_experts",
    )(blk_e, nblk, first.astype(I32), slot.astype(I32), next_e.astype(I32), xs, w_gate, w_up, w_down, after)


def _combine_kernel(x_ref, yg_ref, w_ref, sg_ref, su_ref, sd_ref, lng_ref, lnb_ref,
                    p_ref, pw_ref, pg_ref, after_ref, o_ref):
    del after_ref
    x = x_ref[...]
    wt = w_ref[...].T
    moe_lo, moe_hi = None, None
    for k in range(TOPK_EXPERTS):
        lo, hi = _unpack_bf16_pairs(yg_ref[k])
        wk = wt[:, k:k + 1]
        moe_lo = wk * lo if k == 0 else moe_lo + wk * lo
        moe_hi = wk * hi if k == 0 else moe_hi + wk * hi
    moe = jnp.concatenate([moe_lo, moe_hi], axis=1)
    xb = x.astype(BF16)
    hdn = _silu(jnp.dot(xb, sg_ref[...], preferred_element_type=F32)) * \
        jnp.dot(xb, su_ref[...], preferred_element_type=F32)
    shared = jnp.dot(hdn.astype(BF16), sd_ref[...], preferred_element_type=F32)
    x2 = _layer_norm(DN_ALPHA * x + (moe + shared), lng_ref[...], lnb_ref[...])
    emb = jnp.dot(p_ref[0].astype(BF16), pw_ref[...], preferred_element_type=F32)
    gate = jax.nn.sigmoid(jnp.dot(x2.astype(BF16), pg_ref[...], preferred_element_type=F32))
    o_ref[...] = x2 + emb * gate


def _combine(x, yg, wgt, s_gate, s_up, s_down, ln_g, ln_b, p_all, p_index, ple_w, ple_gate_w, after):
    n, d = x.shape
    tm = min(TOKEN_ROWS // 2, n)
    k = TOPK_EXPERTS
    row = lambda v: v.reshape(1, -1).astype(F32)
    return pl.pallas_call(
        _combine_kernel,
        grid=(n // tm,),
        in_specs=[pl.BlockSpec((tm, d), lambda i: (i, 0)),
                  pl.BlockSpec((k, tm, d // 2), lambda i: (0, i, 0)),
                  pl.BlockSpec((k, tm), lambda i: (0, i)),
                  _full((d, D_SHARED)), _full((d, D_SHARED)), _full((D_SHARED, d)),
                  _full((1, d)), _full((1, d)),
                  pl.BlockSpec((1, tm, PLE_DIM), lambda i: (p_index, i, 0)),
                  _full((PLE_DIM, d)), _full((d, d)), _ORDER_ONLY],
        out_specs=pl.BlockSpec((tm, d), lambda i: (i, 0)),
        out_shape=jax.ShapeDtypeStruct((n, d), F32),
        compiler_params=_params("parallel"),
        name="moe_combine",
    )(x, yg, wgt, s_gate.astype(BF16), s_up.astype(BF16), s_down.astype(BF16), row(ln_g), row(ln_b),
      p_all, ple_w.astype(BF16), ple_gate_w.astype(BF16), after)


def _positions_kernel(e_ref, r_ref, start_ref, pos_ref):
    e = e_ref[...]
    start = start_ref[...]
    grp_id = lax.broadcasted_iota(I32, (start.shape[0], e.shape[1]), 0)
    rows = [jnp.sum(jnp.where(grp_id == e[k:k + 1], start, 0), axis=0, keepdims=True)
            for k in range(e.shape[0])]
    pos_ref[...] = jnp.concatenate(rows, axis=0) + r_ref[...]


def _positions(e_idx, rank, starts):
    k, n = e_idx.shape
    tm = min(4 * TOKEN_ROWS, n)
    groups = starts.shape[0]
    tok_spec = pl.BlockSpec((k, tm), lambda i: (0, i))
    return pl.pallas_call(
        _positions_kernel,
        grid=(n // tm,),
        in_specs=[tok_spec, tok_spec, _full((groups, 1))],
        out_specs=tok_spec,
        out_shape=jax.ShapeDtypeStruct((k, n), I32),
        compiler_params=_params("parallel"),
        name="group_positions",
    )(e_idx, rank, starts.reshape(groups, 1))


def _moe_layer(streams, layer, router_w, router_b, w_gate, w_up, w_down, s_gate, s_up, s_down,
               ln_g, ln_b, p_all, p_index, ple_w, ple_gate_w):
    m = EXPERT_ROWS
    k = TOPK_EXPERTS
    chunk = SC_INDEX_CHUNK
    routed = []
    for x, xp in streams:
        n, d = x.shape
        e_idx, wgt, rank, cnt = _router(x, router_w, router_b)
        counts = cnt[:, 0]
        padded = (counts + m - 1) // m * m
        pends = jnp.cumsum(padded)
        n_blocks = (n * k) // m + N_EXPERTS
        blk_first = jnp.arange(n_blocks, dtype=I32) * m
        blk_e = jnp.minimum(_count_le(pends, blk_first), N_EXPERTS - 1)
        nblk = (pends[-1] // m).astype(I32).reshape(1)
        pos = _positions(e_idx, rank, (pends - padded).astype(I32))
        pos_chunks = pos.reshape(k, n // chunk, chunk).transpose(1, 0, 2)
        xs = _sc_scatter_rows(xp, pos_chunks, n_blocks * m)
        routed.append((xs, blk_e, nblk, pos, wgt))
    gathered, order = [], routed[-1][3]
    for (x, _), (xs, blk_e, nblk, pos, wgt) in zip(streams, routed):
        n, d = x.shape
        ys = _experts(xs, blk_e, nblk, layer, w_gate, w_up, w_down, order)
        gathered.append(_sc_gather_rows(ys, pos.reshape(-1)).reshape(k, n, d // 2))
        order = ys
    outs = []
    for s, ((x, _), (_, _, _, _, wgt), yg) in enumerate(zip(streams, routed, gathered)):
        outs.append(_combine(x, yg, wgt, s_gate, s_up, s_down, ln_g, ln_b, p_all, p_index + s,
                             ple_w, ple_gate_w, order))
        order = outs[-1]
    return outs


def _rope(t, cos_t, sin_t):
    half = ROT_DIM // 2
    width = t.shape[1]
    lane = lax.broadcasted_iota(I32, (1, width), 1) % HEAD_DIM
    partner = jnp.where(lane < half, pltpu.roll(t, width - half, 1), pltpu.roll(t, half, 1))
    cos_f = jnp.concatenate([cos_t] * N_HEADS, axis=1)
    sin_f = jnp.concatenate([sin_t] * N_HEADS, axis=1)
    return t * cos_f + partner * sin_f


def _kv_kernel(x_ref, g_ref, b_ref, wkv_ref, cos_ref, sin_ref, k_ref, vt_ref, km_ref):
    h = _layer_norm(x_ref[...], g_ref[...], b_ref[...])
    kv = jnp.dot(h.astype(BF16), wkv_ref[...], preferred_element_type=F32)
    k = _rope(kv[:, :D_MODEL], cos_ref[...], sin_ref[...])
    vt = kv[:, D_MODEL:].T
    km_ref[0] = jnp.mean(k, axis=0, keepdims=True)
    for hd in range(N_HEADS):
        k_ref[hd, 0] = k[:, hd * HEAD_DIM:(hd + 1) * HEAD_DIM].astype(BF16)
        vt_ref[hd, 0] = vt[hd * HEAD_DIM:(hd + 1) * HEAD_DIM, :].astype(BF16)


def _shared_kv(x, seq, ln_g, ln_b, w_kv, cos_t, sin_t):
    n, d = x.shape
    blk = MOBA_BLOCK
    nbt = n // blk
    spb = seq // blk
    row = lambda v: v.reshape(1, -1).astype(F32)
    return pl.pallas_call(
        _kv_kernel,
        grid=(nbt,),
        in_specs=[pl.BlockSpec((blk, d), lambda i: (i, 0)), _full((1, d)), _full((1, d)),
                  _full((d, 2 * d)),
                  pl.BlockSpec((blk, HEAD_DIM), lambda i: (i % spb, 0)),
                  pl.BlockSpec((blk, HEAD_DIM), lambda i: (i % spb, 0))],
        out_specs=[pl.BlockSpec((N_HEADS, 1, blk, HEAD_DIM), lambda i: (0, i, 0, 0)),
                   pl.BlockSpec((N_HEADS, 1, HEAD_DIM, blk), lambda i: (0, i, 0, 0)),
                   pl.BlockSpec((1, 1, d), lambda i: (i, 0, 0))],
        out_shape=[jax.ShapeDtypeStruct((N_HEADS, nbt, blk, HEAD_DIM), BF16),
                   jax.ShapeDtypeStruct((N_HEADS, nbt, HEAD_DIM, blk), BF16),
                   jax.ShapeDtypeStruct((nbt, 1, d), F32)],
        compiler_params=_params("parallel"),
        name="shared_kv",
    )(x, row(ln_g), row(ln_b), w_kv.astype(BF16), cos_t, sin_t)


SEL_ROWS = 8


def _query_kernel(x_ref, wq_ref, cos_ref, sin_ref, km_ref, qp_ref, sel_ref, cnt_ref, carry_ref,
                  *, blocks_per_seq):
    i = pl.program_id(0)
    tm = x_ref.shape[0]
    nb = km_ref.shape[0]
    own = i % blocks_per_seq

    @pl.when(own == 0)
    def _():
        carry_ref[...] = jnp.zeros_like(carry_ref)

    q = jnp.dot(x_ref[...].astype(BF16), wq_ref[...], preferred_element_type=F32)
    q = _rope(q, cos_ref[...], sin_ref[...]) * (HEAD_DIM ** -0.5)

    t_row = lax.broadcasted_iota(I32, (tm, tm), 0)
    t_col = lax.broadcasted_iota(I32, (tm, tm), 1)
    before = (t_row < t_col).astype(BF16)
    km = km_ref[...]
    km_hi = km.astype(BF16)
    km_both = jnp.concatenate([km_hi, (km - km_hi.astype(F32)).astype(BF16)], axis=0)
    q_hi = q.astype(BF16)
    q_lo = (q - q_hi.astype(F32)).astype(BF16)
    nt = (((1,), (1,)), ((), ()))
    gates = []
    for hd in range(N_HEADS):
        lo, hi = hd * HEAD_DIM, (hd + 1) * HEAD_DIM
        qp_ref[0, hd] = q[:, lo:hi]
        both = lax.dot_general(km_both[:, lo:hi], q_hi[:, lo:hi], nt, preferred_element_type=F32)
        gates.append(both[:nb] + both[nb:]
                     + lax.dot_general(km_hi[:, lo:hi], q_lo[:, lo:hi], nt, preferred_element_type=F32))

    gate = jnp.concatenate(gates, axis=0).reshape(N_HEADS, nb, tm)
    blk_id = lax.broadcasted_iota(I32, (N_HEADS, nb, tm), 1)
    cur = jnp.where(blk_id < own, gate, NEG_INF)
    sel_rows = []
    chosen = jnp.zeros((N_HEADS, nb, tm), F32)
    for _ in range(MOBA_TOPK):
        m, bi = _first_argmax(cur, blk_id, nb, 1)
        valid = m > NEG_INF
        hit = jnp.logical_and(blk_id == bi, valid)
        sel_rows.append(jnp.where(valid, bi, -1))
        chosen = jnp.where(hit, 1.0, chosen)
        cur = jnp.where(blk_id == bi, NEG_INF, cur)
    chosen2 = chosen.reshape(N_HEADS * nb, tm)
    carry = carry_ref[...]
    cum = (jnp.dot(chosen2.astype(BF16), before, preferred_element_type=F32) + carry).reshape(N_HEADS, nb, tm)
    rank_rows = [jnp.sum(jnp.where(blk_id == s, cum, 0.0), axis=1, keepdims=True).astype(I32)
                 for s in sel_rows]
    carry = carry + jnp.sum(chosen2, axis=1, keepdims=True)
    carry_ref[...] = carry
    pad = jnp.zeros((N_HEADS, SEL_ROWS - 2 * MOBA_TOPK, tm), I32)
    table = jnp.concatenate(sel_rows + rank_rows + [pad], axis=1)
    sel_ref[...] = table.reshape(N_HEADS * SEL_ROWS, tm)
    cnt_ref[0] = jnp.broadcast_to(carry, cnt_ref.shape[1:]).astype(I32)


def _queries(x, seq, w_q, cos_t, sin_t, k_mean):
    n, d = x.shape
    blk = MOBA_BLOCK
    nb = seq // blk
    return pl.pallas_call(
        functools.partial(_query_kernel, blocks_per_seq=nb),
        grid=(n // blk,),
        in_specs=[pl.BlockSpec((blk, d), lambda i: (i, 0)), _full((d, d)),
                  pl.BlockSpec((blk, HEAD_DIM), lambda i: (i % nb, 0)),
                  pl.BlockSpec((blk, HEAD_DIM), lambda i: (i % nb, 0)),
                  pl.BlockSpec((nb, d), lambda i: (i // nb, 0))],
        out_specs=[pl.BlockSpec((1, N_HEADS, blk, HEAD_DIM), lambda i: (i, 0, 0, 0)),
                   pl.BlockSpec((N_HEADS * SEL_ROWS, blk), lambda i: (0, i)),
                   pl.BlockSpec((1, N_HEADS * nb, 128), lambda i: (i // nb, 0, 0))],
        out_shape=[jax.ShapeDtypeStruct((n // blk, N_HEADS, blk, HEAD_DIM), F32),
                   jax.ShapeDtypeStruct((N_HEADS * SEL_ROWS, n), I32),
                   jax.ShapeDtypeStruct((n // seq, N_HEADS * nb, 128), I32)],
        scratch_shapes=[pltpu.VMEM((N_HEADS * nb, 1), F32)],
        compiler_params=_params("arbitrary"),
        name="moba_queries",
    )(x, w_q.astype(BF16), cos_t, sin_t, k_mean)


def _moba_positions_kernel(sel_ref, start_ref, pos_ref, *, dump_row):
    tm = sel_ref.shape[1]
    nb = start_ref.shape[1] // N_HEADS
    blk_id = lax.broadcasted_iota(I32, (nb, tm), 0)
    dump = dump_row + lax.broadcasted_iota(I32, (1, tm), 1) % SC_INDEX_CHUNK
    rows = []
    for hd in range(N_HEADS):
        start = start_ref[0, hd * nb:(hd + 1) * nb, :]
        for s in range(MOBA_TOPK):
            sel = sel_ref[hd * SEL_ROWS + s:hd * SEL_ROWS + s + 1, :]
            rank = sel_ref[hd * SEL_ROWS + MOBA_TOPK + s:hd * SEL_ROWS + MOBA_TOPK + s + 1, :]
            base = jnp.sum(jnp.where(blk_id == sel, start, 0), axis=0, keepdims=True)
            rows.append(jnp.where(sel >= 0, base + rank, dump))
        rows.extend([dump] * (SEL_ROWS - MOBA_TOPK))
    pos_ref[...] = jnp.concatenate(rows, axis=0)


def _moba_positions(sel, starts, seq, dump_row):
    rows, n = sel.shape
    tm = min(4 * TOKEN_ROWS, seq)
    tps = seq // tm
    groups = starts.shape[1]
    return pl.pallas_call(
        functools.partial(_moba_positions_kernel, dump_row=dump_row),
        grid=(n // tm,),
        in_specs=[pl.BlockSpec((rows, tm), lambda i: (0, i)),
                  pl.BlockSpec((1, groups, 1), lambda i: (i // tps, 0, 0))],
        out_specs=pl.BlockSpec((rows, tm), lambda i: (0, i)),
        out_shape=jax.ShapeDtypeStruct((rows, n), I32),
        compiler_params=_params("parallel"),
        name="moba_positions",
    )(sel, starts)


ATTN_SUB_ROWS = 128
ATTN_SUBS_PER_STEP = 32
ATTN_STEP_ROWS = ATTN_SUB_ROWS * ATTN_SUBS_PER_STEP


def _pack_bf16_row_pairs(x):
    h = x.shape[0] // 2
    bits = lax.bitcast_convert_type(x.astype(BF16).astype(F32), U32)
    return (bits[:h] >> 16) | (bits[h:] & U32(HIGH_HALF))


def _transpose_u32(x):
    return lax.bitcast_convert_type(lax.bitcast_convert_type(x, I32).T, U32)


def _scores_t(k, q, keep):
    st = lax.dot_general(k, q, (((1,), (1,)), ((), ())), preferred_element_type=F32)
    if keep is not None:
        st = jnp.where(keep, st, NEG_INF)
    m = jnp.max(st, axis=0, keepdims=True)
    p = jnp.exp(st - m)
    return m, p, jnp.sum(p, axis=0, keepdims=True)


def _group_attn_kernel(step_h_ref, step_b_ref, sub_j_ref, nsteps_ref, qs_ref, k_ref, vt_ref, after_ref,
                       part_ref):
    del step_h_ref, step_b_ref, after_ref
    i = pl.program_id(0)
    half = HEAD_DIM // 2

    @pl.when(i < nsteps_ref[0])
    def _():
        for u in range(ATTN_SUBS_PER_STEP):
            j = sub_j_ref[i * ATTN_SUBS_PER_STEP + u]
            rows = pl.ds(u * ATTN_SUB_ROWS, ATTN_SUB_ROWS)
            q = qs_ref[rows, :].astype(BF16)
            m, p, l = _scores_t(k_ref[0, j], q, None)
            ot = jnp.dot(vt_ref[0, j], p.astype(BF16), preferred_element_type=F32) * (1.0 / l)
            lse = lax.bitcast_convert_type(m + jnp.log(l), U32)
            packed_t = jnp.concatenate(
                [_pack_bf16_row_pairs(ot), jnp.broadcast_to(lse, (half, ATTN_SUB_ROWS))], axis=0)
            part_ref[rows, :] = _transpose_u32(packed_t)


def _group_attention(qs, k_blk, vt_blk, step_h, step_b, sub_j, nsteps, nb, after):
    n_steps = step_h.shape[0]
    row_map = lambda i, sh, sb, sj, ns: (jnp.minimum(i, ns[0] - 1), 0)
    kv_map = lambda i, sh, sb, sj, ns: (sh[i], sb[i], 0, 0)
    return pl.pallas_call(
        _group_attn_kernel,
        grid_spec=pltpu.PrefetchScalarGridSpec(
            num_scalar_prefetch=4,
            grid=(n_steps,),
            in_specs=[pl.BlockSpec((ATTN_STEP_ROWS, HEAD_DIM), row_map),
                      pl.BlockSpec((1, nb, MOBA_BLOCK, HEAD_DIM), kv_map),
                      pl.BlockSpec((1, nb, HEAD_DIM, MOBA_BLOCK), kv_map), _ORDER_ONLY],
            out_specs=pl.BlockSpec((ATTN_STEP_ROWS, HEAD_DIM), row_map)),
        out_shape=jax.ShapeDtypeStruct(((n_steps + 1) * ATTN_STEP_ROWS, HEAD_DIM), U32),
        compiler_params=_params("arbitrary"),
        name="moba_group_attention",
    )(step_h, step_b, sub_j, nsteps, qs, k_blk, vt_blk, after)


def _attn_merge_kernel(x_ref, qp_ref, k_ref, vt_ref, pg_ref, sel_ref, wo_ref, lng_ref, lnb_ref,
                       after_ref, o_ref, op_ref):
    del after_ref
    tm = x_ref.shape[0]
    half = HEAD_DIM // 2
    sel = sel_ref[...]
    key = lax.broadcasted_iota(I32, (tm, tm), 0)
    qry = lax.broadcasted_iota(I32, (tm, tm), 1)
    causal = key <= qry
    heads = []
    for hd in range(N_HEADS):
        q = qp_ref[0, hd].astype(BF16)
        m_own, p, l_own = _scores_t(k_ref[hd, 0], q, causal)
        acc = jnp.dot(vt_ref[hd, 0], p.astype(BF16), preferred_element_type=F32)

        outs, lses = [], []
        m_tot = m_own
        for c in range(MOBA_TOPK):
            part_t = _transpose_u32(pg_ref[c, 0, hd])
            lo, hi = _unpack_bf16_pairs(part_t[:half])
            valid = sel[hd * SEL_ROWS + c:hd * SEL_ROWS + c + 1, :] >= 0
            outs.append(jnp.where(valid, jnp.concatenate([lo, hi], axis=0), 0.0))
            lse = jnp.where(valid, lax.bitcast_convert_type(part_t[half:half + 1], F32), NEG_INF)
            lses.append(lse)
            m_tot = jnp.maximum(m_tot, lse)
        w_own = jnp.exp(m_own - m_tot)
        num = acc * w_own
        den = l_own * w_own
        for c in range(MOBA_TOPK):
            w_c = jnp.exp(lses[c] - m_tot)
            num = num + outs[c] * w_c
            den = den + w_c
        heads.append(num * (1.0 / den))
    att = jnp.concatenate(heads, axis=0).T.astype(BF16)
    mix = jnp.dot(att, wo_ref[...], preferred_element_type=F32)
    x1 = _layer_norm(DN_ALPHA * x_ref[...] + mix, lng_ref[...], lnb_ref[...])
    o_ref[...] = x1
    op_ref[...] = _pack_bf16_pairs(x1)


def _attn_merge(x, qp, k_blk, vt_blk, pg, sel, w_o, ln_g, ln_b, after):
    n, d = x.shape
    blk = MOBA_BLOCK
    row = lambda v: v.reshape(1, -1).astype(F32)
    return pl.pallas_call(
        _attn_merge_kernel,
        grid=(n // blk,),
        in_specs=[pl.BlockSpec((blk, d), lambda i: (i, 0)),
                  pl.BlockSpec((1, N_HEADS, blk, HEAD_DIM), lambda i: (i, 0, 0, 0)),
                  pl.BlockSpec((N_HEADS, 1, blk, HEAD_DIM), lambda i: (0, i, 0, 0)),
                  pl.BlockSpec((N_HEADS, 1, HEAD_DIM, blk), lambda i: (0, i, 0, 0)),
                  pl.BlockSpec((MOBA_TOPK, 1, N_HEADS, blk, HEAD_DIM), lambda i: (0, i, 0, 0, 0)),
                  pl.BlockSpec((N_HEADS * SEL_ROWS, blk), lambda i: (0, i)),
                  _full((d, d)), _full((1, d)), _full((1, d)), _ORDER_ONLY],
        out_specs=[pl.BlockSpec((blk, d), lambda i: (i, 0)), pl.BlockSpec((blk, d // 2), lambda i: (i, 0))],
        out_shape=[jax.ShapeDtypeStruct((n, d), F32), jax.ShapeDtypeStruct((n, d // 2), U32)],
        compiler_params=_params("parallel"),
        name="moba_merge",
    )(x, qp, k_blk, vt_blk, pg, sel, w_o.astype(BF16), row(ln_g), row(ln_b), after)


def _moba_layer(streams, seq, w_q, w_o, ln_g, ln_b, cos_t, sin_t):
    staged = [_moba_regroup(x, seq, w_q, cos_t, sin_t, k_mean) for x, _, _, k_mean in streams]
    gathered, order = [], staged[-1][1]
    for (x, k_blk, vt_blk, _), (qp, sel, qs, tables, gather_idx) in zip(streams, staged):
        nbt = x.shape[0] // MOBA_BLOCK
        part = _group_attention(qs, k_blk, vt_blk, *tables, seq // MOBA_BLOCK, order)
        gathered.append(_sc_gather_rows(part, gather_idx).reshape(MOBA_TOPK, nbt, N_HEADS, MOBA_BLOCK, HEAD_DIM))
        order = part
    outs = []
    for (x, k_blk, vt_blk, _), (qp, sel, _, _, _), pg in zip(streams, staged, gathered):
        outs.append(_attn_merge(x, qp, k_blk, vt_blk, pg, sel, w_o, ln_g, ln_b, order))
        order = outs[-1][0]
    return outs


def _moba_regroup(x, seq, w_q, cos_t, sin_t, k_mean):
    n, d = x.shape
    batch = n // seq
    blk = MOBA_BLOCK
    nb = seq // blk
    nbt = n // blk
    chunk = SC_INDEX_CHUNK
    qp, sel, cnt = _queries(x, seq, w_q, cos_t, sin_t, k_mean)

    counts = cnt[:, :, 0].reshape(batch * N_HEADS, nb)
    gpad = (counts + ATTN_SUB_ROWS - 1) // ATTN_SUB_ROWS * ATTN_SUB_ROWS
    seg = jnp.sum(gpad, axis=1)
    seg_pad = (seg + ATTN_STEP_ROWS - 1) // ATTN_STEP_ROWS * ATTN_STEP_ROWS
    seg_end = jnp.cumsum(seg_pad)
    gend = (seg_end - seg_pad)[:, None] + jnp.cumsum(gpad, axis=1)
    gstart = (gend - gpad).astype(I32)
    steps_per_seg = -(-(MOBA_TOPK * seq + nb * (ATTN_SUB_ROWS - 1)) // ATTN_STEP_ROWS)
    n_steps = batch * N_HEADS * steps_per_seg
    step_first = jnp.arange(n_steps, dtype=I32) * ATTN_STEP_ROWS
    step_seg = jnp.minimum(_count_le(seg_end, step_first), batch * N_HEADS - 1)
    sub_first = jnp.arange(n_steps * ATTN_SUBS_PER_STEP, dtype=I32) * ATTN_SUB_ROWS
    sub_grp = jnp.minimum(_count_le(gend.reshape(-1), sub_first), batch * N_HEADS * nb - 1)
    nsteps = (seg_end[-1] // ATTN_STEP_ROWS).astype(I32).reshape(1)
    dump_row = n_steps * ATTN_STEP_ROWS

    pos = _moba_positions(sel, gstart.reshape(batch, N_HEADS * nb, 1), seq, dump_row)
    pos5 = pos.reshape(N_HEADS, SEL_ROWS, nbt, blk // chunk, chunk)[:, :MOBA_TOPK]
    scatter_idx = pos5.transpose(2, 0, 3, 1, 4).reshape(nbt * N_HEADS * (blk // chunk), MOBA_TOPK, chunk)
    gather_idx = pos5.transpose(1, 2, 0, 3, 4).reshape(-1)

    qs = _sc_scatter_rows(qp.reshape(n * N_HEADS, HEAD_DIM), scatter_idx, dump_row + chunk)
    tables = ((step_seg % N_HEADS).astype(I32), (step_seg // N_HEADS).astype(I32),
              (sub_grp % nb).astype(I32), nsteps)
    return qp, sel, qs, tables, gather_idx


def _rope_tables(seq):
    half = ROT_DIM // 2
    inv = ROPE_THETA ** (-jnp.arange(0, ROT_DIM, 2, dtype=F32) / ROT_DIM)
    ang = jnp.arange(seq, dtype=F32)[:, None] * inv[None, :]
    cos, sin = jnp.cos(ang), jnp.sin(ang)
    rest = HEAD_DIM - ROT_DIM
    cos_t = jnp.concatenate([cos, cos, jnp.ones((seq, rest), F32)], axis=1)
    sin_t = jnp.concatenate([-sin, sin, jnp.zeros((seq, rest), F32)], axis=1)
    del half
    return cos_t, sin_t


def kernel(x, p, ln_g, ln_b, a_w_in, a_b_in, a_conv_w, a_conv_b, a_gate_a_w, a_gate_a_b, a_gate_i_w,
           a_gate_i_b, a_lambda, a_w_out, kv_ln_g, kv_ln_b, w_kv, b_w_q, b_w_o, router_w, router_b,
           exp_w_gate, exp_w_up, exp_w_down, sh_w_gate, sh_w_up, sh_w_down, ple_w, ple_gate_w):
    batch, seq, d = x.shape
    xs = [x.reshape(batch * seq, d)] * batch
    p_all = p.reshape(DEPTH * batch, seq, PLE_DIM)
    cos_t, sin_t = _rope_tables(seq)
    kv = None
    for i in range(DEPTH):
        if i < N_A_LAYERS:
            mixed = [_rglru_layer(xb, seq, b if i == 0 else 0, a_w_in[i], a_b_in[i], a_conv_w[i],
                                  a_conv_b[i], a_gate_a_w[i], a_gate_a_b[i], a_gate_i_w[i], a_gate_i_b[i],
                                  a_lambda[i], a_w_out[i], ln_g[i, 0], ln_b[i, 0])
                     for b, xb in enumerate(xs)]
        else:
            if i == N_A_LAYERS:
                kv = []
                for xb in xs:
                    k_blk, vt_blk, k_mean = _shared_kv(xb, seq, kv_ln_g, kv_ln_b, w_kv, cos_t, sin_t)
                    kv.append((k_blk, vt_blk, k_mean.reshape(seq // MOBA_BLOCK, d)))
            j = i - N_A_LAYERS
            mixed = _moba_layer([(xb,) + kvb for xb, kvb in zip(xs, kv)], seq, b_w_q[j], b_w_o[j],
                                ln_g[i, 0], ln_b[i, 0], cos_t, sin_t)
        xs = _moe_layer(mixed, i, router_w[i], router_b[i], exp_w_gate, exp_w_up, exp_w_down,
                        sh_w_gate[i], sh_w_up[i], sh_w_down[i], ln_g[i, 1], ln_b[i, 1],
                        p_all, i * batch, ple_w[i], ple_gate_w[i])
    return jnp.stack(xs, axis=0)
```

```python
import functools

import jax
import jax.numpy as jnp
from jax import lax
from jax.experimental import pallas as pl
from jax.experimental.pallas import tpu as pltpu
from jax.experimental.pallas import tpu_sc as plsc

F32 = jnp.float32
BF16 = jnp.bfloat16
I32 = jnp.int32
U32 = jnp.uint32
HIGH_HALF = 0xFFFF0000

SC_CORES = 2
SC_SUBCORES = 16
SC_WORKERS = SC_CORES * SC_SUBCORES
SC_LANES = 16
SC_INDEX_CHUNK = 128
SC_STAGE_BYTES = 256 * 1024

D_MODEL = 1024
DEPTH = 4
N_A_LAYERS = DEPTH // 2
D_RNN = D_MODEL
LRU_BLOCKS = 4
LRU_BLOCK_W = D_RNN // LRU_BLOCKS
CONV_W = 4
LRU_C = 8.0
N_HEADS = 8
HEAD_DIM = D_MODEL // N_HEADS
ROT_DIM = HEAD_DIM // 4
ROPE_THETA = 500000.0
MOBA_BLOCK = 256
MOBA_TOPK = 3
N_EXPERTS = 64
N_GROUPS = 8
GROUP_SIZE = N_EXPERTS // N_GROUPS
TOPK_GROUPS = 4
TOPK_EXPERTS = 8
D_EXPERT = 256
D_SHARED = 256
ROUTED_SCALE = 2.5
PLE_DIM = 256
DN_ALPHA = (2 * DEPTH) ** 0.25
LN_EPS = 1e-5

V7X_VMEM_LIMIT_BYTES = 56 * 1024 * 1024

MIXER_ROWS = 256
TOKEN_ROWS = 512
EXPERT_ROWS = 2304
NEG_INF = float("-inf")


def _params(*sem):
    return pltpu.CompilerParams(dimension_semantics=sem, vmem_limit_bytes=V7X_VMEM_LIMIT_BYTES)


def _layer_norm(z, g, b):
    mu = jnp.mean(z, axis=-1, keepdims=True)
    zc = z - mu
    var = jnp.mean(zc * zc, axis=-1, keepdims=True)
    return zc * lax.rsqrt(var + LN_EPS) * g + b


def _silu(x):
    return x * jax.nn.sigmoid(x)


def _gelu_tanh(x):
    return x * jax.nn.sigmoid(x * (1.5957691216057308 + 0.07135481627159855 * (x * x)))


def _full(shape):
    return pl.BlockSpec(shape, lambda *_: (0,) * len(shape))


def _count_le(ends, values):
    return jnp.sum((ends[None, :] <= values[:, None]).astype(I32), axis=1)


def _pack_bf16_pairs(x):
    w = x.shape[1] // 2
    bits = lax.bitcast_convert_type(x.astype(BF16).astype(F32), U32)
    return (bits[:, :w] >> 16) | (bits[:, w:] & U32(HIGH_HALF))


def _unpack_bf16_pairs(u):
    lo = lax.bitcast_convert_type(u << 16, F32)
    hi = lax.bitcast_convert_type(u & U32(HIGH_HALF), F32)
    return lo, hi


def _sc_mesh():
    return plsc.VectorSubcoreMesh(core_axis_name="c", subcore_axis_name="s")


def _sc_worker_id():
    return lax.axis_index("s") * SC_CORES + lax.axis_index("c")


def _sc_chunks_per_step(chunks_per_worker, row_words):
    g = max(1, SC_STAGE_BYTES // (SC_INDEX_CHUNK * row_words * 4))
    while chunks_per_worker % g:
        g -= 1
    return g


def _sc_gather_rows(table, idx):
    b = idx.shape[0]
    w = table.shape[1]
    chunk = SC_INDEX_CHUNK
    chunks_per_worker = b // (SC_WORKERS * chunk)
    assert chunks_per_worker * SC_WORKERS * chunk == b
    g = _sc_chunks_per_step(chunks_per_worker, w)

    @functools.partial(
        pl.kernel, mesh=_sc_mesh(), out_type=jax.ShapeDtypeStruct((b, w), table.dtype),
        scratch_types=[pltpu.VMEM((g, chunk), I32), pltpu.VMEM((g * chunk, w), table.dtype),
                       pltpu.SemaphoreType.DMA])
    def gather(table_hbm, idx_hbm, out_hbm, idx_v, rows_v, sem):
        first = _sc_worker_id() * chunks_per_worker

        @pl.loop(0, chunks_per_worker // g)
        def _(j):
            c0 = first + j * g
            pltpu.sync_copy(idx_hbm.at[pl.ds(c0, g)], idx_v)
            copies = [pltpu.async_copy(table_hbm.at[idx_v.at[q]], rows_v.at[pl.ds(q * chunk, chunk)], sem)
                      for q in range(g)]
            for cp in copies:
                cp.wait()
            pltpu.sync_copy(rows_v, out_hbm.at[pl.ds(c0 * chunk, g * chunk)])

    return gather(table, idx.reshape(b // chunk, chunk))


def _sc_scatter_rows(src, idx, out_rows):
    n, w = src.shape
    chunk = SC_INDEX_CHUNK
    fan = idx.shape[1]
    chunks_per_worker = n // (SC_WORKERS * chunk)
    assert idx.shape == (n // chunk, fan, chunk) and chunks_per_worker * SC_WORKERS * chunk == n
    g = _sc_chunks_per_step(chunks_per_worker, w)

    @functools.partial(
        pl.kernel, mesh=_sc_mesh(), out_type=jax.ShapeDtypeStruct((out_rows, w), src.dtype),
        scratch_types=[pltpu.VMEM((g, fan, chunk), I32), pltpu.VMEM((g * chunk, w), src.dtype),
                       pltpu.SemaphoreType.DMA])
    def scatter(src_hbm, idx_hbm, out_hbm, idx_v, rows_v, sem):
        first = _sc_worker_id() * chunks_per_worker

        @pl.loop(0, chunks_per_worker // g)
        def _(j):
            c0 = first + j * g
            pltpu.sync_copy(src_hbm.at[pl.ds(c0 * chunk, g * chunk)], rows_v)
            pltpu.sync_copy(idx_hbm.at[pl.ds(c0, g)], idx_v)
            copies = [pltpu.async_copy(rows_v.at[pl.ds(q * chunk, chunk)], out_hbm.at[idx_v.at[q, f]], sem)
                      for q in range(g) for f in range(fan)]
            for cp in copies:
                cp.wait()

    return scatter(src, idx)


def _sc_weighted_gather(table, idx, wrep, fan):
    nf = idx.shape[0]
    n = nf // fan
    w = table.shape[1]
    chunk = SC_INDEX_CHUNK
    lanes = SC_LANES
    tokens = chunk // fan
    chunks_per_worker = nf // (SC_WORKERS * chunk)
    assert chunks_per_worker * SC_WORKERS * chunk == nf and w % lanes == 0 and wrep.shape == (nf, lanes)

    @functools.partial(
        pl.kernel, mesh=_sc_mesh(), out_type=jax.ShapeDtypeStruct((n, 2 * w), F32),
        scratch_types=[pltpu.VMEM((chunk,), I32), pltpu.VMEM((chunk, w), U32), pltpu.VMEM((chunk, lanes), F32),
                       pltpu.VMEM((tokens, 2 * w), F32), pltpu.SemaphoreType.DMA],
        compiler_params=pltpu.CompilerParams(needs_layout_passes=False))
    def weighted_gather(table_hbm, idx_hbm, w_hbm, out_hbm, idx_v, rows_v, w_v, out_v, sem):
        first = _sc_worker_id() * chunks_per_worker

        @pl.loop(0, chunks_per_worker)
        def _(j):
            c0 = first + j
            pltpu.sync_copy(idx_hbm.at[pl.ds(c0 * chunk, chunk)], idx_v)
            gather = pltpu.async_copy(table_hbm.at[idx_v], rows_v, sem)
            pltpu.sync_copy(w_hbm.at[pl.ds(c0 * chunk, chunk)], w_v)
            gather.wait()

            @pl.loop(0, tokens)
            def _(t):
                weights = [w_v[t * fan + q, pl.ds(0, lanes)] for q in range(fan)]

                @pl.loop(0, w, step=lanes)
                def _(c):
                    acc_lo = jnp.zeros((lanes,), F32)
                    acc_hi = jnp.zeros((lanes,), F32)
                    for q in range(fan):
                        u = rows_v[t * fan + q, pl.ds(c, lanes)]
                        acc_lo = acc_lo + weights[q] * lax.bitcast_convert_type(u << 16, F32)
                        acc_hi = acc_hi + weights[q] * lax.bitcast_convert_type(u & U32(HIGH_HALF), F32)
                    out_v[t, pl.ds(c, lanes)] = acc_lo
                    out_v[t, pl.ds(w + c, lanes)] = acc_hi

            pltpu.sync_copy(out_v, out_hbm.at[pl.ds(c0 * tokens, tokens)])

    return weighted_gather(table, idx, wrep)


def _rglru_kernel(x_ref, win_ref, bin_ref, cw_ref, cb_ref, gaw_ref, gab_ref, giw_ref, gib_ref,
                  lam_ref, wout_ref, lng_ref, lnb_ref, o_ref, op_ref, tail_ref, h_ref, *, tiles_per_seq):
    i = pl.program_id(0)
    tm = x_ref.shape[0]

    @pl.when(i % tiles_per_seq == 0)
    def _():
        tail_ref[...] = jnp.zeros_like(tail_ref)
        h_ref[...] = jnp.zeros_like(h_ref)

    x = x_ref[...]
    xy = jnp.dot(x.astype(BF16), win_ref[...], preferred_element_type=F32) + bin_ref[...]
    xb = xy[:, :D_RNN]
    y = _gelu_tanh(xy[:, D_RNN:])

    tail = tail_ref[...]
    row8 = lax.broadcasted_iota(I32, (8, 1), 0)
    xc = cb_ref[...] + xb * cw_ref[CONV_W - 1:CONV_W, :]
    for d in range(1, CONV_W):
        rolled = pltpu.roll(xb, d, 0)
        head = jnp.where(row8 < d, pltpu.roll(tail, d, 0), rolled[:8])
        shifted = jnp.concatenate([head, rolled[8:]], axis=0)
        xc = xc + shifted * cw_ref[CONV_W - 1 - d:CONV_W - d, :]
    tail_ref[...] = xb[tm - 8:, :]

    r_parts, i_parts = [], []
    for n in range(LRU_BLOCKS):
        xg = xc[:, n * LRU_BLOCK_W:(n + 1) * LRU_BLOCK_W].astype(BF16)
        r_parts.append(jnp.dot(xg, gaw_ref[n], preferred_element_type=F32))
        i_parts.append(jnp.dot(xg, giw_ref[n], preferred_element_type=F32))
    r = jax.nn.sigmoid(jnp.concatenate(r_parts, axis=1) + gab_ref[...])
    ig = jax.nn.sigmoid(jnp.concatenate(i_parts, axis=1) + gib_ref[...])

    lam = lam_ref[...]
    softplus_neg_lam = jnp.maximum(-lam, 0.0) + jnp.log1p(jnp.exp(-jnp.abs(lam)))
    log_a = (-LRU_C * r) * softplus_neg_lam
    a = jnp.exp(log_a)
    u = jnp.sqrt(1.0 - a * a) * (ig * xc)

    n_grp = tm // 8
    sub = lax.broadcasted_iota(I32, (1, 8, 1), 1)
    acc_a = a.reshape(n_grp, 8, D_RNN)
    acc_h = u.reshape(n_grp, 8, D_RNN)
    for d in (1, 2, 4):
        keep = sub >= d
        sh_a = pltpu.roll(acc_a, d, 1)
        sh_h = pltpu.roll(acc_h, d, 1)
        acc_h = jnp.where(keep, acc_a * sh_h + acc_h, acc_h)
        acc_a = jnp.where(keep, acc_a * sh_a, acc_a)
    state = h_ref[...]
    groups = []
    for g in range(n_grp):
        hg = acc_h[g] + acc_a[g] * state
        state = hg[7:8, :]
        groups.append(hg)
    h = jnp.concatenate(groups, axis=0)
    h_ref[...] = state

    mix = jnp.dot((h * y).astype(BF16), wout_ref[...], preferred_element_type=F32)
    x1 = _layer_norm(DN_ALPHA * x + mix, lng_ref[...], lnb_ref[...])
    o_ref[...] = x1
    op_ref[...] = _pack_bf16_pairs(x1)


def _rglru_layer(x, seq, stream, w_in, b_in, conv_w, conv_b, ga_w, ga_b, gi_w, gi_b, lam, w_out,
                 ln_g, ln_b):
    n, d = seq, x.shape[1]
    tm = min(MIXER_ROWS, seq)
    first_tile = stream * (seq // tm)
    row = lambda v: v.reshape(1, -1).astype(F32)
    return pl.pallas_call(
        functools.partial(_rglru_kernel, tiles_per_seq=seq // tm),
        grid=(n // tm,),
        in_specs=[pl.BlockSpec((tm, d), lambda i: (i + first_tile, 0)),
                  _full((d, 2 * D_RNN)), _full((1, 2 * D_RNN)),
                  _full((CONV_W, D_RNN)), _full((1, D_RNN)),
                  _full((LRU_BLOCKS, LRU_BLOCK_W, LRU_BLOCK_W)), _full((1, D_RNN)),
                  _full((LRU_BLOCKS, LRU_BLOCK_W, LRU_BLOCK_W)), _full((1, D_RNN)),
                  _full((1, D_RNN)), _full((D_RNN, d)), _full((1, d)), _full((1, d))],
        out_specs=[pl.BlockSpec((tm, d), lambda i: (i, 0)), pl.BlockSpec((tm, d // 2), lambda i: (i, 0))],
        out_shape=[jax.ShapeDtypeStruct((n, d), F32), jax.ShapeDtypeStruct((n, d // 2), U32)],
        scratch_shapes=[pltpu.VMEM((8, D_RNN), F32), pltpu.VMEM((1, D_RNN), F32)],
        compiler_params=_params("arbitrary"),
        name="rglru_mixer",
    )(x, w_in.astype(BF16), row(b_in), conv_w, row(conv_b), ga_w.astype(BF16), row(ga_b),
      gi_w.astype(BF16), row(gi_b), row(lam), w_out.astype(BF16), row(ln_g), row(ln_b))


def _first_argmax(cur, idx, size, axis):
    m = jnp.max(cur, axis=axis, keepdims=True)
    first = jnp.min(jnp.where(cur == m, idx, size), axis=axis, keepdims=True)
    return m, first


def _router_kernel(x_ref, rw_ref, rb_ref, e_ref, w_ref, r_ref, cnt_ref, carry_ref):
    i = pl.program_id(0)
    tm = x_ref.shape[0]

    @pl.when(i == 0)
    def _():
        carry_ref[...] = jnp.zeros_like(carry_ref)

    x = x_ref[...]
    x_hi = x.astype(BF16)
    x_lo = (x - x_hi.astype(F32)).astype(BF16)
    rw = rw_ref[...]
    hi_both = jnp.dot(x_hi, rw, preferred_element_type=F32)
    logits = (hi_both[:, :128] + hi_both[:, 128:]
              + jnp.dot(x_lo, rw[:, :128], preferred_element_type=F32))
    scores = jax.nn.sigmoid(logits.T[:N_EXPERTS, :])
    choice = scores + rb_ref[...]

    c3 = choice.reshape(N_GROUPS, GROUP_SIZE, tm)
    in_grp = lax.broadcasted_iota(I32, c3.shape, 1)
    m1, i1 = _first_argmax(c3, in_grp, GROUP_SIZE, 1)
    m2 = jnp.max(jnp.where(in_grp == i1, NEG_INF, c3), axis=1, keepdims=True)
    grp_score = (m1 + m2)[:, 0, :]

    grp_id = lax.broadcasted_iota(I32, grp_score.shape, 0)
    grp_sel = jnp.zeros(grp_score.shape, jnp.bool_)
    cur = grp_score
    for _ in range(TOPK_GROUPS):
        _, gi = _first_argmax(cur, grp_id, N_GROUPS, 0)
        hit = grp_id == gi
        grp_sel = jnp.logical_or(grp_sel, hit)
        cur = jnp.where(hit, NEG_INF, cur)

    cur = jnp.where(grp_sel[:, None, :], c3, NEG_INF).reshape(N_EXPERTS, tm)
    exp_id = lax.broadcasted_iota(I32, cur.shape, 0)
    sel = jnp.zeros(cur.shape, F32)
    e_rows, s_rows = [], []
    for _ in range(TOPK_EXPERTS):
        _, ei = _first_argmax(cur, exp_id, N_EXPERTS, 0)
        hit = exp_id == ei
        e_rows.append(ei)
        s_rows.append(jnp.sum(jnp.where(hit, scores, 0.0), axis=0, keepdims=True))
        sel = jnp.where(hit, 1.0, sel)
        cur = jnp.where(hit, NEG_INF, cur)
    e_top = jnp.concatenate(e_rows, axis=0)
    s_top = jnp.concatenate(s_rows, axis=0)
    w_ref[...] = s_top / jnp.sum(s_top, axis=0, keepdims=True) * ROUTED_SCALE
    e_ref[...] = e_top

    t_row = lax.broadcasted_iota(I32, (tm, tm), 0)
    t_col = lax.broadcasted_iota(I32, (tm, tm), 1)
    before = (t_row < t_col).astype(BF16)
    cum = jnp.dot(sel.astype(BF16), before, preferred_element_type=F32) + carry_ref[...]
    r_rows = [jnp.sum(jnp.where(exp_id == e_rows[k], cum, 0.0), axis=0, keepdims=True)
              for k in range(TOPK_EXPERTS)]
    r_ref[...] = jnp.concatenate(r_rows, axis=0).astype(I32)
    carry_ref[...] = carry_ref[...] + jnp.sum(sel, axis=1, keepdims=True)
    cnt_ref[...] = jnp.broadcast_to(carry_ref[...], cnt_ref.shape).astype(I32)


def _router(x, router_w, router_b):
    n, d = x.shape
    tm = min(TOKEN_ROWS, n)
    rw = jnp.pad(router_w, ((0, 0), (0, 128 - N_EXPERTS)))
    rw_hi = rw.astype(BF16)
    rw = jnp.concatenate([rw_hi, (rw - rw_hi.astype(F32)).astype(BF16)], axis=1)
    k = TOPK_EXPERTS
    tok_spec = pl.BlockSpec((k, tm), lambda i: (0, i))
    return pl.pallas_call(
        _router_kernel,
        grid=(n // tm,),
        in_specs=[pl.BlockSpec((tm, d), lambda i: (i, 0)), _full((d, 256)), _full((N_EXPERTS, 1))],
        out_specs=[tok_spec, tok_spec, tok_spec, _full((N_EXPERTS, 128))],
        out_shape=[jax.ShapeDtypeStruct((k, n), I32), jax.ShapeDtypeStruct((k, n), F32),
                   jax.ShapeDtypeStruct((k, n), I32), jax.ShapeDtypeStruct((N_EXPERTS, 128), I32)],
        scratch_shapes=[pltpu.VMEM((N_EXPERTS, 1), F32)],
        compiler_params=_params("arbitrary"),
        name="moe_router",
    )(x, rw, router_b.reshape(N_EXPERTS, 1).astype(F32))


def _expert_kernel(blk_e_ref, nblk_ref, first_ref, slot_ref, next_ref, xs_ref, wg_hbm, wu_hbm, wd_hbm,
                   after_ref, ys_ref, wg_f, wu_f, wd_f, wg_s, wu_s, wd_s, sem, *, layer):
    del after_ref
    i = pl.program_id(0)

    def fetch(expert, slot):
        return (pltpu.make_async_copy(wg_hbm.at[layer, expert], wg_f.at[slot], sem.at[slot, 0]),
                pltpu.make_async_copy(wu_hbm.at[layer, expert], wu_f.at[slot], sem.at[slot, 1]),
                pltpu.make_async_copy(wd_hbm.at[layer, expert], wd_f.at[slot], sem.at[slot, 2]))

    @pl.when(i < nblk_ref[0])
    def _():
        slot = slot_ref[i]

        @pl.when(i == 0)
        def _():
            for copy in fetch(blk_e_ref[0], 0):
                copy.start()

        @pl.when(first_ref[i] == 1)
        def _():
            for copy in fetch(blk_e_ref[i], slot):
                copy.wait()
            wg_s[...] = wg_f[slot].astype(BF16)
            wu_s[...] = wu_f[slot].astype(BF16)
            wd_s[...] = wd_f[slot].astype(BF16)

            @pl.when(next_ref[i] >= 0)
            def _():
                for copy in fetch(next_ref[i], 1 - slot):
                    copy.start()

        lo, hi = _unpack_bf16_pairs(xs_ref[...])
        xs = jnp.concatenate([lo.astype(BF16), hi.astype(BF16)], axis=1)
        g = jnp.dot(xs, wg_s[...], preferred_element_type=F32)
        u = jnp.dot(xs, wu_s[...], preferred_element_type=F32)
        hdn = (_silu(g) * u).astype(BF16)
        ys_ref[...] = _pack_bf16_pairs(jnp.dot(hdn, wd_s[...], preferred_element_type=F32))


_ORDER_ONLY = pl.BlockSpec(memory_space=pl.ANY)


def _experts(xs, blk_e, nblk, layer, w_gate, w_up, w_down, after):
    rows, half = xs.shape
    d = 2 * half
    m = EXPERT_ROWS
    n_blocks = rows // m
    idx = jnp.arange(n_blocks, dtype=I32)
    used = idx < nblk[0]
    first = jnp.logical_and(used, jnp.logical_or(idx == 0, blk_e != jnp.roll(blk_e, 1)))
    slot = (jnp.cumsum(first.astype(I32)) - 1) % 2
    larger = jnp.logical_and(blk_e[None, :] > blk_e[:, None], used[None, :])
    next_e = jnp.min(jnp.where(larger, blk_e[None, :], N_EXPERTS), axis=1)
    next_e = jnp.where(next_e < N_EXPERTS, next_e, -1)
    row_map = lambda i, be, nb, fi, sl, nx: (jnp.minimum(i, nb[0] - 1), 0)
    hbm = pl.BlockSpec(memory_space=pl.ANY)
    return pl.pallas_call(
        functools.partial(_expert_kernel, layer=layer),
        grid_spec=pltpu.PrefetchScalarGridSpec(
            num_scalar_prefetch=5,
            grid=(n_blocks,),
            in_specs=[pl.BlockSpec((m, half), row_map), hbm, hbm, hbm, _ORDER_ONLY],
            out_specs=pl.BlockSpec((m, half), row_map),
            scratch_shapes=[pltpu.VMEM((2, d, D_EXPERT), F32), pltpu.VMEM((2, d, D_EXPERT), F32),
                            pltpu.VMEM((2, D_EXPERT, d), F32),
                            pltpu.VMEM((d, D_EXPERT), BF16), pltpu.VMEM((d, D_EXPERT), BF16),
                            pltpu.VMEM((D_EXPERT, d), BF16), pltpu.SemaphoreType.DMA((2, 3))]),
        out_shape=jax.ShapeDtypeStruct((rows, half), U32),
        compiler_params=_params("arbitrary"),
        name="moe_experts",
    )(blk_e, nblk, first.astype(I32), slot.astype(I32), next_e.astype(I32), xs, w_gate, w_up, w_down, after)


def _combine_kernel(x_ref, moe_ref, sg_ref, su_ref, sd_ref, lng_ref, lnb_ref,
                    p_ref, pw_ref, pg_ref, after_ref, o_ref):
    del after_ref
    x = x_ref[...]
    moe = moe_ref[...]
    xb = x.astype(BF16)
    hdn = _silu(jnp.dot(xb, sg_ref[...], preferred_element_type=F32)) * \
        jnp.dot(xb, su_ref[...], preferred_element_type=F32)
    shared = jnp.dot(hdn.astype(BF16), sd_ref[...], preferred_element_type=F32)
    x2 = _layer_norm(DN_ALPHA * x + (moe + shared), lng_ref[...], lnb_ref[...])
    emb = jnp.dot(p_ref[0].astype(BF16), pw_ref[...], preferred_element_type=F32)
    gate = jax.nn.sigmoid(jnp.dot(x2.astype(BF16), pg_ref[...], preferred_element_type=F32))
    o_ref[...] = x2 + emb * gate


def _combine(x, moe, s_gate, s_up, s_down, ln_g, ln_b, p_all, p_index, ple_w, ple_gate_w, after):
    n, d = x.shape
    tm = min(TOKEN_ROWS // 2, n)
    row = lambda v: v.reshape(1, -1).astype(F32)
    return pl.pallas_call(
        _combine_kernel,
        grid=(n // tm,),
        in_specs=[pl.BlockSpec((tm, d), lambda i: (i, 0)),
                  pl.BlockSpec((tm, d), lambda i: (i, 0)),
                  _full((d, D_SHARED)), _full((d, D_SHARED)), _full((D_SHARED, d)),
                  _full((1, d)), _full((1, d)),
                  pl.BlockSpec((1, tm, PLE_DIM), lambda i: (p_index, i, 0)),
                  _full((PLE_DIM, d)), _full((d, d)), _ORDER_ONLY],
        out_specs=pl.BlockSpec((tm, d), lambda i: (i, 0)),
        out_shape=jax.ShapeDtypeStruct((n, d), F32),
        compiler_params=_params("parallel"),
        name="moe_combine",
    )(x, moe, s_gate.astype(BF16), s_up.astype(BF16), s_down.astype(BF16), row(ln_g), row(ln_b),
      p_all, ple_w.astype(BF16), ple_gate_w.astype(BF16), after)


def _positions_kernel(e_ref, r_ref, start_ref, pos_ref):
    e = e_ref[...]
    start = start_ref[...]
    grp_id = lax.broadcasted_iota(I32, (start.shape[0], e.shape[1]), 0)
    rows = [jnp.sum(jnp.where(grp_id == e[k:k + 1], start, 0), axis=0, keepdims=True)
            for k in range(e.shape[0])]
    pos_ref[...] = jnp.concatenate(rows, axis=0) + r_ref[...]


def _positions(e_idx, rank, starts):
    k, n = e_idx.shape
    tm = min(4 * TOKEN_ROWS, n)
    groups = starts.shape[0]
    tok_spec = pl.BlockSpec((k, tm), lambda i: (0, i))
    return pl.pallas_call(
        _positions_kernel,
        grid=(n // tm,),
        in_specs=[tok_spec, tok_spec, _full((groups, 1))],
        out_specs=tok_spec,
        out_shape=jax.ShapeDtypeStruct((k, n), I32),
        compiler_params=_params("parallel"),
        name="group_positions",
    )(e_idx, rank, starts.reshape(groups, 1))


def _moe_layer(streams, layer, router_w, router_b, w_gate, w_up, w_down, s_gate, s_up, s_down,
               ln_g, ln_b, p_all, p_index, ple_w, ple_gate_w):
    m = EXPERT_ROWS
    k = TOPK_EXPERTS
    chunk = SC_INDEX_CHUNK
    routed = []
    for x, xp in streams:
        n, d = x.shape
        e_idx, wgt, rank, cnt = _router(x, router_w, router_b)
        counts = cnt[:, 0]
        padded = (counts + m - 1) // m * m
        pends = jnp.cumsum(padded)
        n_blocks = (n * k) // m + N_EXPERTS
        blk_first = jnp.arange(n_blocks, dtype=I32) * m
        blk_e = jnp.minimum(_count_le(pends, blk_first), N_EXPERTS - 1)
        nblk = (pends[-1] // m).astype(I32).reshape(1)
        pos = _positions(e_idx, rank, (pends - padded).astype(I32))
        pos_chunks = pos.reshape(k, n // chunk, chunk).transpose(1, 0, 2)
        xs = _sc_scatter_rows(xp, pos_chunks, n_blocks * m)
        routed.append((xs, blk_e, nblk, pos, wgt))
    gathered, order = [], routed[-1][3]
    for (x, _), (xs, blk_e, nblk, pos, wgt) in zip(streams, routed):
        n, d = x.shape
        ys = _experts(xs, blk_e, nblk, layer, w_gate, w_up, w_down, order)
        wrep = jnp.broadcast_to(wgt.T.reshape(n * k, 1), (n * k, SC_LANES))
        gathered.append(_sc_weighted_gather(ys, pos.T.reshape(-1), wrep, k))
        order = ys
    outs = []
    for s, ((x, _), moe) in enumerate(zip(streams, gathered)):
        outs.append(_combine(x, moe, s_gate, s_up, s_down, ln_g, ln_b, p_all, p_index + s,
                             ple_w, ple_gate_w, order))
        order = outs[-1]
    return outs


def _rope(t, cos_t, sin_t):
    half = ROT_DIM // 2
    width = t.shape[1]
    lane = lax.broadcasted_iota(I32, (1, width), 1) % HEAD_DIM
    partner = jnp.where(lane < half, pltpu.roll(t, width - half, 1), pltpu.roll(t, half, 1))
    cos_f = jnp.concatenate([cos_t] * N_HEADS, axis=1)
    sin_f = jnp.concatenate([sin_t] * N_HEADS, axis=1)
    return t * cos_f + partner * sin_f


def _kv_kernel(x_ref, g_ref, b_ref, wkv_ref, cos_ref, sin_ref, k_ref, vt_ref, km_ref):
    h = _layer_norm(x_ref[...], g_ref[...], b_ref[...])
    kv = jnp.dot(h.astype(BF16), wkv_ref[...], preferred_element_type=F32)
    k = _rope(kv[:, :D_MODEL], cos_ref[...], sin_ref[...])
    vt = kv[:, D_MODEL:].T
    km_ref[0] = jnp.mean(k, axis=0, keepdims=True)
    for hd in range(N_HEADS):
        k_ref[hd, 0] = k[:, hd * HEAD_DIM:(hd + 1) * HEAD_DIM].astype(BF16)
        vt_ref[hd, 0] = vt[hd * HEAD_DIM:(hd + 1) * HEAD_DIM, :].astype(BF16)


def _shared_kv(x, seq, ln_g, ln_b, w_kv, cos_t, sin_t):
    n, d = x.shape
    blk = MOBA_BLOCK
    nbt = n // blk
    spb = seq // blk
    row = lambda v: v.reshape(1, -1).astype(F32)
    return pl.pallas_call(
        _kv_kernel,
        grid=(nbt,),
        in_specs=[pl.BlockSpec((blk, d), lambda i: (i, 0)), _full((1, d)), _full((1, d)),
                  _full((d, 2 * d)),
                  pl.BlockSpec((blk, HEAD_DIM), lambda i: (i % spb, 0)),
                  pl.BlockSpec((blk, HEAD_DIM), lambda i: (i % spb, 0))],
        out_specs=[pl.BlockSpec((N_HEADS, 1, blk, HEAD_DIM), lambda i: (0, i, 0, 0)),
                   pl.BlockSpec((N_HEADS, 1, HEAD_DIM, blk), lambda i: (0, i, 0, 0)),
                   pl.BlockSpec((1, 1, d), lambda i: (i, 0, 0))],
        out_shape=[jax.ShapeDtypeStruct((N_HEADS, nbt, blk, HEAD_DIM), BF16),
                   jax.ShapeDtypeStruct((N_HEADS, nbt, HEAD_DIM, blk), BF16),
                   jax.ShapeDtypeStruct((nbt, 1, d), F32)],
        compiler_params=_params("parallel"),
        name="shared_kv",
    )(x, row(ln_g), row(ln_b), w_kv.astype(BF16), cos_t, sin_t)


SEL_ROWS = 8


def _query_kernel(x_ref, wq_ref, cos_ref, sin_ref, km_ref, qp_ref, sel_ref, cnt_ref, carry_ref,
                  *, blocks_per_seq):
    i = pl.program_id(0)
    tm = x_ref.shape[0]
    nb = km_ref.shape[0]
    own = i % blocks_per_seq

    @pl.when(own == 0)
    def _():
        carry_ref[...] = jnp.zeros_like(carry_ref)

    q = jnp.dot(x_ref[...].astype(BF16), wq_ref[...], preferred_element_type=F32)
    q = _rope(q, cos_ref[...], sin_ref[...]) * (HEAD_DIM ** -0.5)

    t_row = lax.broadcasted_iota(I32, (tm, tm), 0)
    t_col = lax.broadcasted_iota(I32, (tm, tm), 1)
    before = (t_row < t_col).astype(BF16)
    km = km_ref[...]
    km_hi = km.astype(BF16)
    km_both = jnp.concatenate([km_hi, (km - km_hi.astype(F32)).astype(BF16)], axis=0)
    q_hi = q.astype(BF16)
    q_lo = (q - q_hi.astype(F32)).astype(BF16)
    nt = (((1,), (1,)), ((), ()))
    gates = []
    for hd in range(N_HEADS):
        lo, hi = hd * HEAD_DIM, (hd + 1) * HEAD_DIM
        qp_ref[0, hd] = q[:, lo:hi]
        both = lax.dot_general(km_both[:, lo:hi], q_hi[:, lo:hi], nt, preferred_element_type=F32)
        gates.append(both[:nb] + both[nb:]
                     + lax.dot_general(km_hi[:, lo:hi], q_lo[:, lo:hi], nt, preferred_element_type=F32))

    gate = jnp.concatenate(gates, axis=0).reshape(N_HEADS, nb, tm)
    blk_id = lax.broadcasted_iota(I32, (N_HEADS, nb, tm), 1)
    cur = jnp.where(blk_id < own, gate, NEG_INF)
    sel_rows = []
    chosen = jnp.zeros((N_HEADS, nb, tm), F32)
    for _ in range(MOBA_TOPK):
        m, bi = _first_argmax(cur, blk_id, nb, 1)
        valid = m > NEG_INF
        hit = jnp.logical_and(blk_id == bi, valid)
        sel_rows.append(jnp.where(valid, bi, -1))
        chosen = jnp.where(hit, 1.0, chosen)
        cur = jnp.where(blk_id == bi, NEG_INF, cur)
    chosen2 = chosen.reshape(N_HEADS * nb, tm)
    carry = carry_ref[...]
    cum = (jnp.dot(chosen2.astype(BF16), before, preferred_element_type=F32) + carry).reshape(N_HEADS, nb, tm)
    rank_rows = [jnp.sum(jnp.where(blk_id == s, cum, 0.0), axis=1, keepdims=True).astype(I32)
                 for s in sel_rows]
    carry = carry + jnp.sum(chosen2, axis=1, keepdims=True)
    carry_ref[...] = carry
    pad = jnp.zeros((N_HEADS, SEL_ROWS - 2 * MOBA_TOPK, tm), I32)
    table = jnp.concatenate(sel_rows + rank_rows + [pad], axis=1)
    sel_ref[...] = table.reshape(N_HEADS * SEL_ROWS, tm)
    cnt_ref[0] = jnp.broadcast_to(carry, cnt_ref.shape[1:]).astype(I32)


def _queries(x, seq, w_q, cos_t, sin_t, k_mean):
    n, d = x.shape
    blk = MOBA_BLOCK
    nb = seq // blk
    return pl.pallas_call(
        functools.partial(_query_kernel, blocks_per_seq=nb),
        grid=(n // blk,),
        in_specs=[pl.BlockSpec((blk, d), lambda i: (i, 0)), _full((d, d)),
                  pl.BlockSpec((blk, HEAD_DIM), lambda i: (i % nb, 0)),
                  pl.BlockSpec((blk, HEAD_DIM), lambda i: (i % nb, 0)),
                  pl.BlockSpec((nb, d), lambda i: (i // nb, 0))],
        out_specs=[pl.BlockSpec((1, N_HEADS, blk, HEAD_DIM), lambda i: (i, 0, 0, 0)),
                   pl.BlockSpec((N_HEADS * SEL_ROWS, blk), lambda i: (0, i)),
                   pl.BlockSpec((1, N_HEADS * nb, 128), lambda i: (i // nb, 0, 0))],
        out_shape=[jax.ShapeDtypeStruct((n // blk, N_HEADS, blk, HEAD_DIM), F32),
                   jax.ShapeDtypeStruct((N_HEADS * SEL_ROWS, n), I32),
                   jax.ShapeDtypeStruct((n // seq, N_HEADS * nb, 128), I32)],
        scratch_shapes=[pltpu.VMEM((N_HEADS * nb, 1), F32)],
        compiler_params=_params("arbitrary"),
        name="moba_queries",
    )(x, w_q.astype(BF16), cos_t, sin_t, k_mean)


def _moba_positions_kernel(sel_ref, start_ref, pos_ref, *, dump_row):
    tm = sel_ref.shape[1]
    nb = start_ref.shape[1] // N_HEADS
    blk_id = lax.broadcasted_iota(I32, (nb, tm), 0)
    dump = dump_row + lax.broadcasted_iota(I32, (1, tm), 1) % SC_INDEX_CHUNK
    rows = []
    for hd in range(N_HEADS):
        start = start_ref[0, hd * nb:(hd + 1) * nb, :]
        for s in range(MOBA_TOPK):
            sel = sel_ref[hd * SEL_ROWS + s:hd * SEL_ROWS + s + 1, :]
            rank = sel_ref[hd * SEL_ROWS + MOBA_TOPK + s:hd * SEL_ROWS + MOBA_TOPK + s + 1, :]
            base = jnp.sum(jnp.where(blk_id == sel, start, 0), axis=0, keepdims=True)
            rows.append(jnp.where(sel >= 0, base + rank, dump))
        rows.extend([dump] * (SEL_ROWS - MOBA_TOPK))
    pos_ref[...] = jnp.concatenate(rows, axis=0)


def _moba_positions(sel, starts, seq, dump_row):
    rows, n = sel.shape
    tm = min(4 * TOKEN_ROWS, seq)
    tps = seq // tm
    groups = starts.shape[1]
    return pl.pallas_call(
        functools.partial(_moba_positions_kernel, dump_row=dump_row),
        grid=(n // tm,),
        in_specs=[pl.BlockSpec((rows, tm), lambda i: (0, i)),
                  pl.BlockSpec((1, groups, 1), lambda i: (i // tps, 0, 0))],
        out_specs=pl.BlockSpec((rows, tm), lambda i: (0, i)),
        out_shape=jax.ShapeDtypeStruct((rows, n), I32),
        compiler_params=_params("parallel"),
        name="moba_positions",
    )(sel, starts)


ATTN_SUB_ROWS = 128
ATTN_SUBS_PER_STEP = 32
ATTN_STEP_ROWS = ATTN_SUB_ROWS * ATTN_SUBS_PER_STEP


def _pack_bf16_row_pairs(x):
    h = x.shape[0] // 2
    bits = lax.bitcast_convert_type(x.astype(BF16).astype(F32), U32)
    return (bits[:h] >> 16) | (bits[h:] & U32(HIGH_HALF))


def _transpose_u32(x):
    return lax.bitcast_convert_type(lax.bitcast_convert_type(x, I32).T, U32)


def _scores_t(k, q, keep):
    st = lax.dot_general(k, q, (((1,), (1,)), ((), ())), preferred_element_type=F32)
    if keep is not None:
        st = jnp.where(keep, st, NEG_INF)
    m = jnp.max(st, axis=0, keepdims=True)
    p = jnp.exp(st - m)
    return m, p, jnp.sum(p, axis=0, keepdims=True)


def _group_attn_kernel(step_h_ref, step_b_ref, sub_j_ref, nsteps_ref, qs_ref, k_ref, vt_ref, after_ref,
                       part_ref):
    del step_h_ref, step_b_ref, after_ref
    i = pl.program_id(0)
    half = HEAD_DIM // 2

    @pl.when(i < nsteps_ref[0])
    def _():
        for u in range(ATTN_SUBS_PER_STEP):
            j = sub_j_ref[i * ATTN_SUBS_PER_STEP + u]
            rows = pl.ds(u * ATTN_SUB_ROWS, ATTN_SUB_ROWS)
            q = qs_ref[rows, :].astype(BF16)
            m, p, l = _scores_t(k_ref[0, j], q, None)
            ot = jnp.dot(vt_ref[0, j], p.astype(BF16), preferred_element_type=F32) * (1.0 / l)
            lse = lax.bitcast_convert_type(m + jnp.log(l), U32)
            packed_t = jnp.concatenate(
                [_pack_bf16_row_pairs(ot), jnp.broadcast_to(lse, (half, ATTN_SUB_ROWS))], axis=0)
            part_ref[rows, :] = _transpose_u32(packed_t)


def _group_attention(qs, k_blk, vt_blk, step_h, step_b, sub_j, nsteps, nb, after):
    n_steps = step_h.shape[0]
    row_map = lambda i, sh, sb, sj, ns: (jnp.minimum(i, ns[0] - 1), 0)
    kv_map = lambda i, sh, sb, sj, ns: (sh[i], sb[i], 0, 0)
    return pl.pallas_call(
        _group_attn_kernel,
        grid_spec=pltpu.PrefetchScalarGridSpec(
            num_scalar_prefetch=4,
            grid=(n_steps,),
            in_specs=[pl.BlockSpec((ATTN_STEP_ROWS, HEAD_DIM), row_map),
                      pl.BlockSpec((1, nb, MOBA_BLOCK, HEAD_DIM), kv_map),
                      pl.BlockSpec((1, nb, HEAD_DIM, MOBA_BLOCK), kv_map), _ORDER_ONLY],
            out_specs=pl.BlockSpec((ATTN_STEP_ROWS, HEAD_DIM), row_map)),
        out_shape=jax.ShapeDtypeStruct(((n_steps + 1) * ATTN_STEP_ROWS, HEAD_DIM), U32),
        compiler_params=_params("arbitrary"),
        name="moba_group_attention",
    )(step_h, step_b, sub_j, nsteps, qs, k_blk, vt_blk, after)


def _attn_merge_kernel(x_ref, qp_ref, k_ref, vt_ref, pg_ref, sel_ref, wo_ref, lng_ref, lnb_ref,
                       after_ref, o_ref, op_ref):
    del after_ref
    tm = x_ref.shape[0]
    half = HEAD_DIM // 2
    sel = sel_ref[...]
    key = lax.broadcasted_iota(I32, (tm, tm), 0)
    qry = lax.broadcasted_iota(I32, (tm, tm), 1)
    causal = key <= qry
    heads = []
    for hd in range(N_HEADS):
        q = qp_ref[0, hd].astype(BF16)
        m_own, p, l_own = _scores_t(k_ref[hd, 0], q, causal)
        acc = jnp.dot(vt_ref[hd, 0], p.astype(BF16), preferred_element_type=F32)

        outs, lses = [], []
        m_tot = m_own
        for c in range(MOBA_TOPK):
            part_t = _transpose_u32(pg_ref[c, 0, hd])
            lo, hi = _unpack_bf16_pairs(part_t[:half])
            valid = sel[hd * SEL_ROWS + c:hd * SEL_ROWS + c + 1, :] >= 0
            outs.append(jnp.where(valid, jnp.concatenate([lo, hi], axis=0), 0.0))
            lse = jnp.where(valid, lax.bitcast_convert_type(part_t[half:half + 1], F32), NEG_INF)
            lses.append(lse)
            m_tot = jnp.maximum(m_tot, lse)
        w_own = jnp.exp(m_own - m_tot)
        num = acc * w_own
        den = l_own * w_own
        for c in range(MOBA_TOPK):
            w_c = jnp.exp(lses[c] - m_tot)
            num = num + outs[c] * w_c
            den = den + w_c
        heads.append(num * (1.0 / den))
    att = jnp.concatenate(heads, axis=0).T.astype(BF16)
    mix = jnp.dot(att, wo_ref[...], preferred_element_type=F32)
    x1 = _layer_norm(DN_ALPHA * x_ref[...] + mix, lng_ref[...], lnb_ref[...])
    o_ref[...] = x1
    op_ref[...] = _pack_bf16_pairs(x1)


def _attn_merge(x, qp, k_blk, vt_blk, pg, sel, w_o, ln_g, ln_b, after):
    n, d = x.shape
    blk = MOBA_BLOCK
    row = lambda v: v.reshape(1, -1).astype(F32)
    return pl.pallas_call(
        _attn_merge_kernel,
        grid=(n // blk,),
        in_specs=[pl.BlockSpec((blk, d), lambda i: (i, 0)),
                  pl.BlockSpec((1, N_HEADS, blk, HEAD_DIM), lambda i: (i, 0, 0, 0)),
                  pl.BlockSpec((N_HEADS, 1, blk, HEAD_DIM), lambda i: (0, i, 0, 0)),
                  pl.BlockSpec((N_HEADS, 1, HEAD_DIM, blk), lambda i: (0, i, 0, 0)),
                  pl.BlockSpec((MOBA_TOPK, 1, N_HEADS, blk, HEAD_DIM), lambda i: (0, i, 0, 0, 0)),
                  pl.BlockSpec((N_HEADS * SEL_ROWS, blk), lambda i: (0, i)),
                  _full((d, d)), _full((1, d)), _full((1, d)), _ORDER_ONLY],
        out_specs=[pl.BlockSpec((blk, d), lambda i: (i, 0)), pl.BlockSpec((blk, d // 2), lambda i: (i, 0))],
        out_shape=[jax.ShapeDtypeStruct((n, d), F32), jax.ShapeDtypeStruct((n, d // 2), U32)],
        compiler_params=_params("parallel"),
        name="moba_merge",
    )(x, qp, k_blk, vt_blk, pg, sel, w_o.astype(BF16), row(ln_g), row(ln_b), after)


def _moba_layer(streams, seq, w_q, w_o, ln_g, ln_b, cos_t, sin_t):
    staged = [_moba_regroup(x, seq, w_q, cos_t, sin_t, k_mean) for x, _, _, k_mean in streams]
    gathered, order = [], staged[-1][1]
    for (x, k_blk, vt_blk, _), (qp, sel, qs, tables, gather_idx) in zip(streams, staged):
        nbt = x.shape[0] // MOBA_BLOCK
        part = _group_attention(qs, k_blk, vt_blk, *tables, seq // MOBA_BLOCK, order)
        gathered.append(_sc_gather_rows(part, gather_idx).reshape(MOBA_TOPK, nbt, N_HEADS, MOBA_BLOCK, HEAD_DIM))
        order = part
    outs = []
    for (x, k_blk, vt_blk, _), (qp, sel, _, _, _), pg in zip(streams, staged, gathered):
        outs.append(_attn_merge(x, qp, k_blk, vt_blk, pg, sel, w_o, ln_g, ln_b, order))
        order = outs[-1][0]
    return outs


def _moba_regroup(x, seq, w_q, cos_t, sin_t, k_mean):
    n, d = x.shape
    batch = n // seq
    blk = MOBA_BLOCK
    nb = seq // blk
    nbt = n // blk
    chunk = SC_INDEX_CHUNK
    qp, sel, cnt = _queries(x, seq, w_q, cos_t, sin_t, k_mean)

    counts = cnt[:, :, 0].reshape(batch * N_HEADS, nb)
    gpad = (counts + ATTN_SUB_ROWS - 1) // ATTN_SUB_ROWS * ATTN_SUB_ROWS
    seg = jnp.sum(gpad, axis=1)
    seg_pad = (seg + ATTN_STEP_ROWS - 1) // ATTN_STEP_ROWS * ATTN_STEP_ROWS
    seg_end = jnp.cumsum(seg_pad)
    gend = (seg_end - seg_pad)[:, None] + jnp.cumsum(gpad, axis=1)
    gstart = (gend - gpad).astype(I32)
    steps_per_seg = -(-(MOBA_TOPK * seq + nb * (ATTN_SUB_ROWS - 1)) // ATTN_STEP_ROWS)
    n_steps = batch * N_HEADS * steps_per_seg
    step_first = jnp.arange(n_steps, dtype=I32) * ATTN_STEP_ROWS
    step_seg = jnp.minimum(_count_le(seg_end, step_first), batch * N_HEADS - 1)
    sub_first = jnp.arange(n_steps * ATTN_SUBS_PER_STEP, dtype=I32) * ATTN_SUB_ROWS
    sub_grp = jnp.minimum(_count_le(gend.reshape(-1), sub_first), batch * N_HEADS * nb - 1)
    nsteps = (seg_end[-1] // ATTN_STEP_ROWS).astype(I32).reshape(1)
    dump_row = n_steps * ATTN_STEP_ROWS

    pos = _moba_positions(sel, gstart.reshape(batch, N_HEADS * nb, 1), seq, dump_row)
    pos5 = pos.reshape(N_HEADS, SEL_ROWS, nbt, blk // chunk, chunk)[:, :MOBA_TOPK]
    scatter_idx = pos5.transpose(2, 0, 3, 1, 4).reshape(nbt * N_HEADS * (blk // chunk), MOBA_TOPK, chunk)
    gather_idx = pos5.transpose(1, 2, 0, 3, 4).reshape(-1)

    qs = _sc_scatter_rows(qp.reshape(n * N_HEADS, HEAD_DIM), scatter_idx, dump_row + chunk)
    tables = ((step_seg % N_HEADS).astype(I32), (step_seg // N_HEADS).astype(I32),
              (sub_grp % nb).astype(I32), nsteps)
    return qp, sel, qs, tables, gather_idx


def _rope_tables(seq):
    half = ROT_DIM // 2
    inv = ROPE_THETA ** (-jnp.arange(0, ROT_DIM, 2, dtype=F32) / ROT_DIM)
    ang = jnp.arange(seq, dtype=F32)[:, None] * inv[None, :]
    cos, sin = jnp.cos(ang), jnp.sin(ang)
    rest = HEAD_DIM - ROT_DIM
    cos_t = jnp.concatenate([cos, cos, jnp.ones((seq, rest), F32)], axis=1)
    sin_t = jnp.concatenate([-sin, sin, jnp.zeros((seq, rest), F32)], axis=1)
    del half
    return cos_t, sin_t


def kernel(x, p, ln_g, ln_b, a_w_in, a_b_in, a_conv_w, a_conv_b, a_gate_a_w, a_gate_a_b, a_gate_i_w,
           a_gate_i_b, a_lambda, a_w_out, kv_ln_g, kv_ln_b, w_kv, b_w_q, b_w_o, router_w, router_b,
           exp_w_gate, exp_w_up, exp_w_down, sh_w_gate, sh_w_up, sh_w_down, ple_w, ple_gate_w):
    batch, seq, d = x.shape
    xs = [x.reshape(batch * seq, d)] * batch
    p_all = p.reshape(DEPTH * batch, seq, PLE_DIM)
    cos_t, sin_t = _rope_tables(seq)
    kv = None
    for i in range(DEPTH):
        if i < N_A_LAYERS:
            mixed = [_rglru_layer(xb, seq, b if i == 0 else 0, a_w_in[i], a_b_in[i], a_conv_w[i],
                                  a_conv_b[i], a_gate_a_w[i], a_gate_a_b[i], a_gate_i_w[i], a_gate_i_b[i],
                                  a_lambda[i], a_w_out[i], ln_g[i, 0], ln_b[i, 0])
                     for b, xb in enumerate(xs)]
        else:
            if i == N_A_LAYERS:
                kv = []
                for xb in xs:
                    k_blk, vt_blk, k_mean = _shared_kv(xb, seq, kv_ln_g, kv_ln_b, w_kv, cos_t, sin_t)
                    kv.append((k_blk, vt_blk, k_mean.reshape(seq // MOBA_BLOCK, d)))
            j = i - N_A_LAYERS
            mixed = _moba_layer([(xb,) + kvb for xb, kvb in zip(xs, kv)], seq, b_w_q[j], b_w_o[j],
                                ln_g[i, 0], ln_b[i, 0], cos_t, sin_t)
        xs = _moe_layer(mixed, i, router_w[i], router_b[i], exp_w_gate, exp_w_up, exp_w_down,
                        sh_w_gate[i], sh_w_up[i], sh_w_down[i], ln_g[i, 1], ln_b[i, 1],
                        p_all, i * batch, ple_w[i], ple_gate_w[i])
    return jnp.stack(xs, axis=0)
```

```python
import functools

import jax
import jax.numpy as jnp
from jax import lax
from jax.experimental import pallas as pl
from jax.experimental.pallas import tpu as pltpu
from jax.experimental.pallas import tpu_sc as plsc

F32 = jnp.float32
BF16 = jnp.bfloat16
I32 = jnp.int32
U32 = jnp.uint32
HIGH_HALF = 0xFFFF0000

SC_CORES = 2
SC_SUBCORES = 16
SC_WORKERS = SC_CORES * SC_SUBCORES
SC_LANES = 16
SC_INDEX_CHUNK = 128
SC_STAGE_BYTES = 256 * 1024

D_MODEL = 1024
DEPTH = 4
N_A_LAYERS = DEPTH // 2
D_RNN = D_MODEL
LRU_BLOCKS = 4
LRU_BLOCK_W = D_RNN // LRU_BLOCKS
CONV_W = 4
LRU_C = 8.0
N_HEADS = 8
HEAD_DIM = D_MODEL // N_HEADS
ROT_DIM = HEAD_DIM // 4
ROPE_THETA = 500000.0
MOBA_BLOCK = 256
MOBA_TOPK = 3
N_EXPERTS = 64
N_GROUPS = 8
GROUP_SIZE = N_EXPERTS // N_GROUPS
TOPK_GROUPS = 4
TOPK_EXPERTS = 8
D_EXPERT = 256
D_SHARED = 256
ROUTED_SCALE = 2.5
PLE_DIM = 256
DN_ALPHA = (2 * DEPTH) ** 0.25
LN_EPS = 1e-5

V7X_VMEM_LIMIT_BYTES = 56 * 1024 * 1024

MIXER_ROWS = 256
TOKEN_ROWS = 512
EXPERT_ROWS = 2304
NEG_INF = float("-inf")


def _params(*sem):
    return pltpu.CompilerParams(dimension_semantics=sem, vmem_limit_bytes=V7X_VMEM_LIMIT_BYTES)


def _layer_norm(z, g, b):
    mu = jnp.mean(z, axis=-1, keepdims=True)
    zc = z - mu
    var = jnp.mean(zc * zc, axis=-1, keepdims=True)
    return zc * lax.rsqrt(var + LN_EPS) * g + b


def _silu(x):
    return x * jax.nn.sigmoid(x)


def _gelu_tanh(x):
    return x * jax.nn.sigmoid(x * (1.5957691216057308 + 0.07135481627159855 * (x * x)))


def _full(shape):
    return pl.BlockSpec(shape, lambda *_: (0,) * len(shape))


def _count_le(ends, values):
    return jnp.sum((ends[None, :] <= values[:, None]).astype(I32), axis=1)


def _pack_bf16_pairs(x):
    w = x.shape[1] // 2
    bits = lax.bitcast_convert_type(x.astype(BF16).astype(F32), U32)
    return (bits[:, :w] >> 16) | (bits[:, w:] & U32(HIGH_HALF))


def _unpack_bf16_pairs(u):
    lo = lax.bitcast_convert_type(u << 16, F32)
    hi = lax.bitcast_convert_type(u & U32(HIGH_HALF), F32)
    return lo, hi


def _sc_mesh():
    return plsc.VectorSubcoreMesh(core_axis_name="c", subcore_axis_name="s")


def _sc_worker_id():
    return lax.axis_index("s") * SC_CORES + lax.axis_index("c")


def _sc_chunks_per_step(chunks_per_worker, row_words):
    g = max(1, SC_STAGE_BYTES // (SC_INDEX_CHUNK * row_words * 4))
    while chunks_per_worker % g:
        g -= 1
    return g


def _sc_gather_rows(table, idx):
    b = idx.shape[0]
    w = table.shape[1]
    chunk = SC_INDEX_CHUNK
    chunks_per_worker = b // (SC_WORKERS * chunk)
    assert chunks_per_worker * SC_WORKERS * chunk == b
    g = _sc_chunks_per_step(chunks_per_worker, w)

    @functools.partial(
        pl.kernel, mesh=_sc_mesh(), out_type=jax.ShapeDtypeStruct((b, w), table.dtype),
        scratch_types=[pltpu.VMEM((g, chunk), I32), pltpu.VMEM((g * chunk, w), table.dtype),
                       pltpu.SemaphoreType.DMA])
    def gather(table_hbm, idx_hbm, out_hbm, idx_v, rows_v, sem):
        first = _sc_worker_id() * chunks_per_worker

        @pl.loop(0, chunks_per_worker // g)
        def _(j):
            c0 = first + j * g
            pltpu.sync_copy(idx_hbm.at[pl.ds(c0, g)], idx_v)
            copies = [pltpu.async_copy(table_hbm.at[idx_v.at[q]], rows_v.at[pl.ds(q * chunk, chunk)], sem)
                      for q in range(g)]
            for cp in copies:
                cp.wait()
            pltpu.sync_copy(rows_v, out_hbm.at[pl.ds(c0 * chunk, g * chunk)])

    return gather(table, idx.reshape(b // chunk, chunk))


def _sc_scatter_rows(src, idx, out_rows):
    n, w = src.shape
    chunk = SC_INDEX_CHUNK
    fan = idx.shape[1]
    chunks_per_worker = n // (SC_WORKERS * chunk)
    assert idx.shape == (n // chunk, fan, chunk) and chunks_per_worker * SC_WORKERS * chunk == n
    g = _sc_chunks_per_step(chunks_per_worker, w)

    @functools.partial(
        pl.kernel, mesh=_sc_mesh(), out_type=jax.ShapeDtypeStruct((out_rows, w), src.dtype),
        scratch_types=[pltpu.VMEM((g, fan, chunk), I32), pltpu.VMEM((g * chunk, w), src.dtype),
                       pltpu.SemaphoreType.DMA])
    def scatter(src_hbm, idx_hbm, out_hbm, idx_v, rows_v, sem):
        first = _sc_worker_id() * chunks_per_worker

        @pl.loop(0, chunks_per_worker // g)
        def _(j):
            c0 = first + j * g
            pltpu.sync_copy(src_hbm.at[pl.ds(c0 * chunk, g * chunk)], rows_v)
            pltpu.sync_copy(idx_hbm.at[pl.ds(c0, g)], idx_v)
            copies = [pltpu.async_copy(rows_v.at[pl.ds(q * chunk, chunk)], out_hbm.at[idx_v.at[q, f]], sem)
                      for q in range(g) for f in range(fan)]
            for cp in copies:
                cp.wait()

    return scatter(src, idx)


def _sc_weighted_gather(table, idx, wrep, fan):
    nf = idx.shape[0]
    n = nf // fan
    w = table.shape[1]
    chunk = SC_INDEX_CHUNK
    lanes = SC_LANES
    tokens = chunk // fan
    chunks_per_worker = nf // (SC_WORKERS * chunk)
    assert chunks_per_worker * SC_WORKERS * chunk == nf and w % lanes == 0 and wrep.shape == (n, fan * lanes)

    @functools.partial(
        pl.kernel, mesh=_sc_mesh(), out_type=jax.ShapeDtypeStruct((n, 2 * w), F32),
        scratch_types=[pltpu.VMEM((chunk,), I32), pltpu.VMEM((chunk, w), U32),
                       pltpu.VMEM((tokens, fan * lanes), F32),
                       pltpu.VMEM((tokens, 2 * w), F32), pltpu.SemaphoreType.DMA],
        compiler_params=pltpu.CompilerParams(needs_layout_passes=False))
    def weighted_gather(table_hbm, idx_hbm, w_hbm, out_hbm, idx_v, rows_v, w_v, out_v, sem):
        first = _sc_worker_id() * chunks_per_worker

        @pl.loop(0, chunks_per_worker)
        def _(j):
            c0 = first + j
            pltpu.sync_copy(idx_hbm.at[pl.ds(c0 * chunk, chunk)], idx_v)
            gather = pltpu.async_copy(table_hbm.at[idx_v], rows_v, sem)
            pltpu.sync_copy(w_hbm.at[pl.ds(c0 * tokens, tokens)], w_v)
            gather.wait()

            @pl.loop(0, tokens)
            def _(t):
                weights = [w_v[t, pl.ds(q * lanes, lanes)] for q in range(fan)]

                @pl.loop(0, w, step=lanes)
                def _(c):
                    acc_lo = jnp.zeros((lanes,), F32)
                    acc_hi = jnp.zeros((lanes,), F32)
                    for q in range(fan):
                        u = rows_v[t * fan + q, pl.ds(c, lanes)]
                        acc_lo = acc_lo + weights[q] * lax.bitcast_convert_type(u << 16, F32)
                        acc_hi = acc_hi + weights[q] * lax.bitcast_convert_type(u & U32(HIGH_HALF), F32)
                    out_v[t, pl.ds(c, lanes)] = acc_lo
                    out_v[t, pl.ds(w + c, lanes)] = acc_hi

            pltpu.sync_copy(out_v, out_hbm.at[pl.ds(c0 * tokens, tokens)])

    return weighted_gather(table, idx, wrep)


def _rglru_kernel(x_ref, win_ref, bin_ref, cw_ref, cb_ref, gaw_ref, gab_ref, giw_ref, gib_ref,
                  lam_ref, wout_ref, lng_ref, lnb_ref, o_ref, op_ref, tail_ref, h_ref, *, tiles_per_seq):
    i = pl.program_id(0)
    tm = x_ref.shape[0]

    @pl.when(i % tiles_per_seq == 0)
    def _():
        tail_ref[...] = jnp.zeros_like(tail_ref)
        h_ref[...] = jnp.zeros_like(h_ref)

    x = x_ref[...]
    xy = jnp.dot(x.astype(BF16), win_ref[...], preferred_element_type=F32) + bin_ref[...]
    xb = xy[:, :D_RNN]
    y = _gelu_tanh(xy[:, D_RNN:])

    tail = tail_ref[...]
    row8 = lax.broadcasted_iota(I32, (8, 1), 0)
    xc = cb_ref[...] + xb * cw_ref[CONV_W - 1:CONV_W, :]
    for d in range(1, CONV_W):
        rolled = pltpu.roll(xb, d, 0)
        head = jnp.where(row8 < d, pltpu.roll(tail, d, 0), rolled[:8])
        shifted = jnp.concatenate([head, rolled[8:]], axis=0)
        xc = xc + shifted * cw_ref[CONV_W - 1 - d:CONV_W - d, :]
    tail_ref[...] = xb[tm - 8:, :]

    r_parts, i_parts = [], []
    for n in range(LRU_BLOCKS):
        xg = xc[:, n * LRU_BLOCK_W:(n + 1) * LRU_BLOCK_W].astype(BF16)
        r_parts.append(jnp.dot(xg, gaw_ref[n], preferred_element_type=F32))
        i_parts.append(jnp.dot(xg, giw_ref[n], preferred_element_type=F32))
    r = jax.nn.sigmoid(jnp.concatenate(r_parts, axis=1) + gab_ref[...])
    ig = jax.nn.sigmoid(jnp.concatenate(i_parts, axis=1) + gib_ref[...])

    lam = lam_ref[...]
    softplus_neg_lam = jnp.maximum(-lam, 0.0) + jnp.log1p(jnp.exp(-jnp.abs(lam)))
    log_a = (-LRU_C * r) * softplus_neg_lam
    a = jnp.exp(log_a)
    u = jnp.sqrt(1.0 - a * a) * (ig * xc)

    n_grp = tm // 8
    sub = lax.broadcasted_iota(I32, (1, 8, 1), 1)
    acc_a = a.reshape(n_grp, 8, D_RNN)
    acc_h = u.reshape(n_grp, 8, D_RNN)
    for d in (1, 2, 4):
        keep = sub >= d
        sh_a = pltpu.roll(acc_a, d, 1)
        sh_h = pltpu.roll(acc_h, d, 1)
        acc_h = jnp.where(keep, acc_a * sh_h + acc_h, acc_h)
        acc_a = jnp.where(keep, acc_a * sh_a, acc_a)
    state = h_ref[...]
    groups = []
    for g in range(n_grp):
        hg = acc_h[g] + acc_a[g] * state
        state = hg[7:8, :]
        groups.append(hg)
    h = jnp.concatenate(groups, axis=0)
    h_ref[...] = state

    mix = jnp.dot((h * y).astype(BF16), wout_ref[...], preferred_element_type=F32)
    x1 = _layer_norm(DN_ALPHA * x + mix, lng_ref[...], lnb_ref[...])
    o_ref[...] = x1
    op_ref[...] = _pack_bf16_pairs(x1)


def _rglru_layer(x, seq, stream, w_in, b_in, conv_w, conv_b, ga_w, ga_b, gi_w, gi_b, lam, w_out,
                 ln_g, ln_b):
    n, d = seq, x.shape[1]
    tm = min(MIXER_ROWS, seq)
    first_tile = stream * (seq // tm)
    row = lambda v: v.reshape(1, -1).astype(F32)
    return pl.pallas_call(
        functools.partial(_rglru_kernel, tiles_per_seq=seq // tm),
        grid=(n // tm,),
        in_specs=[pl.BlockSpec((tm, d), lambda i: (i + first_tile, 0)),
                  _full((d, 2 * D_RNN)), _full((1, 2 * D_RNN)),
                  _full((CONV_W, D_RNN)), _full((1, D_RNN)),
                  _full((LRU_BLOCKS, LRU_BLOCK_W, LRU_BLOCK_W)), _full((1, D_RNN)),
                  _full((LRU_BLOCKS, LRU_BLOCK_W, LRU_BLOCK_W)), _full((1, D_RNN)),
                  _full((1, D_RNN)), _full((D_RNN, d)), _full((1, d)), _full((1, d))],
        out_specs=[pl.BlockSpec((tm, d), lambda i: (i, 0)), pl.BlockSpec((tm, d // 2), lambda i: (i, 0))],
        out_shape=[jax.ShapeDtypeStruct((n, d), F32), jax.ShapeDtypeStruct((n, d // 2), U32)],
        scratch_shapes=[pltpu.VMEM((8, D_RNN), F32), pltpu.VMEM((1, D_RNN), F32)],
        compiler_params=_params("arbitrary"),
        name="rglru_mixer",
    )(x, w_in.astype(BF16), row(b_in), conv_w, row(conv_b), ga_w.astype(BF16), row(ga_b),
      gi_w.astype(BF16), row(gi_b), row(lam), w_out.astype(BF16), row(ln_g), row(ln_b))


def _first_argmax(cur, idx, size, axis):
    m = jnp.max(cur, axis=axis, keepdims=True)
    first = jnp.min(jnp.where(cur == m, idx, size), axis=axis, keepdims=True)
    return m, first


def _router_kernel(x_ref, rw_ref, rb_ref, e_ref, w_ref, r_ref, cnt_ref, carry_ref):
    i = pl.program_id(0)
    tm = x_ref.shape[0]

    @pl.when(i == 0)
    def _():
        carry_ref[...] = jnp.zeros_like(carry_ref)

    x = x_ref[...]
    x_hi = x.astype(BF16)
    x_lo = (x - x_hi.astype(F32)).astype(BF16)
    rw = rw_ref[...]
    hi_both = jnp.dot(x_hi, rw, preferred_element_type=F32)
    logits = (hi_both[:, :128] + hi_both[:, 128:]
              + jnp.dot(x_lo, rw[:, :128], preferred_element_type=F32))
    scores = jax.nn.sigmoid(logits.T[:N_EXPERTS, :])
    choice = scores + rb_ref[...]

    c3 = choice.reshape(N_GROUPS, GROUP_SIZE, tm)
    in_grp = lax.broadcasted_iota(I32, c3.shape, 1)
    m1, i1 = _first_argmax(c3, in_grp, GROUP_SIZE, 1)
    m2 = jnp.max(jnp.where(in_grp == i1, NEG_INF, c3), axis=1, keepdims=True)
    grp_score = (m1 + m2)[:, 0, :]

    grp_id = lax.broadcasted_iota(I32, grp_score.shape, 0)
    grp_sel = jnp.zeros(grp_score.shape, jnp.bool_)
    cur = grp_score
    for _ in range(TOPK_GROUPS):
        _, gi = _first_argmax(cur, grp_id, N_GROUPS, 0)
        hit = grp_id == gi
        grp_sel = jnp.logical_or(grp_sel, hit)
        cur = jnp.where(hit, NEG_INF, cur)

    cur = jnp.where(grp_sel[:, None, :], c3, NEG_INF).reshape(N_EXPERTS, tm)
    exp_id = lax.broadcasted_iota(I32, cur.shape, 0)
    sel = jnp.zeros(cur.shape, F32)
    e_rows, s_rows = [], []
    for _ in range(TOPK_EXPERTS):
        _, ei = _first_argmax(cur, exp_id, N_EXPERTS, 0)
        hit = exp_id == ei
        e_rows.append(ei)
        s_rows.append(jnp.sum(jnp.where(hit, scores, 0.0), axis=0, keepdims=True))
        sel = jnp.where(hit, 1.0, sel)
        cur = jnp.where(hit, NEG_INF, cur)
    e_top = jnp.concatenate(e_rows, axis=0)
    s_top = jnp.concatenate(s_rows, axis=0)
    w_top = s_top / jnp.sum(s_top, axis=0, keepdims=True) * ROUTED_SCALE
    w_ref[...] = jnp.broadcast_to(w_top[:, None, :], (TOPK_EXPERTS, SC_LANES, tm)).reshape(
        TOPK_EXPERTS * SC_LANES, tm).T
    e_ref[...] = e_top

    t_row = lax.broadcasted_iota(I32, (tm, tm), 0)
    t_col = lax.broadcasted_iota(I32, (tm, tm), 1)
    before = (t_row < t_col).astype(BF16)
    cum = jnp.dot(sel.astype(BF16), before, preferred_element_type=F32) + carry_ref[...]
    r_rows = [jnp.sum(jnp.where(exp_id == e_rows[k], cum, 0.0), axis=0, keepdims=True)
              for k in range(TOPK_EXPERTS)]
    r_ref[...] = jnp.concatenate(r_rows, axis=0).astype(I32)
    carry_ref[...] = carry_ref[...] + jnp.sum(sel, axis=1, keepdims=True)
    cnt_ref[...] = jnp.broadcast_to(carry_ref[...], cnt_ref.shape).astype(I32)


def _router(x, router_w, router_b):
    n, d = x.shape
    tm = min(TOKEN_ROWS, n)
    rw = jnp.pad(router_w, ((0, 0), (0, 128 - N_EXPERTS)))
    rw_hi = rw.astype(BF16)
    rw = jnp.concatenate([rw_hi, (rw - rw_hi.astype(F32)).astype(BF16)], axis=1)
    k = TOPK_EXPERTS
    tok_spec = pl.BlockSpec((k, tm), lambda i: (0, i))
    return pl.pallas_call(
        _router_kernel,
        grid=(n // tm,),
        in_specs=[pl.BlockSpec((tm, d), lambda i: (i, 0)), _full((d, 256)), _full((N_EXPERTS, 1))],
        out_specs=[tok_spec, pl.BlockSpec((tm, k * SC_LANES), lambda i: (i, 0)), tok_spec,
                   _full((N_EXPERTS, 128))],
        out_shape=[jax.ShapeDtypeStruct((k, n), I32), jax.ShapeDtypeStruct((n, k * SC_LANES), F32),
                   jax.ShapeDtypeStruct((k, n), I32), jax.ShapeDtypeStruct((N_EXPERTS, 128), I32)],
        scratch_shapes=[pltpu.VMEM((N_EXPERTS, 1), F32)],
        compiler_params=_params("arbitrary"),
        name="moe_router",
    )(x, rw, router_b.reshape(N_EXPERTS, 1).astype(F32))


def _expert_kernel(blk_e_ref, nblk_ref, first_ref, slot_ref, next_ref, xs_ref, wg_hbm, wu_hbm, wd_hbm,
                   after_ref, ys_ref, wg_f, wu_f, wd_f, wg_s, wu_s, wd_s, sem, *, layer):
    del after_ref
    i = pl.program_id(0)

    def fetch(expert, slot):
        return (pltpu.make_async_copy(wg_hbm.at[layer, expert], wg_f.at[slot], sem.at[slot, 0]),
                pltpu.make_async_copy(wu_hbm.at[layer, expert], wu_f.at[slot], sem.at[slot, 1]),
                pltpu.make_async_copy(wd_hbm.at[layer, expert], wd_f.at[slot], sem.at[slot, 2]))

    @pl.when(i < nblk_ref[0])
    def _():
        slot = slot_ref[i]

        @pl.when(i == 0)
        def _():
            for copy in fetch(blk_e_ref[0], 0):
                copy.start()

        @pl.when(first_ref[i] == 1)
        def _():
            for copy in fetch(blk_e_ref[i], slot):
                copy.wait()
            wg_s[...] = wg_f[slot].astype(BF16)
            wu_s[...] = wu_f[slot].astype(BF16)
            wd_s[...] = wd_f[slot].astype(BF16)

            @pl.when(next_ref[i] >= 0)
            def _():
                for copy in fetch(next_ref[i], 1 - slot):
                    copy.start()

        lo, hi = _unpack_bf16_pairs(xs_ref[...])
        xs = jnp.concatenate([lo.astype(BF16), hi.astype(BF16)], axis=1)
        g = jnp.dot(xs, wg_s[...], preferred_element_type=F32)
        u = jnp.dot(xs, wu_s[...], preferred_element_type=F32)
        hdn = (_silu(g) * u).astype(BF16)
        ys_ref[...] = _pack_bf16_pairs(jnp.dot(hdn, wd_s[...], preferred_element_type=F32))


_ORDER_ONLY = pl.BlockSpec(memory_space=pl.ANY)


def _experts(xs, blk_e, nblk, layer, w_gate, w_up, w_down, after):
    rows, half = xs.shape
    d = 2 * half
    m = EXPERT_ROWS
    n_blocks = rows // m
    idx = jnp.arange(n_blocks, dtype=I32)
    used = idx < nblk[0]
    first = jnp.logical_and(used, jnp.logical_or(idx == 0, blk_e != jnp.roll(blk_e, 1)))
    slot = (jnp.cumsum(first.astype(I32)) - 1) % 2
    larger = jnp.logical_and(blk_e[None, :] > blk_e[:, None], used[None, :])
    next_e = jnp.min(jnp.where(larger, blk_e[None, :], N_EXPERTS), axis=1)
    next_e = jnp.where(next_e < N_EXPERTS, next_e, -1)
    row_map = lambda i, be, nb, fi, sl, nx: (jnp.minimum(i, nb[0] - 1), 0)
    hbm = pl.BlockSpec(memory_space=pl.ANY)
    return pl.pallas_call(
        functools.partial(_expert_kernel, layer=layer),
        grid_spec=pltpu.PrefetchScalarGridSpec(
            num_scalar_prefetch=5,
            grid=(n_blocks,),
            in_specs=[pl.BlockSpec((m, half), row_map), hbm, hbm, hbm, _ORDER_ONLY],
            out_specs=pl.BlockSpec((m, half), row_map),
            scratch_shapes=[pltpu.VMEM((2, d, D_EXPERT), F32), pltpu.VMEM((2, d, D_EXPERT), F32),
                            pltpu.VMEM((2, D_EXPERT, d), F32),
                            pltpu.VMEM((d, D_EXPERT), BF16), pltpu.VMEM((d, D_EXPERT), BF16),
                            pltpu.VMEM((D_EXPERT, d), BF16), pltpu.SemaphoreType.DMA((2, 3))]),
        out_shape=jax.ShapeDtypeStruct((rows, half), U32),
        compiler_params=_params("arbitrary"),
        name="moe_experts",
    )(blk_e, nblk, first.astype(I32), slot.astype(I32), next_e.astype(I32), xs, w_gate, w_up, w_down, after)


def _combine_kernel(x_ref, moe_ref, sg_ref, su_ref, sd_ref, lng_ref, lnb_ref,
                    p_ref, pw_ref, pg_ref, after_ref, o_ref):
    del after_ref
    x = x_ref[...]
    moe = moe_ref[...]
    xb = x.astype(BF16)
    hdn = _silu(jnp.dot(xb, sg_ref[...], preferred_element_type=F32)) * \
        jnp.dot(xb, su_ref[...], preferred_element_type=F32)
    shared = jnp.dot(hdn.astype(BF16), sd_ref[...], preferred_element_type=F32)
    x2 = _layer_norm(DN_ALPHA * x + (moe + shared), lng_ref[...], lnb_ref[...])
    emb = jnp.dot(p_ref[0].astype(BF16), pw_ref[...], preferred_element_type=F32)
    gate = jax.nn.sigmoid(jnp.dot(x2.astype(BF16), pg_ref[...], preferred_element_type=F32))
    o_ref[...] = x2 + emb * gate


def _combine(x, moe, s_gate, s_up, s_down, ln_g, ln_b, p_all, p_index, ple_w, ple_gate_w, after):
    n, d = x.shape
    tm = min(TOKEN_ROWS // 2, n)
    row = lambda v: v.reshape(1, -1).astype(F32)
    return pl.pallas_call(
        _combine_kernel,
        grid=(n // tm,),
        in_specs=[pl.BlockSpec((tm, d), lambda i: (i, 0)),
                  pl.BlockSpec((tm, d), lambda i: (i, 0)),
                  _full((d, D_SHARED)), _full((d, D_SHARED)), _full((D_SHARED, d)),
                  _full((1, d)), _full((1, d)),
                  pl.BlockSpec((1, tm, PLE_DIM), lambda i: (p_index, i, 0)),
                  _full((PLE_DIM, d)), _full((d, d)), _ORDER_ONLY],
        out_specs=pl.BlockSpec((tm, d), lambda i: (i, 0)),
        out_shape=jax.ShapeDtypeStruct((n, d), F32),
        compiler_params=_params("parallel"),
        name="moe_combine",
    )(x, moe, s_gate.astype(BF16), s_up.astype(BF16), s_down.astype(BF16), row(ln_g), row(ln_b),
      p_all, ple_w.astype(BF16), ple_gate_w.astype(BF16), after)


def _positions_kernel(e_ref, r_ref, start_ref, pos_ref):
    e = e_ref[...]
    start = start_ref[...]
    grp_id = lax.broadcasted_iota(I32, (start.shape[0], e.shape[1]), 0)
    rows = [jnp.sum(jnp.where(grp_id == e[k:k + 1], start, 0), axis=0, keepdims=True)
            for k in range(e.shape[0])]
    pos_ref[...] = jnp.concatenate(rows, axis=0) + r_ref[...]


def _positions(e_idx, rank, starts):
    k, n = e_idx.shape
    tm = min(4 * TOKEN_ROWS, n)
    groups = starts.shape[0]
    tok_spec = pl.BlockSpec((k, tm), lambda i: (0, i))
    return pl.pallas_call(
        _positions_kernel,
        grid=(n // tm,),
        in_specs=[tok_spec, tok_spec, _full((groups, 1))],
        out_specs=tok_spec,
        out_shape=jax.ShapeDtypeStruct((k, n), I32),
        compiler_params=_params("parallel"),
        name="group_positions",
    )(e_idx, rank, starts.reshape(groups, 1))


def _moe_layer(streams, layer, router_w, router_b, w_gate, w_up, w_down, s_gate, s_up, s_down,
               ln_g, ln_b, p_all, p_index, ple_w, ple_gate_w):
    m = EXPERT_ROWS
    k = TOPK_EXPERTS
    chunk = SC_INDEX_CHUNK
    routed = []
    for x, xp in streams:
        n, d = x.shape
        e_idx, wgt, rank, cnt = _router(x, router_w, router_b)
        counts = cnt[:, 0]
        padded = (counts + m - 1) // m * m
        pends = jnp.cumsum(padded)
        n_blocks = (n * k) // m + N_EXPERTS
        blk_first = jnp.arange(n_blocks, dtype=I32) * m
        blk_e = jnp.minimum(_count_le(pends, blk_first), N_EXPERTS - 1)
        nblk = (pends[-1] // m).astype(I32).reshape(1)
        pos = _positions(e_idx, rank, (pends - padded).astype(I32))
        pos_chunks = pos.reshape(k, n // chunk, chunk).transpose(1, 0, 2)
        xs = _sc_scatter_rows(xp, pos_chunks, n_blocks * m)
        routed.append((xs, blk_e, nblk, pos, wgt))
    gathered, order = [], routed[-1][3]
    for (x, _), (xs, blk_e, nblk, pos, wgt) in zip(streams, routed):
        n, d = x.shape
        ys = _experts(xs, blk_e, nblk, layer, w_gate, w_up, w_down, order)
        gathered.append(_sc_weighted_gather(ys, pos.T.reshape(-1), wgt, k))
        order = ys
    outs = []
    for s, ((x, _), moe) in enumerate(zip(streams, gathered)):
        outs.append(_combine(x, moe, s_gate, s_up, s_down, ln_g, ln_b, p_all, p_index + s,
                             ple_w, ple_gate_w, order))
        order = outs[-1]
    return outs


def _rope(t, cos_t, sin_t):
    half = ROT_DIM // 2
    width = t.shape[1]
    lane = lax.broadcasted_iota(I32, (1, width), 1) % HEAD_DIM
    partner = jnp.where(lane < half, pltpu.roll(t, width - half, 1), pltpu.roll(t, half, 1))
    cos_f = jnp.concatenate([cos_t] * N_HEADS, axis=1)
    sin_f = jnp.concatenate([sin_t] * N_HEADS, axis=1)
    return t * cos_f + partner * sin_f


def _kv_kernel(x_ref, g_ref, b_ref, wkv_ref, cos_ref, sin_ref, k_ref, vt_ref, km_ref):
    h = _layer_norm(x_ref[...], g_ref[...], b_ref[...])
    kv = jnp.dot(h.astype(BF16), wkv_ref[...], preferred_element_type=F32)
    k = _rope(kv[:, :D_MODEL], cos_ref[...], sin_ref[...])
    vt = kv[:, D_MODEL:].T
    km_ref[0] = jnp.mean(k, axis=0, keepdims=True)
    for hd in range(N_HEADS):
        k_ref[hd, 0] = k[:, hd * HEAD_DIM:(hd + 1) * HEAD_DIM].astype(BF16)
        vt_ref[hd, 0] = vt[hd * HEAD_DIM:(hd + 1) * HEAD_DIM, :].astype(BF16)


def _shared_kv(x, seq, ln_g, ln_b, w_kv, cos_t, sin_t):
    n, d = x.shape
    blk = MOBA_BLOCK
    nbt = n // blk
    spb = seq // blk
    row = lambda v: v.reshape(1, -1).astype(F32)
    return pl.pallas_call(
        _kv_kernel,
        grid=(nbt,),
        in_specs=[pl.BlockSpec((blk, d), lambda i: (i, 0)), _full((1, d)), _full((1, d)),
                  _full((d, 2 * d)),
                  pl.BlockSpec((blk, HEAD_DIM), lambda i: (i % spb, 0)),
                  pl.BlockSpec((blk, HEAD_DIM), lambda i: (i % spb, 0))],
        out_specs=[pl.BlockSpec((N_HEADS, 1, blk, HEAD_DIM), lambda i: (0, i, 0, 0)),
                   pl.BlockSpec((N_HEADS, 1, HEAD_DIM, blk), lambda i: (0, i, 0, 0)),
                   pl.BlockSpec((1, 1, d), lambda i: (i, 0, 0))],
        out_shape=[jax.ShapeDtypeStruct((N_HEADS, nbt, blk, HEAD_DIM), BF16),
                   jax.ShapeDtypeStruct((N_HEADS, nbt, HEAD_DIM, blk), BF16),
                   jax.ShapeDtypeStruct((nbt, 1, d), F32)],
        compiler_params=_params("parallel"),
        name="shared_kv",
    )(x, row(ln_g), row(ln_b), w_kv.astype(BF16), cos_t, sin_t)


SEL_ROWS = 8


def _query_kernel(x_ref, wq_ref, cos_ref, sin_ref, km_ref, qp_ref, sel_ref, cnt_ref, carry_ref,
                  *, blocks_per_seq):
    i = pl.program_id(0)
    tm = x_ref.shape[0]
    nb = km_ref.shape[0]
    own = i % blocks_per_seq

    @pl.when(own == 0)
    def _():
        carry_ref[...] = jnp.zeros_like(carry_ref)

    q = jnp.dot(x_ref[...].astype(BF16), wq_ref[...], preferred_element_type=F32)
    q = _rope(q, cos_ref[...], sin_ref[...]) * (HEAD_DIM ** -0.5)

    t_row = lax.broadcasted_iota(I32, (tm, tm), 0)
    t_col = lax.broadcasted_iota(I32, (tm, tm), 1)
    before = (t_row < t_col).astype(BF16)
    km = km_ref[...]
    km_hi = km.astype(BF16)
    km_both = jnp.concatenate([km_hi, (km - km_hi.astype(F32)).astype(BF16)], axis=0)
    q_hi = q.astype(BF16)
    q_lo = (q - q_hi.astype(F32)).astype(BF16)
    nt = (((1,), (1,)), ((), ()))
    gates = []
    for hd in range(N_HEADS):
        lo, hi = hd * HEAD_DIM, (hd + 1) * HEAD_DIM
        qp_ref[0, hd] = q[:, lo:hi]
        both = lax.dot_general(km_both[:, lo:hi], q_hi[:, lo:hi], nt, preferred_element_type=F32)
        gates.append(both[:nb] + both[nb:]
                     + lax.dot_general(km_hi[:, lo:hi], q_lo[:, lo:hi], nt, preferred_element_type=F32))

    gate = jnp.concatenate(gates, axis=0).reshape(N_HEADS, nb, tm)
    blk_id = lax.broadcasted_iota(I32, (N_HEADS, nb, tm), 1)
    cur = jnp.where(blk_id < own, gate, NEG_INF)
    sel_rows = []
    chosen = jnp.zeros((N_HEADS, nb, tm), F32)
    for _ in range(MOBA_TOPK):
        m, bi = _first_argmax(cur, blk_id, nb, 1)
        valid = m > NEG_INF
        hit = jnp.logical_and(blk_id == bi, valid)
        sel_rows.append(jnp.where(valid, bi, -1))
        chosen = jnp.where(hit, 1.0, chosen)
        cur = jnp.where(blk_id == bi, NEG_INF, cur)
    chosen2 = chosen.reshape(N_HEADS * nb, tm)
    carry = carry_ref[...]
    cum = (jnp.dot(chosen2.astype(BF16), before, preferred_element_type=F32) + carry).reshape(N_HEADS, nb, tm)
    rank_rows = [jnp.sum(jnp.where(blk_id == s, cum, 0.0), axis=1, keepdims=True).astype(I32)
                 for s in sel_rows]
    carry = carry + jnp.sum(chosen2, axis=1, keepdims=True)
    carry_ref[...] = carry
    pad = jnp.zeros((N_HEADS, SEL_ROWS - 2 * MOBA_TOPK, tm), I32)
    table = jnp.concatenate(sel_rows + rank_rows + [pad], axis=1)
    sel_ref[...] = table.reshape(N_HEADS * SEL_ROWS, tm)
    cnt_ref[0] = jnp.broadcast_to(carry, cnt_ref.shape[1:]).astype(I32)


def _queries(x, seq, w_q, cos_t, sin_t, k_mean):
    n, d = x.shape
    blk = MOBA_BLOCK
    nb = seq // blk
    return pl.pallas_call(
        functools.partial(_query_kernel, blocks_per_seq=nb),
        grid=(n // blk,),
        in_specs=[pl.BlockSpec((blk, d), lambda i: (i, 0)), _full((d, d)),
                  pl.BlockSpec((blk, HEAD_DIM), lambda i: (i % nb, 0)),
                  pl.BlockSpec((blk, HEAD_DIM), lambda i: (i % nb, 0)),
                  pl.BlockSpec((nb, d), lambda i: (i // nb, 0))],
        out_specs=[pl.BlockSpec((1, N_HEADS, blk, HEAD_DIM), lambda i: (i, 0, 0, 0)),
                   pl.BlockSpec((N_HEADS * SEL_ROWS, blk), lambda i: (0, i)),
                   pl.BlockSpec((1, N_HEADS * nb, 128), lambda i: (i // nb, 0, 0))],
        out_shape=[jax.ShapeDtypeStruct((n // blk, N_HEADS, blk, HEAD_DIM), F32),
                   jax.ShapeDtypeStruct((N_HEADS * SEL_ROWS, n), I32),
                   jax.ShapeDtypeStruct((n // seq, N_HEADS * nb, 128), I32)],
        scratch_shapes=[pltpu.VMEM((N_HEADS * nb, 1), F32)],
        compiler_params=_params("arbitrary"),
        name="moba_queries",
    )(x, w_q.astype(BF16), cos_t, sin_t, k_mean)


def _moba_positions_kernel(sel_ref, start_ref, pos_ref, *, dump_row):
    tm = sel_ref.shape[1]
    nb = start_ref.shape[1] // N_HEADS
    blk_id = lax.broadcasted_iota(I32, (nb, tm), 0)
    dump = dump_row + lax.broadcasted_iota(I32, (1, tm), 1) % SC_INDEX_CHUNK
    rows = []
    for hd in range(N_HEADS):
        start = start_ref[0, hd * nb:(hd + 1) * nb, :]
        for s in range(MOBA_TOPK):
            sel = sel_ref[hd * SEL_ROWS + s:hd * SEL_ROWS + s + 1, :]
            rank = sel_ref[hd * SEL_ROWS + MOBA_TOPK + s:hd * SEL_ROWS + MOBA_TOPK + s + 1, :]
            base = jnp.sum(jnp.where(blk_id == sel, start, 0), axis=0, keepdims=True)
            rows.append(jnp.where(sel >= 0, base + rank, dump))
        rows.extend([dump] * (SEL_ROWS - MOBA_TOPK))
    pos_ref[...] = jnp.concatenate(rows, axis=0)


def _moba_positions(sel, starts, seq, dump_row):
    rows, n = sel.shape
    tm = min(4 * TOKEN_ROWS, seq)
    tps = seq // tm
    groups = starts.shape[1]
    return pl.pallas_call(
        functools.partial(_moba_positions_kernel, dump_row=dump_row),
        grid=(n // tm,),
        in_specs=[pl.BlockSpec((rows, tm), lambda i: (0, i)),
                  pl.BlockSpec((1, groups, 1), lambda i: (i // tps, 0, 0))],
        out_specs=pl.BlockSpec((rows, tm), lambda i: (0, i)),
        out_shape=jax.ShapeDtypeStruct((rows, n), I32),
        compiler_params=_params("parallel"),
        name="moba_positions",
    )(sel, starts)


ATTN_SUB_ROWS = 128
ATTN_SUBS_PER_STEP = 32
ATTN_STEP_ROWS = ATTN_SUB_ROWS * ATTN_SUBS_PER_STEP


def _pack_bf16_row_pairs(x):
    h = x.shape[0] // 2
    bits = lax.bitcast_convert_type(x.astype(BF16).astype(F32), U32)
    return (bits[:h] >> 16) | (bits[h:] & U32(HIGH_HALF))


def _transpose_u32(x):
    return lax.bitcast_convert_type(lax.bitcast_convert_type(x, I32).T, U32)


def _scores_t(k, q, keep):
    st = lax.dot_general(k, q, (((1,), (1,)), ((), ())), preferred_element_type=F32)
    if keep is not None:
        st = jnp.where(keep, st, NEG_INF)
    m = jnp.max(st, axis=0, keepdims=True)
    p = jnp.exp(st - m)
    return m, p, jnp.sum(p, axis=0, keepdims=True)


def _group_attn_kernel(step_h_ref, step_b_ref, sub_j_ref, nsteps_ref, qs_ref, k_ref, vt_ref, after_ref,
                       part_ref):
    del step_h_ref, step_b_ref, after_ref
    i = pl.program_id(0)
    half = HEAD_DIM // 2

    @pl.when(i < nsteps_ref[0])
    def _():
        for u in range(ATTN_SUBS_PER_STEP):
            j = sub_j_ref[i * ATTN_SUBS_PER_STEP + u]
            rows = pl.ds(u * ATTN_SUB_ROWS, ATTN_SUB_ROWS)
            q = qs_ref[rows, :].astype(BF16)
            m, p, l = _scores_t(k_ref[0, j], q, None)
            ot = jnp.dot(vt_ref[0, j], p.astype(BF16), preferred_element_type=F32) * (1.0 / l)
            lse = lax.bitcast_convert_type(m + jnp.log(l), U32)
            packed_t = jnp.concatenate(
                [_pack_bf16_row_pairs(ot), jnp.broadcast_to(lse, (half, ATTN_SUB_ROWS))], axis=0)
            part_ref[rows, :] = _transpose_u32(packed_t)


def _group_attention(qs, k_blk, vt_blk, step_h, step_b, sub_j, nsteps, nb, after):
    n_steps = step_h.shape[0]
    row_map = lambda i, sh, sb, sj, ns: (jnp.minimum(i, ns[0] - 1), 0)
    kv_map = lambda i, sh, sb, sj, ns: (sh[i], sb[i], 0, 0)
    return pl.pallas_call(
        _group_attn_kernel,
        grid_spec=pltpu.PrefetchScalarGridSpec(
            num_scalar_prefetch=4,
            grid=(n_steps,),
            in_specs=[pl.BlockSpec((ATTN_STEP_ROWS, HEAD_DIM), row_map),
                      pl.BlockSpec((1, nb, MOBA_BLOCK, HEAD_DIM), kv_map),
                      pl.BlockSpec((1, nb, HEAD_DIM, MOBA_BLOCK), kv_map), _ORDER_ONLY],
            out_specs=pl.BlockSpec((ATTN_STEP_ROWS, HEAD_DIM), row_map)),
        out_shape=jax.ShapeDtypeStruct(((n_steps + 1) * ATTN_STEP_ROWS, HEAD_DIM), U32),
        compiler_params=_params("arbitrary"),
        name="moba_group_attention",
    )(step_h, step_b, sub_j, nsteps, qs, k_blk, vt_blk, after)


def _attn_merge_kernel(x_ref, qp_ref, k_ref, vt_ref, pg_ref, sel_ref, wo_ref, lng_ref, lnb_ref,
                       after_ref, o_ref, op_ref):
    del after_ref
    tm = x_ref.shape[0]
    half = HEAD_DIM // 2
    sel = sel_ref[...]
    key = lax.broadcasted_iota(I32, (tm, tm), 0)
    qry = lax.broadcasted_iota(I32, (tm, tm), 1)
    causal = key <= qry
    heads = []
    for hd in range(N_HEADS):
        q = qp_ref[0, hd].astype(BF16)
        m_own, p, l_own = _scores_t(k_ref[hd, 0], q, causal)
        acc = jnp.dot(vt_ref[hd, 0], p.astype(BF16), preferred_element_type=F32)

        outs, lses = [], []
        m_tot = m_own
        for c in range(MOBA_TOPK):
            part_t = _transpose_u32(pg_ref[c, 0, hd])
            lo, hi = _unpack_bf16_pairs(part_t[:half])
            valid = sel[hd * SEL_ROWS + c:hd * SEL_ROWS + c + 1, :] >= 0
            outs.append(jnp.where(valid, jnp.concatenate([lo, hi], axis=0), 0.0))
            lse = jnp.where(valid, lax.bitcast_convert_type(part_t[half:half + 1], F32), NEG_INF)
            lses.append(lse)
            m_tot = jnp.maximum(m_tot, lse)
        w_own = jnp.exp(m_own - m_tot)
        num = acc * w_own
        den = l_own * w_own
        for c in range(MOBA_TOPK):
            w_c = jnp.exp(lses[c] - m_tot)
            num = num + outs[c] * w_c
            den = den + w_c
        heads.append(num * (1.0 / den))
    att = jnp.concatenate(heads, axis=0).T.astype(BF16)
    mix = jnp.dot(att, wo_ref[...], preferred_element_type=F32)
    x1 = _layer_norm(DN_ALPHA * x_ref[...] + mix, lng_ref[...], lnb_ref[...])
    o_ref[...] = x1
    op_ref[...] = _pack_bf16_pairs(x1)


def _attn_merge(x, qp, k_blk, vt_blk, pg, sel, w_o, ln_g, ln_b, after):
    n, d = x.shape
    blk = MOBA_BLOCK
    row = lambda v: v.reshape(1, -1).astype(F32)
    return pl.pallas_call(
        _attn_merge_kernel,
        grid=(n // blk,),
        in_specs=[pl.BlockSpec((blk, d), lambda i: (i, 0)),
                  pl.BlockSpec((1, N_HEADS, blk, HEAD_DIM), lambda i: (i, 0, 0, 0)),
                  pl.BlockSpec((N_HEADS, 1, blk, HEAD_DIM), lambda i: (0, i, 0, 0)),
                  pl.BlockSpec((N_HEADS, 1, HEAD_DIM, blk), lambda i: (0, i, 0, 0)),
                  pl.BlockSpec((MOBA_TOPK, 1, N_HEADS, blk, HEAD_DIM), lambda i: (0, i, 0, 0, 0)),
                  pl.BlockSpec((N_HEADS * SEL_ROWS, blk), lambda i: (0, i)),
                  _full((d, d)), _full((1, d)), _full((1, d)), _ORDER_ONLY],
        out_specs=[pl.BlockSpec((blk, d), lambda i: (i, 0)), pl.BlockSpec((blk, d // 2), lambda i: (i, 0))],
        out_shape=[jax.ShapeDtypeStruct((n, d), F32), jax.ShapeDtypeStruct((n, d // 2), U32)],
        compiler_params=_params("parallel"),
        name="moba_merge",
    )(x, qp, k_blk, vt_blk, pg, sel, w_o.astype(BF16), row(ln_g), row(ln_b), after)


def _moba_layer(streams, seq, w_q, w_o, ln_g, ln_b, cos_t, sin_t):
    staged = [_moba_regroup(x, seq, w_q, cos_t, sin_t, k_mean) for x, _, _, k_mean in streams]
    gathered, order = [], staged[-1][1]
    for (x, k_blk, vt_blk, _), (qp, sel, qs, tables, gather_idx) in zip(streams, staged):
        nbt = x.shape[0] // MOBA_BLOCK
        part = _group_attention(qs, k_blk, vt_blk, *tables, seq // MOBA_BLOCK, order)
        gathered.append(_sc_gather_rows(part, gather_idx).reshape(MOBA_TOPK, nbt, N_HEADS, MOBA_BLOCK, HEAD_DIM))
        order = part
    outs = []
    for (x, k_blk, vt_blk, _), (qp, sel, _, _, _), pg in zip(streams, staged, gathered):
        outs.append(_attn_merge(x, qp, k_blk, vt_blk, pg, sel, w_o, ln_g, ln_b, order))
        order = outs[-1][0]
    return outs


def _moba_regroup(x, seq, w_q, cos_t, sin_t, k_mean):
    n, d = x.shape
    batch = n // seq
    blk = MOBA_BLOCK
    nb = seq // blk
    nbt = n // blk
    chunk = SC_INDEX_CHUNK
    qp, sel, cnt = _queries(x, seq, w_q, cos_t, sin_t, k_mean)

    counts = cnt[:, :, 0].reshape(batch * N_HEADS, nb)
    gpad = (counts + ATTN_SUB_ROWS - 1) // ATTN_SUB_ROWS * ATTN_SUB_ROWS
    seg = jnp.sum(gpad, axis=1)
    seg_pad = (seg + ATTN_STEP_ROWS - 1) // ATTN_STEP_ROWS * ATTN_STEP_ROWS
    seg_end = jnp.cumsum(seg_pad)
    gend = (seg_end - seg_pad)[:, None] + jnp.cumsum(gpad, axis=1)
    gstart = (gend - gpad).astype(I32)
    steps_per_seg = -(-(MOBA_TOPK * seq + nb * (ATTN_SUB_ROWS - 1)) // ATTN_STEP_ROWS)
    n_steps = batch * N_HEADS * steps_per_seg
    step_first = jnp.arange(n_steps, dtype=I32) * ATTN_STEP_ROWS
    step_seg = jnp.minimum(_count_le(seg_end, step_first), batch * N_HEADS - 1)
    sub_first = jnp.arange(n_steps * ATTN_SUBS_PER_STEP, dtype=I32) * ATTN_SUB_ROWS
    sub_grp = jnp.minimum(_count_le(gend.reshape(-1), sub_first), batch * N_HEADS * nb - 1)
    nsteps = (seg_end[-1] // ATTN_STEP_ROWS).astype(I32).reshape(1)
    dump_row = n_steps * ATTN_STEP_ROWS

    pos = _moba_positions(sel, gstart.reshape(batch, N_HEADS * nb, 1), seq, dump_row)
    pos5 = pos.reshape(N_HEADS, SEL_ROWS, nbt, blk // chunk, chunk)[:, :MOBA_TOPK]
    scatter_idx = pos5.transpose(2, 0, 3, 1, 4).reshape(nbt * N_HEADS * (blk // chunk), MOBA_TOPK, chunk)
    gather_idx = pos5.transpose(1, 2, 0, 3, 4).reshape(-1)

    qs = _sc_scatter_rows(qp.reshape(n * N_HEADS, HEAD_DIM), scatter_idx, dump_row + chunk)
    tables = ((step_seg % N_HEADS).astype(I32), (step_seg // N_HEADS).astype(I32),
              (sub_grp % nb).astype(I32), nsteps)
    return qp, sel, qs, tables, gather_idx


def _rope_tables(seq):
    half = ROT_DIM // 2
    inv = ROPE_THETA ** (-jnp.arange(0, ROT_DIM, 2, dtype=F32) / ROT_DIM)
    ang = jnp.arange(seq, dtype=F32)[:, None] * inv[None, :]
    cos, sin = jnp.cos(ang), jnp.sin(ang)
    rest = HEAD_DIM - ROT_DIM
    cos_t = jnp.concatenate([cos, cos, jnp.ones((seq, rest), F32)], axis=1)
    sin_t = jnp.concatenate([-sin, sin, jnp.zeros((seq, rest), F32)], axis=1)
    del half
    return cos_t, sin_t


def kernel(x, p, ln_g, ln_b, a_w_in, a_b_in, a_conv_w, a_conv_b, a_gate_a_w, a_gate_a_b, a_gate_i_w,
           a_gate_i_b, a_lambda, a_w_out, kv_ln_g, kv_ln_b, w_kv, b_w_q, b_w_o, router_w, router_b,
           exp_w_gate, exp_w_up, exp_w_down, sh_w_gate, sh_w_up, sh_w_down, ple_w, ple_gate_w):
    batch, seq, d = x.shape
    xs = [x.reshape(batch * seq, d)] * batch
    p_all = p.reshape(DEPTH * batch, seq, PLE_DIM)
    cos_t, sin_t = _rope_tables(seq)
    kv = None
    for i in range(DEPTH):
        if i < N_A_LAYERS:
            mixed = [_rglru_layer(xb, seq, b if i == 0 else 0, a_w_in[i], a_b_in[i], a_conv_w[i],
                                  a_conv_b[i], a_gate_a_w[i], a_gate_a_b[i], a_gate_i_w[i], a_gate_i_b[i],
                                  a_lambda[i], a_w_out[i], ln_g[i, 0], ln_b[i, 0])
                     for b, xb in enumerate(xs)]
        else:
            if i == N_A_LAYERS:
                kv = []
                for xb in xs:
                    k_blk, vt_blk, k_mean = _shared_kv(xb, seq, kv_ln_g, kv_ln_b, w_kv, cos_t, sin_t)
                    kv.append((k_blk, vt_blk, k_mean.reshape(seq // MOBA_BLOCK, d)))
            j = i - N_A_LAYERS
            mixed = _moba_layer([(xb,) + kvb for xb, kvb in zip(xs, kv)], seq, b_w_q[j], b_w_o[j],
                                ln_g[i, 0], ln_b[i, 0], cos_t, sin_t)
        xs = _moe_layer(mixed, i, router_w[i], router_b[i], exp_w_gate, exp_w_up, exp_w_down,
                        sh_w_gate[i], sh_w_up[i], sh_w_down[i], ln_g[i, 1], ln_b[i, 1],
                        p_all, i * batch, ple_w[i], ple_gate_w[i])
    return jnp.stack(xs, axis=0)
```

```python
import functools

import jax
import jax.numpy as jnp
from jax import lax
from jax.experimental import pallas as pl
from jax.experimental.pallas import tpu as pltpu
from jax.experimental.pallas import tpu_sc as plsc

F32 = jnp.float32
BF16 = jnp.bfloat16
I32 = jnp.int32
U32 = jnp.uint32
HIGH_HALF = 0xFFFF0000

SC_CORES = 2
SC_SUBCORES = 16
SC_WORKERS = SC_CORES * SC_SUBCORES
SC_LANES = 16
SC_INDEX_CHUNK = 128
SC_STAGE_BYTES = 256 * 1024

D_MODEL = 1024
DEPTH = 4
N_A_LAYERS = DEPTH // 2
D_RNN = D_MODEL
LRU_BLOCKS = 4
LRU_BLOCK_W = D_RNN // LRU_BLOCKS
CONV_W = 4
LRU_C = 8.0
N_HEADS = 8
HEAD_DIM = D_MODEL // N_HEADS
ROT_DIM = HEAD_DIM // 4
ROPE_THETA = 500000.0
MOBA_BLOCK = 256
MOBA_TOPK = 3
N_EXPERTS = 64
N_GROUPS = 8
GROUP_SIZE = N_EXPERTS // N_GROUPS
TOPK_GROUPS = 4
TOPK_EXPERTS = 8
D_EXPERT = 256
D_SHARED = 256
ROUTED_SCALE = 2.5
PLE_DIM = 256
DN_ALPHA = (2 * DEPTH) ** 0.25
LN_EPS = 1e-5

V7X_VMEM_LIMIT_BYTES = 56 * 1024 * 1024

MIXER_ROWS = 512
TOKEN_ROWS = 1024
EXPERT_ROWS = 2304
NEG_INF = float("-inf")


def _params(*sem):
    return pltpu.CompilerParams(dimension_semantics=sem, vmem_limit_bytes=V7X_VMEM_LIMIT_BYTES)


def _layer_norm(z, g, b):
    mu = jnp.mean(z, axis=-1, keepdims=True)
    zc = z - mu
    var = jnp.mean(zc * zc, axis=-1, keepdims=True)
    return zc * lax.rsqrt(var + LN_EPS) * g + b


def _silu(x):
    return x * jax.nn.sigmoid(x)


def _gelu_tanh(x):
    return x * jax.nn.sigmoid(x * (1.5957691216057308 + 0.07135481627159855 * (x * x)))


def _full(shape):
    return pl.BlockSpec(shape, lambda *_: (0,) * len(shape))


def _count_le(ends, values):
    return jnp.sum((ends[None, :] <= values[:, None]).astype(I32), axis=1)


def _pack_bf16_pairs(x):
    w = x.shape[1] // 2
    bits = lax.bitcast_convert_type(x.astype(BF16).astype(F32), U32)
    return (bits[:, :w] >> 16) | (bits[:, w:] & U32(HIGH_HALF))


def _unpack_bf16_pairs(u):
    lo = lax.bitcast_convert_type(u << 16, F32)
    hi = lax.bitcast_convert_type(u & U32(HIGH_HALF), F32)
    return lo, hi


def _sc_mesh():
    return plsc.VectorSubcoreMesh(core_axis_name="c", subcore_axis_name="s")


def _sc_worker_id():
    return lax.axis_index("s") * SC_CORES + lax.axis_index("c")


def _sc_chunks_per_step(chunks_per_worker, row_words):
    g = max(1, SC_STAGE_BYTES // (SC_INDEX_CHUNK * row_words * 4))
    while chunks_per_worker % g:
        g -= 1
    return g


def _sc_gather_rows(table, idx):
    b = idx.shape[0]
    w = table.shape[1]
    chunk = SC_INDEX_CHUNK
    chunks_per_worker = b // (SC_WORKERS * chunk)
    assert chunks_per_worker * SC_WORKERS * chunk == b
    g = _sc_chunks_per_step(chunks_per_worker, w)

    @functools.partial(
        pl.kernel, mesh=_sc_mesh(), out_type=jax.ShapeDtypeStruct((b, w), table.dtype),
        scratch_types=[pltpu.VMEM((g, chunk), I32), pltpu.VMEM((g * chunk, w), table.dtype),
                       pltpu.SemaphoreType.DMA])
    def gather(table_hbm, idx_hbm, out_hbm, idx_v, rows_v, sem):
        first = _sc_worker_id() * chunks_per_worker

        @pl.loop(0, chunks_per_worker // g)
        def _(j):
            c0 = first + j * g
            pltpu.sync_copy(idx_hbm.at[pl.ds(c0, g)], idx_v)
            copies = [pltpu.async_copy(table_hbm.at[idx_v.at[q]], rows_v.at[pl.ds(q * chunk, chunk)], sem)
                      for q in range(g)]
            for cp in copies:
                cp.wait()
            pltpu.sync_copy(rows_v, out_hbm.at[pl.ds(c0 * chunk, g * chunk)])

    return gather(table, idx.reshape(b // chunk, chunk))


def _sc_scatter_rows(src, idx, out_rows):
    n, w = src.shape
    chunk = SC_INDEX_CHUNK
    fan = idx.shape[1]
    chunks_per_worker = n // (SC_WORKERS * chunk)
    assert idx.shape == (n // chunk, fan, chunk) and chunks_per_worker * SC_WORKERS * chunk == n
    g = _sc_chunks_per_step(chunks_per_worker, w)

    @functools.partial(
        pl.kernel, mesh=_sc_mesh(), out_type=jax.ShapeDtypeStruct((out_rows, w), src.dtype),
        scratch_types=[pltpu.VMEM((g, fan, chunk), I32), pltpu.VMEM((g * chunk, w), src.dtype),
                       pltpu.SemaphoreType.DMA])
    def scatter(src_hbm, idx_hbm, out_hbm, idx_v, rows_v, sem):
        first = _sc_worker_id() * chunks_per_worker

        @pl.loop(0, chunks_per_worker // g)
        def _(j):
            c0 = first + j * g
            pltpu.sync_copy(src_hbm.at[pl.ds(c0 * chunk, g * chunk)], rows_v)
            pltpu.sync_copy(idx_hbm.at[pl.ds(c0, g)], idx_v)
            copies = [pltpu.async_copy(rows_v.at[pl.ds(q * chunk, chunk)], out_hbm.at[idx_v.at[q, f]], sem)
                      for q in range(g) for f in range(fan)]
            for cp in copies:
                cp.wait()

    return scatter(src, idx)


def _sc_weighted_gather(table, idx, wrep, fan):
    nf = idx.shape[0]
    n = nf // fan
    w = table.shape[1]
    chunk = SC_INDEX_CHUNK
    lanes = SC_LANES
    half = chunk // 2
    tokens = half // fan
    chunks_per_worker = nf // (SC_WORKERS * chunk)
    assert chunks_per_worker * SC_WORKERS * chunk == nf and w % lanes == 0 and wrep.shape == (n, fan * lanes)

    @functools.partial(
        pl.kernel, mesh=_sc_mesh(), out_type=jax.ShapeDtypeStruct((n, 2 * w), F32),
        scratch_types=[pltpu.VMEM((chunks_per_worker, chunk), I32), pltpu.VMEM((2, half, w), U32),
                       pltpu.VMEM((2, tokens, fan * lanes), F32), pltpu.VMEM((2, tokens, 2 * w), F32),
                       pltpu.SemaphoreType.DMA((2,)), pltpu.SemaphoreType.DMA((2,)),
                       pltpu.SemaphoreType.DMA((2,))],
        compiler_params=pltpu.CompilerParams(needs_layout_passes=False))
    def weighted_gather(table_hbm, idx_hbm, w_hbm, out_hbm, idx_v, rows_v, w_v, out_v,
                        row_sem, weight_sem, out_sem):
        worker = _sc_worker_id()
        first_token = worker * (2 * chunks_per_worker * tokens)
        pltpu.sync_copy(idx_hbm.at[pl.ds(worker * chunks_per_worker, chunks_per_worker)], idx_v)

        def loads(pair, slot):
            step = 2 * pair + slot
            return (pltpu.make_async_copy(table_hbm.at[idx_v.at[pair, pl.ds(slot * half, half)]],
                                          rows_v.at[slot], row_sem.at[slot]),
                    pltpu.make_async_copy(w_hbm.at[pl.ds(first_token + step * tokens, tokens)],
                                          w_v.at[slot], weight_sem.at[slot]))

        def store(pair, slot):
            step = 2 * pair + slot
            return pltpu.make_async_copy(out_v.at[slot], out_hbm.at[pl.ds(first_token + step * tokens, tokens)],
                                         out_sem.at[slot])

        def reduce_rows(slot):
            @pl.loop(0, tokens)
            def _(t):
                weights = [w_v[slot, t, pl.ds(q * lanes, lanes)] for q in range(fan)]

                @pl.loop(0, w, step=lanes)
                def _(c):
                    acc_lo = jnp.zeros((lanes,), F32)
                    acc_hi = jnp.zeros((lanes,), F32)
                    for q in range(fan):
                        u = rows_v[slot, t * fan + q, pl.ds(c, lanes)]
                        acc_lo = acc_lo + weights[q] * lax.bitcast_convert_type(u << 16, F32)
                        acc_hi = acc_hi + weights[q] * lax.bitcast_convert_type(u & U32(HIGH_HALF), F32)
                    out_v[slot, t, pl.ds(c, lanes)] = acc_lo
                    out_v[slot, t, pl.ds(w + c, lanes)] = acc_hi

        for copy in loads(0, 0):
            copy.start()

        @pl.loop(0, chunks_per_worker)
        def _(pair):
            for slot in range(2):
                for copy in loads(pair, slot):
                    copy.wait()
                if slot == 0:
                    for copy in loads(pair, 1):
                        copy.start()
                else:
                    @pl.when(pair + 1 < chunks_per_worker)
                    def _():
                        for copy in loads(pair + 1, 0):
                            copy.start()

                @pl.when(pair > 0)
                def _():
                    store(pair - 1, slot).wait()
                reduce_rows(slot)
                store(pair, slot).start()

        for slot in range(2):
            store(chunks_per_worker - 1, slot).wait()

    return weighted_gather(table, idx.reshape(nf // chunk, chunk), wrep)


def _rglru_kernel(x_ref, win_ref, bin_ref, cw_ref, cb_ref, gaw_ref, gab_ref, giw_ref, gib_ref,
                  lam_ref, wout_ref, lng_ref, lnb_ref, o_ref, op_ref, tail_ref, h_ref, *, tiles_per_seq):
    i = pl.program_id(0)
    tm = x_ref.shape[0]

    @pl.when(i % tiles_per_seq == 0)
    def _():
        tail_ref[...] = jnp.zeros_like(tail_ref)
        h_ref[...] = jnp.zeros_like(h_ref)

    x = x_ref[...]
    xy = jnp.dot(x.astype(BF16), win_ref[...], preferred_element_type=F32) + bin_ref[...]
    xb = xy[:, :D_RNN]
    y = _gelu_tanh(xy[:, D_RNN:])

    tail = tail_ref[...]
    row8 = lax.broadcasted_iota(I32, (8, 1), 0)
    xc = cb_ref[...] + xb * cw_ref[CONV_W - 1:CONV_W, :]
    for d in range(1, CONV_W):
        rolled = pltpu.roll(xb, d, 0)
        head = jnp.where(row8 < d, pltpu.roll(tail, d, 0), rolled[:8])
        shifted = jnp.concatenate([head, rolled[8:]], axis=0)
        xc = xc + shifted * cw_ref[CONV_W - 1 - d:CONV_W - d, :]
    tail_ref[...] = xb[tm - 8:, :]

    r_parts, i_parts = [], []
    for n in range(LRU_BLOCKS):
        xg = xc[:, n * LRU_BLOCK_W:(n + 1) * LRU_BLOCK_W].astype(BF16)
        r_parts.append(jnp.dot(xg, gaw_ref[n], preferred_element_type=F32))
        i_parts.append(jnp.dot(xg, giw_ref[n], preferred_element_type=F32))
    r = jax.nn.sigmoid(jnp.concatenate(r_parts, axis=1) + gab_ref[...])
    ig = jax.nn.sigmoid(jnp.concatenate(i_parts, axis=1) + gib_ref[...])

    lam = lam_ref[...]
    softplus_neg_lam = jnp.maximum(-lam, 0.0) + jnp.log1p(jnp.exp(-jnp.abs(lam)))
    log_a = (-LRU_C * r) * softplus_neg_lam
    a = jnp.exp(log_a)
    u = jnp.sqrt(1.0 - a * a) * (ig * xc)

    n_grp = tm // 8
    sub = lax.broadcasted_iota(I32, (1, 8, 1), 1)
    acc_a = a.reshape(n_grp, 8, D_RNN)
    acc_h = u.reshape(n_grp, 8, D_RNN)
    for d in (1, 2, 4):
        keep = sub >= d
        sh_a = pltpu.roll(acc_a, d, 1)
        sh_h = pltpu.roll(acc_h, d, 1)
        acc_h = jnp.where(keep, acc_a * sh_h + acc_h, acc_h)
        acc_a = jnp.where(keep, acc_a * sh_a, acc_a)
    state = h_ref[...]
    groups = []
    for g in range(n_grp):
        hg = acc_h[g] + acc_a[g] * state
        state = hg[7:8, :]
        groups.append(hg)
    h = jnp.concatenate(groups, axis=0)
    h_ref[...] = state

    mix = jnp.dot((h * y).astype(BF16), wout_ref[...], preferred_element_type=F32)
    x1 = _layer_norm(DN_ALPHA * x + mix, lng_ref[...], lnb_ref[...])
    o_ref[...] = x1
    op_ref[...] = _pack_bf16_pairs(x1)


def _rglru_layer(x, seq, stream, w_in, b_in, conv_w, conv_b, ga_w, ga_b, gi_w, gi_b, lam, w_out,
                 ln_g, ln_b):
    n, d = seq, x.shape[1]
    tm = min(MIXER_ROWS, seq)
    first_tile = stream * (seq // tm)
    row = lambda v: v.reshape(1, -1).astype(F32)
    return pl.pallas_call(
        functools.partial(_rglru_kernel, tiles_per_seq=seq // tm),
        grid=(n // tm,),
        in_specs=[pl.BlockSpec((tm, d), lambda i: (i + first_tile, 0)),
                  _full((d, 2 * D_RNN)), _full((1, 2 * D_RNN)),
                  _full((CONV_W, D_RNN)), _full((1, D_RNN)),
                  _full((LRU_BLOCKS, LRU_BLOCK_W, LRU_BLOCK_W)), _full((1, D_RNN)),
                  _full((LRU_BLOCKS, LRU_BLOCK_W, LRU_BLOCK_W)), _full((1, D_RNN)),
                  _full((1, D_RNN)), _full((D_RNN, d)), _full((1, d)), _full((1, d))],
        out_specs=[pl.BlockSpec((tm, d), lambda i: (i, 0)), pl.BlockSpec((tm, d // 2), lambda i: (i, 0))],
        out_shape=[jax.ShapeDtypeStruct((n, d), F32), jax.ShapeDtypeStruct((n, d // 2), U32)],
        scratch_shapes=[pltpu.VMEM((8, D_RNN), F32), pltpu.VMEM((1, D_RNN), F32)],
        compiler_params=_params("arbitrary"),
        name="rglru_mixer",
    )(x, w_in.astype(BF16), row(b_in), conv_w, row(conv_b), ga_w.astype(BF16), row(ga_b),
      gi_w.astype(BF16), row(gi_b), row(lam), w_out.astype(BF16), row(ln_g), row(ln_b))


def _first_argmax(cur, idx, size, axis):
    m = jnp.max(cur, axis=axis, keepdims=True)
    first = jnp.min(jnp.where(cur == m, idx, size), axis=axis, keepdims=True)
    return m, first


def _router_kernel(x_ref, rw_ref, rb_ref, e_ref, w_ref, r_ref, cnt_ref, carry_ref):
    i = pl.program_id(0)
    tm = x_ref.shape[0]

    @pl.when(i == 0)
    def _():
        carry_ref[...] = jnp.zeros_like(carry_ref)

    x = x_ref[...]
    x_hi = x.astype(BF16)
    x_lo = (x - x_hi.astype(F32)).astype(BF16)
    rw = rw_ref[...]
    hi_both = jnp.dot(x_hi, rw, preferred_element_type=F32)
    logits = (hi_both[:, :128] + hi_both[:, 128:]
              + jnp.dot(x_lo, rw[:, :128], preferred_element_type=F32))
    scores = jax.nn.sigmoid(logits.T[:N_EXPERTS, :])
    choice = scores + rb_ref[...]

    c3 = choice.reshape(N_GROUPS, GROUP_SIZE, tm)
    in_grp = lax.broadcasted_iota(I32, c3.shape, 1)
    m1, i1 = _first_argmax(c3, in_grp, GROUP_SIZE, 1)
    m2 = jnp.max(jnp.where(in_grp == i1, NEG_INF, c3), axis=1, keepdims=True)
    grp_score = (m1 + m2)[:, 0, :]

    grp_id = lax.broadcasted_iota(I32, grp_score.shape, 0)
    grp_sel = jnp.zeros(grp_score.shape, jnp.bool_)
    cur = grp_score
    for _ in range(TOPK_GROUPS):
        _, gi = _first_argmax(cur, grp_id, N_GROUPS, 0)
        hit = grp_id == gi
        grp_sel = jnp.logical_or(grp_sel, hit)
        cur = jnp.where(hit, NEG_INF, cur)

    cur = jnp.where(grp_sel[:, None, :], c3, NEG_INF).reshape(N_EXPERTS, tm)
    exp_id = lax.broadcasted_iota(I32, cur.shape, 0)
    sel = jnp.zeros(cur.shape, F32)
    e_rows, s_rows = [], []
    for _ in range(TOPK_EXPERTS):
        _, ei = _first_argmax(cur, exp_id, N_EXPERTS, 0)
        hit = exp_id == ei
        e_rows.append(ei)
        s_rows.append(jnp.sum(jnp.where(hit, scores, 0.0), axis=0, keepdims=True))
        sel = jnp.where(hit, 1.0, sel)
        cur = jnp.where(hit, NEG_INF, cur)
    e_top = jnp.concatenate(e_rows, axis=0)
    s_top = jnp.concatenate(s_rows, axis=0)
    w_top = s_top / jnp.sum(s_top, axis=0, keepdims=True) * ROUTED_SCALE
    w_ref[...] = jnp.broadcast_to(w_top[:, None, :], (TOPK_EXPERTS, SC_LANES, tm)).reshape(
        TOPK_EXPERTS * SC_LANES, tm).T
    e_ref[...] = e_top

    t_row = lax.broadcasted_iota(I32, (tm, tm), 0)
    t_col = lax.broadcasted_iota(I32, (tm, tm), 1)
    before = (t_row < t_col).astype(BF16)
    cum = jnp.dot(sel.astype(BF16), before, preferred_element_type=F32) + carry_ref[...]
    r_rows = [jnp.sum(jnp.where(exp_id == e_rows[k], cum, 0.0), axis=0, keepdims=True)
              for k in range(TOPK_EXPERTS)]
    r_ref[...] = jnp.concatenate(r_rows, axis=0).astype(I32)
    carry_ref[...] = carry_ref[...] + jnp.sum(sel, axis=1, keepdims=True)
    cnt_ref[...] = jnp.broadcast_to(carry_ref[...], cnt_ref.shape).astype(I32)


def _router(x, router_w, router_b):
    n, d = x.shape
    tm = min(TOKEN_ROWS, n)
    rw = jnp.pad(router_w, ((0, 0), (0, 128 - N_EXPERTS)))
    rw_hi = rw.astype(BF16)
    rw = jnp.concatenate([rw_hi, (rw - rw_hi.astype(F32)).astype(BF16)], axis=1)
    k = TOPK_EXPERTS
    tok_spec = pl.BlockSpec((k, tm), lambda i: (0, i))
    return pl.pallas_call(
        _router_kernel,
        grid=(n // tm,),
        in_specs=[pl.BlockSpec((tm, d), lambda i: (i, 0)), _full((d, 256)), _full((N_EXPERTS, 1))],
        out_specs=[tok_spec, pl.BlockSpec((tm, k * SC_LANES), lambda i: (i, 0)), tok_spec,
                   _full((N_EXPERTS, 128))],
        out_shape=[jax.ShapeDtypeStruct((k, n), I32), jax.ShapeDtypeStruct((n, k * SC_LANES), F32),
                   jax.ShapeDtypeStruct((k, n), I32), jax.ShapeDtypeStruct((N_EXPERTS, 128), I32)],
        scratch_shapes=[pltpu.VMEM((N_EXPERTS, 1), F32)],
        compiler_params=_params("arbitrary"),
        name="moe_router",
    )(x, rw, router_b.reshape(N_EXPERTS, 1).astype(F32))


def _expert_kernel(blk_e_ref, nblk_ref, first_ref, slot_ref, next_ref, xs_ref, wg_hbm, wu_hbm, wd_hbm,
                   after_ref, ys_ref, wg_f, wu_f, wd_f, wg_s, wu_s, wd_s, sem, *, layer):
    del after_ref
    i = pl.program_id(0)

    def fetch(expert, slot):
        return (pltpu.make_async_copy(wg_hbm.at[layer, expert], wg_f.at[slot], sem.at[slot, 0]),
                pltpu.make_async_copy(wu_hbm.at[layer, expert], wu_f.at[slot], sem.at[slot, 1]),
                pltpu.make_async_copy(wd_hbm.at[layer, expert], wd_f.at[slot], sem.at[slot, 2]))

    @pl.when(i < nblk_ref[0])
    def _():
        slot = slot_ref[i]

        @pl.when(i == 0)
        def _():
            for copy in fetch(blk_e_ref[0], 0):
                copy.start()

        @pl.when(first_ref[i] == 1)
        def _():
            for copy in fetch(blk_e_ref[i], slot):
                copy.wait()
            wg_s[...] = wg_f[slot].astype(BF16)
            wu_s[...] = wu_f[slot].astype(BF16)
            wd_s[...] = wd_f[slot].astype(BF16)

            @pl.when(next_ref[i] >= 0)
            def _():
                for copy in fetch(next_ref[i], 1 - slot):
                    copy.start()

        lo, hi = _unpack_bf16_pairs(xs_ref[...])
        xs = jnp.concatenate([lo.astype(BF16), hi.astype(BF16)], axis=1)
        g = jnp.dot(xs, wg_s[...], preferred_element_type=F32)
        u = jnp.dot(xs, wu_s[...], preferred_element_type=F32)
        hdn = (_silu(g) * u).astype(BF16)
        ys_ref[...] = _pack_bf16_pairs(jnp.dot(hdn, wd_s[...], preferred_element_type=F32))


_ORDER_ONLY = pl.BlockSpec(memory_space=pl.ANY)


def _experts(xs, blk_e, nblk, layer, w_gate, w_up, w_down, after):
    rows, half = xs.shape
    d = 2 * half
    m = EXPERT_ROWS
    n_blocks = rows // m
    idx = jnp.arange(n_blocks, dtype=I32)
    used = idx < nblk[0]
    first = jnp.logical_and(used, jnp.logical_or(idx == 0, blk_e != jnp.roll(blk_e, 1)))
    slot = (jnp.cumsum(first.astype(I32)) - 1) % 2
    larger = jnp.logical_and(blk_e[None, :] > blk_e[:, None], used[None, :])
    next_e = jnp.min(jnp.where(larger, blk_e[None, :], N_EXPERTS), axis=1)
    next_e = jnp.where(next_e < N_EXPERTS, next_e, -1)
    row_map = lambda i, be, nb, fi, sl, nx: (jnp.minimum(i, nb[0] - 1), 0)
    hbm = pl.BlockSpec(memory_space=pl.ANY)
    return pl.pallas_call(
        functools.partial(_expert_kernel, layer=layer),
        grid_spec=pltpu.PrefetchScalarGridSpec(
            num_scalar_prefetch=5,
            grid=(n_blocks,),
            in_specs=[pl.BlockSpec((m, half), row_map), hbm, hbm, hbm, _ORDER_ONLY],
            out_specs=pl.BlockSpec((m, half), row_map),
            scratch_shapes=[pltpu.VMEM((2, d, D_EXPERT), F32), pltpu.VMEM((2, d, D_EXPERT), F32),
                            pltpu.VMEM((2, D_EXPERT, d), F32),
                            pltpu.VMEM((d, D_EXPERT), BF16), pltpu.VMEM((d, D_EXPERT), BF16),
                            pltpu.VMEM((D_EXPERT, d), BF16), pltpu.SemaphoreType.DMA((2, 3))]),
        out_shape=jax.ShapeDtypeStruct((rows, half), U32),
        compiler_params=_params("arbitrary"),
        name="moe_experts",
    )(blk_e, nblk, first.astype(I32), slot.astype(I32), next_e.astype(I32), xs, w_gate, w_up, w_down, after)


def _combine_kernel(x_ref, moe_ref, sg_ref, su_ref, sd_ref, lng_ref, lnb_ref,
                    p_ref, pw_ref, pg_ref, after_ref, o_ref):
    del after_ref
    x = x_ref[...]
    moe = moe_ref[...]
    xb = x.astype(BF16)
    hdn = _silu(jnp.dot(xb, sg_ref[...], preferred_element_type=F32)) * \
        jnp.dot(xb, su_ref[...], preferred_element_type=F32)
    shared = jnp.dot(hdn.astype(BF16), sd_ref[...], preferred_element_type=F32)
    x2 = _layer_norm(DN_ALPHA * x + (moe + shared), lng_ref[...], lnb_ref[...])
    emb = jnp.dot(p_ref[0].astype(BF16), pw_ref[...], preferred_element_type=F32)
    gate = jax.nn.sigmoid(jnp.dot(x2.astype(BF16), pg_ref[...], preferred_element_type=F32))
    o_ref[...] = x2 + emb * gate


def _combine(x, moe, s_gate, s_up, s_down, ln_g, ln_b, p_all, p_index, ple_w, ple_gate_w, after):
    n, d = x.shape
    tm = min(TOKEN_ROWS // 2, n)
    row = lambda v: v.reshape(1, -1).astype(F32)
    return pl.pallas_call(
        _combine_kernel,
        grid=(n // tm,),
        in_specs=[pl.BlockSpec((tm, d), lambda i: (i, 0)),
                  pl.BlockSpec((tm, d), lambda i: (i, 0)),
                  _full((d, D_SHARED)), _full((d, D_SHARED)), _full((D_SHARED, d)),
                  _full((1, d)), _full((1, d)),
                  pl.BlockSpec((1, tm, PLE_DIM), lambda i: (p_index, i, 0)),
                  _full((PLE_DIM, d)), _full((d, d)), _ORDER_ONLY],
        out_specs=pl.BlockSpec((tm, d), lambda i: (i, 0)),
        out_shape=jax.ShapeDtypeStruct((n, d), F32),
        compiler_params=_params("parallel"),
        name="moe_combine",
    )(x, moe, s_gate.astype(BF16), s_up.astype(BF16), s_down.astype(BF16), row(ln_g), row(ln_b),
      p_all, ple_w.astype(BF16), ple_gate_w.astype(BF16), after)


def _positions_kernel(e_ref, r_ref, start_ref, pos_ref):
    e = e_ref[...]
    start = start_ref[...]
    grp_id = lax.broadcasted_iota(I32, (start.shape[0], e.shape[1]), 0)
    rows = [jnp.sum(jnp.where(grp_id == e[k:k + 1], start, 0), axis=0, keepdims=True)
            for k in range(e.shape[0])]
    pos_ref[...] = jnp.concatenate(rows, axis=0) + r_ref[...]


def _positions(e_idx, rank, starts):
    k, n = e_idx.shape
    tm = min(4 * TOKEN_ROWS, n)
    groups = starts.shape[0]
    tok_spec = pl.BlockSpec((k, tm), lambda i: (0, i))
    return pl.pallas_call(
        _positions_kernel,
        grid=(n // tm,),
        in_specs=[tok_spec, tok_spec, _full((groups, 1))],
        out_specs=tok_spec,
        out_shape=jax.ShapeDtypeStruct((k, n), I32),
        compiler_params=_params("parallel"),
        name="group_positions",
    )(e_idx, rank, starts.reshape(groups, 1))


def _moe_layer(streams, layer, router_w, router_b, w_gate, w_up, w_down, s_gate, s_up, s_down,
               ln_g, ln_b, p_all, p_index, ple_w, ple_gate_w):
    m = EXPERT_ROWS
    k = TOPK_EXPERTS
    chunk = SC_INDEX_CHUNK
    routed = []
    for x, xp in streams:
        n, d = x.shape
        e_idx, wgt, rank, cnt = _router(x, router_w, router_b)
        counts = cnt[:, 0]
        padded = (counts + m - 1) // m * m
        pends = jnp.cumsum(padded)
        n_blocks = (n * k) // m + N_EXPERTS
        blk_first = jnp.arange(n_blocks, dtype=I32) * m
        blk_e = jnp.minimum(_count_le(pends, blk_first), N_EXPERTS - 1)
        nblk = (pends[-1] // m).astype(I32).reshape(1)
        pos = _positions(e_idx, rank, (pends - padded).astype(I32))
        pos_chunks = pos.reshape(k, n // chunk, chunk).transpose(1, 0, 2)
        xs = _sc_scatter_rows(xp, pos_chunks, n_blocks * m)
        routed.append((xs, blk_e, nblk, pos, wgt))
    gathered, order = [], routed[-1][3]
    for (x, _), (xs, blk_e, nblk, pos, wgt) in zip(streams, routed):
        n, d = x.shape
        ys = _experts(xs, blk_e, nblk, layer, w_gate, w_up, w_down, order)
        gathered.append(_sc_weighted_gather(ys, pos.T.reshape(-1), wgt, k))
        order = ys
    outs = []
    for s, ((x, _), moe) in enumerate(zip(streams, gathered)):
        outs.append(_combine(x, moe, s_gate, s_up, s_down, ln_g, ln_b, p_all, p_index + s,
                             ple_w, ple_gate_w, order))
        order = outs[-1]
    return outs


def _rope(t, cos_t, sin_t):
    half = ROT_DIM // 2
    width = t.shape[1]
    lane = lax.broadcasted_iota(I32, (1, width), 1) % HEAD_DIM
    partner = jnp.where(lane < half, pltpu.roll(t, width - half, 1), pltpu.roll(t, half, 1))
    cos_f = jnp.concatenate([cos_t] * N_HEADS, axis=1)
    sin_f = jnp.concatenate([sin_t] * N_HEADS, axis=1)
    return t * cos_f + partner * sin_f


def _kv_kernel(x_ref, g_ref, b_ref, wkv_ref, cos_ref, sin_ref, k_ref, vt_ref, km_ref):
    h = _layer_norm(x_ref[...], g_ref[...], b_ref[...])
    kv = jnp.dot(h.astype(BF16), wkv_ref[...], preferred_element_type=F32)
    k = _rope(kv[:, :D_MODEL], cos_ref[...], sin_ref[...])
    vt = kv[:, D_MODEL:].T
    km_ref[0] = jnp.mean(k, axis=0, keepdims=True)
    for hd in range(N_HEADS):
        k_ref[hd, 0] = k[:, hd * HEAD_DIM:(hd + 1) * HEAD_DIM].astype(BF16)
        vt_ref[hd, 0] = vt[hd * HEAD_DIM:(hd + 1) * HEAD_DIM, :].astype(BF16)


def _shared_kv(x, seq, ln_g, ln_b, w_kv, cos_t, sin_t):
    n, d = x.shape
    blk = MOBA_BLOCK
    nbt = n // blk
    spb = seq // blk
    row = lambda v: v.reshape(1, -1).astype(F32)
    return pl.pallas_call(
        _kv_kernel,
        grid=(nbt,),
        in_specs=[pl.BlockSpec((blk, d), lambda i: (i, 0)), _full((1, d)), _full((1, d)),
                  _full((d, 2 * d)),
                  pl.BlockSpec((blk, HEAD_DIM), lambda i: (i % spb, 0)),
                  pl.BlockSpec((blk, HEAD_DIM), lambda i: (i % spb, 0))],
        out_specs=[pl.BlockSpec((N_HEADS, 1, blk, HEAD_DIM), lambda i: (0, i, 0, 0)),
                   pl.BlockSpec((N_HEADS, 1, HEAD_DIM, blk), lambda i: (0, i, 0, 0)),
                   pl.BlockSpec((1, 1, d), lambda i: (i, 0, 0))],
        out_shape=[jax.ShapeDtypeStruct((N_HEADS, nbt, blk, HEAD_DIM), BF16),
                   jax.ShapeDtypeStruct((N_HEADS, nbt, HEAD_DIM, blk), BF16),
                   jax.ShapeDtypeStruct((nbt, 1, d), F32)],
        compiler_params=_params("parallel"),
        name="shared_kv",
    )(x, row(ln_g), row(ln_b), w_kv.astype(BF16), cos_t, sin_t)


SEL_ROWS = 8


def _query_kernel(x_ref, wq_ref, cos_ref, sin_ref, km_ref, qp_ref, sel_ref, cnt_ref, carry_ref,
                  *, blocks_per_seq):
    i = pl.program_id(0)
    tm = x_ref.shape[0]
    nb = km_ref.shape[0]
    own = i % blocks_per_seq

    @pl.when(own == 0)
    def _():
        carry_ref[...] = jnp.zeros_like(carry_ref)

    q = jnp.dot(x_ref[...].astype(BF16), wq_ref[...], preferred_element_type=F32)
    q = _rope(q, cos_ref[...], sin_ref[...]) * (HEAD_DIM ** -0.5)

    t_row = lax.broadcasted_iota(I32, (tm, tm), 0)
    t_col = lax.broadcasted_iota(I32, (tm, tm), 1)
    before = (t_row < t_col).astype(BF16)
    km = km_ref[...]
    km_hi = km.astype(BF16)
    km_both = jnp.concatenate([km_hi, (km - km_hi.astype(F32)).astype(BF16)], axis=0)
    q_hi = q.astype(BF16)
    q_lo = (q - q_hi.astype(F32)).astype(BF16)
    nt = (((1,), (1,)), ((), ()))
    gates = []
    for hd in range(N_HEADS):
        lo, hi = hd * HEAD_DIM, (hd + 1) * HEAD_DIM
        qp_ref[0, hd] = q[:, lo:hi]
        both = lax.dot_general(km_both[:, lo:hi], q_hi[:, lo:hi], nt, preferred_element_type=F32)
        gates.append(both[:nb] + both[nb:]
                     + lax.dot_general(km_hi[:, lo:hi], q_lo[:, lo:hi], nt, preferred_element_type=F32))

    gate = jnp.concatenate(gates, axis=0).reshape(N_HEADS, nb, tm)
    blk_id = lax.broadcasted_iota(I32, (N_HEADS, nb, tm), 1)
    cur = jnp.where(blk_id < own, gate, NEG_INF)
    sel_rows = []
    chosen = jnp.zeros((N_HEADS, nb, tm), F32)
    for _ in range(MOBA_TOPK):
        m, bi = _first_argmax(cur, blk_id, nb, 1)
        valid = m > NEG_INF
        hit = jnp.logical_and(blk_id == bi, valid)
        sel_rows.append(jnp.where(valid, bi, -1))
        chosen = jnp.where(hit, 1.0, chosen)
        cur = jnp.where(blk_id == bi, NEG_INF, cur)
    chosen2 = chosen.reshape(N_HEADS * nb, tm)
    carry = carry_ref[...]
    cum = (jnp.dot(chosen2.astype(BF16), before, preferred_element_type=F32) + carry).reshape(N_HEADS, nb, tm)
    rank_rows = [jnp.sum(jnp.where(blk_id == s, cum, 0.0), axis=1, keepdims=True).astype(I32)
                 for s in sel_rows]
    carry = carry + jnp.sum(chosen2, axis=1, keepdims=True)
    carry_ref[...] = carry
    pad = jnp.zeros((N_HEADS, SEL_ROWS - 2 * MOBA_TOPK, tm), I32)
    table = jnp.concatenate(sel_rows + rank_rows + [pad], axis=1)
    sel_ref[...] = table.reshape(N_HEADS * SEL_ROWS, tm)
    cnt_ref[0] = jnp.broadcast_to(carry, cnt_ref.shape[1:]).astype(I32)


def _queries(x, seq, w_q, cos_t, sin_t, k_mean):
    n, d = x.shape
    blk = MOBA_BLOCK
    nb = seq // blk
    return pl.pallas_call(
        functools.partial(_query_kernel, blocks_per_seq=nb),
        grid=(n // blk,),
        in_specs=[pl.BlockSpec((blk, d), lambda i: (i, 0)), _full((d, d)),
                  pl.BlockSpec((blk, HEAD_DIM), lambda i: (i % nb, 0)),
                  pl.BlockSpec((blk, HEAD_DIM), lambda i: (i % nb, 0)),
                  pl.BlockSpec((nb, d), lambda i: (i // nb, 0))],
        out_specs=[pl.BlockSpec((1, N_HEADS, blk, HEAD_DIM), lambda i: (i, 0, 0, 0)),
                   pl.BlockSpec((N_HEADS * SEL_ROWS, blk), lambda i: (0, i)),
                   pl.BlockSpec((1, N_HEADS * nb, 128), lambda i: (i // nb, 0, 0))],
        out_shape=[jax.ShapeDtypeStruct((n // blk, N_HEADS, blk, HEAD_DIM), F32),
                   jax.ShapeDtypeStruct((N_HEADS * SEL_ROWS, n), I32),
                   jax.ShapeDtypeStruct((n // seq, N_HEADS * nb, 128), I32)],
        scratch_shapes=[pltpu.VMEM((N_HEADS * nb, 1), F32)],
        compiler_params=_params("arbitrary"),
        name="moba_queries",
    )(x, w_q.astype(BF16), cos_t, sin_t, k_mean)


def _moba_positions_kernel(sel_ref, start_ref, pos_ref, *, dump_row):
    tm = sel_ref.shape[1]
    nb = start_ref.shape[1] // N_HEADS
    blk_id = lax.broadcasted_iota(I32, (nb, tm), 0)
    dump = dump_row + lax.broadcasted_iota(I32, (1, tm), 1) % SC_INDEX_CHUNK
    rows = []
    for hd in range(N_HEADS):
        start = start_ref[0, hd * nb:(hd + 1) * nb, :]
        for s in range(MOBA_TOPK):
            sel = sel_ref[hd * SEL_ROWS + s:hd * SEL_ROWS + s + 1, :]
            rank = sel_ref[hd * SEL_ROWS + MOBA_TOPK + s:hd * SEL_ROWS + MOBA_TOPK + s + 1, :]
            base = jnp.sum(jnp.where(blk_id == sel, start, 0), axis=0, keepdims=True)
            rows.append(jnp.where(sel >= 0, base + rank, dump))
        rows.extend([dump] * (SEL_ROWS - MOBA_TOPK))
    pos_ref[...] = jnp.concatenate(rows, axis=0)


def _moba_positions(sel, starts, seq, dump_row):
    rows, n = sel.shape
    tm = min(4 * TOKEN_ROWS, seq)
    tps = seq // tm
    groups = starts.shape[1]
    return pl.pallas_call(
        functools.partial(_moba_positions_kernel, dump_row=dump_row),
        grid=(n // tm,),
        in_specs=[pl.BlockSpec((rows, tm), lambda i: (0, i)),
                  pl.BlockSpec((1, groups, 1), lambda i: (i // tps, 0, 0))],
        out_specs=pl.BlockSpec((rows, tm), lambda i: (0, i)),
        out_shape=jax.ShapeDtypeStruct((rows, n), I32),
        compiler_params=_params("parallel"),
        name="moba_positions",
    )(sel, starts)


ATTN_SUB_ROWS = 128
ATTN_SUBS_PER_STEP = 32
ATTN_STEP_ROWS = ATTN_SUB_ROWS * ATTN_SUBS_PER_STEP


def _pack_bf16_row_pairs(x):
    h = x.shape[0] // 2
    bits = lax.bitcast_convert_type(x.astype(BF16).astype(F32), U32)
    return (bits[:h] >> 16) | (bits[h:] & U32(HIGH_HALF))


def _transpose_u32(x):
    return lax.bitcast_convert_type(lax.bitcast_convert_type(x, I32).T, U32)


def _scores_t(k, q, keep):
    st = lax.dot_general(k, q, (((1,), (1,)), ((), ())), preferred_element_type=F32)
    if keep is not None:
        st = jnp.where(keep, st, NEG_INF)
    m = jnp.max(st, axis=0, keepdims=True)
    p = jnp.exp(st - m)
    return m, p, jnp.sum(p, axis=0, keepdims=True)


def _group_attn_kernel(step_h_ref, step_b_ref, sub_j_ref, nsteps_ref, qs_ref, k_ref, vt_ref, after_ref,
                       part_ref):
    del step_h_ref, step_b_ref, after_ref
    i = pl.program_id(0)
    half = HEAD_DIM // 2

    @pl.when(i < nsteps_ref[0])
    def _():
        for u in range(ATTN_SUBS_PER_STEP):
            j = sub_j_ref[i * ATTN_SUBS_PER_STEP + u]
            rows = pl.ds(u * ATTN_SUB_ROWS, ATTN_SUB_ROWS)
            q = qs_ref[rows, :].astype(BF16)
            m, p, l = _scores_t(k_ref[0, j], q, None)
            ot = jnp.dot(vt_ref[0, j], p.astype(BF16), preferred_element_type=F32) * (1.0 / l)
            lse = lax.bitcast_convert_type(m + jnp.log(l), U32)
            packed_t = jnp.concatenate(
                [_pack_bf16_row_pairs(ot), jnp.broadcast_to(lse, (half, ATTN_SUB_ROWS))], axis=0)
            part_ref[rows, :] = _transpose_u32(packed_t)


def _group_attention(qs, k_blk, vt_blk, step_h, step_b, sub_j, nsteps, nb, after):
    n_steps = step_h.shape[0]
    row_map = lambda i, sh, sb, sj, ns: (jnp.minimum(i, ns[0] - 1), 0)
    kv_map = lambda i, sh, sb, sj, ns: (sh[i], sb[i], 0, 0)
    return pl.pallas_call(
        _group_attn_kernel,
        grid_spec=pltpu.PrefetchScalarGridSpec(
            num_scalar_prefetch=4,
            grid=(n_steps,),
            in_specs=[pl.BlockSpec((ATTN_STEP_ROWS, HEAD_DIM), row_map),
                      pl.BlockSpec((1, nb, MOBA_BLOCK, HEAD_DIM), kv_map),
                      pl.BlockSpec((1, nb, HEAD_DIM, MOBA_BLOCK), kv_map), _ORDER_ONLY],
            out_specs=pl.BlockSpec((ATTN_STEP_ROWS, HEAD_DIM), row_map)),
        out_shape=jax.ShapeDtypeStruct(((n_steps + 1) * ATTN_STEP_ROWS, HEAD_DIM), U32),
        compiler_params=_params("arbitrary"),
        name="moba_group_attention",
    )(step_h, step_b, sub_j, nsteps, qs, k_blk, vt_blk, after)


def _attn_merge_kernel(x_ref, qp_ref, k_ref, vt_ref, pg_ref, sel_ref, wo_ref, lng_ref, lnb_ref,
                       after_ref, o_ref, op_ref):
    del after_ref
    tm = x_ref.shape[0]
    half = HEAD_DIM // 2
    sel = sel_ref[...]
    key = lax.broadcasted_iota(I32, (tm, tm), 0)
    qry = lax.broadcasted_iota(I32, (tm, tm), 1)
    causal = key <= qry
    heads = []
    for hd in range(N_HEADS):
        q = qp_ref[0, hd].astype(BF16)
        m_own, p, l_own = _scores_t(k_ref[hd, 0], q, causal)
        acc = jnp.dot(vt_ref[hd, 0], p.astype(BF16), preferred_element_type=F32)

        outs, lses = [], []
        m_tot = m_own
        for c in range(MOBA_TOPK):
            part_t = _transpose_u32(pg_ref[c, 0, hd])
            lo, hi = _unpack_bf16_pairs(part_t[:half])
            valid = sel[hd * SEL_ROWS + c:hd * SEL_ROWS + c + 1, :] >= 0
            outs.append(jnp.where(valid, jnp.concatenate([lo, hi], axis=0), 0.0))
            lse = jnp.where(valid, lax.bitcast_convert_type(part_t[half:half + 1], F32), NEG_INF)
            lses.append(lse)
            m_tot = jnp.maximum(m_tot, lse)
        w_own = jnp.exp(m_own - m_tot)
        num = acc * w_own
        den = l_own * w_own
        for c in range(MOBA_TOPK):
            w_c = jnp.exp(lses[c] - m_tot)
            num = num + outs[c] * w_c
            den = den + w_c
        heads.append(num * (1.0 / den))
    att = jnp.concatenate(heads, axis=0).T.astype(BF16)
    mix = jnp.dot(att, wo_ref[...], preferred_element_type=F32)
    x1 = _layer_norm(DN_ALPHA * x_ref[...] + mix, lng_ref[...], lnb_ref[...])
    o_ref[...] = x1
    op_ref[...] = _pack_bf16_pairs(x1)


def _attn_merge(x, qp, k_blk, vt_blk, pg, sel, w_o, ln_g, ln_b, after):
    n, d = x.shape
    blk = MOBA_BLOCK
    row = lambda v: v.reshape(1, -1).astype(F32)
    return pl.pallas_call(
        _attn_merge_kernel,
        grid=(n // blk,),
        in_specs=[pl.BlockSpec((blk, d), lambda i: (i, 0)),
                  pl.BlockSpec((1, N_HEADS, blk, HEAD_DIM), lambda i: (i, 0, 0, 0)),
                  pl.BlockSpec((N_HEADS, 1, blk, HEAD_DIM), lambda i: (0, i, 0, 0)),
                  pl.BlockSpec((N_HEADS, 1, HEAD_DIM, blk), lambda i: (0, i, 0, 0)),
                  pl.BlockSpec((MOBA_TOPK, 1, N_HEADS, blk, HEAD_DIM), lambda i: (0, i, 0, 0, 0)),
                  pl.BlockSpec((N_HEADS * SEL_ROWS, blk), lambda i: (0, i)),
                  _full((d, d)), _full((1, d)), _full((1, d)), _ORDER_ONLY],
        out_specs=[pl.BlockSpec((blk, d), lambda i: (i, 0)), pl.BlockSpec((blk, d // 2), lambda i: (i, 0))],
        out_shape=[jax.ShapeDtypeStruct((n, d), F32), jax.ShapeDtypeStruct((n, d // 2), U32)],
        compiler_params=_params("parallel"),
        name="moba_merge",
    )(x, qp, k_blk, vt_blk, pg, sel, w_o.astype(BF16), row(ln_g), row(ln_b), after)


def _moba_layer(streams, seq, w_q, w_o, ln_g, ln_b, cos_t, sin_t):
    staged = [_moba_regroup(x, seq, w_q, cos_t, sin_t, k_mean) for x, _, _, k_mean in streams]
    gathered, order = [], staged[-1][1]
    for (x, k_blk, vt_blk, _), (qp, sel, qs, tables, gather_idx) in zip(streams, staged):
        nbt = x.shape[0] // MOBA_BLOCK
        part = _group_attention(qs, k_blk, vt_blk, *tables, seq // MOBA_BLOCK, order)
        gathered.append(_sc_gather_rows(part, gather_idx).reshape(MOBA_TOPK, nbt, N_HEADS, MOBA_BLOCK, HEAD_DIM))
        order = part
    outs = []
    for (x, k_blk, vt_blk, _), (qp, sel, _, _, _), pg in zip(streams, staged, gathered):
        outs.append(_attn_merge(x, qp, k_blk, vt_blk, pg, sel, w_o, ln_g, ln_b, order))
        order = outs[-1][0]
    return outs


def _moba_regroup(x, seq, w_q, cos_t, sin_t, k_mean):
    n, d = x.shape
    batch = n // seq
    blk = MOBA_BLOCK
    nb = seq // blk
    nbt = n // blk
    chunk = SC_INDEX_CHUNK
    qp, sel, cnt = _queries(x, seq, w_q, cos_t, sin_t, k_mean)

    counts = cnt[:, :, 0].reshape(batch * N_HEADS, nb)
    gpad = (counts + ATTN_SUB_ROWS - 1) // ATTN_SUB_ROWS * ATTN_SUB_ROWS
    seg = jnp.sum(gpad, axis=1)
    seg_pad = (seg + ATTN_STEP_ROWS - 1) // ATTN_STEP_ROWS * ATTN_STEP_ROWS
    seg_end = jnp.cumsum(seg_pad)
    gend = (seg_end - seg_pad)[:, None] + jnp.cumsum(gpad, axis=1)
    gstart = (gend - gpad).astype(I32)
    steps_per_seg = -(-(MOBA_TOPK * seq + nb * (ATTN_SUB_ROWS - 1)) // ATTN_STEP_ROWS)
    n_steps = batch * N_HEADS * steps_per_seg
    step_first = jnp.arange(n_steps, dtype=I32) * ATTN_STEP_ROWS
    step_seg = jnp.minimum(_count_le(seg_end, step_first), batch * N_HEADS - 1)
    sub_first = jnp.arange(n_steps * ATTN_SUBS_PER_STEP, dtype=I32) * ATTN_SUB_ROWS
    sub_grp = jnp.minimum(_count_le(gend.reshape(-1), sub_first), batch * N_HEADS * nb - 1)
    nsteps = (seg_end[-1] // ATTN_STEP_ROWS).astype(I32).reshape(1)
    dump_row = n_steps * ATTN_STEP_ROWS

    pos = _moba_positions(sel, gstart.reshape(batch, N_HEADS * nb, 1), seq, dump_row)
    pos5 = pos.reshape(N_HEADS, SEL_ROWS, nbt, blk // chunk, chunk)[:, :MOBA_TOPK]
    scatter_idx = pos5.transpose(2, 0, 3, 1, 4).reshape(nbt * N_HEADS * (blk // chunk), MOBA_TOPK, chunk)
    gather_idx = pos5.transpose(1, 2, 0, 3, 4).reshape(-1)

    qs = _sc_scatter_rows(qp.reshape(n * N_HEADS, HEAD_DIM), scatter_idx, dump_row + chunk)
    tables = ((step_seg % N_HEADS).astype(I32), (step_seg // N_HEADS).astype(I32),
              (sub_grp % nb).astype(I32), nsteps)
    return qp, sel, qs, tables, gather_idx


def _rope_tables(seq):
    half = ROT_DIM // 2
    lane = jnp.arange(HEAD_DIM)
    inv = ROPE_THETA ** (-(2 * (lane % half)).astype(F32) / ROT_DIM)
    ang = jnp.arange(seq, dtype=F32)[:, None] * jnp.where(lane < ROT_DIM, inv, 0.0)[None, :]
    sign = jnp.where(lane < half, -1.0, 1.0).astype(F32)
    return jnp.cos(ang), jnp.sin(ang) * sign[None, :]


def kernel(x, p, ln_g, ln_b, a_w_in, a_b_in, a_conv_w, a_conv_b, a_gate_a_w, a_gate_a_b, a_gate_i_w,
           a_gate_i_b, a_lambda, a_w_out, kv_ln_g, kv_ln_b, w_kv, b_w_q, b_w_o, router_w, router_b,
           exp_w_gate, exp_w_up, exp_w_down, sh_w_gate, sh_w_up, sh_w_down, ple_w, ple_gate_w):
    batch, seq, d = x.shape
    xs = [x.reshape(batch * seq, d)] * batch
    p_all = p.reshape(DEPTH * batch, seq, PLE_DIM)
    cos_t, sin_t = _rope_tables(seq)
    kv = None
    for i in range(DEPTH):
        if i < N_A_LAYERS:
            mixed = [_rglru_layer(xb, seq, b if i == 0 else 0, a_w_in[i], a_b_in[i], a_conv_w[i],
                                  a_conv_b[i], a_gate_a_w[i], a_gate_a_b[i], a_gate_i_w[i], a_gate_i_b[i],
                                  a_lambda[i], a_w_out[i], ln_g[i, 0], ln_b[i, 0])
                     for b, xb in enumerate(xs)]
        else:
            if i == N_A_LAYERS:
                kv = []
                for xb in xs:
                    k_blk, vt_blk, k_mean = _shared_kv(xb, seq, kv_ln_g, kv_ln_b, w_kv, cos_t, sin_t)
                    kv.append((k_blk, vt_blk, k_mean.reshape(seq // MOBA_BLOCK, d)))
            j = i - N_A_LAYERS
            mixed = _moba_layer([(xb,) + kvb for xb, kvb in zip(xs, kv)], seq, b_w_q[j], b_w_o[j],
                                ln_g[i, 0], ln_b[i, 0], cos_t, sin_t)
        xs = _moe_layer(mixed, i, router_w[i], router_b[i], exp_w_gate, exp_w_up, exp_w_down,
                        sh_w_gate[i], sh_w_up[i], sh_w_down[i], ln_g[i, 1], ln_b[i, 1],
                        p_all, i * batch, ple_w[i], ple_gate_w[i])
    return jnp.stack(xs, axis=0)
```

```python
import functools

import jax
import jax.numpy as jnp
from jax import lax
from jax.experimental import pallas as pl
from jax.experimental.pallas import tpu as pltpu
from jax.experimental.pallas import tpu_sc as plsc

F32 = jnp.float32
BF16 = jnp.bfloat16
I32 = jnp.int32
U32 = jnp.uint32
HIGH_HALF = 0xFFFF0000

SC_CORES = 2
SC_SUBCORES = 16
SC_WORKERS = SC_CORES * SC_SUBCORES
SC_LANES = 16
SC_INDEX_CHUNK = 128
SC_STAGE_BYTES = 256 * 1024

D_MODEL = 1024
DEPTH = 4
N_A_LAYERS = DEPTH // 2
D_RNN = D_MODEL
LRU_BLOCKS = 4
LRU_BLOCK_W = D_RNN // LRU_BLOCKS
CONV_W = 4
LRU_C = 8.0
N_HEADS = 8
HEAD_DIM = D_MODEL // N_HEADS
ROT_DIM = HEAD_DIM // 4
ROPE_THETA = 500000.0
MOBA_BLOCK = 256
MOBA_TOPK = 3
N_EXPERTS = 64
N_GROUPS = 8
GROUP_SIZE = N_EXPERTS // N_GROUPS
TOPK_GROUPS = 4
TOPK_EXPERTS = 8
D_EXPERT = 256
D_SHARED = 256
ROUTED_SCALE = 2.5
PLE_DIM = 256
DN_ALPHA = (2 * DEPTH) ** 0.25
LN_EPS = 1e-5

V7X_VMEM_LIMIT_BYTES = 56 * 1024 * 1024

MIXER_ROWS = 512
TOKEN_ROWS = 1024
EXPERT_ROWS = 2304
NEG_INF = float("-inf")


def _params(*sem):
    return pltpu.CompilerParams(dimension_semantics=sem, vmem_limit_bytes=V7X_VMEM_LIMIT_BYTES)


def _layer_norm(z, g, b):
    mu = jnp.mean(z, axis=-1, keepdims=True)
    zc = z - mu
    var = jnp.mean(zc * zc, axis=-1, keepdims=True)
    return zc * lax.rsqrt(var + LN_EPS) * g + b


def _silu(x):
    return x * jax.nn.sigmoid(x)


def _gelu_tanh(x):
    return x * jax.nn.sigmoid(x * (1.5957691216057308 + 0.07135481627159855 * (x * x)))


def _full(shape):
    return pl.BlockSpec(shape, lambda *_: (0,) * len(shape))


def _count_le(ends, values):
    return jnp.sum((ends[None, :] <= values[:, None]).astype(I32), axis=1)


def _pack_bf16_pairs(x):
    w = x.shape[1] // 2
    bits = lax.bitcast_convert_type(x.astype(BF16).astype(F32), U32)
    return (bits[:, :w] >> 16) | (bits[:, w:] & U32(HIGH_HALF))


def _unpack_bf16_pairs(u):
    lo = lax.bitcast_convert_type(u << 16, F32)
    hi = lax.bitcast_convert_type(u & U32(HIGH_HALF), F32)
    return lo, hi


def _sc_mesh():
    return plsc.VectorSubcoreMesh(core_axis_name="c", subcore_axis_name="s")


def _sc_worker_id():
    return lax.axis_index("s") * SC_CORES + lax.axis_index("c")


def _sc_chunks_per_step(chunks_per_worker, row_words):
    g = max(1, SC_STAGE_BYTES // (SC_INDEX_CHUNK * row_words * 4))
    while chunks_per_worker % g:
        g -= 1
    return g


def _sc_gather_rows(table, idx):
    b = idx.shape[0]
    w = table.shape[1]
    chunk = SC_INDEX_CHUNK
    chunks_per_worker = b // (SC_WORKERS * chunk)
    assert chunks_per_worker * SC_WORKERS * chunk == b
    g = _sc_chunks_per_step(chunks_per_worker, w)

    @functools.partial(
        pl.kernel, mesh=_sc_mesh(), out_type=jax.ShapeDtypeStruct((b, w), table.dtype),
        scratch_types=[pltpu.VMEM((g, chunk), I32), pltpu.VMEM((g * chunk, w), table.dtype),
                       pltpu.SemaphoreType.DMA])
    def gather(table_hbm, idx_hbm, out_hbm, idx_v, rows_v, sem):
        first = _sc_worker_id() * chunks_per_worker

        @pl.loop(0, chunks_per_worker // g)
        def _(j):
            c0 = first + j * g
            pltpu.sync_copy(idx_hbm.at[pl.ds(c0, g)], idx_v)
            copies = [pltpu.async_copy(table_hbm.at[idx_v.at[q]], rows_v.at[pl.ds(q * chunk, chunk)], sem)
                      for q in range(g)]
            for cp in copies:
                cp.wait()
            pltpu.sync_copy(rows_v, out_hbm.at[pl.ds(c0 * chunk, g * chunk)])

    return gather(table, idx.reshape(b // chunk, chunk))


def _sc_scatter_rows(src, idx, out_rows):
    n, w = src.shape
    chunk = SC_INDEX_CHUNK
    fan = idx.shape[0]
    chunks_per_worker = n // (SC_WORKERS * chunk)
    assert idx.shape == (fan, n // chunk, chunk) and chunks_per_worker * SC_WORKERS * chunk == n
    g = _sc_chunks_per_step(chunks_per_worker, w)

    @functools.partial(
        pl.kernel, mesh=_sc_mesh(), out_type=jax.ShapeDtypeStruct((out_rows, w), src.dtype),
        scratch_types=[pltpu.VMEM((fan, g, chunk), I32), pltpu.VMEM((g * chunk, w), src.dtype),
                       pltpu.SemaphoreType.DMA])
    def scatter(src_hbm, idx_hbm, out_hbm, idx_v, rows_v, sem):
        first = _sc_worker_id() * chunks_per_worker

        @pl.loop(0, chunks_per_worker // g)
        def _(j):
            c0 = first + j * g
            pltpu.sync_copy(src_hbm.at[pl.ds(c0 * chunk, g * chunk)], rows_v)
            for f in range(fan):
                pltpu.sync_copy(idx_hbm.at[f, pl.ds(c0, g)], idx_v.at[f])
            copies = [pltpu.async_copy(rows_v.at[pl.ds(q * chunk, chunk)], out_hbm.at[idx_v.at[f, q]], sem)
                      for q in range(g) for f in range(fan)]
            for cp in copies:
                cp.wait()

    return scatter(src, idx)


def _sc_weighted_gather(table, idx, wrep, fan):
    nf = idx.shape[0]
    n = nf // fan
    w = table.shape[1]
    chunk = SC_INDEX_CHUNK
    lanes = SC_LANES
    half = chunk // 2
    tokens = half // fan
    chunks_per_worker = nf // (SC_WORKERS * chunk)
    assert chunks_per_worker * SC_WORKERS * chunk == nf and w % lanes == 0 and wrep.shape == (n, fan * lanes)

    @functools.partial(
        pl.kernel, mesh=_sc_mesh(), out_type=jax.ShapeDtypeStruct((n, 2 * w), F32),
        scratch_types=[pltpu.VMEM((chunks_per_worker, chunk), I32), pltpu.VMEM((2, half, w), U32),
                       pltpu.VMEM((2, tokens, fan * lanes), F32), pltpu.VMEM((2, tokens, 2 * w), F32),
                       pltpu.SemaphoreType.DMA((2,)), pltpu.SemaphoreType.DMA((2,)),
                       pltpu.SemaphoreType.DMA((2,))],
        compiler_params=pltpu.CompilerParams(needs_layout_passes=False))
    def weighted_gather(table_hbm, idx_hbm, w_hbm, out_hbm, idx_v, rows_v, w_v, out_v,
                        row_sem, weight_sem, out_sem):
        worker = _sc_worker_id()
        first_token = worker * (2 * chunks_per_worker * tokens)
        pltpu.sync_copy(idx_hbm.at[pl.ds(worker * chunks_per_worker, chunks_per_worker)], idx_v)

        def loads(pair, slot):
            step = 2 * pair + slot
            return (pltpu.make_async_copy(table_hbm.at[idx_v.at[pair, pl.ds(slot * half, half)]],
                                          rows_v.at[slot], row_sem.at[slot]),
                    pltpu.make_async_copy(w_hbm.at[pl.ds(first_token + step * tokens, tokens)],
                                          w_v.at[slot], weight_sem.at[slot]))

        def store(pair, slot):
            step = 2 * pair + slot
            return pltpu.make_async_copy(out_v.at[slot], out_hbm.at[pl.ds(first_token + step * tokens, tokens)],
                                         out_sem.at[slot])

        def reduce_rows(slot):
            @pl.loop(0, tokens)
            def _(t):
                weights = [w_v[slot, t, pl.ds(q * lanes, lanes)] for q in range(fan)]

                @pl.loop(0, w, step=lanes)
                def _(c):
                    acc_lo = jnp.zeros((lanes,), F32)
                    acc_hi = jnp.zeros((lanes,), F32)
                    for q in range(fan):
                        u = rows_v[slot, t * fan + q, pl.ds(c, lanes)]
                        acc_lo = acc_lo + weights[q] * lax.bitcast_convert_type(u << 16, F32)
                        acc_hi = acc_hi + weights[q] * lax.bitcast_convert_type(u & U32(HIGH_HALF), F32)
                    out_v[slot, t, pl.ds(c, lanes)] = acc_lo
                    out_v[slot, t, pl.ds(w + c, lanes)] = acc_hi

        for copy in loads(0, 0):
            copy.start()

        @pl.loop(0, chunks_per_worker)
        def _(pair):
            for slot in range(2):
                for copy in loads(pair, slot):
                    copy.wait()
                if slot == 0:
                    for copy in loads(pair, 1):
                        copy.start()
                else:
                    @pl.when(pair + 1 < chunks_per_worker)
                    def _():
                        for copy in loads(pair + 1, 0):
                            copy.start()

                @pl.when(pair > 0)
                def _():
                    store(pair - 1, slot).wait()
                reduce_rows(slot)
                store(pair, slot).start()

        for slot in range(2):
            store(chunks_per_worker - 1, slot).wait()

    return weighted_gather(table, idx.reshape(nf // chunk, chunk), wrep)


def _rglru_kernel(x_ref, win_ref, bin_ref, cw_ref, cb_ref, gaw_ref, gab_ref, giw_ref, gib_ref,
                  lam_ref, wout_ref, lng_ref, lnb_ref, o_ref, op_ref, tail_ref, h_ref, *, tiles_per_seq):
    i = pl.program_id(0)
    tm = x_ref.shape[0]

    @pl.when(i % tiles_per_seq == 0)
    def _():
        tail_ref[...] = jnp.zeros_like(tail_ref)
        h_ref[...] = jnp.zeros_like(h_ref)

    x = x_ref[...]
    xy = jnp.dot(x.astype(BF16), win_ref[...], preferred_element_type=F32) + bin_ref[...]
    xb = xy[:, :D_RNN]
    y = _gelu_tanh(xy[:, D_RNN:])

    tail = tail_ref[...]
    row8 = lax.broadcasted_iota(I32, (8, 1), 0)
    xc = cb_ref[...] + xb * cw_ref[CONV_W - 1:CONV_W, :]
    for d in range(1, CONV_W):
        rolled = pltpu.roll(xb, d, 0)
        head = jnp.where(row8 < d, pltpu.roll(tail, d, 0), rolled[:8])
        shifted = jnp.concatenate([head, rolled[8:]], axis=0)
        xc = xc + shifted * cw_ref[CONV_W - 1 - d:CONV_W - d, :]
    tail_ref[...] = xb[tm - 8:, :]

    r_parts, i_parts = [], []
    for n in range(LRU_BLOCKS):
        xg = xc[:, n * LRU_BLOCK_W:(n + 1) * LRU_BLOCK_W].astype(BF16)
        r_parts.append(jnp.dot(xg, gaw_ref[n], preferred_element_type=F32))
        i_parts.append(jnp.dot(xg, giw_ref[n], preferred_element_type=F32))
    r = jax.nn.sigmoid(jnp.concatenate(r_parts, axis=1) + gab_ref[...])
    ig = jax.nn.sigmoid(jnp.concatenate(i_parts, axis=1) + gib_ref[...])

    lam = lam_ref[...]
    softplus_neg_lam = jnp.maximum(-lam, 0.0) + jnp.log1p(jnp.exp(-jnp.abs(lam)))
    log_a = (-LRU_C * r) * softplus_neg_lam
    a = jnp.exp(log_a)
    u = jnp.sqrt(1.0 - a * a) * (ig * xc)

    n_grp = tm // 8
    sub = lax.broadcasted_iota(I32, (1, 8, 1), 1)
    acc_a = a.reshape(n_grp, 8, D_RNN)
    acc_h = u.reshape(n_grp, 8, D_RNN)
    for d in (1, 2, 4):
        keep = sub >= d
        sh_a = pltpu.roll(acc_a, d, 1)
        sh_h = pltpu.roll(acc_h, d, 1)
        acc_h = jnp.where(keep, acc_a * sh_h + acc_h, acc_h)
        acc_a = jnp.where(keep, acc_a * sh_a, acc_a)
    state = h_ref[...]
    groups = []
    for g in range(n_grp):
        hg = acc_h[g] + acc_a[g] * state
        state = hg[7:8, :]
        groups.append(hg)
    h = jnp.concatenate(groups, axis=0)
    h_ref[...] = state

    mix = jnp.dot((h * y).astype(BF16), wout_ref[...], preferred_element_type=F32)
    x1 = _layer_norm(DN_ALPHA * x + mix, lng_ref[...], lnb_ref[...])
    o_ref[...] = x1
    op_ref[...] = _pack_bf16_pairs(x1)


def _rglru_layer(x, seq, stream, w_in, b_in, conv_w, conv_b, ga_w, ga_b, gi_w, gi_b, lam, w_out,
                 ln_g, ln_b):
    n, d = seq, x.shape[1]
    tm = min(MIXER_ROWS, seq)
    first_tile = stream * (seq // tm)
    row = lambda v: v.reshape(1, -1).astype(F32)
    return pl.pallas_call(
        functools.partial(_rglru_kernel, tiles_per_seq=seq // tm),
        grid=(n // tm,),
        in_specs=[pl.BlockSpec((tm, d), lambda i: (i + first_tile, 0)),
                  _full((d, 2 * D_RNN)), _full((1, 2 * D_RNN)),
                  _full((CONV_W, D_RNN)), _full((1, D_RNN)),
                  _full((LRU_BLOCKS, LRU_BLOCK_W, LRU_BLOCK_W)), _full((1, D_RNN)),
                  _full((LRU_BLOCKS, LRU_BLOCK_W, LRU_BLOCK_W)), _full((1, D_RNN)),
                  _full((1, D_RNN)), _full((D_RNN, d)), _full((1, d)), _full((1, d))],
        out_specs=[pl.BlockSpec((tm, d), lambda i: (i, 0)), pl.BlockSpec((tm, d // 2), lambda i: (i, 0))],
        out_shape=[jax.ShapeDtypeStruct((n, d), F32), jax.ShapeDtypeStruct((n, d // 2), U32)],
        scratch_shapes=[pltpu.VMEM((8, D_RNN), F32), pltpu.VMEM((1, D_RNN), F32)],
        compiler_params=_params("arbitrary"),
        name="rglru_mixer",
    )(x, w_in.astype(BF16), row(b_in), conv_w, row(conv_b), ga_w.astype(BF16), row(ga_b),
      gi_w.astype(BF16), row(gi_b), row(lam), w_out.astype(BF16), row(ln_g), row(ln_b))


def _first_argmax(cur, idx, size, axis):
    m = jnp.max(cur, axis=axis, keepdims=True)
    first = jnp.min(jnp.where(cur == m, idx, size), axis=axis, keepdims=True)
    return m, first


def _router_kernel(x_ref, rw_ref, rb_ref, e_ref, w_ref, r_ref, cnt_ref, carry_ref):
    i = pl.program_id(0)
    tm = x_ref.shape[0]

    @pl.when(i == 0)
    def _():
        carry_ref[...] = jnp.zeros_like(carry_ref)

    x = x_ref[...]
    x_hi = x.astype(BF16)
    x_lo = (x - x_hi.astype(F32)).astype(BF16)
    rw = rw_ref[...]
    hi_both = jnp.dot(x_hi, rw, preferred_element_type=F32)
    logits = (hi_both[:, :128] + hi_both[:, 128:]
              + jnp.dot(x_lo, rw[:, :128], preferred_element_type=F32))
    scores = jax.nn.sigmoid(logits.T[:N_EXPERTS, :])
    choice = scores + rb_ref[...]

    c3 = choice.reshape(N_GROUPS, GROUP_SIZE, tm)
    in_grp = lax.broadcasted_iota(I32, c3.shape, 1)
    m1, i1 = _first_argmax(c3, in_grp, GROUP_SIZE, 1)
    m2 = jnp.max(jnp.where(in_grp == i1, NEG_INF, c3), axis=1, keepdims=True)
    grp_score = (m1 + m2)[:, 0, :]

    grp_id = lax.broadcasted_iota(I32, grp_score.shape, 0)
    grp_sel = jnp.zeros(grp_score.shape, jnp.bool_)
    cur = grp_score
    for _ in range(TOPK_GROUPS):
        _, gi = _first_argmax(cur, grp_id, N_GROUPS, 0)
        hit = grp_id == gi
        grp_sel = jnp.logical_or(grp_sel, hit)
        cur = jnp.where(hit, NEG_INF, cur)

    cur = jnp.where(grp_sel[:, None, :], c3, NEG_INF).reshape(N_EXPERTS, tm)
    exp_id = lax.broadcasted_iota(I32, cur.shape, 0)
    sel = jnp.zeros(cur.shape, F32)
    e_rows, s_rows = [], []
    for _ in range(TOPK_EXPERTS):
        _, ei = _first_argmax(cur, exp_id, N_EXPERTS, 0)
        hit = exp_id == ei
        e_rows.append(ei)
        s_rows.append(jnp.sum(jnp.where(hit, scores, 0.0), axis=0, keepdims=True))
        sel = jnp.where(hit, 1.0, sel)
        cur = jnp.where(hit, NEG_INF, cur)
    e_top = jnp.concatenate(e_rows, axis=0)
    s_top = jnp.concatenate(s_rows, axis=0)
    w_top = s_top / jnp.sum(s_top, axis=0, keepdims=True) * ROUTED_SCALE
    w_ref[...] = jnp.broadcast_to(w_top[:, None, :], (TOPK_EXPERTS, SC_LANES, tm)).reshape(
        TOPK_EXPERTS * SC_LANES, tm).T
    e_ref[...] = e_top

    t_row = lax.broadcasted_iota(I32, (tm, tm), 0)
    t_col = lax.broadcasted_iota(I32, (tm, tm), 1)
    before = (t_row < t_col).astype(BF16)
    cum = jnp.dot(sel.astype(BF16), before, preferred_element_type=F32) + carry_ref[...]
    r_rows = [jnp.sum(jnp.where(exp_id == e_rows[k], cum, 0.0), axis=0, keepdims=True)
              for k in range(TOPK_EXPERTS)]
    r_ref[...] = jnp.concatenate(r_rows, axis=0).astype(I32)
    carry_ref[...] = carry_ref[...] + jnp.sum(sel, axis=1, keepdims=True)
    cnt_ref[...] = jnp.broadcast_to(carry_ref[...], cnt_ref.shape).astype(I32)


def _router(x, router_w, router_b):
    n, d = x.shape
    tm = min(TOKEN_ROWS, n)
    rw = jnp.pad(router_w, ((0, 0), (0, 128 - N_EXPERTS)))
    rw_hi = rw.astype(BF16)
    rw = jnp.concatenate([rw_hi, (rw - rw_hi.astype(F32)).astype(BF16)], axis=1)
    k = TOPK_EXPERTS
    tok_spec = pl.BlockSpec((k, tm), lambda i: (0, i))
    return pl.pallas_call(
        _router_kernel,
        grid=(n // tm,),
        in_specs=[pl.BlockSpec((tm, d), lambda i: (i, 0)), _full((d, 256)), _full((N_EXPERTS, 1))],
        out_specs=[tok_spec, pl.BlockSpec((tm, k * SC_LANES), lambda i: (i, 0)), tok_spec,
                   _full((N_EXPERTS, 128))],
        out_shape=[jax.ShapeDtypeStruct((k, n), I32), jax.ShapeDtypeStruct((n, k * SC_LANES), F32),
                   jax.ShapeDtypeStruct((k, n), I32), jax.ShapeDtypeStruct((N_EXPERTS, 128), I32)],
        scratch_shapes=[pltpu.VMEM((N_EXPERTS, 1), F32)],
        compiler_params=_params("arbitrary"),
        name="moe_router",
    )(x, rw, router_b.reshape(N_EXPERTS, 1).astype(F32))


def _expert_kernel(blk_e_ref, nblk_ref, first_ref, slot_ref, next_ref, xs_ref, wg_hbm, wu_hbm, wd_hbm,
                   after_ref, ys_ref, wg_f, wu_f, wd_f, wg_s, wu_s, wd_s, sem, *, layer):
    del after_ref
    i = pl.program_id(0)

    def fetch(expert, slot):
        return (pltpu.make_async_copy(wg_hbm.at[layer, expert], wg_f.at[slot], sem.at[slot, 0]),
                pltpu.make_async_copy(wu_hbm.at[layer, expert], wu_f.at[slot], sem.at[slot, 1]),
                pltpu.make_async_copy(wd_hbm.at[layer, expert], wd_f.at[slot], sem.at[slot, 2]))

    @pl.when(i < nblk_ref[0])
    def _():
        slot = slot_ref[i]

        @pl.when(i == 0)
        def _():
            for copy in fetch(blk_e_ref[0], 0):
                copy.start()

        @pl.when(first_ref[i] == 1)
        def _():
            for copy in fetch(blk_e_ref[i], slot):
                copy.wait()
            wg_s[...] = wg_f[slot].astype(BF16)
            wu_s[...] = wu_f[slot].astype(BF16)
            wd_s[...] = wd_f[slot].astype(BF16)

            @pl.when(next_ref[i] >= 0)
            def _():
                for copy in fetch(next_ref[i], 1 - slot):
                    copy.start()

        lo, hi = _unpack_bf16_pairs(xs_ref[...])
        xs = jnp.concatenate([lo.astype(BF16), hi.astype(BF16)], axis=1)
        g = jnp.dot(xs, wg_s[...], preferred_element_type=F32)
        u = jnp.dot(xs, wu_s[...], preferred_element_type=F32)
        hdn = (_silu(g) * u).astype(BF16)
        ys_ref[...] = _pack_bf16_pairs(jnp.dot(hdn, wd_s[...], preferred_element_type=F32))


_ORDER_ONLY = pl.BlockSpec(memory_space=pl.ANY)


def _experts(xs, blk_e, nblk, layer, w_gate, w_up, w_down, after):
    rows, half = xs.shape
    d = 2 * half
    m = EXPERT_ROWS
    n_blocks = rows // m
    idx = jnp.arange(n_blocks, dtype=I32)
    used = idx < nblk[0]
    first = jnp.logical_and(used, jnp.logical_or(idx == 0, blk_e != jnp.roll(blk_e, 1)))
    slot = (jnp.cumsum(first.astype(I32)) - 1) % 2
    larger = jnp.logical_and(blk_e[None, :] > blk_e[:, None], used[None, :])
    next_e = jnp.min(jnp.where(larger, blk_e[None, :], N_EXPERTS), axis=1)
    next_e = jnp.where(next_e < N_EXPERTS, next_e, -1)
    row_map = lambda i, be, nb, fi, sl, nx: (jnp.minimum(i, nb[0] - 1), 0)
    hbm = pl.BlockSpec(memory_space=pl.ANY)
    return pl.pallas_call(
        functools.partial(_expert_kernel, layer=layer),
        grid_spec=pltpu.PrefetchScalarGridSpec(
            num_scalar_prefetch=5,
            grid=(n_blocks,),
            in_specs=[pl.BlockSpec((m, half), row_map), hbm, hbm, hbm, _ORDER_ONLY],
            out_specs=pl.BlockSpec((m, half), row_map),
            scratch_shapes=[pltpu.VMEM((2, d, D_EXPERT), F32), pltpu.VMEM((2, d, D_EXPERT), F32),
                            pltpu.VMEM((2, D_EXPERT, d), F32),
                            pltpu.VMEM((d, D_EXPERT), BF16), pltpu.VMEM((d, D_EXPERT), BF16),
                            pltpu.VMEM((D_EXPERT, d), BF16), pltpu.SemaphoreType.DMA((2, 3))]),
        out_shape=jax.ShapeDtypeStruct((rows, half), U32),
        compiler_params=_params("arbitrary"),
        name="moe_experts",
    )(blk_e, nblk, first.astype(I32), slot.astype(I32), next_e.astype(I32), xs, w_gate, w_up, w_down, after)


def _combine_kernel(x_ref, moe_ref, sg_ref, su_ref, sd_ref, lng_ref, lnb_ref,
                    p_ref, pw_ref, pg_ref, after_ref, o_ref):
    del after_ref
    x = x_ref[...]
    moe = moe_ref[...]
    xb = x.astype(BF16)
    hdn = _silu(jnp.dot(xb, sg_ref[...], preferred_element_type=F32)) * \
        jnp.dot(xb, su_ref[...], preferred_element_type=F32)
    shared = jnp.dot(hdn.astype(BF16), sd_ref[...], preferred_element_type=F32)
    x2 = _layer_norm(DN_ALPHA * x + (moe + shared), lng_ref[...], lnb_ref[...])
    emb = jnp.dot(p_ref[0].astype(BF16), pw_ref[...], preferred_element_type=F32)
    gate = jax.nn.sigmoid(jnp.dot(x2.astype(BF16), pg_ref[...], preferred_element_type=F32))
    o_ref[...] = x2 + emb * gate


def _combine(x, moe, s_gate, s_up, s_down, ln_g, ln_b, p_all, p_index, ple_w, ple_gate_w, after,
             out_streams=1, out_stream=0):
    n, d = x.shape
    tm = min(TOKEN_ROWS, n)
    first_tile = out_stream * (n // tm)
    row = lambda v: v.reshape(1, -1).astype(F32)
    return pl.pallas_call(
        _combine_kernel,
        grid=(n // tm,),
        input_output_aliases={10: 0} if out_stream > 0 else {},
        in_specs=[pl.BlockSpec((tm, d), lambda i: (i, 0)),
                  pl.BlockSpec((tm, d), lambda i: (i, 0)),
                  _full((d, D_SHARED)), _full((d, D_SHARED)), _full((D_SHARED, d)),
                  _full((1, d)), _full((1, d)),
                  pl.BlockSpec((1, tm, PLE_DIM), lambda i: (p_index, i, 0)),
                  _full((PLE_DIM, d)), _full((d, d)), _ORDER_ONLY],
        out_specs=pl.BlockSpec((tm, d), lambda i: (i + first_tile, 0)),
        out_shape=jax.ShapeDtypeStruct((out_streams * n, d), F32),
        compiler_params=_params("parallel"),
        name="moe_combine",
    )(x, moe, s_gate.astype(BF16), s_up.astype(BF16), s_down.astype(BF16), row(ln_g), row(ln_b),
      p_all, ple_w.astype(BF16), ple_gate_w.astype(BF16), after)


def _positions_kernel(e_ref, r_ref, start_ref, pos_ref):
    e = e_ref[...]
    start = start_ref[...]
    grp_id = lax.broadcasted_iota(I32, (start.shape[0], e.shape[1]), 0)
    rows = [jnp.sum(jnp.where(grp_id == e[k:k + 1], start, 0), axis=0, keepdims=True)
            for k in range(e.shape[0])]
    pos_ref[...] = jnp.concatenate(rows, axis=0) + r_ref[...]


def _positions(e_idx, rank, starts):
    k, n = e_idx.shape
    tm = min(4 * TOKEN_ROWS, n)
    groups = starts.shape[0]
    tok_spec = pl.BlockSpec((k, tm), lambda i: (0, i))
    return pl.pallas_call(
        _positions_kernel,
        grid=(n // tm,),
        in_specs=[tok_spec, tok_spec, _full((groups, 1))],
        out_specs=tok_spec,
        out_shape=jax.ShapeDtypeStruct((k, n), I32),
        compiler_params=_params("parallel"),
        name="group_positions",
    )(e_idx, rank, starts.reshape(groups, 1))


def _moe_layer(streams, layer, router_w, router_b, w_gate, w_up, w_down, s_gate, s_up, s_down,
               ln_g, ln_b, p_all, p_index, ple_w, ple_gate_w, merge_output):
    m = EXPERT_ROWS
    k = TOPK_EXPERTS
    chunk = SC_INDEX_CHUNK
    routed = []
    for x, xp in streams:
        n, d = x.shape
        e_idx, wgt, rank, cnt = _router(x, router_w, router_b)
        counts = cnt[:, 0]
        padded = (counts + m - 1) // m * m
        pends = jnp.cumsum(padded)
        n_blocks = (n * k) // m + N_EXPERTS
        blk_first = jnp.arange(n_blocks, dtype=I32) * m
        blk_e = jnp.minimum(_count_le(pends, blk_first), N_EXPERTS - 1)
        nblk = (pends[-1] // m).astype(I32).reshape(1)
        pos = _positions(e_idx, rank, (pends - padded).astype(I32))
        xs = _sc_scatter_rows(xp, pos.reshape(k, n // chunk, chunk), n_blocks * m)
        routed.append((xs, blk_e, nblk, pos, wgt))
    gathered, order = [], routed[-1][3]
    for (x, _), (xs, blk_e, nblk, pos, wgt) in zip(streams, routed):
        n, d = x.shape
        ys = _experts(xs, blk_e, nblk, layer, w_gate, w_up, w_down, order)
        gathered.append(_sc_weighted_gather(ys, pos.T.reshape(-1), wgt, k))
        order = ys
    outs = []
    for s, ((x, _), moe) in enumerate(zip(streams, gathered)):
        outs.append(_combine(x, moe, s_gate, s_up, s_down, ln_g, ln_b, p_all, p_index + s,
                             ple_w, ple_gate_w, order, *((len(streams), s) if merge_output else ())))
        order = outs[-1]
    return outs[-1] if merge_output else outs


def _rope(t, cos_t, sin_t):
    half = ROT_DIM // 2
    width = t.shape[1]
    lane = lax.broadcasted_iota(I32, (1, width), 1) % HEAD_DIM
    partner = jnp.where(lane < half, pltpu.roll(t, width - half, 1), pltpu.roll(t, half, 1))
    cos_f = jnp.concatenate([cos_t] * N_HEADS, axis=1)
    sin_f = jnp.concatenate([sin_t] * N_HEADS, axis=1)
    return t * cos_f + partner * sin_f


def _kv_kernel(x_ref, g_ref, b_ref, wkv_ref, cos_ref, sin_ref, k_ref, vt_ref, km_ref):
    h = _layer_norm(x_ref[...], g_ref[...], b_ref[...])
    kv = jnp.dot(h.astype(BF16), wkv_ref[...], preferred_element_type=F32)
    k = _rope(kv[:, :D_MODEL], cos_ref[...], sin_ref[...])
    vt = kv[:, D_MODEL:].T
    km_ref[0] = jnp.mean(k, axis=0, keepdims=True)
    for hd in range(N_HEADS):
        k_ref[hd, 0] = k[:, hd * HEAD_DIM:(hd + 1) * HEAD_DIM].astype(BF16)
        vt_ref[hd, 0] = vt[hd * HEAD_DIM:(hd + 1) * HEAD_DIM, :].astype(BF16)


def _shared_kv(x, seq, ln_g, ln_b, w_kv, cos_t, sin_t):
    n, d = x.shape
    blk = MOBA_BLOCK
    nbt = n // blk
    spb = seq // blk
    row = lambda v: v.reshape(1, -1).astype(F32)
    return pl.pallas_call(
        _kv_kernel,
        grid=(nbt,),
        in_specs=[pl.BlockSpec((blk, d), lambda i: (i, 0)), _full((1, d)), _full((1, d)),
                  _full((d, 2 * d)),
                  pl.BlockSpec((blk, HEAD_DIM), lambda i: (i % spb, 0)),
                  pl.BlockSpec((blk, HEAD_DIM), lambda i: (i % spb, 0))],
        out_specs=[pl.BlockSpec((N_HEADS, 1, blk, HEAD_DIM), lambda i: (0, i, 0, 0)),
                   pl.BlockSpec((N_HEADS, 1, HEAD_DIM, blk), lambda i: (0, i, 0, 0)),
                   pl.BlockSpec((1, 1, d), lambda i: (i, 0, 0))],
        out_shape=[jax.ShapeDtypeStruct((N_HEADS, nbt, blk, HEAD_DIM), BF16),
                   jax.ShapeDtypeStruct((N_HEADS, nbt, HEAD_DIM, blk), BF16),
                   jax.ShapeDtypeStruct((nbt, 1, d), F32)],
        compiler_params=_params("parallel"),
        name="shared_kv",
    )(x, row(ln_g), row(ln_b), w_kv.astype(BF16), cos_t, sin_t)


SEL_ROWS = 8


def _query_kernel(x_ref, wq_ref, cos_ref, sin_ref, km_ref, qp_ref, sel_ref, cnt_ref, carry_ref,
                  *, blocks_per_seq):
    i = pl.program_id(0)
    tm = x_ref.shape[0]
    nb = km_ref.shape[0]
    own = i % blocks_per_seq

    @pl.when(own == 0)
    def _():
        carry_ref[...] = jnp.zeros_like(carry_ref)

    q = jnp.dot(x_ref[...].astype(BF16), wq_ref[...], preferred_element_type=F32)
    q = _rope(q, cos_ref[...], sin_ref[...]) * (HEAD_DIM ** -0.5)

    t_row = lax.broadcasted_iota(I32, (tm, tm), 0)
    t_col = lax.broadcasted_iota(I32, (tm, tm), 1)
    before = (t_row < t_col).astype(BF16)
    km = km_ref[...]
    km_hi = km.astype(BF16)
    km_both = jnp.concatenate([km_hi, (km - km_hi.astype(F32)).astype(BF16)], axis=0)
    q_hi = q.astype(BF16)
    q_lo = (q - q_hi.astype(F32)).astype(BF16)
    nt = (((1,), (1,)), ((), ()))
    gates = []
    for hd in range(N_HEADS):
        lo, hi = hd * HEAD_DIM, (hd + 1) * HEAD_DIM
        qp_ref[0, hd] = q[:, lo:hi]
        both = lax.dot_general(km_both[:, lo:hi], q_hi[:, lo:hi], nt, preferred_element_type=F32)
        gates.append(both[:nb] + both[nb:]
                     + lax.dot_general(km_hi[:, lo:hi], q_lo[:, lo:hi], nt, preferred_element_type=F32))

    gate = jnp.concatenate(gates, axis=0).reshape(N_HEADS, nb, tm)
    blk_id = lax.broadcasted_iota(I32, (N_HEADS, nb, tm), 1)
    cur = jnp.where(blk_id < own, gate, NEG_INF)
    sel_rows = []
    chosen = jnp.zeros((N_HEADS, nb, tm), F32)
    for _ in range(MOBA_TOPK):
        m, bi = _first_argmax(cur, blk_id, nb, 1)
        valid = m > NEG_INF
        hit = jnp.logical_and(blk_id == bi, valid)
        sel_rows.append(jnp.where(valid, bi, -1))
        chosen = jnp.where(hit, 1.0, chosen)
        cur = jnp.where(blk_id == bi, NEG_INF, cur)
    chosen2 = chosen.reshape(N_HEADS * nb, tm)
    carry = carry_ref[...]
    cum = (jnp.dot(chosen2.astype(BF16), before, preferred_element_type=F32) + carry).reshape(N_HEADS, nb, tm)
    rank_rows = [jnp.sum(jnp.where(blk_id == s, cum, 0.0), axis=1, keepdims=True).astype(I32)
                 for s in sel_rows]
    carry = carry + jnp.sum(chosen2, axis=1, keepdims=True)
    carry_ref[...] = carry
    pad = jnp.zeros((N_HEADS, SEL_ROWS - 2 * MOBA_TOPK, tm), I32)
    table = jnp.concatenate(sel_rows + rank_rows + [pad], axis=1)
    sel_ref[...] = table.reshape(N_HEADS * SEL_ROWS, tm)
    cnt_ref[0] = jnp.broadcast_to(carry, cnt_ref.shape[1:]).astype(I32)


def _queries(x, seq, w_q, cos_t, sin_t, k_mean):
    n, d = x.shape
    blk = MOBA_BLOCK
    nb = seq // blk
    return pl.pallas_call(
        functools.partial(_query_kernel, blocks_per_seq=nb),
        grid=(n // blk,),
        in_specs=[pl.BlockSpec((blk, d), lambda i: (i, 0)), _full((d, d)),
                  pl.BlockSpec((blk, HEAD_DIM), lambda i: (i % nb, 0)),
                  pl.BlockSpec((blk, HEAD_DIM), lambda i: (i % nb, 0)),
                  pl.BlockSpec((nb, d), lambda i: (i // nb, 0))],
        out_specs=[pl.BlockSpec((1, N_HEADS, blk, HEAD_DIM), lambda i: (i, 0, 0, 0)),
                   pl.BlockSpec((N_HEADS * SEL_ROWS, blk), lambda i: (0, i)),
                   pl.BlockSpec((1, N_HEADS * nb, 128), lambda i: (i // nb, 0, 0))],
        out_shape=[jax.ShapeDtypeStruct((n // blk, N_HEADS, blk, HEAD_DIM), F32),
                   jax.ShapeDtypeStruct((N_HEADS * SEL_ROWS, n), I32),
                   jax.ShapeDtypeStruct((n // seq, N_HEADS * nb, 128), I32)],
        scratch_shapes=[pltpu.VMEM((N_HEADS * nb, 1), F32)],
        compiler_params=_params("arbitrary"),
        name="moba_queries",
    )(x, w_q.astype(BF16), cos_t, sin_t, k_mean)


def _moba_positions_kernel(sel_ref, start_ref, pos_ref, *, dump_row):
    tm = sel_ref.shape[1]
    nb = start_ref.shape[1] // N_HEADS
    blk_id = lax.broadcasted_iota(I32, (nb, tm), 0)
    dump = dump_row + lax.broadcasted_iota(I32, (1, tm), 1) % SC_INDEX_CHUNK
    rows = []
    for hd in range(N_HEADS):
        start = start_ref[0, hd * nb:(hd + 1) * nb, :]
        for s in range(MOBA_TOPK):
            sel = sel_ref[hd * SEL_ROWS + s:hd * SEL_ROWS + s + 1, :]
            rank = sel_ref[hd * SEL_ROWS + MOBA_TOPK + s:hd * SEL_ROWS + MOBA_TOPK + s + 1, :]
            base = jnp.sum(jnp.where(blk_id == sel, start, 0), axis=0, keepdims=True)
            rows.append(jnp.where(sel >= 0, base + rank, dump))
        rows.extend([dump] * (SEL_ROWS - MOBA_TOPK))
    pos_ref[...] = jnp.concatenate(rows, axis=0)


def _moba_positions(sel, starts, seq, dump_row):
    rows, n = sel.shape
    tm = min(4 * TOKEN_ROWS, seq)
    tps = seq // tm
    groups = starts.shape[1]
    return pl.pallas_call(
        functools.partial(_moba_positions_kernel, dump_row=dump_row),
        grid=(n // tm,),
        in_specs=[pl.BlockSpec((rows, tm), lambda i: (0, i)),
                  pl.BlockSpec((1, groups, 1), lambda i: (i // tps, 0, 0))],
        out_specs=pl.BlockSpec((rows, tm), lambda i: (0, i)),
        out_shape=jax.ShapeDtypeStruct((rows, n), I32),
        compiler_params=_params("parallel"),
        name="moba_positions",
    )(sel, starts)


ATTN_SUB_ROWS = 128
ATTN_SUBS_PER_STEP = 32
ATTN_STEP_ROWS = ATTN_SUB_ROWS * ATTN_SUBS_PER_STEP


def _pack_bf16_row_pairs(x):
    h = x.shape[0] // 2
    bits = lax.bitcast_convert_type(x.astype(BF16).astype(F32), U32)
    return (bits[:h] >> 16) | (bits[h:] & U32(HIGH_HALF))


def _transpose_u32(x):
    return lax.bitcast_convert_type(lax.bitcast_convert_type(x, I32).T, U32)


def _scores_t(k, q, keep):
    st = lax.dot_general(k, q, (((1,), (1,)), ((), ())), preferred_element_type=F32)
    if keep is not None:
        st = jnp.where(keep, st, NEG_INF)
    m = jnp.max(st, axis=0, keepdims=True)
    p = jnp.exp(st - m)
    return m, p, jnp.sum(p, axis=0, keepdims=True)


def _group_attn_kernel(step_h_ref, step_b_ref, sub_j_ref, nsteps_ref, qs_ref, k_ref, vt_ref, after_ref,
                       part_ref):
    del step_h_ref, step_b_ref, after_ref
    i = pl.program_id(0)
    half = HEAD_DIM // 2

    @pl.when(i < nsteps_ref[0])
    def _():
        for u in range(ATTN_SUBS_PER_STEP):
            j = sub_j_ref[i * ATTN_SUBS_PER_STEP + u]
            rows = pl.ds(u * ATTN_SUB_ROWS, ATTN_SUB_ROWS)
            q = qs_ref[rows, :].astype(BF16)
            m, p, l = _scores_t(k_ref[0, j], q, None)
            ot = jnp.dot(vt_ref[0, j], p.astype(BF16), preferred_element_type=F32) * (1.0 / l)
            lse = lax.bitcast_convert_type(m + jnp.log(l), U32)
            packed_t = jnp.concatenate(
                [_pack_bf16_row_pairs(ot), jnp.broadcast_to(lse, (half, ATTN_SUB_ROWS))], axis=0)
            part_ref[rows, :] = _transpose_u32(packed_t)


def _group_attention(qs, k_blk, vt_blk, step_h, step_b, sub_j, nsteps, nb, after):
    n_steps = step_h.shape[0]
    row_map = lambda i, sh, sb, sj, ns: (jnp.minimum(i, ns[0] - 1), 0)
    kv_map = lambda i, sh, sb, sj, ns: (sh[i], sb[i], 0, 0)
    return pl.pallas_call(
        _group_attn_kernel,
        grid_spec=pltpu.PrefetchScalarGridSpec(
            num_scalar_prefetch=4,
            grid=(n_steps,),
            in_specs=[pl.BlockSpec((ATTN_STEP_ROWS, HEAD_DIM), row_map),
                      pl.BlockSpec((1, nb, MOBA_BLOCK, HEAD_DIM), kv_map),
                      pl.BlockSpec((1, nb, HEAD_DIM, MOBA_BLOCK), kv_map), _ORDER_ONLY],
            out_specs=pl.BlockSpec((ATTN_STEP_ROWS, HEAD_DIM), row_map)),
        out_shape=jax.ShapeDtypeStruct(((n_steps + 1) * ATTN_STEP_ROWS, HEAD_DIM), U32),
        compiler_params=_params("arbitrary"),
        name="moba_group_attention",
    )(step_h, step_b, sub_j, nsteps, qs, k_blk, vt_blk, after)


def _attn_merge_kernel(x_ref, qp_ref, k_ref, vt_ref, pg_ref, sel_ref, wo_ref, lng_ref, lnb_ref,
                       after_ref, o_ref, op_ref):
    del after_ref
    tm = x_ref.shape[0]
    half = HEAD_DIM // 2
    sel = sel_ref[...]
    key = lax.broadcasted_iota(I32, (tm, tm), 0)
    qry = lax.broadcasted_iota(I32, (tm, tm), 1)
    causal = key <= qry
    heads = []
    for hd in range(N_HEADS):
        q = qp_ref[0, hd].astype(BF16)
        m_own, p, l_own = _scores_t(k_ref[hd, 0], q, causal)
        acc = jnp.dot(vt_ref[hd, 0], p.astype(BF16), preferred_element_type=F32)

        outs, lses = [], []
        m_tot = m_own
        for c in range(MOBA_TOPK):
            part_t = _transpose_u32(pg_ref[c, 0, hd])
            lo, hi = _unpack_bf16_pairs(part_t[:half])
            valid = sel[hd * SEL_ROWS + c:hd * SEL_ROWS + c + 1, :] >= 0
            outs.append(jnp.where(valid, jnp.concatenate([lo, hi], axis=0), 0.0))
            lse = jnp.where(valid, lax.bitcast_convert_type(part_t[half:half + 1], F32), NEG_INF)
            lses.append(lse)
            m_tot = jnp.maximum(m_tot, lse)
        w_own = jnp.exp(m_own - m_tot)
        num = acc * w_own
        den = l_own * w_own
        for c in range(MOBA_TOPK):
            w_c = jnp.exp(lses[c] - m_tot)
            num = num + outs[c] * w_c
            den = den + w_c
        heads.append(num * (1.0 / den))
    att = jnp.concatenate(heads, axis=0).T.astype(BF16)
    mix = jnp.dot(att, wo_ref[...], preferred_element_type=F32)
    x1 = _layer_norm(DN_ALPHA * x_ref[...] + mix, lng_ref[...], lnb_ref[...])
    o_ref[...] = x1
    op_ref[...] = _pack_bf16_pairs(x1)


def _attn_merge(x, qp, k_blk, vt_blk, pg, sel, w_o, ln_g, ln_b, after):
    n, d = x.shape
    blk = MOBA_BLOCK
    row = lambda v: v.reshape(1, -1).astype(F32)
    return pl.pallas_call(
        _attn_merge_kernel,
        grid=(n // blk,),
        in_specs=[pl.BlockSpec((blk, d), lambda i: (i, 0)),
                  pl.BlockSpec((1, N_HEADS, blk, HEAD_DIM), lambda i: (i, 0, 0, 0)),
                  pl.BlockSpec((N_HEADS, 1, blk, HEAD_DIM), lambda i: (0, i, 0, 0)),
                  pl.BlockSpec((N_HEADS, 1, HEAD_DIM, blk), lambda i: (0, i, 0, 0)),
                  pl.BlockSpec((MOBA_TOPK, 1, N_HEADS, blk, HEAD_DIM), lambda i: (0, i, 0, 0, 0)),
                  pl.BlockSpec((N_HEADS * SEL_ROWS, blk), lambda i: (0, i)),
                  _full((d, d)), _full((1, d)), _full((1, d)), _ORDER_ONLY],
        out_specs=[pl.BlockSpec((blk, d), lambda i: (i, 0)), pl.BlockSpec((blk, d // 2), lambda i: (i, 0))],
        out_shape=[jax.ShapeDtypeStruct((n, d), F32), jax.ShapeDtypeStruct((n, d // 2), U32)],
        compiler_params=_params("parallel"),
        name="moba_merge",
    )(x, qp, k_blk, vt_blk, pg, sel, w_o.astype(BF16), row(ln_g), row(ln_b), after)


def _moba_layer(streams, seq, w_q, w_o, ln_g, ln_b, cos_t, sin_t):
    staged = [_moba_regroup(x, seq, w_q, cos_t, sin_t, k_mean) for x, _, _, k_mean in streams]
    gathered, order = [], staged[-1][1]
    for (x, k_blk, vt_blk, _), (qp, sel, qs, tables, gather_idx) in zip(streams, staged):
        nbt = x.shape[0] // MOBA_BLOCK
        part = _group_attention(qs, k_blk, vt_blk, *tables, seq // MOBA_BLOCK, order)
        gathered.append(_sc_gather_rows(part, gather_idx).reshape(MOBA_TOPK, nbt, N_HEADS, MOBA_BLOCK, HEAD_DIM))
        order = part
    outs = []
    for (x, k_blk, vt_blk, _), (qp, sel, _, _, _), pg in zip(streams, staged, gathered):
        outs.append(_attn_merge(x, qp, k_blk, vt_blk, pg, sel, w_o, ln_g, ln_b, order))
        order = outs[-1][0]
    return outs


def _moba_regroup(x, seq, w_q, cos_t, sin_t, k_mean):
    n, d = x.shape
    batch = n // seq
    blk = MOBA_BLOCK
    nb = seq // blk
    nbt = n // blk
    chunk = SC_INDEX_CHUNK
    qp, sel, cnt = _queries(x, seq, w_q, cos_t, sin_t, k_mean)

    counts = cnt[:, :, 0].reshape(batch * N_HEADS, nb)
    gpad = (counts + ATTN_SUB_ROWS - 1) // ATTN_SUB_ROWS * ATTN_SUB_ROWS
    seg = jnp.sum(gpad, axis=1)
    seg_pad = (seg + ATTN_STEP_ROWS - 1) // ATTN_STEP_ROWS * ATTN_STEP_ROWS
    seg_end = jnp.cumsum(seg_pad)
    gend = (seg_end - seg_pad)[:, None] + jnp.cumsum(gpad, axis=1)
    gstart = (gend - gpad).astype(I32)
    steps_per_seg = -(-(MOBA_TOPK * seq + nb * (ATTN_SUB_ROWS - 1)) // ATTN_STEP_ROWS)
    n_steps = batch * N_HEADS * steps_per_seg
    step_first = jnp.arange(n_steps, dtype=I32) * ATTN_STEP_ROWS
    step_seg = jnp.minimum(_count_le(seg_end, step_first), batch * N_HEADS - 1)
    sub_first = jnp.arange(n_steps * ATTN_SUBS_PER_STEP, dtype=I32) * ATTN_SUB_ROWS
    sub_grp = jnp.minimum(_count_le(gend.reshape(-1), sub_first), batch * N_HEADS * nb - 1)
    nsteps = (seg_end[-1] // ATTN_STEP_ROWS).astype(I32).reshape(1)
    dump_row = n_steps * ATTN_STEP_ROWS

    pos = _moba_positions(sel, gstart.reshape(batch, N_HEADS * nb, 1), seq, dump_row)
    pos5 = pos.reshape(N_HEADS, SEL_ROWS, nbt, blk // chunk, chunk)[:, :MOBA_TOPK]
    by_choice = pos5.transpose(1, 2, 0, 3, 4)
    gather_idx = by_choice.reshape(-1)
    scatter_idx = by_choice.reshape(MOBA_TOPK, nbt * N_HEADS * (blk // chunk), chunk)

    qs = _sc_scatter_rows(qp.reshape(n * N_HEADS, HEAD_DIM), scatter_idx, dump_row + chunk)
    tables = ((step_seg % N_HEADS).astype(I32), (step_seg // N_HEADS).astype(I32),
              (sub_grp % nb).astype(I32), nsteps)
    return qp, sel, qs, tables, gather_idx


def _rope_tables(seq):
    half = ROT_DIM // 2
    lane = jnp.arange(HEAD_DIM)
    inv = ROPE_THETA ** (-(2 * (lane % half)).astype(F32) / ROT_DIM)
    ang = jnp.arange(seq, dtype=F32)[:, None] * jnp.where(lane < ROT_DIM, inv, 0.0)[None, :]
    sign = jnp.where(lane < half, -1.0, 1.0).astype(F32)
    return jnp.cos(ang), jnp.sin(ang) * sign[None, :]


def kernel(x, p, ln_g, ln_b, a_w_in, a_b_in, a_conv_w, a_conv_b, a_gate_a_w, a_gate_a_b, a_gate_i_w,
           a_gate_i_b, a_lambda, a_w_out, kv_ln_g, kv_ln_b, w_kv, b_w_q, b_w_o, router_w, router_b,
           exp_w_gate, exp_w_up, exp_w_down, sh_w_gate, sh_w_up, sh_w_down, ple_w, ple_gate_w):
    batch, seq, d = x.shape
    xs = [x.reshape(batch * seq, d)] * batch
    p_all = p.reshape(DEPTH * batch, seq, PLE_DIM)
    cos_t, sin_t = _rope_tables(seq)
    kv = None
    for i in range(DEPTH):
        if i < N_A_LAYERS:
            mixed = [_rglru_layer(xb, seq, b if i == 0 else 0, a_w_in[i], a_b_in[i], a_conv_w[i],
                                  a_conv_b[i], a_gate_a_w[i], a_gate_a_b[i], a_gate_i_w[i], a_gate_i_b[i],
                                  a_lambda[i], a_w_out[i], ln_g[i, 0], ln_b[i, 0])
                     for b, xb in enumerate(xs)]
        else:
            if i == N_A_LAYERS:
                kv = []
                for xb in xs:
                    k_blk, vt_blk, k_mean = _shared_kv(xb, seq, kv_ln_g, kv_ln_b, w_kv, cos_t, sin_t)
                    kv.append((k_blk, vt_blk, k_mean.reshape(seq // MOBA_BLOCK, d)))
            j = i - N_A_LAYERS
            mixed = _moba_layer([(xb,) + kvb for xb, kvb in zip(xs, kv)], seq, b_w_q[j], b_w_o[j],
                                ln_g[i, 0], ln_b[i, 0], cos_t, sin_t)
        xs = _moe_layer(mixed, i, router_w[i], router_b[i], exp_w_gate, exp_w_up, exp_w_down,
                        sh_w_gate[i], sh_w_up[i], sh_w_down[i], ln_g[i, 1], ln_b[i, 1],
                        p_all, i * batch, ple_w[i], ple_gate_w[i], i == DEPTH - 1)
    return xs.reshape(batch, seq, d)
```

```python
import functools

import jax
import jax.numpy as jnp
from jax import lax
from jax.experimental import pallas as pl
from jax.experimental.pallas import tpu as pltpu
from jax.experimental.pallas import tpu_sc as plsc

F32 = jnp.float32
BF16 = jnp.bfloat16
I32 = jnp.int32
U32 = jnp.uint32
HIGH_HALF = 0xFFFF0000

SC_CORES = 2
SC_SUBCORES = 16
SC_WORKERS = SC_CORES * SC_SUBCORES
SC_LANES = 16
SC_INDEX_CHUNK = 128
SC_STAGE_BYTES = 256 * 1024

D_MODEL = 1024
DEPTH = 4
N_A_LAYERS = DEPTH // 2
D_RNN = D_MODEL
LRU_BLOCKS = 4
LRU_BLOCK_W = D_RNN // LRU_BLOCKS
CONV_W = 4
LRU_C = 8.0
N_HEADS = 8
HEAD_DIM = D_MODEL // N_HEADS
ROT_DIM = HEAD_DIM // 4
ROPE_THETA = 500000.0
MOBA_BLOCK = 256
MOBA_TOPK = 3
N_EXPERTS = 64
N_GROUPS = 8
GROUP_SIZE = N_EXPERTS // N_GROUPS
TOPK_GROUPS = 4
TOPK_EXPERTS = 8
D_EXPERT = 256
D_SHARED = 256
ROUTED_SCALE = 2.5
PLE_DIM = 256
DN_ALPHA = (2 * DEPTH) ** 0.25
LN_EPS = 1e-5

V7X_VMEM_LIMIT_BYTES = 56 * 1024 * 1024

MIXER_ROWS = 1024
TOKEN_ROWS = 1024
EXPERT_ROWS = 2304
NEG_INF = float("-inf")


def _params(*sem):
    return pltpu.CompilerParams(dimension_semantics=sem, vmem_limit_bytes=V7X_VMEM_LIMIT_BYTES)


def _layer_norm(z, g, b):
    mu = jnp.mean(z, axis=-1, keepdims=True)
    zc = z - mu
    var = jnp.mean(zc * zc, axis=-1, keepdims=True)
    return zc * lax.rsqrt(var + LN_EPS) * g + b


def _silu(x):
    return x * jax.nn.sigmoid(x)


def _gelu_tanh(x):
    return x * jax.nn.sigmoid(x * (1.5957691216057308 + 0.07135481627159855 * (x * x)))


def _full(shape):
    return pl.BlockSpec(shape, lambda *_: (0,) * len(shape))


def _count_le(ends, values):
    return jnp.sum((ends[None, :] <= values[:, None]).astype(I32), axis=1)


def _pack_bf16_pairs(x):
    w = x.shape[1] // 2
    bits = lax.bitcast_convert_type(x.astype(BF16).astype(F32), U32)
    return (bits[:, :w] >> 16) | (bits[:, w:] & U32(HIGH_HALF))


def _unpack_bf16_pairs(u):
    lo = lax.bitcast_convert_type(u << 16, F32)
    hi = lax.bitcast_convert_type(u & U32(HIGH_HALF), F32)
    return lo, hi


def _sc_mesh():
    return plsc.VectorSubcoreMesh(core_axis_name="c", subcore_axis_name="s")


def _sc_worker_id():
    return lax.axis_index("s") * SC_CORES + lax.axis_index("c")


def _sc_chunks_per_step(chunks_per_worker, row_words):
    g = max(1, SC_STAGE_BYTES // (SC_INDEX_CHUNK * row_words * 4))
    while chunks_per_worker % g:
        g -= 1
    return g


def _sc_gather_rows(table, idx):
    b = idx.shape[0]
    w = table.shape[1]
    chunk = SC_INDEX_CHUNK
    chunks_per_worker = b // (SC_WORKERS * chunk)
    assert chunks_per_worker * SC_WORKERS * chunk == b
    g = _sc_chunks_per_step(chunks_per_worker, w)

    @functools.partial(
        pl.kernel, mesh=_sc_mesh(), out_type=jax.ShapeDtypeStruct((b, w), table.dtype),
        scratch_types=[pltpu.VMEM((g, chunk), I32), pltpu.VMEM((g * chunk, w), table.dtype),
                       pltpu.SemaphoreType.DMA])
    def gather(table_hbm, idx_hbm, out_hbm, idx_v, rows_v, sem):
        first = _sc_worker_id() * chunks_per_worker

        @pl.loop(0, chunks_per_worker // g)
        def _(j):
            c0 = first + j * g
            pltpu.sync_copy(idx_hbm.at[pl.ds(c0, g)], idx_v)
            copies = [pltpu.async_copy(table_hbm.at[idx_v.at[q]], rows_v.at[pl.ds(q * chunk, chunk)], sem)
                      for q in range(g)]
            for cp in copies:
                cp.wait()
            pltpu.sync_copy(rows_v, out_hbm.at[pl.ds(c0 * chunk, g * chunk)])

    return gather(table, idx.reshape(b // chunk, chunk))


def _sc_scatter_rows(src, idx, out_rows):
    n, w = src.shape
    chunk = SC_INDEX_CHUNK
    fan = idx.shape[0]
    chunks_per_worker = n // (SC_WORKERS * chunk)
    assert idx.shape == (fan, n // chunk, chunk) and chunks_per_worker * SC_WORKERS * chunk == n
    g = _sc_chunks_per_step(chunks_per_worker, w)

    @functools.partial(
        pl.kernel, mesh=_sc_mesh(), out_type=jax.ShapeDtypeStruct((out_rows, w), src.dtype),
        scratch_types=[pltpu.VMEM((fan, g, chunk), I32), pltpu.VMEM((g * chunk, w), src.dtype),
                       pltpu.SemaphoreType.DMA])
    def scatter(src_hbm, idx_hbm, out_hbm, idx_v, rows_v, sem):
        first = _sc_worker_id() * chunks_per_worker

        @pl.loop(0, chunks_per_worker // g)
        def _(j):
            c0 = first + j * g
            pltpu.sync_copy(src_hbm.at[pl.ds(c0 * chunk, g * chunk)], rows_v)
            for f in range(fan):
                pltpu.sync_copy(idx_hbm.at[f, pl.ds(c0, g)], idx_v.at[f])
            copies = [pltpu.async_copy(rows_v.at[pl.ds(q * chunk, chunk)], out_hbm.at[idx_v.at[f, q]], sem)
                      for q in range(g) for f in range(fan)]
            for cp in copies:
                cp.wait()

    return scatter(src, idx)


def _sc_weighted_gather(table, idx, wrep, fan):
    nf = idx.shape[0]
    n = nf // fan
    w = table.shape[1]
    chunk = SC_INDEX_CHUNK
    lanes = SC_LANES
    half = chunk // 2
    tokens = half // fan
    chunks_per_worker = nf // (SC_WORKERS * chunk)
    assert chunks_per_worker * SC_WORKERS * chunk == nf and w % lanes == 0 and wrep.shape == (n, fan * lanes)

    @functools.partial(
        pl.kernel, mesh=_sc_mesh(), out_type=jax.ShapeDtypeStruct((n, 2 * w), F32),
        scratch_types=[pltpu.VMEM((chunks_per_worker, chunk), I32), pltpu.VMEM((2, half, w), U32),
                       pltpu.VMEM((2, tokens, fan * lanes), F32), pltpu.VMEM((2, tokens, 2 * w), F32),
                       pltpu.SemaphoreType.DMA((2,)), pltpu.SemaphoreType.DMA((2,)),
                       pltpu.SemaphoreType.DMA((2,))],
        compiler_params=pltpu.CompilerParams(needs_layout_passes=False))
    def weighted_gather(table_hbm, idx_hbm, w_hbm, out_hbm, idx_v, rows_v, w_v, out_v,
                        row_sem, weight_sem, out_sem):
        worker = _sc_worker_id()
        first_token = worker * (2 * chunks_per_worker * tokens)
        pltpu.sync_copy(idx_hbm.at[pl.ds(worker * chunks_per_worker, chunks_per_worker)], idx_v)

        def loads(pair, slot):
            step = 2 * pair + slot
            return (pltpu.make_async_copy(table_hbm.at[idx_v.at[pair, pl.ds(slot * half, half)]],
                                          rows_v.at[slot], row_sem.at[slot]),
                    pltpu.make_async_copy(w_hbm.at[pl.ds(first_token + step * tokens, tokens)],
                                          w_v.at[slot], weight_sem.at[slot]))

        def store(pair, slot):
            step = 2 * pair + slot
            return pltpu.make_async_copy(out_v.at[slot], out_hbm.at[pl.ds(first_token + step * tokens, tokens)],
                                         out_sem.at[slot])

        def reduce_rows(slot):
            @pl.loop(0, tokens)
            def _(t):
                weights = [w_v[slot, t, pl.ds(q * lanes, lanes)] for q in range(fan)]

                @pl.loop(0, w, step=lanes)
                def _(c):
                    acc_lo = jnp.zeros((lanes,), F32)
                    acc_hi = jnp.zeros((lanes,), F32)
                    for q in range(fan):
                        u = rows_v[slot, t * fan + q, pl.ds(c, lanes)]
                        acc_lo = acc_lo + weights[q] * lax.bitcast_convert_type(u << 16, F32)
                        acc_hi = acc_hi + weights[q] * lax.bitcast_convert_type(u & U32(HIGH_HALF), F32)
                    out_v[slot, t, pl.ds(c, lanes)] = acc_lo
                    out_v[slot, t, pl.ds(w + c, lanes)] = acc_hi

        for copy in loads(0, 0):
            copy.start()

        @pl.loop(0, chunks_per_worker)
        def _(pair):
            for slot in range(2):
                for copy in loads(pair, slot):
                    copy.wait()
                if slot == 0:
                    for copy in loads(pair, 1):
                        copy.start()
                else:
                    @pl.when(pair + 1 < chunks_per_worker)
                    def _():
                        for copy in loads(pair + 1, 0):
                            copy.start()

                @pl.when(pair > 0)
                def _():
                    store(pair - 1, slot).wait()
                reduce_rows(slot)
                store(pair, slot).start()

        for slot in range(2):
            store(chunks_per_worker - 1, slot).wait()

    return weighted_gather(table, idx.reshape(nf // chunk, chunk), wrep)


def _rglru_kernel(x_ref, win_ref, bin_ref, cw_ref, cb_ref, gaw_ref, gab_ref, giw_ref, gib_ref,
                  lam_ref, wout_ref, lng_ref, lnb_ref, o_ref, op_ref, tail_ref, h_ref, *, tiles_per_seq):
    i = pl.program_id(0)
    tm = x_ref.shape[0]

    @pl.when(i % tiles_per_seq == 0)
    def _():
        tail_ref[...] = jnp.zeros_like(tail_ref)
        h_ref[...] = jnp.zeros_like(h_ref)

    x = x_ref[...]
    xy = jnp.dot(x.astype(BF16), win_ref[...], preferred_element_type=F32) + bin_ref[...]
    xb = xy[:, :D_RNN]
    y = _gelu_tanh(xy[:, D_RNN:])

    tail = tail_ref[...]
    row8 = lax.broadcasted_iota(I32, (8, 1), 0)
    xc = cb_ref[...] + xb * cw_ref[CONV_W - 1:CONV_W, :]
    for d in range(1, CONV_W):
        rolled = pltpu.roll(xb, d, 0)
        head = jnp.where(row8 < d, pltpu.roll(tail, d, 0), rolled[:8])
        shifted = jnp.concatenate([head, rolled[8:]], axis=0)
        xc = xc + shifted * cw_ref[CONV_W - 1 - d:CONV_W - d, :]
    tail_ref[...] = xb[tm - 8:, :]

    r_parts, i_parts = [], []
    for n in range(LRU_BLOCKS):
        xg = xc[:, n * LRU_BLOCK_W:(n + 1) * LRU_BLOCK_W].astype(BF16)
        r_parts.append(jnp.dot(xg, gaw_ref[n], preferred_element_type=F32))
        i_parts.append(jnp.dot(xg, giw_ref[n], preferred_element_type=F32))
    r = jax.nn.sigmoid(jnp.concatenate(r_parts, axis=1) + gab_ref[...])
    ig = jax.nn.sigmoid(jnp.concatenate(i_parts, axis=1) + gib_ref[...])

    lam = lam_ref[...]
    softplus_neg_lam = jnp.maximum(-lam, 0.0) + jnp.log1p(jnp.exp(-jnp.abs(lam)))
    log_a = (-LRU_C * r) * softplus_neg_lam
    a = jnp.exp(log_a)
    u = jnp.sqrt(1.0 - a * a) * (ig * xc)

    n_grp = tm // 8
    sub = lax.broadcasted_iota(I32, (1, 8, 1), 1)
    acc_a = a.reshape(n_grp, 8, D_RNN)
    acc_h = u.reshape(n_grp, 8, D_RNN)
    for d in (1, 2, 4):
        keep = sub >= d
        sh_a = pltpu.roll(acc_a, d, 1)
        sh_h = pltpu.roll(acc_h, d, 1)
        acc_h = jnp.where(keep, acc_a * sh_h + acc_h, acc_h)
        acc_a = jnp.where(keep, acc_a * sh_a, acc_a)
    state = h_ref[...]
    groups = []
    for g in range(n_grp):
        hg = acc_h[g] + acc_a[g] * state
        state = hg[7:8, :]
        groups.append(hg)
    h = jnp.concatenate(groups, axis=0)
    h_ref[...] = state

    mix = jnp.dot((h * y).astype(BF16), wout_ref[...], preferred_element_type=F32)
    x1 = _layer_norm(DN_ALPHA * x + mix, lng_ref[...], lnb_ref[...])
    o_ref[...] = x1
    op_ref[...] = _pack_bf16_pairs(x1)


def _rglru_layer(x, seq, stream, w_in, b_in, conv_w, conv_b, ga_w, ga_b, gi_w, gi_b, lam, w_out,
                 ln_g, ln_b):
    n, d = seq, x.shape[1]
    tm = min(MIXER_ROWS, seq)
    first_tile = stream * (seq // tm)
    row = lambda v: v.reshape(1, -1).astype(F32)
    return pl.pallas_call(
        functools.partial(_rglru_kernel, tiles_per_seq=seq // tm),
        grid=(n // tm,),
        in_specs=[pl.BlockSpec((tm, d), lambda i: (i + first_tile, 0)),
                  _full((d, 2 * D_RNN)), _full((1, 2 * D_RNN)),
                  _full((CONV_W, D_RNN)), _full((1, D_RNN)),
                  _full((LRU_BLOCKS, LRU_BLOCK_W, LRU_BLOCK_W)), _full((1, D_RNN)),
                  _full((LRU_BLOCKS, LRU_BLOCK_W, LRU_BLOCK_W)), _full((1, D_RNN)),
                  _full((1, D_RNN)), _full((D_RNN, d)), _full((1, d)), _full((1, d))],
        out_specs=[pl.BlockSpec((tm, d), lambda i: (i, 0)), pl.BlockSpec((tm, d // 2), lambda i: (i, 0))],
        out_shape=[jax.ShapeDtypeStruct((n, d), F32), jax.ShapeDtypeStruct((n, d // 2), U32)],
        scratch_shapes=[pltpu.VMEM((8, D_RNN), F32), pltpu.VMEM((1, D_RNN), F32)],
        compiler_params=_params("arbitrary"),
        name="rglru_mixer",
    )(x, w_in.astype(BF16), row(b_in), conv_w, row(conv_b), ga_w.astype(BF16), row(ga_b),
      gi_w.astype(BF16), row(gi_b), row(lam), w_out.astype(BF16), row(ln_g), row(ln_b))


def _first_argmax(cur, idx, size, axis):
    m = jnp.max(cur, axis=axis, keepdims=True)
    first = jnp.min(jnp.where(cur == m, idx, size), axis=axis, keepdims=True)
    return m, first


def _router_kernel(x_ref, rw_ref, rb_ref, e_ref, w_ref, r_ref, cnt_ref, carry_ref):
    i = pl.program_id(0)
    tm = x_ref.shape[0]

    @pl.when(i == 0)
    def _():
        carry_ref[...] = jnp.zeros_like(carry_ref)

    x = x_ref[...]
    x_hi = x.astype(BF16)
    x_lo = (x - x_hi.astype(F32)).astype(BF16)
    rw = rw_ref[...]
    hi_both = jnp.dot(x_hi, rw, preferred_element_type=F32)
    logits = (hi_both[:, :128] + hi_both[:, 128:]
              + jnp.dot(x_lo, rw[:, :128], preferred_element_type=F32))
    scores = jax.nn.sigmoid(logits.T[:N_EXPERTS, :])
    choice = scores + rb_ref[...]

    c3 = choice.reshape(N_GROUPS, GROUP_SIZE, tm)
    in_grp = lax.broadcasted_iota(I32, c3.shape, 1)
    m1, i1 = _first_argmax(c3, in_grp, GROUP_SIZE, 1)
    m2 = jnp.max(jnp.where(in_grp == i1, NEG_INF, c3), axis=1, keepdims=True)
    grp_score = (m1 + m2)[:, 0, :]

    grp_id = lax.broadcasted_iota(I32, grp_score.shape, 0)
    grp_sel = jnp.zeros(grp_score.shape, jnp.bool_)
    cur = grp_score
    for _ in range(TOPK_GROUPS):
        _, gi = _first_argmax(cur, grp_id, N_GROUPS, 0)
        hit = grp_id == gi
        grp_sel = jnp.logical_or(grp_sel, hit)
        cur = jnp.where(hit, NEG_INF, cur)

    cur = jnp.where(grp_sel[:, None, :], c3, NEG_INF).reshape(N_EXPERTS, tm)
    exp_id = lax.broadcasted_iota(I32, cur.shape, 0)
    sel = jnp.zeros(cur.shape, F32)
    e_rows, s_rows = [], []
    for _ in range(TOPK_EXPERTS):
        _, ei = _first_argmax(cur, exp_id, N_EXPERTS, 0)
        hit = exp_id == ei
        e_rows.append(ei)
        s_rows.append(jnp.sum(jnp.where(hit, scores, 0.0), axis=0, keepdims=True))
        sel = jnp.where(hit, 1.0, sel)
        cur = jnp.where(hit, NEG_INF, cur)
    e_top = jnp.concatenate(e_rows, axis=0)
    s_top = jnp.concatenate(s_rows, axis=0)
    w_top = s_top / jnp.sum(s_top, axis=0, keepdims=True) * ROUTED_SCALE
    w_ref[...] = jnp.broadcast_to(w_top[:, None, :], (TOPK_EXPERTS, SC_LANES, tm)).reshape(
        TOPK_EXPERTS * SC_LANES, tm).T
    e_ref[...] = e_top

    t_row = lax.broadcasted_iota(I32, (tm, tm), 0)
    t_col = lax.broadcasted_iota(I32, (tm, tm), 1)
    before = (t_row < t_col).astype(BF16)
    cum = jnp.dot(sel.astype(BF16), before, preferred_element_type=F32) + carry_ref[...]
    r_rows = [jnp.sum(jnp.where(exp_id == e_rows[k], cum, 0.0), axis=0, keepdims=True)
              for k in range(TOPK_EXPERTS)]
    r_ref[...] = jnp.concatenate(r_rows, axis=0).astype(I32)
    carry_ref[...] = carry_ref[...] + jnp.sum(sel, axis=1, keepdims=True)
    cnt_ref[...] = jnp.broadcast_to(carry_ref[...], cnt_ref.shape).astype(I32)


def _router(x, router_w, router_b):
    n, d = x.shape
    tm = min(TOKEN_ROWS, n)
    rw = jnp.pad(router_w, ((0, 0), (0, 128 - N_EXPERTS)))
    rw_hi = rw.astype(BF16)
    rw = jnp.concatenate([rw_hi, (rw - rw_hi.astype(F32)).astype(BF16)], axis=1)
    k = TOPK_EXPERTS
    tok_spec = pl.BlockSpec((k, tm), lambda i: (0, i))
    return pl.pallas_call(
        _router_kernel,
        grid=(n // tm,),
        in_specs=[pl.BlockSpec((tm, d), lambda i: (i, 0)), _full((d, 256)), _full((N_EXPERTS, 1))],
        out_specs=[tok_spec, pl.BlockSpec((tm, k * SC_LANES), lambda i: (i, 0)), tok_spec,
                   _full((N_EXPERTS, 128))],
        out_shape=[jax.ShapeDtypeStruct((k, n), I32), jax.ShapeDtypeStruct((n, k * SC_LANES), F32),
                   jax.ShapeDtypeStruct((k, n), I32), jax.ShapeDtypeStruct((N_EXPERTS, 128), I32)],
        scratch_shapes=[pltpu.VMEM((N_EXPERTS, 1), F32)],
        compiler_params=_params("arbitrary"),
        name="moe_router",
    )(x, rw, router_b.reshape(N_EXPERTS, 1).astype(F32))


def _expert_kernel(blk_e_ref, nblk_ref, first_ref, slot_ref, next_ref, xs_ref, wg_hbm, wu_hbm, wd_hbm,
                   after_ref, ys_ref, wg_f, wu_f, wd_f, wg_s, wu_s, wd_s, sem, *, layer):
    del after_ref
    i = pl.program_id(0)

    def fetch(expert, slot):
        return (pltpu.make_async_copy(wg_hbm.at[layer, expert], wg_f.at[slot], sem.at[slot, 0]),
                pltpu.make_async_copy(wu_hbm.at[layer, expert], wu_f.at[slot], sem.at[slot, 1]),
                pltpu.make_async_copy(wd_hbm.at[layer, expert], wd_f.at[slot], sem.at[slot, 2]))

    @pl.when(i < nblk_ref[0])
    def _():
        slot = slot_ref[i]

        @pl.when(i == 0)
        def _():
            for copy in fetch(blk_e_ref[0], 0):
                copy.start()

        @pl.when(first_ref[i] == 1)
        def _():
            for copy in fetch(blk_e_ref[i], slot):
                copy.wait()
            wg_s[...] = wg_f[slot].astype(BF16)
            wu_s[...] = wu_f[slot].astype(BF16)
            wd_s[...] = wd_f[slot].astype(BF16)

            @pl.when(next_ref[i] >= 0)
            def _():
                for copy in fetch(next_ref[i], 1 - slot):
                    copy.start()

        lo, hi = _unpack_bf16_pairs(xs_ref[...])
        xs = jnp.concatenate([lo.astype(BF16), hi.astype(BF16)], axis=1)
        g = jnp.dot(xs, wg_s[...], preferred_element_type=F32)
        u = jnp.dot(xs, wu_s[...], preferred_element_type=F32)
        hdn = (_silu(g) * u).astype(BF16)
        ys_ref[...] = _pack_bf16_pairs(jnp.dot(hdn, wd_s[...], preferred_element_type=F32))


_ORDER_ONLY = pl.BlockSpec(memory_space=pl.ANY)


def _experts(xs, blk_e, nblk, layer, w_gate, w_up, w_down, after):
    rows, half = xs.shape
    d = 2 * half
    m = EXPERT_ROWS
    n_blocks = rows // m
    idx = jnp.arange(n_blocks, dtype=I32)
    used = idx < nblk[0]
    first = jnp.logical_and(used, jnp.logical_or(idx == 0, blk_e != jnp.roll(blk_e, 1)))
    slot = (jnp.cumsum(first.astype(I32)) - 1) % 2
    larger = jnp.logical_and(blk_e[None, :] > blk_e[:, None], used[None, :])
    next_e = jnp.min(jnp.where(larger, blk_e[None, :], N_EXPERTS), axis=1)
    next_e = jnp.where(next_e < N_EXPERTS, next_e, -1)
    row_map = lambda i, be, nb, fi, sl, nx: (jnp.minimum(i, nb[0] - 1), 0)
    hbm = pl.BlockSpec(memory_space=pl.ANY)
    return pl.pallas_call(
        functools.partial(_expert_kernel, layer=layer),
        grid_spec=pltpu.PrefetchScalarGridSpec(
            num_scalar_prefetch=5,
            grid=(n_blocks,),
            in_specs=[pl.BlockSpec((m, half), row_map), hbm, hbm, hbm, _ORDER_ONLY],
            out_specs=pl.BlockSpec((m, half), row_map),
            scratch_shapes=[pltpu.VMEM((2, d, D_EXPERT), F32), pltpu.VMEM((2, d, D_EXPERT), F32),
                            pltpu.VMEM((2, D_EXPERT, d), F32),
                            pltpu.VMEM((d, D_EXPERT), BF16), pltpu.VMEM((d, D_EXPERT), BF16),
                            pltpu.VMEM((D_EXPERT, d), BF16), pltpu.SemaphoreType.DMA((2, 3))]),
        out_shape=jax.ShapeDtypeStruct((rows, half), U32),
        compiler_params=_params("arbitrary"),
        name="moe_experts",
    )(blk_e, nblk, first.astype(I32), slot.astype(I32), next_e.astype(I32), xs, w_gate, w_up, w_down, after)


def _combine_kernel(x_ref, moe_ref, sg_ref, su_ref, sd_ref, lng_ref, lnb_ref,
                    p_ref, pw_ref, pg_ref, after_ref, o_ref):
    del after_ref
    x = x_ref[...]
    moe = moe_ref[...]
    xb = x.astype(BF16)
    hdn = _silu(jnp.dot(xb, sg_ref[...], preferred_element_type=F32)) * \
        jnp.dot(xb, su_ref[...], preferred_element_type=F32)
    shared = jnp.dot(hdn.astype(BF16), sd_ref[...], preferred_element_type=F32)
    x2 = _layer_norm(DN_ALPHA * x + (moe + shared), lng_ref[...], lnb_ref[...])
    emb = jnp.dot(p_ref[0].astype(BF16), pw_ref[...], preferred_element_type=F32)
    gate = jax.nn.sigmoid(jnp.dot(x2.astype(BF16), pg_ref[...], preferred_element_type=F32))
    o_ref[...] = x2 + emb * gate


def _combine(x, moe, s_gate, s_up, s_down, ln_g, ln_b, p_all, p_index, ple_w, ple_gate_w, after,
             out_streams=1, out_stream=0):
    n, d = x.shape
    tm = min(TOKEN_ROWS, n)
    first_tile = out_stream * (n // tm)
    row = lambda v: v.reshape(1, -1).astype(F32)
    return pl.pallas_call(
        _combine_kernel,
        grid=(n // tm,),
        input_output_aliases={10: 0} if out_stream > 0 else {},
        in_specs=[pl.BlockSpec((tm, d), lambda i: (i, 0)),
                  pl.BlockSpec((tm, d), lambda i: (i, 0)),
                  _full((d, D_SHARED)), _full((d, D_SHARED)), _full((D_SHARED, d)),
                  _full((1, d)), _full((1, d)),
                  pl.BlockSpec((1, tm, PLE_DIM), lambda i: (p_index, i, 0)),
                  _full((PLE_DIM, d)), _full((d, d)), _ORDER_ONLY],
        out_specs=pl.BlockSpec((tm, d), lambda i: (i + first_tile, 0)),
        out_shape=jax.ShapeDtypeStruct((out_streams * n, d), F32),
        compiler_params=_params("parallel"),
        name="moe_combine",
    )(x, moe, s_gate.astype(BF16), s_up.astype(BF16), s_down.astype(BF16), row(ln_g), row(ln_b),
      p_all, ple_w.astype(BF16), ple_gate_w.astype(BF16), after)


def _positions_kernel(e_ref, r_ref, start_ref, pos_ref):
    e = e_ref[...]
    start = start_ref[...]
    grp_id = lax.broadcasted_iota(I32, (start.shape[0], e.shape[1]), 0)
    rows = [jnp.sum(jnp.where(grp_id == e[k:k + 1], start, 0), axis=0, keepdims=True)
            for k in range(e.shape[0])]
    pos_ref[...] = jnp.concatenate(rows, axis=0) + r_ref[...]


def _positions(e_idx, rank, starts):
    k, n = e_idx.shape
    tm = min(4 * TOKEN_ROWS, n)
    groups = starts.shape[0]
    tok_spec = pl.BlockSpec((k, tm), lambda i: (0, i))
    return pl.pallas_call(
        _positions_kernel,
        grid=(n // tm,),
        in_specs=[tok_spec, tok_spec, _full((groups, 1))],
        out_specs=tok_spec,
        out_shape=jax.ShapeDtypeStruct((k, n), I32),
        compiler_params=_params("parallel"),
        name="group_positions",
    )(e_idx, rank, starts.reshape(groups, 1))


def _moe_layer(streams, layer, router_w, router_b, w_gate, w_up, w_down, s_gate, s_up, s_down,
               ln_g, ln_b, p_all, p_index, ple_w, ple_gate_w, merge_output):
    m = EXPERT_ROWS
    k = TOPK_EXPERTS
    chunk = SC_INDEX_CHUNK
    routed = []
    for x, xp in streams:
        n, d = x.shape
        e_idx, wgt, rank, cnt = _router(x, router_w, router_b)
        counts = cnt[:, 0]
        padded = (counts + m - 1) // m * m
        pends = jnp.cumsum(padded)
        n_blocks = (n * k) // m + N_EXPERTS
        blk_first = jnp.arange(n_blocks, dtype=I32) * m
        blk_e = jnp.minimum(_count_le(pends, blk_first), N_EXPERTS - 1)
        nblk = (pends[-1] // m).astype(I32).reshape(1)
        pos = _positions(e_idx, rank, (pends - padded).astype(I32))
        xs = _sc_scatter_rows(xp, pos.reshape(k, n // chunk, chunk), n_blocks * m)
        routed.append((xs, blk_e, nblk, pos, wgt))
    gathered, order = [], routed[-1][3]
    for (x, _), (xs, blk_e, nblk, pos, wgt) in zip(streams, routed):
        n, d = x.shape
        ys = _experts(xs, blk_e, nblk, layer, w_gate, w_up, w_down, order)
        gathered.append(_sc_weighted_gather(ys, pos.T.reshape(-1), wgt, k))
        order = ys
    outs = []
    for s, ((x, _), moe) in enumerate(zip(streams, gathered)):
        outs.append(_combine(x, moe, s_gate, s_up, s_down, ln_g, ln_b, p_all, p_index + s,
                             ple_w, ple_gate_w, order, *((len(streams), s) if merge_output else ())))
        order = outs[-1]
    return outs[-1] if merge_output else outs


def _rope(t, cos_t, sin_t):
    half = ROT_DIM // 2
    width = t.shape[1]
    lane = lax.broadcasted_iota(I32, (1, width), 1) % HEAD_DIM
    partner = jnp.where(lane < half, pltpu.roll(t, width - half, 1), pltpu.roll(t, half, 1))
    cos_f = jnp.concatenate([cos_t] * N_HEADS, axis=1)
    sin_f = jnp.concatenate([sin_t] * N_HEADS, axis=1)
    return t * cos_f + partner * sin_f


def _kv_kernel(x_ref, g_ref, b_ref, wkv_ref, cos_ref, sin_ref, k_ref, vt_ref, km_ref):
    h = _layer_norm(x_ref[...], g_ref[...], b_ref[...])
    kv = jnp.dot(h.astype(BF16), wkv_ref[...], preferred_element_type=F32)
    k = _rope(kv[:, :D_MODEL], cos_ref[...], sin_ref[...])
    vt = kv[:, D_MODEL:].T
    km_ref[0] = jnp.mean(k, axis=0, keepdims=True)
    for hd in range(N_HEADS):
        k_ref[hd, 0] = k[:, hd * HEAD_DIM:(hd + 1) * HEAD_DIM].astype(BF16)
        vt_ref[hd, 0] = vt[hd * HEAD_DIM:(hd + 1) * HEAD_DIM, :].astype(BF16)


def _shared_kv(x, seq, ln_g, ln_b, w_kv, cos_t, sin_t):
    n, d = x.shape
    blk = MOBA_BLOCK
    nbt = n // blk
    spb = seq // blk
    row = lambda v: v.reshape(1, -1).astype(F32)
    return pl.pallas_call(
        _kv_kernel,
        grid=(nbt,),
        in_specs=[pl.BlockSpec((blk, d), lambda i: (i, 0)), _full((1, d)), _full((1, d)),
                  _full((d, 2 * d)),
                  pl.BlockSpec((blk, HEAD_DIM), lambda i: (i % spb, 0)),
                  pl.BlockSpec((blk, HEAD_DIM), lambda i: (i % spb, 0))],
        out_specs=[pl.BlockSpec((N_HEADS, 1, blk, HEAD_DIM), lambda i: (0, i, 0, 0)),
                   pl.BlockSpec((N_HEADS, 1, HEAD_DIM, blk), lambda i: (0, i, 0, 0)),
                   pl.BlockSpec((1, 1, d), lambda i: (i, 0, 0))],
        out_shape=[jax.ShapeDtypeStruct((N_HEADS, nbt, blk, HEAD_DIM), BF16),
                   jax.ShapeDtypeStruct((N_HEADS, nbt, HEAD_DIM, blk), BF16),
                   jax.ShapeDtypeStruct((nbt, 1, d), F32)],
        compiler_params=_params("parallel"),
        name="shared_kv",
    )(x, row(ln_g), row(ln_b), w_kv.astype(BF16), cos_t, sin_t)


SEL_ROWS = 8


def _query_kernel(x_ref, wq_ref, cos_ref, sin_ref, km_ref, qp_ref, sel_ref, cnt_ref, carry_ref,
                  *, blocks_per_seq):
    i = pl.program_id(0)
    tm = x_ref.shape[0]
    nb = km_ref.shape[0]
    own = i % blocks_per_seq

    @pl.when(own == 0)
    def _():
        carry_ref[...] = jnp.zeros_like(carry_ref)

    q = jnp.dot(x_ref[...].astype(BF16), wq_ref[...], preferred_element_type=F32)
    q = _rope(q, cos_ref[...], sin_ref[...]) * (HEAD_DIM ** -0.5)

    t_row = lax.broadcasted_iota(I32, (tm, tm), 0)
    t_col = lax.broadcasted_iota(I32, (tm, tm), 1)
    before = (t_row < t_col).astype(BF16)
    km = km_ref[...]
    km_hi = km.astype(BF16)
    km_both = jnp.concatenate([km_hi, (km - km_hi.astype(F32)).astype(BF16)], axis=0)
    q_hi = q.astype(BF16)
    q_lo = (q - q_hi.astype(F32)).astype(BF16)
    nt = (((1,), (1,)), ((), ()))
    gates = []
    for hd in range(N_HEADS):
        lo, hi = hd * HEAD_DIM, (hd + 1) * HEAD_DIM
        qp_ref[0, hd] = q[:, lo:hi]
        both = lax.dot_general(km_both[:, lo:hi], q_hi[:, lo:hi], nt, preferred_element_type=F32)
        gates.append(both[:nb] + both[nb:]
                     + lax.dot_general(km_hi[:, lo:hi], q_lo[:, lo:hi], nt, preferred_element_type=F32))

    gate = jnp.concatenate(gates, axis=0).reshape(N_HEADS, nb, tm)
    blk_id = lax.broadcasted_iota(I32, (N_HEADS, nb, tm), 1)
    cur = jnp.where(blk_id < own, gate, NEG_INF)
    sel_rows = []
    chosen = jnp.zeros((N_HEADS, nb, tm), F32)
    for _ in range(MOBA_TOPK):
        m, bi = _first_argmax(cur, blk_id, nb, 1)
        valid = m > NEG_INF
        hit = jnp.logical_and(blk_id == bi, valid)
        sel_rows.append(jnp.where(valid, bi, -1))
        chosen = jnp.where(hit, 1.0, chosen)
        cur = jnp.where(blk_id == bi, NEG_INF, cur)
    chosen2 = chosen.reshape(N_HEADS * nb, tm)
    carry = carry_ref[...]
    cum = (jnp.dot(chosen2.astype(BF16), before, preferred_element_type=F32) + carry).reshape(N_HEADS, nb, tm)
    rank_rows = [jnp.sum(jnp.where(blk_id == s, cum, 0.0), axis=1, keepdims=True).astype(I32)
                 for s in sel_rows]
    carry = carry + jnp.sum(chosen2, axis=1, keepdims=True)
    carry_ref[...] = carry
    pad = jnp.zeros((N_HEADS, SEL_ROWS - 2 * MOBA_TOPK, tm), I32)
    table = jnp.concatenate(sel_rows + rank_rows + [pad], axis=1)
    sel_ref[...] = table.reshape(N_HEADS * SEL_ROWS, tm)
    cnt_ref[0] = jnp.broadcast_to(carry, cnt_ref.shape[1:]).astype(I32)


def _queries(x, seq, w_q, cos_t, sin_t, k_mean):
    n, d = x.shape
    blk = MOBA_BLOCK
    nb = seq // blk
    return pl.pallas_call(
        functools.partial(_query_kernel, blocks_per_seq=nb),
        grid=(n // blk,),
        in_specs=[pl.BlockSpec((blk, d), lambda i: (i, 0)), _full((d, d)),
                  pl.BlockSpec((blk, HEAD_DIM), lambda i: (i % nb, 0)),
                  pl.BlockSpec((blk, HEAD_DIM), lambda i: (i % nb, 0)),
                  pl.BlockSpec((nb, d), lambda i: (i // nb, 0))],
        out_specs=[pl.BlockSpec((1, N_HEADS, blk, HEAD_DIM), lambda i: (i, 0, 0, 0)),
                   pl.BlockSpec((N_HEADS * SEL_ROWS, blk), lambda i: (0, i)),
                   pl.BlockSpec((1, N_HEADS * nb, 128), lambda i: (i // nb, 0, 0))],
        out_shape=[jax.ShapeDtypeStruct((n // blk, N_HEADS, blk, HEAD_DIM), F32),
                   jax.ShapeDtypeStruct((N_HEADS * SEL_ROWS, n), I32),
                   jax.ShapeDtypeStruct((n // seq, N_HEADS * nb, 128), I32)],
        scratch_shapes=[pltpu.VMEM((N_HEADS * nb, 1), F32)],
        compiler_params=_params("arbitrary"),
        name="moba_queries",
    )(x, w_q.astype(BF16), cos_t, sin_t, k_mean)


def _moba_positions_kernel(sel_ref, start_ref, pos_ref, *, dump_row):
    tm = sel_ref.shape[1]
    nb = start_ref.shape[1] // N_HEADS
    blk_id = lax.broadcasted_iota(I32, (nb, tm), 0)
    dump = dump_row + lax.broadcasted_iota(I32, (1, tm), 1) % SC_INDEX_CHUNK
    rows = []
    for hd in range(N_HEADS):
        start = start_ref[0, hd * nb:(hd + 1) * nb, :]
        for s in range(MOBA_TOPK):
            sel = sel_ref[hd * SEL_ROWS + s:hd * SEL_ROWS + s + 1, :]
            rank = sel_ref[hd * SEL_ROWS + MOBA_TOPK + s:hd * SEL_ROWS + MOBA_TOPK + s + 1, :]
            base = jnp.sum(jnp.where(blk_id == sel, start, 0), axis=0, keepdims=True)
            rows.append(jnp.where(sel >= 0, base + rank, dump))
        rows.extend([dump] * (SEL_ROWS - MOBA_TOPK))
    pos_ref[...] = jnp.concatenate(rows, axis=0)


def _moba_positions(sel, starts, seq, dump_row):
    rows, n = sel.shape
    tm = min(4 * TOKEN_ROWS, seq)
    tps = seq // tm
    groups = starts.shape[1]
    return pl.pallas_call(
        functools.partial(_moba_positions_kernel, dump_row=dump_row),
        grid=(n // tm,),
        in_specs=[pl.BlockSpec((rows, tm), lambda i: (0, i)),
                  pl.BlockSpec((1, groups, 1), lambda i: (i // tps, 0, 0))],
        out_specs=pl.BlockSpec((rows, tm), lambda i: (0, i)),
        out_shape=jax.ShapeDtypeStruct((rows, n), I32),
        compiler_params=_params("parallel"),
        name="moba_positions",
    )(sel, starts)


ATTN_SUB_ROWS = 128
ATTN_SUBS_PER_STEP = 32
ATTN_STEP_ROWS = ATTN_SUB_ROWS * ATTN_SUBS_PER_STEP


def _pack_bf16_row_pairs(x):
    h = x.shape[0] // 2
    bits = lax.bitcast_convert_type(x.astype(BF16).astype(F32), U32)
    return (bits[:h] >> 16) | (bits[h:] & U32(HIGH_HALF))


def _transpose_u32(x):
    return lax.bitcast_convert_type(lax.bitcast_convert_type(x, I32).T, U32)


def _scores_t(k, q, keep):
    st = lax.dot_general(k, q, (((1,), (1,)), ((), ())), preferred_element_type=F32)
    if keep is not None:
        st = jnp.where(keep, st, NEG_INF)
    m = jnp.max(st, axis=0, keepdims=True)
    p = jnp.exp(st - m)
    return m, p, jnp.sum(p, axis=0, keepdims=True)


def _group_attn_kernel(step_h_ref, step_b_ref, sub_j_ref, nsteps_ref, qs_ref, k_ref, vt_ref, after_ref,
                       part_ref):
    del step_h_ref, step_b_ref, after_ref
    i = pl.program_id(0)
    half = HEAD_DIM // 2

    @pl.when(i < nsteps_ref[0])
    def _():
        for u in range(ATTN_SUBS_PER_STEP):
            j = sub_j_ref[i * ATTN_SUBS_PER_STEP + u]
            rows = pl.ds(u * ATTN_SUB_ROWS, ATTN_SUB_ROWS)
            q = qs_ref[rows, :].astype(BF16)
            m, p, l = _scores_t(k_ref[0, j], q, None)
            ot = jnp.dot(vt_ref[0, j], p.astype(BF16), preferred_element_type=F32) * (1.0 / l)
            lse = lax.bitcast_convert_type(m + jnp.log(l), U32)
            packed_t = jnp.concatenate(
                [_pack_bf16_row_pairs(ot), jnp.broadcast_to(lse, (half, ATTN_SUB_ROWS))], axis=0)
            part_ref[rows, :] = _transpose_u32(packed_t)


def _group_attention(qs, k_blk, vt_blk, step_h, step_b, sub_j, nsteps, nb, after):
    n_steps = step_h.shape[0]
    row_map = lambda i, sh, sb, sj, ns: (jnp.minimum(i, ns[0] - 1), 0)
    kv_map = lambda i, sh, sb, sj, ns: (sh[i], sb[i], 0, 0)
    return pl.pallas_call(
        _group_attn_kernel,
        grid_spec=pltpu.PrefetchScalarGridSpec(
            num_scalar_prefetch=4,
            grid=(n_steps,),
            in_specs=[pl.BlockSpec((ATTN_STEP_ROWS, HEAD_DIM), row_map),
                      pl.BlockSpec((1, nb, MOBA_BLOCK, HEAD_DIM), kv_map),
                      pl.BlockSpec((1, nb, HEAD_DIM, MOBA_BLOCK), kv_map), _ORDER_ONLY],
            out_specs=pl.BlockSpec((ATTN_STEP_ROWS, HEAD_DIM), row_map)),
        out_shape=jax.ShapeDtypeStruct(((n_steps + 1) * ATTN_STEP_ROWS, HEAD_DIM), U32),
        compiler_params=_params("arbitrary"),
        name="moba_group_attention",
    )(step_h, step_b, sub_j, nsteps, qs, k_blk, vt_blk, after)


def _attn_merge_kernel(x_ref, qp_ref, k_ref, vt_ref, pg_ref, sel_ref, wo_ref, lng_ref, lnb_ref,
                       after_ref, o_ref, op_ref):
    del after_ref
    tm = x_ref.shape[0]
    half = HEAD_DIM // 2
    sel = sel_ref[...]
    key = lax.broadcasted_iota(I32, (tm, tm), 0)
    qry = lax.broadcasted_iota(I32, (tm, tm), 1)
    causal = key <= qry
    heads = []
    for hd in range(N_HEADS):
        q = qp_ref[0, hd].astype(BF16)
        m_own, p, l_own = _scores_t(k_ref[hd, 0], q, causal)
        acc = jnp.dot(vt_ref[hd, 0], p.astype(BF16), preferred_element_type=F32)

        outs, lses = [], []
        m_tot = m_own
        for c in range(MOBA_TOPK):
            part_t = _transpose_u32(pg_ref[c, 0, hd])
            lo, hi = _unpack_bf16_pairs(part_t[:half])
            valid = sel[hd * SEL_ROWS + c:hd * SEL_ROWS + c + 1, :] >= 0
            outs.append(jnp.where(valid, jnp.concatenate([lo, hi], axis=0), 0.0))
            lse = jnp.where(valid, lax.bitcast_convert_type(part_t[half:half + 1], F32), NEG_INF)
            lses.append(lse)
            m_tot = jnp.maximum(m_tot, lse)
        w_own = jnp.exp(m_own - m_tot)
        num = acc * w_own
        den = l_own * w_own
        for c in range(MOBA_TOPK):
            w_c = jnp.exp(lses[c] - m_tot)
            num = num + outs[c] * w_c
            den = den + w_c
        heads.append(num * (1.0 / den))
    att = jnp.concatenate(heads, axis=0).T.astype(BF16)
    mix = jnp.dot(att, wo_ref[...], preferred_element_type=F32)
    x1 = _layer_norm(DN_ALPHA * x_ref[...] + mix, lng_ref[...], lnb_ref[...])
    o_ref[...] = x1
    op_ref[...] = _pack_bf16_pairs(x1)


def _attn_merge(x, qp, k_blk, vt_blk, pg, sel, w_o, ln_g, ln_b, after):
    n, d = x.shape
    blk = MOBA_BLOCK
    row = lambda v: v.reshape(1, -1).astype(F32)
    return pl.pallas_call(
        _attn_merge_kernel,
        grid=(n // blk,),
        in_specs=[pl.BlockSpec((blk, d), lambda i: (i, 0)),
                  pl.BlockSpec((1, N_HEADS, blk, HEAD_DIM), lambda i: (i, 0, 0, 0)),
                  pl.BlockSpec((N_HEADS, 1, blk, HEAD_DIM), lambda i: (0, i, 0, 0)),
                  pl.BlockSpec((N_HEADS, 1, HEAD_DIM, blk), lambda i: (0, i, 0, 0)),
                  pl.BlockSpec((MOBA_TOPK, 1, N_HEADS, blk, HEAD_DIM), lambda i: (0, i, 0, 0, 0)),
                  pl.BlockSpec((N_HEADS * SEL_ROWS, blk), lambda i: (0, i)),
                  _full((d, d)), _full((1, d)), _full((1, d)), _ORDER_ONLY],
        out_specs=[pl.BlockSpec((blk, d), lambda i: (i, 0)), pl.BlockSpec((blk, d // 2), lambda i: (i, 0))],
        out_shape=[jax.ShapeDtypeStruct((n, d), F32), jax.ShapeDtypeStruct((n, d // 2), U32)],
        compiler_params=_params("parallel"),
        name="moba_merge",
    )(x, qp, k_blk, vt_blk, pg, sel, w_o.astype(BF16), row(ln_g), row(ln_b), after)


def _moba_layer(streams, seq, w_q, w_o, ln_g, ln_b, cos_t, sin_t):
    staged = [_moba_regroup(x, seq, w_q, cos_t, sin_t, k_mean) for x, _, _, k_mean in streams]
    gathered, order = [], staged[-1][1]
    for (x, k_blk, vt_blk, _), (qp, sel, qs, tables, gather_idx) in zip(streams, staged):
        nbt = x.shape[0] // MOBA_BLOCK
        part = _group_attention(qs, k_blk, vt_blk, *tables, seq // MOBA_BLOCK, order)
        gathered.append(_sc_gather_rows(part, gather_idx).reshape(MOBA_TOPK, nbt, N_HEADS, MOBA_BLOCK, HEAD_DIM))
        order = part
    outs = []
    for (x, k_blk, vt_blk, _), (qp, sel, _, _, _), pg in zip(streams, staged, gathered):
        outs.append(_attn_merge(x, qp, k_blk, vt_blk, pg, sel, w_o, ln_g, ln_b, order))
        order = outs[-1][0]
    return outs


def _moba_regroup(x, seq, w_q, cos_t, sin_t, k_mean):
    n, d = x.shape
    batch = n // seq
    blk = MOBA_BLOCK
    nb = seq // blk
    nbt = n // blk
    chunk = SC_INDEX_CHUNK
    qp, sel, cnt = _queries(x, seq, w_q, cos_t, sin_t, k_mean)

    counts = cnt[:, :, 0].reshape(batch * N_HEADS, nb)
    gpad = (counts + ATTN_SUB_ROWS - 1) // ATTN_SUB_ROWS * ATTN_SUB_ROWS
    seg = jnp.sum(gpad, axis=1)
    seg_pad = (seg + ATTN_STEP_ROWS - 1) // ATTN_STEP_ROWS * ATTN_STEP_ROWS
    seg_end = jnp.cumsum(seg_pad)
    gend = (seg_end - seg_pad)[:, None] + jnp.cumsum(gpad, axis=1)
    gstart = (gend - gpad).astype(I32)
    steps_per_seg = -(-(MOBA_TOPK * seq + nb * (ATTN_SUB_ROWS - 1)) // ATTN_STEP_ROWS)
    n_steps = batch * N_HEADS * steps_per_seg
    step_first = jnp.arange(n_steps, dtype=I32) * ATTN_STEP_ROWS
    step_seg = jnp.minimum(_count_le(seg_end, step_first), batch * N_HEADS - 1)
    sub_first = jnp.arange(n_steps * ATTN_SUBS_PER_STEP, dtype=I32) * ATTN_SUB_ROWS
    sub_grp = jnp.minimum(_count_le(gend.reshape(-1), sub_first), batch * N_HEADS * nb - 1)
    nsteps = (seg_end[-1] // ATTN_STEP_ROWS).astype(I32).reshape(1)
    dump_row = n_steps * ATTN_STEP_ROWS

    pos = _moba_positions(sel, gstart.reshape(batch, N_HEADS * nb, 1), seq, dump_row)
    pos5 = pos.reshape(N_HEADS, SEL_ROWS, nbt, blk // chunk, chunk)[:, :MOBA_TOPK]
    by_choice = pos5.transpose(1, 2, 0, 3, 4)
    gather_idx = by_choice.reshape(-1)
    scatter_idx = by_choice.reshape(MOBA_TOPK, nbt * N_HEADS * (blk // chunk), chunk)

    qs = _sc_scatter_rows(qp.reshape(n * N_HEADS, HEAD_DIM), scatter_idx, dump_row + chunk)
    tables = ((step_seg % N_HEADS).astype(I32), (step_seg // N_HEADS).astype(I32),
              (sub_grp % nb).astype(I32), nsteps)
    return qp, sel, qs, tables, gather_idx


def _rope_tables(seq):
    half = ROT_DIM // 2
    lane = jnp.arange(HEAD_DIM)
    inv = ROPE_THETA ** (-(2 * (lane % half)).astype(F32) / ROT_DIM)
    ang = jnp.arange(seq, dtype=F32)[:, None] * jnp.where(lane < ROT_DIM, inv, 0.0)[None, :]
    sign = jnp.where(lane < half, -1.0, 1.0).astype(F32)
    return jnp.cos(ang), jnp.sin(ang) * sign[None, :]


def kernel(x, p, ln_g, ln_b, a_w_in, a_b_in, a_conv_w, a_conv_b, a_gate_a_w, a_gate_a_b, a_gate_i_w,
           a_gate_i_b, a_lambda, a_w_out, kv_ln_g, kv_ln_b, w_kv, b_w_q, b_w_o, router_w, router_b,
           exp_w_gate, exp_w_up, exp_w_down, sh_w_gate, sh_w_up, sh_w_down, ple_w, ple_gate_w):
    batch, seq, d = x.shape
    xs = [x.reshape(batch * seq, d)] * batch
    p_all = p.reshape(DEPTH * batch, seq, PLE_DIM)
    cos_t, sin_t = _rope_tables(seq)
    kv = None
    for i in range(DEPTH):
        if i < N_A_LAYERS:
            mixed = [_rglru_layer(xb, seq, b if i == 0 else 0, a_w_in[i], a_b_in[i], a_conv_w[i],
                                  a_conv_b[i], a_gate_a_w[i], a_gate_a_b[i], a_gate_i_w[i], a_gate_i_b[i],
                                  a_lambda[i], a_w_out[i], ln_g[i, 0], ln_b[i, 0])
                     for b, xb in enumerate(xs)]
        else:
            if i == N_A_LAYERS:
                kv = []
                for xb in xs:
                    k_blk, vt_blk, k_mean = _shared_kv(xb, seq, kv_ln_g, kv_ln_b, w_kv, cos_t, sin_t)
                    kv.append((k_blk, vt_blk, k_mean.reshape(seq // MOBA_BLOCK, d)))
            j = i - N_A_LAYERS
            mixed = _moba_layer([(xb,) + kvb for xb, kvb in zip(xs, kv)], seq, b_w_q[j], b_w_o[j],
                                ln_g[i, 0], ln_b[i, 0], cos_t, sin_t)
        xs = _moe_layer(mixed, i, router_w[i], router_b[i], exp_w_gate, exp_w_up, exp_w_down,
                        sh_w_gate[i], sh_w_up[i], sh_w_down[i], ln_g[i, 1], ln_b[i, 1],
                        p_all, i * batch, ple_w[i], ple_gate_w[i], i == DEPTH - 1)
    return xs.reshape(batch, seq, d)
```

```python
import functools

import jax
import jax.numpy as jnp
from jax import lax
from jax.experimental import pallas as pl
from jax.experimental.pallas import tpu as pltpu
from jax.experimental.pallas import tpu_sc as plsc

F32 = jnp.float32
BF16 = jnp.bfloat16
I32 = jnp.int32
U32 = jnp.uint32
HIGH_HALF = 0xFFFF0000

SC_CORES = 2
SC_SUBCORES = 16
SC_WORKERS = SC_CORES * SC_SUBCORES
SC_LANES = 16
SC_INDEX_CHUNK = 128
SC_STAGE_BYTES = 256 * 1024

D_MODEL = 1024
DEPTH = 4
N_A_LAYERS = DEPTH // 2
D_RNN = D_MODEL
LRU_BLOCKS = 4
LRU_BLOCK_W = D_RNN // LRU_BLOCKS
CONV_W = 4
LRU_C = 8.0
N_HEADS = 8
HEAD_DIM = D_MODEL // N_HEADS
ROT_DIM = HEAD_DIM // 4
ROPE_THETA = 500000.0
MOBA_BLOCK = 256
MOBA_TOPK = 3
N_EXPERTS = 64
N_GROUPS = 8
GROUP_SIZE = N_EXPERTS // N_GROUPS
TOPK_GROUPS = 4
TOPK_EXPERTS = 8
D_EXPERT = 256
D_SHARED = 256
ROUTED_SCALE = 2.5
PLE_DIM = 256
DN_ALPHA = (2 * DEPTH) ** 0.25
LN_EPS = 1e-5

V7X_VMEM_LIMIT_BYTES = 56 * 1024 * 1024

MIXER_ROWS = 1024
TOKEN_ROWS = 1024
EXPERT_ROWS = 2304
NEG_INF = float("-inf")


def _params(*sem):
    return pltpu.CompilerParams(dimension_semantics=sem, vmem_limit_bytes=V7X_VMEM_LIMIT_BYTES)


def _layer_norm(z, g, b):
    mu = jnp.mean(z, axis=-1, keepdims=True)
    zc = z - mu
    var = jnp.mean(zc * zc, axis=-1, keepdims=True)
    return zc * lax.rsqrt(var + LN_EPS) * g + b


def _silu(x):
    return x * jax.nn.sigmoid(x)


def _gelu_tanh(x):
    return x * jax.nn.sigmoid(x * (1.5957691216057308 + 0.07135481627159855 * (x * x)))


def _full(shape):
    return pl.BlockSpec(shape, lambda *_: (0,) * len(shape))


def _count_le(ends, values):
    return jnp.sum((ends[None, :] <= values[:, None]).astype(I32), axis=1)


def _pack_bf16_pairs(x):
    w = x.shape[1] // 2
    bits = lax.bitcast_convert_type(x.astype(BF16).astype(F32), U32)
    return (bits[:, :w] >> 16) | (bits[:, w:] & U32(HIGH_HALF))


def _unpack_bf16_pairs(u):
    lo = lax.bitcast_convert_type(u << 16, F32)
    hi = lax.bitcast_convert_type(u & U32(HIGH_HALF), F32)
    return lo, hi


def _sc_mesh():
    return plsc.VectorSubcoreMesh(core_axis_name="c", subcore_axis_name="s")


def _sc_worker_id():
    return lax.axis_index("s") * SC_CORES + lax.axis_index("c")


def _sc_chunks_per_step(chunks_per_worker, row_words):
    g = max(1, SC_STAGE_BYTES // (SC_INDEX_CHUNK * row_words * 4))
    while chunks_per_worker % g:
        g -= 1
    return g


def _sc_gather_rows(table, idx):
    b = idx.shape[0]
    w = table.shape[1]
    chunk = SC_INDEX_CHUNK
    chunks_per_worker = b // (SC_WORKERS * chunk)
    assert chunks_per_worker * SC_WORKERS * chunk == b
    g = _sc_chunks_per_step(chunks_per_worker, w)

    @functools.partial(
        pl.kernel, mesh=_sc_mesh(), out_type=jax.ShapeDtypeStruct((b, w), table.dtype),
        scratch_types=[pltpu.VMEM((g, chunk), I32), pltpu.VMEM((g * chunk, w), table.dtype),
                       pltpu.SemaphoreType.DMA])
    def gather(table_hbm, idx_hbm, out_hbm, idx_v, rows_v, sem):
        first = _sc_worker_id() * chunks_per_worker

        @pl.loop(0, chunks_per_worker // g)
        def _(j):
            c0 = first + j * g
            pltpu.sync_copy(idx_hbm.at[pl.ds(c0, g)], idx_v)
            copies = [pltpu.async_copy(table_hbm.at[idx_v.at[q]], rows_v.at[pl.ds(q * chunk, chunk)], sem)
                      for q in range(g)]
            for cp in copies:
                cp.wait()
            pltpu.sync_copy(rows_v, out_hbm.at[pl.ds(c0 * chunk, g * chunk)])

    return gather(table, idx.reshape(b // chunk, chunk))


def _sc_scatter_rows(src, idx, out_rows):
    n, w = src.shape
    chunk = SC_INDEX_CHUNK
    fan = idx.shape[0]
    chunks_per_worker = n // (SC_WORKERS * chunk)
    assert idx.shape == (fan, n // chunk, chunk) and chunks_per_worker * SC_WORKERS * chunk == n
    g = _sc_chunks_per_step(chunks_per_worker, w)

    @functools.partial(
        pl.kernel, mesh=_sc_mesh(), out_type=jax.ShapeDtypeStruct((out_rows, w), src.dtype),
        scratch_types=[pltpu.VMEM((fan, g, chunk), I32), pltpu.VMEM((g * chunk, w), src.dtype),
                       pltpu.SemaphoreType.DMA])
    def scatter(src_hbm, idx_hbm, out_hbm, idx_v, rows_v, sem):
        first = _sc_worker_id() * chunks_per_worker

        @pl.loop(0, chunks_per_worker // g)
        def _(j):
            c0 = first + j * g
            pltpu.sync_copy(src_hbm.at[pl.ds(c0 * chunk, g * chunk)], rows_v)
            for f in range(fan):
                pltpu.sync_copy(idx_hbm.at[f, pl.ds(c0, g)], idx_v.at[f])
            copies = [pltpu.async_copy(rows_v.at[pl.ds(q * chunk, chunk)], out_hbm.at[idx_v.at[f, q]], sem)
                      for q in range(g) for f in range(fan)]
            for cp in copies:
                cp.wait()

    return scatter(src, idx)


def _sc_weighted_gather(table, idx, wrep, fan):
    nf = idx.shape[0]
    n = nf // fan
    w = table.shape[1]
    chunk = SC_INDEX_CHUNK
    lanes = SC_LANES
    half = chunk // 2
    tokens = half // fan
    chunks_per_worker = nf // (SC_WORKERS * chunk)
    assert chunks_per_worker * SC_WORKERS * chunk == nf and w % lanes == 0 and wrep.shape == (n, fan * lanes)

    @functools.partial(
        pl.kernel, mesh=_sc_mesh(), out_type=jax.ShapeDtypeStruct((n, 2 * w), F32),
        scratch_types=[pltpu.VMEM((chunks_per_worker, chunk), I32), pltpu.VMEM((2, half, w), U32),
                       pltpu.VMEM((2, tokens, fan * lanes), F32), pltpu.VMEM((2, tokens, 2 * w), F32),
                       pltpu.SemaphoreType.DMA((2,)), pltpu.SemaphoreType.DMA((2,)),
                       pltpu.SemaphoreType.DMA((2,))],
        compiler_params=pltpu.CompilerParams(needs_layout_passes=False))
    def weighted_gather(table_hbm, idx_hbm, w_hbm, out_hbm, idx_v, rows_v, w_v, out_v,
                        row_sem, weight_sem, out_sem):
        worker = _sc_worker_id()
        first_token = worker * (2 * chunks_per_worker * tokens)
        pltpu.sync_copy(idx_hbm.at[pl.ds(worker * chunks_per_worker, chunks_per_worker)], idx_v)

        def loads(pair, slot):
            step = 2 * pair + slot
            return (pltpu.make_async_copy(table_hbm.at[idx_v.at[pair, pl.ds(slot * half, half)]],
                                          rows_v.at[slot], row_sem.at[slot]),
                    pltpu.make_async_copy(w_hbm.at[pl.ds(first_token + step * tokens, tokens)],
                                          w_v.at[slot], weight_sem.at[slot]))

        def store(pair, slot):
            step = 2 * pair + slot
            return pltpu.make_async_copy(out_v.at[slot], out_hbm.at[pl.ds(first_token + step * tokens, tokens)],
                                         out_sem.at[slot])

        def reduce_rows(slot):
            @pl.loop(0, tokens)
            def _(t):
                weights = [w_v[slot, t, pl.ds(q * lanes, lanes)] for q in range(fan)]

                @pl.loop(0, w, step=lanes)
                def _(c):
                    acc_lo = jnp.zeros((lanes,), F32)
                    acc_hi = jnp.zeros((lanes,), F32)
                    for q in range(fan):
                        u = rows_v[slot, t * fan + q, pl.ds(c, lanes)]
                        acc_lo = acc_lo + weights[q] * lax.bitcast_convert_type(u << 16, F32)
                        acc_hi = acc_hi + weights[q] * lax.bitcast_convert_type(u & U32(HIGH_HALF), F32)
                    out_v[slot, t, pl.ds(c, lanes)] = acc_lo
                    out_v[slot, t, pl.ds(w + c, lanes)] = acc_hi

        for copy in loads(0, 0):
            copy.start()

        @pl.loop(0, chunks_per_worker)
        def _(pair):
            for slot in range(2):
                for copy in loads(pair, slot):
                    copy.wait()
                if slot == 0:
                    for copy in loads(pair, 1):
                        copy.start()
                else:
                    @pl.when(pair + 1 < chunks_per_worker)
                    def _():
                        for copy in loads(pair + 1, 0):
                            copy.start()

                @pl.when(pair > 0)
                def _():
                    store(pair - 1, slot).wait()
                reduce_rows(slot)
                store(pair, slot).start()

        for slot in range(2):
            store(chunks_per_worker - 1, slot).wait()

    return weighted_gather(table, idx.reshape(nf // chunk, chunk), wrep)


def _rglru_kernel(x_ref, win_ref, bin_ref, cw_ref, cb_ref, gaw_ref, gab_ref, giw_ref, gib_ref,
                  lam_ref, wout_ref, lng_ref, lnb_ref, o_ref, op_ref, tail_ref, h_ref, *, tiles_per_seq):
    i = pl.program_id(0)
    tm = x_ref.shape[0]

    @pl.when(i % tiles_per_seq == 0)
    def _():
        tail_ref[...] = jnp.zeros_like(tail_ref)
        h_ref[...] = jnp.zeros_like(h_ref)

    x = x_ref[...]
    xy = jnp.dot(x.astype(BF16), win_ref[...], preferred_element_type=F32) + bin_ref[...]
    xb = xy[:, :D_RNN]
    y = _gelu_tanh(xy[:, D_RNN:])

    tail = tail_ref[...]
    row8 = lax.broadcasted_iota(I32, (8, 1), 0)
    xc = cb_ref[...] + xb * cw_ref[CONV_W - 1:CONV_W, :]
    for d in range(1, CONV_W):
        rolled = pltpu.roll(xb, d, 0)
        head = jnp.where(row8 < d, pltpu.roll(tail, d, 0), rolled[:8])
        shifted = jnp.concatenate([head, rolled[8:]], axis=0)
        xc = xc + shifted * cw_ref[CONV_W - 1 - d:CONV_W - d, :]
    tail_ref[...] = xb[tm - 8:, :]

    r_parts, i_parts = [], []
    for n in range(LRU_BLOCKS):
        xg = xc[:, n * LRU_BLOCK_W:(n + 1) * LRU_BLOCK_W].astype(BF16)
        r_parts.append(jnp.dot(xg, gaw_ref[n], preferred_element_type=F32))
        i_parts.append(jnp.dot(xg, giw_ref[n], preferred_element_type=F32))
    r = jax.nn.sigmoid(jnp.concatenate(r_parts, axis=1) + gab_ref[...])
    ig = jax.nn.sigmoid(jnp.concatenate(i_parts, axis=1) + gib_ref[...])

    lam = lam_ref[...]
    softplus_neg_lam = jnp.maximum(-lam, 0.0) + jnp.log1p(jnp.exp(-jnp.abs(lam)))
    log_a = (-LRU_C * r) * softplus_neg_lam
    a = jnp.exp(log_a)
    u = jnp.sqrt(1.0 - a * a) * (ig * xc)

    n_grp = tm // 8
    sub = lax.broadcasted_iota(I32, (1, 8, 1), 1)
    acc_a = a.reshape(n_grp, 8, D_RNN)
    acc_h = u.reshape(n_grp, 8, D_RNN)
    for d in (1, 2, 4):
        keep = sub >= d
        sh_a = pltpu.roll(acc_a, d, 1)
        sh_h = pltpu.roll(acc_h, d, 1)
        acc_h = jnp.where(keep, acc_a * sh_h + acc_h, acc_h)
        acc_a = jnp.where(keep, acc_a * sh_a, acc_a)
    state = h_ref[...]
    groups = []
    for g in range(n_grp):
        hg = acc_h[g] + acc_a[g] * state
        state = hg[7:8, :]
        groups.append(hg)
    h = jnp.concatenate(groups, axis=0)
    h_ref[...] = state

    mix = jnp.dot((h * y).astype(BF16), wout_ref[...], preferred_element_type=F32)
    x1 = _layer_norm(DN_ALPHA * x + mix, lng_ref[...], lnb_ref[...])
    o_ref[...] = x1
    op_ref[...] = _pack_bf16_pairs(x1)


def _rglru_layer(x, seq, stream, w_in, b_in, conv_w, conv_b, ga_w, ga_b, gi_w, gi_b, lam, w_out,
                 ln_g, ln_b):
    n, d = seq, x.shape[1]
    tm = min(MIXER_ROWS, seq)
    first_tile = stream * (seq // tm)
    row = lambda v: v.reshape(1, -1).astype(F32)
    return pl.pallas_call(
        functools.partial(_rglru_kernel, tiles_per_seq=seq // tm),
        grid=(n // tm,),
        in_specs=[pl.BlockSpec((tm, d), lambda i: (i + first_tile, 0)),
                  _full((d, 2 * D_RNN)), _full((1, 2 * D_RNN)),
                  _full((CONV_W, D_RNN)), _full((1, D_RNN)),
                  _full((LRU_BLOCKS, LRU_BLOCK_W, LRU_BLOCK_W)), _full((1, D_RNN)),
                  _full((LRU_BLOCKS, LRU_BLOCK_W, LRU_BLOCK_W)), _full((1, D_RNN)),
                  _full((1, D_RNN)), _full((D_RNN, d)), _full((1, d)), _full((1, d))],
        out_specs=[pl.BlockSpec((tm, d), lambda i: (i, 0)), pl.BlockSpec((tm, d // 2), lambda i: (i, 0))],
        out_shape=[jax.ShapeDtypeStruct((n, d), F32), jax.ShapeDtypeStruct((n, d // 2), U32)],
        scratch_shapes=[pltpu.VMEM((8, D_RNN), F32), pltpu.VMEM((1, D_RNN), F32)],
        compiler_params=_params("arbitrary"),
        name="rglru_mixer",
    )(x, w_in.astype(BF16), row(b_in), conv_w, row(conv_b), ga_w.astype(BF16), row(ga_b),
      gi_w.astype(BF16), row(gi_b), row(lam), w_out.astype(BF16), row(ln_g), row(ln_b))


def _first_argmax(cur, idx, size, axis):
    m = jnp.max(cur, axis=axis, keepdims=True)
    first = jnp.min(jnp.where(cur == m, idx, size), axis=axis, keepdims=True)
    return m, first


def _router_kernel(x_ref, rw_ref, rb_ref, e_ref, w_ref, r_ref, cnt_ref, carry_ref):
    i = pl.program_id(0)
    tm = x_ref.shape[0]

    @pl.when(i == 0)
    def _():
        carry_ref[...] = jnp.zeros_like(carry_ref)

    x = x_ref[...]
    x_hi = x.astype(BF16)
    x_lo = (x - x_hi.astype(F32)).astype(BF16)
    rw = rw_ref[...]
    hi_both = jnp.dot(x_hi, rw, preferred_element_type=F32)
    logits = (hi_both[:, :128] + hi_both[:, 128:]
              + jnp.dot(x_lo, rw[:, :128], preferred_element_type=F32))
    scores = jax.nn.sigmoid(logits.T[:N_EXPERTS, :])
    choice = scores + rb_ref[...]

    c3 = choice.reshape(N_GROUPS, GROUP_SIZE, tm)
    in_grp = lax.broadcasted_iota(I32, c3.shape, 1)
    m1, i1 = _first_argmax(c3, in_grp, GROUP_SIZE, 1)
    m2 = jnp.max(jnp.where(in_grp == i1, NEG_INF, c3), axis=1, keepdims=True)
    grp_score = (m1 + m2)[:, 0, :]

    grp_id = lax.broadcasted_iota(I32, grp_score.shape, 0)
    grp_sel = jnp.zeros(grp_score.shape, jnp.bool_)
    cur = grp_score
    for _ in range(TOPK_GROUPS):
        _, gi = _first_argmax(cur, grp_id, N_GROUPS, 0)
        hit = grp_id == gi
        grp_sel = jnp.logical_or(grp_sel, hit)
        cur = jnp.where(hit, NEG_INF, cur)

    cur = jnp.where(grp_sel[:, None, :], c3, NEG_INF).reshape(N_EXPERTS, tm)
    exp_id = lax.broadcasted_iota(I32, cur.shape, 0)
    sel = jnp.zeros(cur.shape, F32)
    e_rows, s_rows = [], []
    for _ in range(TOPK_EXPERTS):
        _, ei = _first_argmax(cur, exp_id, N_EXPERTS, 0)
        hit = exp_id == ei
        e_rows.append(ei)
        s_rows.append(jnp.sum(jnp.where(hit, scores, 0.0), axis=0, keepdims=True))
        sel = jnp.where(hit, 1.0, sel)
        cur = jnp.where(hit, NEG_INF, cur)
    e_top = jnp.concatenate(e_rows, axis=0)
    s_top = jnp.concatenate(s_rows, axis=0)
    w_top = s_top / jnp.sum(s_top, axis=0, keepdims=True) * ROUTED_SCALE
    w_ref[...] = jnp.broadcast_to(w_top[:, None, :], (TOPK_EXPERTS, SC_LANES, tm)).reshape(
        TOPK_EXPERTS * SC_LANES, tm).T
    e_ref[...] = e_top

    t_row = lax.broadcasted_iota(I32, (tm, tm), 0)
    t_col = lax.broadcasted_iota(I32, (tm, tm), 1)
    before = (t_row < t_col).astype(BF16)
    cum = jnp.dot(sel.astype(BF16), before, preferred_element_type=F32) + carry_ref[...]
    r_rows = [jnp.sum(jnp.where(exp_id == e_rows[k], cum, 0.0), axis=0, keepdims=True)
              for k in range(TOPK_EXPERTS)]
    r_ref[...] = jnp.concatenate(r_rows, axis=0).astype(I32)
    carry_ref[...] = carry_ref[...] + jnp.sum(sel, axis=1, keepdims=True)
    cnt_ref[...] = jnp.broadcast_to(carry_ref[...], cnt_ref.shape).astype(I32)


def _router(x, router_w, router_b):
    n, d = x.shape
    tm = min(TOKEN_ROWS, n)
    rw = jnp.pad(router_w, ((0, 0), (0, 128 - N_EXPERTS)))
    rw_hi = rw.astype(BF16)
    rw = jnp.concatenate([rw_hi, (rw - rw_hi.astype(F32)).astype(BF16)], axis=1)
    k = TOPK_EXPERTS
    tok_spec = pl.BlockSpec((k, tm), lambda i: (0, i))
    return pl.pallas_call(
        _router_kernel,
        grid=(n // tm,),
        in_specs=[pl.BlockSpec((tm, d), lambda i: (i, 0)), _full((d, 256)), _full((N_EXPERTS, 1))],
        out_specs=[tok_spec, pl.BlockSpec((tm, k * SC_LANES), lambda i: (i, 0)), tok_spec,
                   _full((N_EXPERTS, 128))],
        out_shape=[jax.ShapeDtypeStruct((k, n), I32), jax.ShapeDtypeStruct((n, k * SC_LANES), F32),
                   jax.ShapeDtypeStruct((k, n), I32), jax.ShapeDtypeStruct((N_EXPERTS, 128), I32)],
        scratch_shapes=[pltpu.VMEM((N_EXPERTS, 1), F32)],
        compiler_params=_params("arbitrary"),
        name="moe_router",
    )(x, rw, router_b.reshape(N_EXPERTS, 1).astype(F32))


def _expert_kernel(blk_e_ref, nblk_ref, first_ref, slot_ref, next_ref, xs_ref, wg_hbm, wu_hbm, wd_hbm,
                   after_ref, ys_ref, wg_f, wu_f, wd_f, wg_s, wu_s, wd_s, sem, *, layer):
    del after_ref
    i = pl.program_id(0)

    def fetch(expert, slot):
        return (pltpu.make_async_copy(wg_hbm.at[layer, expert], wg_f.at[slot], sem.at[slot, 0]),
                pltpu.make_async_copy(wu_hbm.at[layer, expert], wu_f.at[slot], sem.at[slot, 1]),
                pltpu.make_async_copy(wd_hbm.at[layer, expert], wd_f.at[slot], sem.at[slot, 2]))

    @pl.when(i < nblk_ref[0])
    def _():
        slot = slot_ref[i]

        @pl.when(i == 0)
        def _():
            for copy in fetch(blk_e_ref[0], 0):
                copy.start()

        @pl.when(first_ref[i] == 1)
        def _():
            for copy in fetch(blk_e_ref[i], slot):
                copy.wait()
            wg_s[...] = wg_f[slot].astype(BF16)
            wu_s[...] = wu_f[slot].astype(BF16)
            wd_s[...] = wd_f[slot].astype(BF16)

            @pl.when(next_ref[i] >= 0)
            def _():
                for copy in fetch(next_ref[i], 1 - slot):
                    copy.start()

        lo, hi = _unpack_bf16_pairs(xs_ref[...])
        xs = jnp.concatenate([lo.astype(BF16), hi.astype(BF16)], axis=1)
        g = jnp.dot(xs, wg_s[...], preferred_element_type=F32)
        u = jnp.dot(xs, wu_s[...], preferred_element_type=F32)
        hdn = (_silu(g) * u).astype(BF16)
        ys_ref[...] = _pack_bf16_pairs(jnp.dot(hdn, wd_s[...], preferred_element_type=F32))


_ORDER_ONLY = pl.BlockSpec(memory_space=pl.ANY)


def _experts(xs, blk_e, nblk, layer, w_gate, w_up, w_down, after):
    rows, half = xs.shape
    d = 2 * half
    m = EXPERT_ROWS
    n_blocks = rows // m
    idx = jnp.arange(n_blocks, dtype=I32)
    used = idx < nblk[0]
    first = jnp.logical_and(used, jnp.logical_or(idx == 0, blk_e != jnp.roll(blk_e, 1)))
    slot = (jnp.cumsum(first.astype(I32)) - 1) % 2
    larger = jnp.logical_and(blk_e[None, :] > blk_e[:, None], used[None, :])
    next_e = jnp.min(jnp.where(larger, blk_e[None, :], N_EXPERTS), axis=1)
    next_e = jnp.where(next_e < N_EXPERTS, next_e, -1)
    row_map = lambda i, be, nb, fi, sl, nx: (jnp.minimum(i, nb[0] - 1), 0)
    hbm = pl.BlockSpec(memory_space=pl.ANY)
    return pl.pallas_call(
        functools.partial(_expert_kernel, layer=layer),
        grid_spec=pltpu.PrefetchScalarGridSpec(
            num_scalar_prefetch=5,
            grid=(n_blocks,),
            in_specs=[pl.BlockSpec((m, half), row_map), hbm, hbm, hbm, _ORDER_ONLY],
            out_specs=pl.BlockSpec((m, half), row_map),
            scratch_shapes=[pltpu.VMEM((2, d, D_EXPERT), F32), pltpu.VMEM((2, d, D_EXPERT), F32),
                            pltpu.VMEM((2, D_EXPERT, d), F32),
                            pltpu.VMEM((d, D_EXPERT), BF16), pltpu.VMEM((d, D_EXPERT), BF16),
                            pltpu.VMEM((D_EXPERT, d), BF16), pltpu.SemaphoreType.DMA((2, 3))]),
        out_shape=jax.ShapeDtypeStruct((rows, half), U32),
        compiler_params=_params("arbitrary"),
        name="moe_experts",
    )(blk_e, nblk, first.astype(I32), slot.astype(I32), next_e.astype(I32), xs, w_gate, w_up, w_down, after)


def _combine_kernel(x_ref, moe_ref, sg_ref, su_ref, sd_ref, lng_ref, lnb_ref,
                    p_ref, pw_ref, pg_ref, after_ref, o_ref):
    del after_ref
    x = x_ref[...]
    moe = moe_ref[...]
    xb = x.astype(BF16)
    hdn = _silu(jnp.dot(xb, sg_ref[...], preferred_element_type=F32)) * \
        jnp.dot(xb, su_ref[...], preferred_element_type=F32)
    shared = jnp.dot(hdn.astype(BF16), sd_ref[...], preferred_element_type=F32)
    x2 = _layer_norm(DN_ALPHA * x + (moe + shared), lng_ref[...], lnb_ref[...])
    emb = jnp.dot(p_ref[0].astype(BF16), pw_ref[...], preferred_element_type=F32)
    gate = jax.nn.sigmoid(jnp.dot(x2.astype(BF16), pg_ref[...], preferred_element_type=F32))
    o_ref[...] = x2 + emb * gate


def _combine(x, moe, s_gate, s_up, s_down, ln_g, ln_b, p_all, p_index, ple_w, ple_gate_w, after,
             out_streams=1, out_stream=0):
    n, d = x.shape
    tm = min(TOKEN_ROWS, n)
    first_tile = out_stream * (n // tm)
    row = lambda v: v.reshape(1, -1).astype(F32)
    return pl.pallas_call(
        _combine_kernel,
        grid=(n // tm,),
        input_output_aliases={10: 0} if out_stream > 0 else {},
        in_specs=[pl.BlockSpec((tm, d), lambda i: (i, 0)),
                  pl.BlockSpec((tm, d), lambda i: (i, 0)),
                  _full((d, D_SHARED)), _full((d, D_SHARED)), _full((D_SHARED, d)),
                  _full((1, d)), _full((1, d)),
                  pl.BlockSpec((1, tm, PLE_DIM), lambda i: (p_index, i, 0)),
                  _full((PLE_DIM, d)), _full((d, d)), _ORDER_ONLY],
        out_specs=pl.BlockSpec((tm, d), lambda i: (i + first_tile, 0)),
        out_shape=jax.ShapeDtypeStruct((out_streams * n, d), F32),
        compiler_params=_params("parallel"),
        name="moe_combine",
    )(x, moe, s_gate.astype(BF16), s_up.astype(BF16), s_down.astype(BF16), row(ln_g), row(ln_b),
      p_all, ple_w.astype(BF16), ple_gate_w.astype(BF16), after)


def _positions_kernel(e_ref, r_ref, start_ref, pos_ref):
    e = e_ref[...]
    start = start_ref[...]
    grp_id = lax.broadcasted_iota(I32, (start.shape[0], e.shape[1]), 0)
    rows = [jnp.sum(jnp.where(grp_id == e[k:k + 1], start, 0), axis=0, keepdims=True)
            for k in range(e.shape[0])]
    pos_ref[...] = jnp.concatenate(rows, axis=0) + r_ref[...]


def _positions(e_idx, rank, starts):
    k, n = e_idx.shape
    tm = min(4 * TOKEN_ROWS, n)
    groups = starts.shape[0]
    tok_spec = pl.BlockSpec((k, tm), lambda i: (0, i))
    return pl.pallas_call(
        _positions_kernel,
        grid=(n // tm,),
        in_specs=[tok_spec, tok_spec, _full((groups, 1))],
        out_specs=tok_spec,
        out_shape=jax.ShapeDtypeStruct((k, n), I32),
        compiler_params=_params("parallel"),
        name="group_positions",
    )(e_idx, rank, starts.reshape(groups, 1))


def _moe_layer(streams, layer, router_w, router_b, w_gate, w_up, w_down, s_gate, s_up, s_down,
               ln_g, ln_b, p_all, p_index, ple_w, ple_gate_w, merge_output):
    m = EXPERT_ROWS
    k = TOPK_EXPERTS
    chunk = SC_INDEX_CHUNK
    routed = []
    for x, xp in streams:
        n, d = x.shape
        e_idx, wgt, rank, cnt = _router(x, router_w, router_b)
        counts = cnt[:, 0]
        padded = (counts + m - 1) // m * m
        pends = jnp.cumsum(padded)
        n_blocks = (n * k) // m + N_EXPERTS
        blk_first = jnp.arange(n_blocks, dtype=I32) * m
        blk_e = jnp.minimum(_count_le(pends, blk_first), N_EXPERTS - 1)
        nblk = (pends[-1] // m).astype(I32).reshape(1)
        pos = _positions(e_idx, rank, (pends - padded).astype(I32))
        xs = _sc_scatter_rows(xp, pos.reshape(k, n // chunk, chunk), n_blocks * m)
        routed.append((xs, blk_e, nblk, pos, wgt))
    gathered, order = [], routed[-1][3]
    for (x, _), (xs, blk_e, nblk, pos, wgt) in zip(streams, routed):
        n, d = x.shape
        ys = _experts(xs, blk_e, nblk, layer, w_gate, w_up, w_down, order)
        gathered.append(_sc_weighted_gather(ys, pos.T.reshape(-1), wgt, k))
        order = ys
    outs = []
    for s, ((x, _), moe) in enumerate(zip(streams, gathered)):
        outs.append(_combine(x, moe, s_gate, s_up, s_down, ln_g, ln_b, p_all, p_index + s,
                             ple_w, ple_gate_w, order, *((len(streams), s) if merge_output else ())))
        order = outs[-1]
    return outs[-1] if merge_output else outs


def _rope(t, cos_t, sin_t):
    half = ROT_DIM // 2
    width = t.shape[1]
    lane = lax.broadcasted_iota(I32, (1, width), 1) % HEAD_DIM
    partner = jnp.where(lane < half, pltpu.roll(t, width - half, 1), pltpu.roll(t, half, 1))
    cos_f = jnp.concatenate([cos_t] * N_HEADS, axis=1)
    sin_f = jnp.concatenate([sin_t] * N_HEADS, axis=1)
    return t * cos_f + partner * sin_f


def _kv_kernel(x_ref, g_ref, b_ref, wkv_ref, cos_ref, sin_ref, k_ref, vt_ref, km_ref):
    h = _layer_norm(x_ref[...], g_ref[...], b_ref[...])
    kv = jnp.dot(h.astype(BF16), wkv_ref[...], preferred_element_type=F32)
    k = _rope(kv[:, :D_MODEL], cos_ref[...], sin_ref[...])
    vt = kv[:, D_MODEL:].T
    km_ref[0] = jnp.mean(k, axis=0, keepdims=True)
    for hd in range(N_HEADS):
        k_ref[hd, 0] = k[:, hd * HEAD_DIM:(hd + 1) * HEAD_DIM].astype(BF16)
        vt_ref[hd, 0] = vt[hd * HEAD_DIM:(hd + 1) * HEAD_DIM, :].astype(BF16)


def _shared_kv(x, seq, ln_g, ln_b, w_kv, cos_t, sin_t):
    n, d = x.shape
    blk = MOBA_BLOCK
    nbt = n // blk
    spb = seq // blk
    row = lambda v: v.reshape(1, -1).astype(F32)
    return pl.pallas_call(
        _kv_kernel,
        grid=(nbt,),
        in_specs=[pl.BlockSpec((blk, d), lambda i: (i, 0)), _full((1, d)), _full((1, d)),
                  _full((d, 2 * d)),
                  pl.BlockSpec((blk, HEAD_DIM), lambda i: (i % spb, 0)),
                  pl.BlockSpec((blk, HEAD_DIM), lambda i: (i % spb, 0))],
        out_specs=[pl.BlockSpec((N_HEADS, 1, blk, HEAD_DIM), lambda i: (0, i, 0, 0)),
                   pl.BlockSpec((N_HEADS, 1, HEAD_DIM, blk), lambda i: (0, i, 0, 0)),
                   pl.BlockSpec((1, 1, d), lambda i: (i, 0, 0))],
        out_shape=[jax.ShapeDtypeStruct((N_HEADS, nbt, blk, HEAD_DIM), BF16),
                   jax.ShapeDtypeStruct((N_HEADS, nbt, HEAD_DIM, blk), BF16),
                   jax.ShapeDtypeStruct((nbt, 1, d), F32)],
        compiler_params=_params("parallel"),
        name="shared_kv",
    )(x, row(ln_g), row(ln_b), w_kv.astype(BF16), cos_t, sin_t)


SEL_ROWS = 8


def _query_kernel(x_ref, wq_ref, cos_ref, sin_ref, km_ref, qp_ref, sel_ref, cnt_ref, carry_ref,
                  *, blocks_per_seq):
    i = pl.program_id(0)
    tm = x_ref.shape[0]
    nb = km_ref.shape[0]
    own = i % blocks_per_seq

    @pl.when(own == 0)
    def _():
        carry_ref[...] = jnp.zeros_like(carry_ref)

    q = jnp.dot(x_ref[...].astype(BF16), wq_ref[...], preferred_element_type=F32)
    q = _rope(q, cos_ref[...], sin_ref[...]) * (HEAD_DIM ** -0.5)

    t_row = lax.broadcasted_iota(I32, (tm, tm), 0)
    t_col = lax.broadcasted_iota(I32, (tm, tm), 1)
    before = (t_row < t_col).astype(BF16)
    km = km_ref[...]
    km_hi = km.astype(BF16)
    km_both = jnp.concatenate([km_hi, (km - km_hi.astype(F32)).astype(BF16)], axis=0)
    q_hi = q.astype(BF16)
    q_lo = (q - q_hi.astype(F32)).astype(BF16)
    nt = (((1,), (1,)), ((), ()))
    gates = []
    for hd in range(N_HEADS):
        lo, hi = hd * HEAD_DIM, (hd + 1) * HEAD_DIM
        qp_ref[0, hd] = q[:, lo:hi]
        both = lax.dot_general(km_both[:, lo:hi], q_hi[:, lo:hi], nt, preferred_element_type=F32)
        gates.append(both[:nb] + both[nb:]
                     + lax.dot_general(km_hi[:, lo:hi], q_lo[:, lo:hi], nt, preferred_element_type=F32))

    gate = jnp.concatenate(gates, axis=0).reshape(N_HEADS, nb, tm)
    blk_id = lax.broadcasted_iota(I32, (N_HEADS, nb, tm), 1)
    cur = jnp.where(blk_id < own, gate, NEG_INF)
    sel_rows = []
    chosen = jnp.zeros((N_HEADS, nb, tm), F32)
    for _ in range(MOBA_TOPK):
        m, bi = _first_argmax(cur, blk_id, nb, 1)
        valid = m > NEG_INF
        hit = jnp.logical_and(blk_id == bi, valid)
        sel_rows.append(jnp.where(valid, bi, -1))
        chosen = jnp.where(hit, 1.0, chosen)
        cur = jnp.where(blk_id == bi, NEG_INF, cur)
    chosen2 = chosen.reshape(N_HEADS * nb, tm)
    carry = carry_ref[...]
    cum = (jnp.dot(chosen2.astype(BF16), before, preferred_element_type=F32) + carry).reshape(N_HEADS, nb, tm)
    rank_rows = [jnp.sum(jnp.where(blk_id == s, cum, 0.0), axis=1, keepdims=True).astype(I32)
                 for s in sel_rows]
    carry = carry + jnp.sum(chosen2, axis=1, keepdims=True)
    carry_ref[...] = carry
    pad = jnp.zeros((N_HEADS, SEL_ROWS - 2 * MOBA_TOPK, tm), I32)
    table = jnp.concatenate(sel_rows + rank_rows + [pad], axis=1)
    sel_ref[...] = table.reshape(N_HEADS * SEL_ROWS, tm)
    cnt_ref[0] = jnp.broadcast_to(carry, cnt_ref.shape[1:]).astype(I32)


def _queries(x, seq, w_q, cos_t, sin_t, k_mean):
    n, d = x.shape
    blk = MOBA_BLOCK
    nb = seq // blk
    return pl.pallas_call(
        functools.partial(_query_kernel, blocks_per_seq=nb),
        grid=(n // blk,),
        in_specs=[pl.BlockSpec((blk, d), lambda i: (i, 0)), _full((d, d)),
                  pl.BlockSpec((blk, HEAD_DIM), lambda i: (i % nb, 0)),
                  pl.BlockSpec((blk, HEAD_DIM), lambda i: (i % nb, 0)),
                  pl.BlockSpec((nb, d), lambda i: (i // nb, 0))],
        out_specs=[pl.BlockSpec((1, N_HEADS, blk, HEAD_DIM), lambda i: (i, 0, 0, 0)),
                   pl.BlockSpec((N_HEADS * SEL_ROWS, blk), lambda i: (0, i)),
                   pl.BlockSpec((1, N_HEADS * nb, 128), lambda i: (i // nb, 0, 0))],
        out_shape=[jax.ShapeDtypeStruct((n // blk, N_HEADS, blk, HEAD_DIM), F32),
                   jax.ShapeDtypeStruct((N_HEADS * SEL_ROWS, n), I32),
                   jax.ShapeDtypeStruct((n // seq, N_HEADS * nb, 128), I32)],
        scratch_shapes=[pltpu.VMEM((N_HEADS * nb, 1), F32)],
        compiler_params=_params("arbitrary"),
        name="moba_queries",
    )(x, w_q.astype(BF16), cos_t, sin_t, k_mean)


def _moba_positions_kernel(sel_ref, start_ref, pos_ref, *, dump_row):
    tm = sel_ref.shape[1]
    nb = start_ref.shape[1] // N_HEADS
    blk_id = lax.broadcasted_iota(I32, (nb, tm), 0)
    dump = dump_row + lax.broadcasted_iota(I32, (1, tm), 1) % SC_INDEX_CHUNK
    rows = []
    for hd in range(N_HEADS):
        start = start_ref[0, hd * nb:(hd + 1) * nb, :]
        for s in range(MOBA_TOPK):
            sel = sel_ref[hd * SEL_ROWS + s:hd * SEL_ROWS + s + 1, :]
            rank = sel_ref[hd * SEL_ROWS + MOBA_TOPK + s:hd * SEL_ROWS + MOBA_TOPK + s + 1, :]
            base = jnp.sum(jnp.where(blk_id == sel, start, 0), axis=0, keepdims=True)
            rows.append(jnp.where(sel >= 0, base + rank, dump))
        rows.extend([dump] * (SEL_ROWS - MOBA_TOPK))
    pos_ref[...] = jnp.concatenate(rows, axis=0)


def _moba_positions(sel, starts, seq, dump_row):
    rows, n = sel.shape
    tm = min(4 * TOKEN_ROWS, seq)
    tps = seq // tm
    groups = starts.shape[1]
    return pl.pallas_call(
        functools.partial(_moba_positions_kernel, dump_row=dump_row),
        grid=(n // tm,),
        in_specs=[pl.BlockSpec((rows, tm), lambda i: (0, i)),
                  pl.BlockSpec((1, groups, 1), lambda i: (i // tps, 0, 0))],
        out_specs=pl.BlockSpec((rows, tm), lambda i: (0, i)),
        out_shape=jax.ShapeDtypeStruct((rows, n), I32),
        compiler_params=_params("parallel"),
        name="moba_positions",
    )(sel, starts)


ATTN_SUB_ROWS = 128
ATTN_SUBS_PER_STEP = 64
ATTN_STEP_ROWS = ATTN_SUB_ROWS * ATTN_SUBS_PER_STEP


def _pack_bf16_row_pairs(x):
    h = x.shape[0] // 2
    bits = lax.bitcast_convert_type(x.astype(BF16).astype(F32), U32)
    return (bits[:h] >> 16) | (bits[h:] & U32(HIGH_HALF))


def _transpose_u32(x):
    return lax.bitcast_convert_type(lax.bitcast_convert_type(x, I32).T, U32)


def _scores_t(k, q, keep):
    st = lax.dot_general(k, q, (((1,), (1,)), ((), ())), preferred_element_type=F32)
    if keep is not None:
        st = jnp.where(keep, st, NEG_INF)
    m = jnp.max(st, axis=0, keepdims=True)
    p = jnp.exp(st - m)
    return m, p, jnp.sum(p, axis=0, keepdims=True)


def _group_attn_kernel(step_h_ref, step_b_ref, sub_j_ref, nsteps_ref, qs_ref, k_ref, vt_ref, after_ref,
                       part_ref):
    del step_h_ref, step_b_ref, after_ref
    i = pl.program_id(0)
    half = HEAD_DIM // 2

    @pl.when(i < nsteps_ref[0])
    def _():
        for u in range(ATTN_SUBS_PER_STEP):
            j = sub_j_ref[i * ATTN_SUBS_PER_STEP + u]
            rows = pl.ds(u * ATTN_SUB_ROWS, ATTN_SUB_ROWS)
            q = qs_ref[rows, :].astype(BF16)
            m, p, l = _scores_t(k_ref[0, j], q, None)
            ot = jnp.dot(vt_ref[0, j], p.astype(BF16), preferred_element_type=F32) * (1.0 / l)
            lse = lax.bitcast_convert_type(m + jnp.log(l), U32)
            packed_t = jnp.concatenate(
                [_pack_bf16_row_pairs(ot), jnp.broadcast_to(lse, (half, ATTN_SUB_ROWS))], axis=0)
            part_ref[rows, :] = _transpose_u32(packed_t)


def _group_attention(qs, k_blk, vt_blk, step_h, step_b, sub_j, nsteps, nb, after):
    n_steps = step_h.shape[0]
    row_map = lambda i, sh, sb, sj, ns: (jnp.minimum(i, ns[0] - 1), 0)
    kv_map = lambda i, sh, sb, sj, ns: (sh[i], sb[i], 0, 0)
    return pl.pallas_call(
        _group_attn_kernel,
        grid_spec=pltpu.PrefetchScalarGridSpec(
            num_scalar_prefetch=4,
            grid=(n_steps,),
            in_specs=[pl.BlockSpec((ATTN_STEP_ROWS, HEAD_DIM), row_map),
                      pl.BlockSpec((1, nb, MOBA_BLOCK, HEAD_DIM), kv_map),
                      pl.BlockSpec((1, nb, HEAD_DIM, MOBA_BLOCK), kv_map), _ORDER_ONLY],
            out_specs=pl.BlockSpec((ATTN_STEP_ROWS, HEAD_DIM), row_map)),
        out_shape=jax.ShapeDtypeStruct(((n_steps + 1) * ATTN_STEP_ROWS, HEAD_DIM), U32),
        compiler_params=_params("arbitrary"),
        name="moba_group_attention",
    )(step_h, step_b, sub_j, nsteps, qs, k_blk, vt_blk, after)


def _attn_merge_kernel(x_ref, qp_ref, k_ref, vt_ref, pg_ref, sel_ref, wo_ref, lng_ref, lnb_ref,
                       after_ref, o_ref, op_ref):
    del after_ref
    tm = x_ref.shape[0]
    half = HEAD_DIM // 2
    sel = sel_ref[...]
    key = lax.broadcasted_iota(I32, (tm, tm), 0)
    qry = lax.broadcasted_iota(I32, (tm, tm), 1)
    causal = key <= qry
    heads = []
    for hd in range(N_HEADS):
        q = qp_ref[0, hd].astype(BF16)
        m_own, p, l_own = _scores_t(k_ref[hd, 0], q, causal)
        acc = jnp.dot(vt_ref[hd, 0], p.astype(BF16), preferred_element_type=F32)

        outs, lses = [], []
        m_tot = m_own
        for c in range(MOBA_TOPK):
            part_t = _transpose_u32(pg_ref[c, 0, hd])
            lo, hi = _unpack_bf16_pairs(part_t[:half])
            valid = sel[hd * SEL_ROWS + c:hd * SEL_ROWS + c + 1, :] >= 0
            outs.append(jnp.where(valid, jnp.concatenate([lo, hi], axis=0), 0.0))
            lse = jnp.where(valid, lax.bitcast_convert_type(part_t[half:half + 1], F32), NEG_INF)
            lses.append(lse)
            m_tot = jnp.maximum(m_tot, lse)
        w_own = jnp.exp(m_own - m_tot)
        num = acc * w_own
        den = l_own * w_own
        for c in range(MOBA_TOPK):
            w_c = jnp.exp(lses[c] - m_tot)
            num = num + outs[c] * w_c
            den = den + w_c
        heads.append(num * (1.0 / den))
    att = jnp.concatenate(heads, axis=0).T.astype(BF16)
    mix = jnp.dot(att, wo_ref[...], preferred_element_type=F32)
    x1 = _layer_norm(DN_ALPHA * x_ref[...] + mix, lng_ref[...], lnb_ref[...])
    o_ref[...] = x1
    op_ref[...] = _pack_bf16_pairs(x1)


def _attn_merge(x, qp, k_blk, vt_blk, pg, sel, w_o, ln_g, ln_b, after):
    n, d = x.shape
    blk = MOBA_BLOCK
    row = lambda v: v.reshape(1, -1).astype(F32)
    return pl.pallas_call(
        _attn_merge_kernel,
        grid=(n // blk,),
        in_specs=[pl.BlockSpec((blk, d), lambda i: (i, 0)),
                  pl.BlockSpec((1, N_HEADS, blk, HEAD_DIM), lambda i: (i, 0, 0, 0)),
                  pl.BlockSpec((N_HEADS, 1, blk, HEAD_DIM), lambda i: (0, i, 0, 0)),
                  pl.BlockSpec((N_HEADS, 1, HEAD_DIM, blk), lambda i: (0, i, 0, 0)),
                  pl.BlockSpec((MOBA_TOPK, 1, N_HEADS, blk, HEAD_DIM), lambda i: (0, i, 0, 0, 0)),
                  pl.BlockSpec((N_HEADS * SEL_ROWS, blk), lambda i: (0, i)),
                  _full((d, d)), _full((1, d)), _full((1, d)), _ORDER_ONLY],
        out_specs=[pl.BlockSpec((blk, d), lambda i: (i, 0)), pl.BlockSpec((blk, d // 2), lambda i: (i, 0))],
        out_shape=[jax.ShapeDtypeStruct((n, d), F32), jax.ShapeDtypeStruct((n, d // 2), U32)],
        compiler_params=_params("parallel"),
        name="moba_merge",
    )(x, qp, k_blk, vt_blk, pg, sel, w_o.astype(BF16), row(ln_g), row(ln_b), after)


def _moba_layer(streams, seq, w_q, w_o, ln_g, ln_b, cos_t, sin_t):
    staged = [_moba_regroup(x, seq, w_q, cos_t, sin_t, k_mean) for x, _, _, k_mean in streams]
    gathered, order = [], staged[-1][1]
    for (x, k_blk, vt_blk, _), (qp, sel, qs, tables, gather_idx) in zip(streams, staged):
        nbt = x.shape[0] // MOBA_BLOCK
        part = _group_attention(qs, k_blk, vt_blk, *tables, seq // MOBA_BLOCK, order)
        gathered.append(_sc_gather_rows(part, gather_idx).reshape(MOBA_TOPK, nbt, N_HEADS, MOBA_BLOCK, HEAD_DIM))
        order = part
    outs = []
    for (x, k_blk, vt_blk, _), (qp, sel, _, _, _), pg in zip(streams, staged, gathered):
        outs.append(_attn_merge(x, qp, k_blk, vt_blk, pg, sel, w_o, ln_g, ln_b, order))
        order = outs[-1][0]
    return outs


def _moba_regroup(x, seq, w_q, cos_t, sin_t, k_mean):
    n, d = x.shape
    batch = n // seq
    blk = MOBA_BLOCK
    nb = seq // blk
    nbt = n // blk
    chunk = SC_INDEX_CHUNK
    qp, sel, cnt = _queries(x, seq, w_q, cos_t, sin_t, k_mean)

    counts = cnt[:, :, 0].reshape(batch * N_HEADS, nb)
    gpad = (counts + ATTN_SUB_ROWS - 1) // ATTN_SUB_ROWS * ATTN_SUB_ROWS
    seg = jnp.sum(gpad, axis=1)
    seg_pad = (seg + ATTN_STEP_ROWS - 1) // ATTN_STEP_ROWS * ATTN_STEP_ROWS
    seg_end = jnp.cumsum(seg_pad)
    gend = (seg_end - seg_pad)[:, None] + jnp.cumsum(gpad, axis=1)
    gstart = (gend - gpad).astype(I32)
    steps_per_seg = -(-(MOBA_TOPK * seq + nb * (ATTN_SUB_ROWS - 1)) // ATTN_STEP_ROWS)
    n_steps = batch * N_HEADS * steps_per_seg
    step_first = jnp.arange(n_steps, dtype=I32) * ATTN_STEP_ROWS
    step_seg = jnp.minimum(_count_le(seg_end, step_first), batch * N_HEADS - 1)
    sub_first = jnp.arange(n_steps * ATTN_SUBS_PER_STEP, dtype=I32) * ATTN_SUB_ROWS
    sub_grp = jnp.minimum(_count_le(gend.reshape(-1), sub_first), batch * N_HEADS * nb - 1)
    nsteps = (seg_end[-1] // ATTN_STEP_ROWS).astype(I32).reshape(1)
    dump_row = n_steps * ATTN_STEP_ROWS

    pos = _moba_positions(sel, gstart.reshape(batch, N_HEADS * nb, 1), seq, dump_row)
    pos5 = pos.reshape(N_HEADS, SEL_ROWS, nbt, blk // chunk, chunk)[:, :MOBA_TOPK]
    by_choice = pos5.transpose(1, 2, 0, 3, 4)
    gather_idx = by_choice.reshape(-1)
    scatter_idx = by_choice.reshape(MOBA_TOPK, nbt * N_HEADS * (blk // chunk), chunk)

    qs = _sc_scatter_rows(qp.reshape(n * N_HEADS, HEAD_DIM), scatter_idx, dump_row + chunk)
    tables = ((step_seg % N_HEADS).astype(I32), (step_seg // N_HEADS).astype(I32),
              (sub_grp % nb).astype(I32), nsteps)
    return qp, sel, qs, tables, gather_idx


def _rope_tables(seq):
    half = ROT_DIM // 2
    lane = jnp.arange(HEAD_DIM)
    inv = ROPE_THETA ** (-(2 * (lane % half)).astype(F32) / ROT_DIM)
    ang = jnp.arange(seq, dtype=F32)[:, None] * jnp.where(lane < ROT_DIM, inv, 0.0)[None, :]
    sign = jnp.where(lane < half, -1.0, 1.0).astype(F32)
    return jnp.cos(ang), jnp.sin(ang) * sign[None, :]


def kernel(x, p, ln_g, ln_b, a_w_in, a_b_in, a_conv_w, a_conv_b, a_gate_a_w, a_gate_a_b, a_gate_i_w,
           a_gate_i_b, a_lambda, a_w_out, kv_ln_g, kv_ln_b, w_kv, b_w_q, b_w_o, router_w, router_b,
           exp_w_gate, exp_w_up, exp_w_down, sh_w_gate, sh_w_up, sh_w_down, ple_w, ple_gate_w):
    batch, seq, d = x.shape
    xs = [x.reshape(batch * seq, d)] * batch
    p_all = p.reshape(DEPTH * batch, seq, PLE_DIM)
    cos_t, sin_t = _rope_tables(seq)
    kv = None
    for i in range(DEPTH):
        if i < N_A_LAYERS:
            mixed = [_rglru_layer(xb, seq, b if i == 0 else 0, a_w_in[i], a_b_in[i], a_conv_w[i],
                                  a_conv_b[i], a_gate_a_w[i], a_gate_a_b[i], a_gate_i_w[i], a_gate_i_b[i],
                                  a_lambda[i], a_w_out[i], ln_g[i, 0], ln_b[i, 0])
                     for b, xb in enumerate(xs)]
        else:
            if i == N_A_LAYERS:
                kv = []
                for xb in xs:
                    k_blk, vt_blk, k_mean = _shared_kv(xb, seq, kv_ln_g, kv_ln_b, w_kv, cos_t, sin_t)
                    kv.append((k_blk, vt_blk, k_mean.reshape(seq // MOBA_BLOCK, d)))
            j = i - N_A_LAYERS
            mixed = _moba_layer([(xb,) + kvb for xb, kvb in zip(xs, kv)], seq, b_w_q[j], b_w_o[j],
                                ln_g[i, 0], ln_b[i, 0], cos_t, sin_t)
        xs = _moe_layer(mixed, i, router_w[i], router_b[i], exp_w_gate, exp_w_up, exp_w_down,
                        sh_w_gate[i], sh_w_up[i], sh_w_down[i], ln_g[i, 1], ln_b[i, 1],
                        p_all, i * batch, ple_w[i], ple_gate_w[i], i == DEPTH - 1)
    return xs.reshape(batch, seq, d)
```
